```python
import jax, jax.numpy as jnp
from jax import lax
import numpy as np

D_MODEL = 1024
BATCH = 2
SEQ = 16384
DEPTH = 1

DA_GROUPS = ((128, 1), (512, 4), (2048, 16))
DA_HEADS = 4
DA_HEAD_DIM = 128
DA_WIDTH = DA_HEADS * DA_HEAD_DIM
DA_BLOCK = 128
GLA_HEADS = 4
GLA_KEY_DIM = D_MODEL // 2
GLA_VALUE_DIM = D_MODEL
GLA_DK = GLA_KEY_DIM // GLA_HEADS
GLA_DV = GLA_VALUE_DIM // GLA_HEADS
GLA_GATE_RANK = 16
GLA_GATE_NORMALIZER = 16.0
GLA_CHUNK = 64
N_EXPERTS = 32
TOP_K = 4
D_FF = D_MODEL
SWIGLU_ALPHA = 1.702
SWIGLU_LIMIT = 7.0
MOE_BLOCK = 128
RMS_EPS = 1e-5
NEG_INF = -1e30
IN_SPLITS = (3 * len(DA_GROUPS) * DA_WIDTH, GLA_KEY_DIM, GLA_KEY_DIM, GLA_VALUE_DIM,
             GLA_VALUE_DIM, GLA_GATE_RANK, 2 * D_MODEL)
IN_WIDTH = sum(IN_SPLITS)
IN_OFFSETS = tuple(sum(IN_SPLITS[:i + 1]) for i in range(len(IN_SPLITS) - 1))

kernel_name = "hybrid_dilated_gla_moe_block"


def rms_norm(x, w):
    x32 = x.astype(jnp.float32)
    y = x32 * lax.rsqrt(jnp.mean(x32 * x32, axis=-1, keepdims=True) + RMS_EPS)
    return (y * w.astype(jnp.float32)).astype(x.dtype)


def dilated_window_attention(q, k, v, window, dilation):
    bsz, seq, n_h, e = q.shape
    steps = window // dilation
    span = dilation * DA_BLOCK
    s_pad = -(-seq // span) * span
    sub_len = s_pad // dilation
    n_blk = sub_len // DA_BLOCK

    def to_blocks(t):
        t = jnp.pad(t, ((0, 0), (0, s_pad - seq), (0, 0), (0, 0)))
        t = t.reshape(bsz, sub_len, dilation, n_h, e).transpose(0, 2, 3, 1, 4)
        return t.reshape(bsz, dilation, n_h, n_blk, DA_BLOCK, e).astype(jnp.float32)

    def with_prev(t):
        prev = jnp.pad(t, ((0, 0), (0, 0), (0, 0), (1, 0), (0, 0), (0, 0)))[:, :, :, :-1]
        return jnp.concatenate([prev, t], axis=4)

    qb = to_blocks(q)
    kb = with_prev(to_blocks(k))
    vb = with_prev(to_blocks(v))
    s = jnp.einsum('bdhnqe,bdhnke->bdhnqk', qb, kb) * (e ** -0.5)
    qi = jnp.arange(DA_BLOCK)[:, None]
    kj = jnp.arange(2 * DA_BLOCK)[None, :]
    dist = qi + DA_BLOCK - kj
    band = (dist >= 0) & (dist <= steps)
    has_prev = (jnp.arange(n_blk)[:, None, None] > 0) | (kj >= DA_BLOCK)[None]
    mask = band[None] & has_prev
    s = jnp.where(mask, s, NEG_INF)
    m = jnp.max(s, axis=-1, keepdims=True)
    p = jnp.exp(s - m)
    l = jnp.sum(p, axis=-1, keepdims=True)
    o = jnp.einsum('bdhnqk,bdhnke->bdhnqe', p, vb) / l
    lse = (m + jnp.log(l))[..., 0]
    o = o.reshape(bsz, dilation, n_h, sub_len, e).transpose(0, 3, 1, 2, 4)
    o = o.reshape(bsz, s_pad, n_h, e)[:, :seq]
    lse = lse.reshape(bsz, dilation, n_h, sub_len).transpose(0, 3, 1, 2)
    lse = lse.reshape(bsz, s_pad, n_h)[:, :seq]
    return o, lse


def dilated_mixer(a_qkv):
    bsz, seq, _ = a_qkv.shape
    a = a_qkv.reshape(bsz, seq, len(DA_GROUPS), 3, DA_HEADS, DA_HEAD_DIM)
    outs, lses = [], []
    for gi, (window, dilation) in enumerate(DA_GROUPS):
        o, lse = dilated_window_attention(a[:, :, gi, 0], a[:, :, gi, 1], a[:, :, gi, 2],
                                          window, dilation)
        outs.append(o)
        lses.append(lse)
    w = jax.nn.softmax(jnp.stack(lses, axis=0), axis=0)
    o = jnp.sum(w[..., None] * jnp.stack(outs, axis=0), axis=0)
    return o.reshape(bsz, seq, DA_WIDTH).astype(a_qkv.dtype)


def gla_chunked(q, k, v, g):
    bsz, n_h, seq, dk = q.shape
    dv = v.shape[-1]
    c = GLA_CHUNK
    n_c = seq // c
    q, k, g = (t.reshape(bsz, n_h, n_c, c, dk) for t in (q, k, g))
    v = v.reshape(bsz, n_h, n_c, c, dv)
    b = jnp.cumsum(g, axis=3)
    b_last = b[:, :, :, -1:, :]
    q_e = q * jnp.exp(b)
    k_e = k * jnp.exp(-b)
    k_to_end = k * jnp.exp(b_last - b)
    causal = jnp.tril(jnp.ones((c, c), dtype=bool))
    att = jnp.where(causal, jnp.einsum('bhncd,bhnsd->bhncs', q_e, k_e), 0.0)
    o_intra = jnp.einsum('bhncs,bhnse->bhnce', att, v)
    decay = jnp.exp(b_last[:, :, :, 0, :])

    def step(state, inp):
        qc, kc, vc, dc = inp
        o = jnp.einsum('bhcd,bhde->bhce', qc, state)
        state = dc[..., None] * state + jnp.einsum('bhcd,bhce->bhde', kc, vc)
        return state, o

    xs = (jnp.moveaxis(q_e, 2, 0), jnp.moveaxis(k_to_end, 2, 0),
          jnp.moveaxis(v, 2, 0), jnp.moveaxis(decay, 2, 0))
    _, o_inter = lax.scan(step, jnp.zeros((bsz, n_h, dk, dv), jnp.float32), xs)
    o = o_intra + jnp.moveaxis(o_inter, 0, 2)
    return o.reshape(bsz, n_h, seq, dv)


def gla_mixer(gq, gk, gv, gog, glr, w_gk2, b_gk2, norm_w):
    bsz, seq, _ = gq.shape
    forget = jax.nn.log_sigmoid((glr @ w_gk2 + b_gk2).astype(jnp.float32)) / GLA_GATE_NORMALIZER

    def heads(t, d):
        return t.reshape(bsz, seq, GLA_HEADS, d).transpose(0, 2, 1, 3).astype(jnp.float32)

    o = gla_chunked(heads(gq, GLA_DK) * (GLA_DK ** -0.5), heads(gk, GLA_DK),
                    heads(gv, GLA_DV), heads(forget, GLA_DK))
    o = o.transpose(0, 2, 1, 3)
    o = o * lax.rsqrt(jnp.mean(o * o, axis=-1, keepdims=True) + RMS_EPS) * norm_w.astype(jnp.float32)
    o = o.reshape(bsz, seq, GLA_VALUE_DIM) * jax.nn.silu(gog.astype(jnp.float32))
    return o.astype(gq.dtype)


def moe_ffn(h, w_router, b_router, w_up, b_up, w_down, b_down):
    bsz, seq, d = h.shape
    n_tok = bsz * seq
    n_asg = n_tok * TOP_K
    xt = h.reshape(n_tok, d)
    logits = (xt @ w_router + b_router).astype(jnp.float32)
    top_val, top_idx = lax.top_k(logits, TOP_K)
    gate = jax.nn.softmax(top_val, axis=-1)
    e_flat = top_idx.reshape(-1)
    g_flat = gate.reshape(-1)
    order = jnp.argsort(e_flat)
    e_sorted = e_flat[order]
    tok_sorted = order // TOP_K
    counts = jnp.bincount(e_flat, length=N_EXPERTS)
    padded = (counts + MOE_BLOCK - 1) // MOE_BLOCK * MOE_BLOCK
    pad_end = jnp.cumsum(padded)
    pad_start = pad_end - padded
    start = jnp.cumsum(counts) - counts
    dest = pad_start[e_sorted] + (jnp.arange(n_asg) - start[e_sorted])
    n_rows = (n_asg + N_EXPERTS * (MOE_BLOCK - 1) + MOE_BLOCK - 1) // MOE_BLOCK * MOE_BLOCK
    n_blocks = n_rows // MOE_BLOCK
    row_tok = jnp.zeros((n_rows,), jnp.int32).at[dest].set(tok_sorted.astype(jnp.int32))
    row_w = jnp.zeros((n_rows,), jnp.float32).at[dest].set(g_flat[order])
    block_e = jnp.minimum(jnp.searchsorted(pad_end, jnp.arange(n_blocks) * MOE_BLOCK, side='right'),
                          N_EXPERTS - 1)
    xin = xt[row_tok].reshape(n_blocks, MOE_BLOCK, d)

    def expert_block(args):
        xb, e = args
        hu = xb @ w_up[e] + b_up[e]
        x_glu = jnp.minimum(hu[:, :D_FF], SWIGLU_LIMIT)
        x_lin = jnp.clip(hu[:, D_FF:], -SWIGLU_LIMIT, SWIGLU_LIMIT)
        act = x_glu * jax.nn.sigmoid(SWIGLU_ALPHA * x_glu) * (x_lin + 1.0)
        return act @ w_down[e] + b_down[e]

    yb = lax.map(expert_block, (xin, block_e)).reshape(n_rows, d)
    y = jnp.zeros((n_tok, d), h.dtype).at[row_tok].add(yb * row_w[:, None].astype(h.dtype))
    return y.reshape(bsz, seq, d)


def hybrid_layer(h, ln1_w, w_in, w_gk2, b_gk2, gla_norm_w, w_proj_a, w_proj_b, w_out,
                 ln2_w, w_router, b_router, w_up, b_up, w_down, b_down):
    xn = rms_norm(h, ln1_w)
    proj = xn @ w_in
    a_qkv, gq, gk, gv, gog, glr, merge = jnp.split(proj, list(IN_OFFSETS), axis=-1)
    o_a = dilated_mixer(a_qkv)
    o_b = gla_mixer(gq, gk, gv, gog, glr, w_gk2, b_gk2, gla_norm_w)
    gate_a, gate_b = jnp.split(jax.nn.sigmoid(merge), 2, axis=-1)
    mixed = gate_a * (o_a @ w_proj_a) + gate_b * (o_b @ w_proj_b)
    h = h + mixed @ w_out
    h = h + moe_ffn(rms_norm(h, ln2_w), w_router, b_router, w_up, b_up, w_down, b_down)
    return h


def setup_inputs(seed: int = 0) -> dict:
    key = jax.random.key(seed)
    ks = jax.random.split(key, 17)
    f32 = jnp.float32

    def nrm(k, shape, fan_in):
        return jax.random.normal(k, shape, f32) * (fan_in ** -0.5)

    def gain(k, shape):
        return 1.0 + 0.02 * jax.random.normal(k, shape, f32)

    def small(k, shape, scale):
        return scale * jax.random.normal(k, shape, f32)

    return {
        'x': jax.random.normal(ks[0], (BATCH, SEQ, D_MODEL), f32),
        'ln1_w': gain(ks[1], (DEPTH, D_MODEL)),
        'w_in': nrm(ks[2], (DEPTH, D_MODEL, IN_WIDTH), D_MODEL),
        'w_gk2': nrm(ks[3], (DEPTH, GLA_GATE_RANK, GLA_KEY_DIM), GLA_GATE_RANK),
        'b_gk2': small(ks[4], (DEPTH, GLA_KEY_DIM), 0.1),
        'gla_norm_w': gain(ks[5], (DEPTH, GLA_DV)),
        'w_proj_a': nrm(ks[6], (DEPTH, DA_WIDTH, D_MODEL), DA_WIDTH),
        'w_proj_b': nrm(ks[7], (DEPTH, GLA_VALUE_DIM, D_MODEL), GLA_VALUE_DIM),
        'w_out': nrm(ks[8], (DEPTH, D_MODEL, D_MODEL), D_MODEL),
        'ln2_w': gain(ks[9], (DEPTH, D_MODEL)),
        'w_router': nrm(ks[10], (DEPTH, D_MODEL, N_EXPERTS), D_MODEL),
        'b_router': small(ks[11], (DEPTH, N_EXPERTS), 0.01),
        'w_up': nrm(ks[12], (DEPTH, N_EXPERTS, D_MODEL, 2 * D_FF), D_MODEL),
        'b_up': small(ks[13], (DEPTH, N_EXPERTS, 2 * D_FF), 0.02),
        'w_down': nrm(ks[14], (DEPTH, N_EXPERTS, D_FF, D_MODEL), D_FF),
        'b_down': small(ks[15], (DEPTH, N_EXPERTS, D_MODEL), 0.02),
        'lnf_w': gain(ks[16], (D_MODEL,)),
    }


def reference(x, ln1_w, w_in, w_gk2, b_gk2, gla_norm_w, w_proj_a, w_proj_b, w_out,
              ln2_w, w_router, b_router, w_up, b_up, w_down, b_down, lnf_w):
    h = x
    for layer in range(DEPTH):
        h = hybrid_layer(h, ln1_w[layer], w_in[layer], w_gk2[layer], b_gk2[layer],
                         gla_norm_w[layer], w_proj_a[layer], w_proj_b[layer], w_out[layer],
                         ln2_w[layer], w_router[layer], b_router[layer], w_up[layer],
                         b_up[layer], w_down[layer], b_down[layer])
    return rms_norm(h, lnf_w)
```

```python
import functools

import jax
import jax.numpy as jnp
from jax import lax
from jax.experimental import pallas as pl
from jax.experimental.pallas import tpu as pltpu

F32 = jnp.float32
BF16 = jnp.bfloat16

D_MODEL = 1024
DA_GROUPS = ((128, 1), (512, 4), (2048, 16))
DA_HEADS = 4
DA_HEAD_DIM = 128
DA_WIDTH = DA_HEADS * DA_HEAD_DIM
DA_BLOCK = 128
GLA_HEADS = 4
GLA_KEY_DIM = D_MODEL // 2
GLA_VALUE_DIM = D_MODEL
GLA_DK = GLA_KEY_DIM // GLA_HEADS
GLA_DV = GLA_VALUE_DIM // GLA_HEADS
GLA_GATE_RANK = 16
GLA_GATE_NORMALIZER = 16.0
GLA_CHUNK = 64
N_EXPERTS = 32
TOP_K = 4
D_FF = D_MODEL
SWIGLU_ALPHA = 1.702
SWIGLU_LIMIT = 7.0
RMS_EPS = 1e-5
NEG_INF = -1e30

LANES = 128
QKV_W = 3 * DA_WIDTH
GLA_W = 2 * GLA_KEY_DIM + 2 * GLA_VALUE_DIM + LANES
MERGE_W = 2 * D_MODEL

TM_IN = 256
N_CHUNK = 512
T_GLA = 512
TM_MIX = 256
TB = 512
TM_ROW = 256
VMEM_LIMIT = 56 * 1024 * 1024

_NT = (((1,), (1,)), ((), ()))
_TN = (((0,), (0,)), ((), ()))


def _dot(a, b):
    return jnp.dot(a, b, preferred_element_type=F32)


def _sigmoid(x):
    return 1.0 / (1.0 + jnp.exp(-x))


def _rms_scale(x):
    return lax.rsqrt(jnp.mean(x * x, axis=-1, keepdims=True) + RMS_EPS)


def _inproj_kernel(x_ref, ln_ref, w_ref, pa0_ref, pa1_ref, pa2_ref, pg_ref, pm_ref, xs_ref, xn_ref):
    tm = x_ref.shape[1]
    n_slab = D_MODEL // LANES

    def project(out_write, col0, width):
        for c0 in range(0, width, N_CHUNK):
            cw = min(N_CHUNK, width - c0)
            out_write(c0, cw, _dot(xs_ref[...], w_ref[:, col0 + c0:col0 + c0 + cw]).astype(BF16))

    x = x_ref[0]
    xn = x * _rms_scale(x) * ln_ref[...]
    xs_ref[...] = xn.astype(BF16)
    for j in range(n_slab):
        xn_ref[j] = xn[:, j * LANES:(j + 1) * LANES]

    def write_to(ref):
        def write(c0, cw, val):
            ref[0, :, c0:c0 + cw] = val
        return write

    project(write_to(pa0_ref), 0, QKV_W)
    project(write_to(pg_ref), 3 * QKV_W, GLA_W)
    project(write_to(pm_ref), 3 * QKV_W + GLA_W, MERGE_W)

    for gi, out_ref in ((1, pa1_ref), (2, pa2_ref)):
        d = DA_GROUPS[gi][1]
        n = tm // d
        for r in range(d):
            for j in range(n_slab):
                xs_ref[r * n:(r + 1) * n, j * LANES:(j + 1) * LANES] = (
                    xn_ref[j, pl.ds(r, n, stride=d), :].astype(BF16))

        def write(c0, cw, val, out_ref=out_ref, d=d, n=n):
            for r in range(d):
                out_ref[0, r, :, c0:c0 + cw] = val[r * n:(r + 1) * n]

        project(write, gi * QKV_W, QKV_W)


def _inproj(x, ln1_w, w_all):
    bsz, seq, _ = x.shape
    tm = TM_IN
    d1, d2 = DA_GROUPS[1][1], DA_GROUPS[2][1]
    grid = (bsz, seq // tm)
    out_shape = (
        jax.ShapeDtypeStruct((bsz, seq, QKV_W), BF16),
        jax.ShapeDtypeStruct((bsz, d1, seq // d1, QKV_W), BF16),
        jax.ShapeDtypeStruct((bsz, d2, seq // d2, QKV_W), BF16),
        jax.ShapeDtypeStruct((bsz, seq, GLA_W), BF16),
        jax.ShapeDtypeStruct((bsz, seq, MERGE_W), BF16),
    )
    return pl.pallas_call(
        _inproj_kernel,
        grid=grid,
        in_specs=[
            pl.BlockSpec((1, tm, D_MODEL), lambda b, i: (b, i, 0)),
            pl.BlockSpec((1, D_MODEL), lambda b, i: (0, 0)),
            pl.BlockSpec(w_all.shape, lambda b, i: (0, 0), pipeline_mode=pl.Buffered(1)),
        ],
        out_specs=(
            pl.BlockSpec((1, tm, QKV_W), lambda b, i: (b, i, 0)),
            pl.BlockSpec((1, d1, tm // d1, QKV_W), lambda b, i: (b, 0, i, 0)),
            pl.BlockSpec((1, d2, tm // d2, QKV_W), lambda b, i: (b, 0, i, 0)),
            pl.BlockSpec((1, tm, GLA_W), lambda b, i: (b, i, 0)),
            pl.BlockSpec((1, tm, MERGE_W), lambda b, i: (b, i, 0)),
        ),
        out_shape=out_shape,
        scratch_shapes=[pltpu.VMEM((tm, D_MODEL), BF16),
                        pltpu.VMEM((D_MODEL // LANES, tm, LANES), F32)],
        compiler_params=pltpu.CompilerParams(
            dimension_semantics=("parallel", "parallel"), vmem_limit_bytes=VMEM_LIMIT),
        name="inproj",
    )(x, ln1_w.reshape(1, D_MODEL), w_all)


def _dil_attn_kernel(q_ref, kp_ref, kc_ref, vp_ref, vc_ref, o_ref, l_ref):
    n = pl.program_id(2)
    blk = DA_BLOCK
    qi = lax.broadcasted_iota(jnp.int32, (blk, blk), 0)
    kj = lax.broadcasted_iota(jnp.int32, (blk, blk), 1)
    mask_cur = kj <= qi
    mask_prev = kj >= qi + jnp.where(n > 0, 0, blk)
    scale = DA_HEAD_DIM ** -0.5
    for h in range(DA_HEADS):
        sl = slice(h * DA_HEAD_DIM, (h + 1) * DA_HEAD_DIM)
        q = q_ref[0, 0, :, sl]
        s_p = lax.dot_general(q, kp_ref[0, 0, :, sl], _NT, preferred_element_type=F32) * scale
        s_c = lax.dot_general(q, kc_ref[0, 0, :, sl], _NT, preferred_element_type=F32) * scale
        s_p = jnp.where(mask_prev, s_p, NEG_INF)
        s_c = jnp.where(mask_cur, s_c, NEG_INF)
        m = jnp.maximum(jnp.max(s_p, axis=-1, keepdims=True), jnp.max(s_c, axis=-1, keepdims=True))
        p_p = jnp.exp(s_p - m)
        p_c = jnp.exp(s_c - m)
        l = jnp.sum(p_p, axis=-1, keepdims=True) + jnp.sum(p_c, axis=-1, keepdims=True)
        acc = _dot(p_p.astype(BF16), vp_ref[0, 0, :, sl]) + _dot(p_c.astype(BF16), vc_ref[0, 0, :, sl])
        o_ref[0, 0, :, sl] = (acc / l).astype(o_ref.dtype)
        l_ref[0, 0, :, sl] = jnp.broadcast_to(m + jnp.log(l), (blk, DA_HEAD_DIM))


def _dil_attn(pa):
    bsz, d, sub_len, _ = pa.shape
    n_blk = sub_len // DA_BLOCK
    blk = (1, 1, DA_BLOCK, DA_WIDTH)

    def cur(col):
        return pl.BlockSpec(blk, lambda b, r, n: (b, r, n, col))

    def prev(col):
        return pl.BlockSpec(blk, lambda b, r, n: (b, r, jnp.maximum(n - 1, 0), col))

    return pl.pallas_call(
        _dil_attn_kernel,
        grid=(bsz, d, n_blk),
        in_specs=[cur(0), prev(1), cur(1), prev(2), cur(2)],
        out_specs=(cur(0), cur(0)),
        out_shape=(jax.ShapeDtypeStruct((bsz, d, sub_len, DA_WIDTH), BF16),
                   jax.ShapeDtypeStruct((bsz, d, sub_len, DA_WIDTH), F32)),
        compiler_params=pltpu.CompilerParams(
            dimension_semantics=("parallel", "parallel", "arbitrary"), vmem_limit_bytes=VMEM_LIMIT),
        name=f"dil_attn_d{d}",
    )(pa, pa, pa, pa, pa)


def _gla_kernel(q_ref, k_ref, v_ref, og_ref, lr_ref, w2_ref, b2_ref, nw_ref, o_ref, st_ref):
    t = pl.program_id(2)
    c = GLA_CHUNK

    @pl.when(t == 0)
    def _():
        st_ref[...] = jnp.zeros_like(st_ref)

    gpre = _dot(lr_ref[0], w2_ref[...]) + b2_ref[...]
    forget = (jnp.minimum(gpre, 0.0) - jnp.log(1.0 + jnp.exp(-jnp.abs(gpre)))) / GLA_GATE_NORMALIZER
    row = lax.broadcasted_iota(jnp.int32, (c, c), 0)
    col = lax.broadcasted_iota(jnp.int32, (c, c), 1)
    causal = col <= row
    tril = jnp.where(causal, 1.0, 0.0).astype(BF16)
    scale = GLA_DK ** -0.5
    nw = nw_ref[...]
    for ci in range(q_ref.shape[1] // c):
        rs = slice(ci * c, (ci + 1) * c)
        g = forget[rs]
        g_hi = g.astype(BF16)
        g_lo = (g - g_hi.astype(F32)).astype(BF16)
        b = _dot(tril, g_hi) + _dot(tril, g_lo)
        b_last = b[c - 1:c, :]
        q = q_ref[0, rs, :].astype(F32)
        k = k_ref[0, rs, :].astype(F32)
        v = v_ref[0, rs, :]
        q_e = (q * (scale * jnp.exp(b))).astype(BF16)
        k_e = (k * jnp.exp(-b)).astype(BF16)
        k_end = (k * jnp.exp(b_last - b)).astype(BF16)
        att = lax.dot_general(q_e, k_e, _NT, preferred_element_type=F32)
        att = jnp.where(causal, att, 0.0).astype(BF16)
        st = st_ref[...]
        o = _dot(att, v) + lax.dot_general(q_e, st.astype(BF16), _NT, preferred_element_type=F32)
        st_ref[...] = jnp.exp(b_last) * st + lax.dot_general(v, k_end, _TN, preferred_element_type=F32)
        o = o * _rms_scale(o) * nw
        gate = og_ref[0, rs, :].astype(F32)
        o_ref[0, rs, :] = (o * (gate * _sigmoid(gate))).astype(BF16)


def _gla(pg, w2_pad, b_gk2, gla_norm_w):
    bsz, seq, _ = pg.shape
    t = T_GLA
    kq = GLA_KEY_DIM // GLA_DK
    kv = 2 * GLA_KEY_DIM // GLA_DV
    kg = kv + GLA_VALUE_DIM // GLA_DV
    klr = (2 * GLA_KEY_DIM + 2 * GLA_VALUE_DIM) // LANES
    return pl.pallas_call(
        _gla_kernel,
        grid=(bsz, GLA_HEADS, seq // t),
        in_specs=[
            pl.BlockSpec((1, t, GLA_DK), lambda b, h, i: (b, i, h)),
            pl.BlockSpec((1, t, GLA_DK), lambda b, h, i: (b, i, kq + h)),
            pl.BlockSpec((1, t, GLA_DV), lambda b, h, i: (b, i, kv + h)),
            pl.BlockSpec((1, t, GLA_DV), lambda b, h, i: (b, i, kg + h)),
            pl.BlockSpec((1, t, LANES), lambda b, h, i: (b, i, klr)),
            pl.BlockSpec((LANES, GLA_DK), lambda b, h, i: (0, h)),
            pl.BlockSpec((1, GLA_DK), lambda b, h, i: (0, h)),
            pl.BlockSpec((1, GLA_DV), lambda b, h, i: (0, 0)),
        ],
        out_specs=pl.BlockSpec((1, t, GLA_DV), lambda b, h, i: (b, i, h)),
        out_shape=jax.ShapeDtypeStruct((bsz, seq, GLA_VALUE_DIM), BF16),
        scratch_shapes=[pltpu.VMEM((GLA_DV, GLA_DK), F32)],
        compiler_params=pltpu.CompilerParams(
            dimension_semantics=("parallel", "parallel", "arbitrary"), vmem_limit_bytes=VMEM_LIMIT),
        name="gla",
    )(pg, pg, pg, pg, pg, w2_pad, b_gk2.reshape(1, GLA_KEY_DIM), gla_norm_w.reshape(1, GLA_DV))


def _mix_kernel(x_ref, o0_ref, l0_ref, o1_ref, l1_ref, o2_ref, l2_ref, ob_ref, pm_ref,
                wa_ref, wb_ref, wo_ref, ln2_ref, wrh_ref, wrl_ref, br_ref,
                h1_ref, hn_ref, ri_ref, rg_ref, cnt_ref,
                po1_ref, pl1_ref, po2_ref, pl2_ref, carry_ref):
    step = pl.program_id(0)
    tm = x_ref.shape[0]

    @pl.when(step == 0)
    def _():
        carry_ref[...] = jnp.zeros_like(carry_ref)

    for o_ref, l_ref, po_ref, pl_ref, (_, d) in ((o1_ref, l1_ref, po1_ref, pl1_ref, DA_GROUPS[1]),
                                                 (o2_ref, l2_ref, po2_ref, pl2_ref, DA_GROUPS[2])):
        n = tm // d
        for r in range(d):
            for h in range(DA_HEADS):
                sl = slice(h * DA_HEAD_DIM, (h + 1) * DA_HEAD_DIM)
                po_ref[h, pl.ds(r, n, stride=d), :] = o_ref[0, r, :, sl].astype(F32)
                pl_ref[h, pl.ds(r, n, stride=d), :] = l_ref[0, r, :, sl]

    heads = []
    for h in range(DA_HEADS):
        sl = slice(h * DA_HEAD_DIM, (h + 1) * DA_HEAD_DIM)
        l0, l1, l2 = l0_ref[:, sl], pl1_ref[h], pl2_ref[h]
        mx = jnp.maximum(jnp.maximum(l0, l1), l2)
        e0, e1, e2 = jnp.exp(l0 - mx), jnp.exp(l1 - mx), jnp.exp(l2 - mx)
        o_h = (e0 * o0_ref[:, sl].astype(F32) + e1 * po1_ref[h] + e2 * po2_ref[h]) / (e0 + e1 + e2)
        heads.append(o_h.astype(BF16))
    o_a = jnp.concatenate(heads, axis=-1)

    gates = _sigmoid(pm_ref[...].astype(F32))
    mixed = (gates[:, :D_MODEL] * _dot(o_a, wa_ref[...])
             + gates[:, D_MODEL:] * _dot(ob_ref[...], wb_ref[...]))
    h1 = x_ref[...] + _dot(mixed.astype(BF16), wo_ref[...])
    h1_ref[...] = h1
    hn = h1 * _rms_scale(h1) * ln2_ref[...]
    hn_ref[...] = hn

    hn_hi = hn.astype(BF16)
    hn_lo = (hn - hn_hi.astype(F32)).astype(BF16)
    logits = (_dot(hn_hi, wrh_ref[...]) + _dot(hn_lo, wrh_ref[...]) + _dot(hn_hi, wrl_ref[...])
              + br_ref[...])
    lane = lax.broadcasted_iota(jnp.int32, (tm, LANES), 1).astype(F32)
    work = jnp.where(lane < N_EXPERTS, logits, -jnp.inf)
    vals, idxs = [], []
    for _ in range(TOP_K):
        m = jnp.max(work, axis=-1, keepdims=True)
        idx = jnp.min(jnp.where(work == m, lane, float(LANES)), axis=-1, keepdims=True)
        vals.append(m)
        idxs.append(idx)
        work = jnp.where(lane == idx, -jnp.inf, work)
    exps = [jnp.exp(v - vals[0]) for v in vals]
    denom = exps[0] + exps[1] + exps[2] + exps[3]

    onehot = jnp.zeros((tm, LANES), F32)
    for idx in idxs:
        onehot = onehot + jnp.where(lane == idx, 1.0, 0.0)
    row = lax.broadcasted_iota(jnp.int32, (tm, tm), 0)
    col = lax.broadcasted_iota(jnp.int32, (tm, tm), 1)
    below = jnp.where(col < row, 1.0, 0.0).astype(BF16)
    before = _dot(below, onehot.astype(BF16)) + carry_ref[0:1, :]
    ranks = [jnp.sum(jnp.where(lane == idx, before, 0.0), axis=-1, keepdims=True) for idx in idxs]
    carry = carry_ref[0:1, :] + jnp.sum(onehot, axis=0, keepdims=True)
    carry_ref[...] = jnp.broadcast_to(carry, carry_ref.shape)
    cnt_ref[...] = jnp.broadcast_to(carry, cnt_ref.shape)

    ri = jnp.zeros((tm, LANES), F32)
    for j, val in enumerate(idxs + ranks):
        ri = jnp.where(lane == float(j), val, ri)
    ri_ref[...] = ri.astype(jnp.int32)
    rg = jnp.zeros((tm, LANES), F32)
    for j, e in enumerate(exps):
        rg = jnp.where(lane == float(j), e / denom, rg)
    rg_ref[...] = rg


def _mix(x2, o0, l0, o1, l1, o2, l2, o_b, pm, wa, wb, wo, ln2_w, wr_hi, wr_lo, br_pad):
    n_tok = x2.shape[0]
    tm = TM_MIX
    bsz = o1.shape[0]
    d1, d2 = DA_GROUPS[1][1], DA_GROUPS[2][1]
    tiles_per_seq = (n_tok // bsz) // tm

    def rows(width):
        return pl.BlockSpec((tm, width), lambda i: (i, 0))

    def residue_major(d):
        return pl.BlockSpec((1, d, tm // d, DA_WIDTH),
                            lambda i: (i // tiles_per_seq, 0, i % tiles_per_seq, 0))

    def whole(arr):
        return pl.BlockSpec(arr.shape, lambda i: (0,) * arr.ndim)

    ln2 = ln2_w.reshape(1, D_MODEL)
    return pl.pallas_call(
        _mix_kernel,
        grid=(n_tok // tm,),
        in_specs=[rows(D_MODEL), rows(DA_WIDTH), rows(DA_WIDTH),
                  residue_major(d1), residue_major(d1), residue_major(d2), residue_major(d2),
                  rows(GLA_VALUE_DIM), rows(MERGE_W),
                  whole(wa), whole(wb), whole(wo), whole(ln2), whole(wr_hi), whole(wr_lo), whole(br_pad)],
        out_specs=(rows(D_MODEL), rows(D_MODEL), rows(LANES), rows(LANES),
                   pl.BlockSpec((8, LANES), lambda i: (0, 0))),
        out_shape=(jax.ShapeDtypeStruct((n_tok, D_MODEL), F32),
                   jax.ShapeDtypeStruct((n_tok, D_MODEL), F32),
                   jax.ShapeDtypeStruct((n_tok, LANES), jnp.int32),
                   jax.ShapeDtypeStruct((n_tok, LANES), F32),
                   jax.ShapeDtypeStruct((8, LANES), F32)),
        scratch_shapes=[pltpu.VMEM((DA_HEADS, tm, DA_HEAD_DIM), F32)] * 4 + [pltpu.VMEM((8, LANES), F32)],
        compiler_params=pltpu.CompilerParams(
            dimension_semantics=("arbitrary",), vmem_limit_bytes=VMEM_LIMIT),
        name="mix",
    )(x2, o0, l0, o1, l1, o2, l2, o_b, pm, wa, wb, wo, ln2, wr_hi, wr_lo, br_pad)


def _dispatch_kernel(zstart_ref, dest_ref, x_ref, xin_ref, zero_ref, sem, zsem):
    tm = x_ref.shape[0]

    @pl.when(pl.program_id(0) == 0)
    def _():
        zero_ref[...] = jnp.zeros_like(zero_ref)
        for j in range(zstart_ref.shape[0]):
            @pl.when(zstart_ref[j] >= 0)
            def _():
                start = pl.multiple_of(zstart_ref[j], TB)
                cp = pltpu.make_async_copy(zero_ref, xin_ref.at[pl.ds(start, TB), :], zsem)
                cp.start()
                cp.wait()

    def issue(t, carry):
        for k in range(TOP_K):
            row = dest_ref[0, 0, t * TOP_K + k]
            pltpu.make_async_copy(x_ref.at[pl.ds(t, 1), :], xin_ref.at[pl.ds(row, 1), :], sem).start()
        return carry

    lax.fori_loop(0, tm, issue, 0)
    for _ in range(TOP_K):
        pltpu.make_async_copy(x_ref, xin_ref.at[pl.ds(0, tm), :], sem).wait()


def _dispatch(zero_start, dest, hn, n_rows):
    n_tok = hn.shape[0]
    tm = TM_ROW
    n_tiles = n_tok // tm
    return pl.pallas_call(
        _dispatch_kernel,
        grid_spec=pltpu.PrefetchScalarGridSpec(
            num_scalar_prefetch=1,
            grid=(n_tiles,),
            in_specs=[
                pl.BlockSpec((1, 1, tm * TOP_K), lambda i, *_: (i, 0, 0), memory_space=pltpu.SMEM),
                pl.BlockSpec((tm, D_MODEL), lambda i, *_: (i, 0)),
            ],
            out_specs=pl.BlockSpec(memory_space=pl.ANY),
            scratch_shapes=[pltpu.VMEM((TB, D_MODEL), F32),
                            pltpu.SemaphoreType.DMA(()), pltpu.SemaphoreType.DMA(())],
        ),
        out_shape=jax.ShapeDtypeStruct((n_rows, D_MODEL), F32),
        compiler_params=pltpu.CompilerParams(
            dimension_semantics=("arbitrary",), vmem_limit_bytes=VMEM_LIMIT),
        name="dispatch",
    )(zero_start, dest.reshape(n_tiles, 1, tm * TOP_K), hn)


def _expert_kernel(be_ref, nused_ref, x_ref, wu_ref, bu_ref, wd_ref, bd_ref, y_ref):
    used = pl.program_id(0) < nused_ref[0]

    @pl.when(jnp.logical_not(used))
    def _():
        y_ref[...] = jnp.zeros_like(y_ref)

    @pl.when(used)
    def _():
        hu = _dot(x_ref[...].astype(BF16), wu_ref[0]) + bu_ref[0]
        x_glu = jnp.minimum(hu[:, :D_FF], SWIGLU_LIMIT)
        x_lin = jnp.clip(hu[:, D_FF:], -SWIGLU_LIMIT, SWIGLU_LIMIT)
        act = x_glu * _sigmoid(SWIGLU_ALPHA * x_glu) * (x_lin + 1.0)
        y_ref[...] = _dot(act.astype(BF16), wd_ref[0]) + bd_ref[0]


def _experts(block_e, n_used, xin, w_up, b_up, w_down, b_down):
    n_rows = xin.shape[0]
    return pl.pallas_call(
        _expert_kernel,
        grid_spec=pltpu.PrefetchScalarGridSpec(
            num_scalar_prefetch=2,
            grid=(n_rows // TB,),
            in_specs=[
                pl.BlockSpec((TB, D_MODEL), lambda i, be, nu: (i, 0)),
                pl.BlockSpec((1, D_MODEL, 2 * D_FF), lambda i, be, nu: (be[i], 0, 0)),
                pl.BlockSpec((1, 1, 2 * D_FF), lambda i, be, nu: (be[i], 0, 0)),
                pl.BlockSpec((1, D_FF, D_MODEL), lambda i, be, nu: (be[i], 0, 0)),
                pl.BlockSpec((1, 1, D_MODEL), lambda i, be, nu: (be[i], 0, 0)),
            ],
            out_specs=pl.BlockSpec((TB, D_MODEL), lambda i, be, nu: (i, 0)),
        ),
        out_shape=jax.ShapeDtypeStruct((n_rows, D_MODEL), F32),
        compiler_params=pltpu.CompilerParams(
            dimension_semantics=("arbitrary",), vmem_limit_bytes=VMEM_LIMIT),
        name="experts",
    )(block_e, n_used, xin, w_up, b_up.reshape(N_EXPERTS, 1, 2 * D_FF),
      w_down, b_down.reshape(N_EXPERTS, 1, D_MODEL))


def _combine_kernel(dest_ref, yb_ref, h1_ref, rg_ref, lnf_ref, o_ref, buf_ref, sem):
    tm = h1_ref.shape[0]

    def issue(t, carry):
        for k in range(TOP_K):
            row = dest_ref[0, 0, t * TOP_K + k]
            pltpu.make_async_copy(yb_ref.at[pl.ds(row, 1), :], buf_ref.at[k, pl.ds(t, 1), :], sem).start()
        return carry

    lax.fori_loop(0, tm, issue, 0)
    for k in range(TOP_K):
        pltpu.make_async_copy(yb_ref.at[pl.ds(0, tm), :], buf_ref.at[k], sem).wait()

    g = rg_ref[...]
    h2 = h1_ref[...]
    for k in range(TOP_K):
        h2 = h2 + g[:, k:k + 1] * buf_ref[k]
    o_ref[...] = h2 * _rms_scale(h2) * lnf_ref[...]


def _combine(dest, yb, h1, rg, lnf_w):
    n_tok = h1.shape[0]
    tm = TM_ROW
    n_tiles = n_tok // tm
    return pl.pallas_call(
        _combine_kernel,
        grid=(n_tiles,),
        in_specs=[
            pl.BlockSpec((1, 1, tm * TOP_K), lambda i: (i, 0, 0), memory_space=pltpu.SMEM),
            pl.BlockSpec(memory_space=pl.ANY),
            pl.BlockSpec((tm, D_MODEL), lambda i: (i, 0)),
            pl.BlockSpec((tm, LANES), lambda i: (i, 0)),
            pl.BlockSpec((1, D_MODEL), lambda i: (0, 0)),
        ],
        out_specs=pl.BlockSpec((tm, D_MODEL), lambda i: (i, 0)),
        out_shape=jax.ShapeDtypeStruct((n_tok, D_MODEL), F32),
        scratch_shapes=[pltpu.VMEM((TOP_K, tm, D_MODEL), F32), pltpu.SemaphoreType.DMA(())],
        compiler_params=pltpu.CompilerParams(
            dimension_semantics=("arbitrary",), vmem_limit_bytes=VMEM_LIMIT),
        name="combine",
    )(dest.reshape(n_tiles, 1, tm * TOP_K), yb, h1, rg, lnf_w.reshape(1, D_MODEL))


def _layer(h, ln1_w, w_in, w_gk2, b_gk2, gla_norm_w, w_proj_a, w_proj_b, w_out,
           ln2_w, w_router, b_router, w_up, b_up, w_down, b_down, lnf_w):
    bsz, seq, _ = h.shape
    n_tok = bsz * seq

    n_main = 3 * QKV_W + 2 * GLA_KEY_DIM + 2 * GLA_VALUE_DIM
    pad = LANES - GLA_GATE_RANK
    w_all = jnp.concatenate(
        [w_in[:, :n_main + GLA_GATE_RANK], jnp.zeros((D_MODEL, pad), F32), w_in[:, n_main + GLA_GATE_RANK:]],
        axis=1).astype(BF16)
    w2_pad = jnp.concatenate([w_gk2, jnp.zeros((pad, GLA_KEY_DIM), F32)], axis=0).astype(BF16)
    wr_pad = jnp.concatenate([w_router, jnp.zeros((D_MODEL, LANES - N_EXPERTS), F32)], axis=1)
    wr_hi = wr_pad.astype(BF16)
    wr_lo = (wr_pad - wr_hi.astype(F32)).astype(BF16)
    br_pad = jnp.concatenate([b_router, jnp.zeros((LANES - N_EXPERTS,), F32)]).reshape(1, LANES)

    pa0, pa1, pa2, pg, pm = _inproj(h, ln1_w, w_all)
    o0, l0 = _dil_attn(pa0.reshape(bsz, 1, seq, QKV_W))
    o1, l1 = _dil_attn(pa1)
    o2, l2 = _dil_attn(pa2)
    o_b = _gla(pg, w2_pad, b_gk2, gla_norm_w)

    h1, hn, ri, rg, cnt = _mix(
        h.reshape(n_tok, D_MODEL), o0.reshape(n_tok, DA_WIDTH), l0.reshape(n_tok, DA_WIDTH),
        o1, l1, o2, l2, o_b.reshape(n_tok, GLA_VALUE_DIM), pm.reshape(n_tok, MERGE_W),
        w_proj_a.astype(BF16), w_proj_b.astype(BF16), w_out.astype(BF16), ln2_w, wr_hi, wr_lo, br_pad)

    counts = cnt[0, :N_EXPERTS].astype(jnp.int32)
    padded = (counts + TB - 1) // TB * TB
    pad_end = jnp.cumsum(padded)
    pad_start = pad_end - padded
    n_asg = n_tok * TOP_K
    n_rows = (n_asg + N_EXPERTS * (TB - 1) + TB - 1) // TB * TB
    n_blocks = n_rows // TB
    dest = pad_start[ri[:, :TOP_K]] + ri[:, TOP_K:2 * TOP_K]
    block_e = jnp.minimum(jnp.searchsorted(pad_end, jnp.arange(n_blocks) * TB, side='right'),
                          N_EXPERTS - 1).astype(jnp.int32)
    n_used = (pad_end[-1:] // TB).astype(jnp.int32)

    tail = n_used[0] + jnp.arange(n_blocks - n_asg // TB)
    zero_start = jnp.concatenate([jnp.where(padded > 0, pad_end - TB, -1),
                                  jnp.where(tail < n_blocks, tail * TB, -1)]).astype(jnp.int32)

    xin = _dispatch(zero_start, dest, hn, n_rows)
    yb = _experts(block_e, n_used, xin, w_up.astype(BF16), b_up, w_down.astype(BF16), b_down)
    out = _combine(dest, yb, h1, rg, lnf_w)
    return out.reshape(bsz, seq, D_MODEL)


def kernel(x, ln1_w, w_in, w_gk2, b_gk2, gla_norm_w, w_proj_a, w_proj_b, w_out, ln2_w, w_router,
           b_router, w_up, b_up, w_down, b_down, lnf_w):
    assert x.shape[-1] == D_MODEL and ln1_w.shape[0] == 1, "one layer of width D_MODEL"
    return _layer(x, ln1_w[0], w_in[0], w_gk2[0], b_gk2[0], gla_norm_w[0], w_proj_a[0], w_proj_b[0],
                  w_out[0], ln2_w[0], w_router[0], b_router[0], w_up[0], b_up[0], w_down[0],
                  b_down[0], lnf_w)
```

```python
import functools

import jax
import jax.numpy as jnp
from jax import lax
from jax.experimental import pallas as pl
from jax.experimental.pallas import tpu as pltpu

F32 = jnp.float32
BF16 = jnp.bfloat16

D_MODEL = 1024
DA_GROUPS = ((128, 1), (512, 4), (2048, 16))
DA_HEADS = 4
DA_HEAD_DIM = 128
DA_WIDTH = DA_HEADS * DA_HEAD_DIM
DA_BLOCK = 128
GLA_HEADS = 4
GLA_KEY_DIM = D_MODEL // 2
GLA_VALUE_DIM = D_MODEL
GLA_DK = GLA_KEY_DIM // GLA_HEADS
GLA_DV = GLA_VALUE_DIM // GLA_HEADS
GLA_GATE_RANK = 16
GLA_GATE_NORMALIZER = 16.0
GLA_CHUNK = 64
N_EXPERTS = 32
TOP_K = 4
D_FF = D_MODEL
SWIGLU_ALPHA = 1.702
SWIGLU_LIMIT = 7.0
RMS_EPS = 1e-5
NEG_INF = -1e30

LANES = 128
QKV_W = 3 * DA_WIDTH
GLA_W = 2 * GLA_KEY_DIM + 2 * GLA_VALUE_DIM + LANES
MERGE_W = 2 * D_MODEL

DA_QB = 2
TM_IN = 256
N_CHUNK = 512
T_GLA = 512
TM_MIX = 256
TB = 512
TM_ROW = TM_MIX
VMEM_LIMIT = 56 * 1024 * 1024

_NT = (((1,), (1,)), ((), ()))
_TN = (((0,), (0,)), ((), ()))


def _dot(a, b):
    return jnp.dot(a, b, preferred_element_type=F32)


def _sigmoid(x):
    return 1.0 / (1.0 + jnp.exp(-x))


def _rms_scale(x):
    return lax.rsqrt(jnp.mean(x * x, axis=-1, keepdims=True) + RMS_EPS)


def _inproj_kernel(x_ref, ln_ref, w_ref, pa0_ref, pa1_ref, pa2_ref, pg_ref, pm_ref, xs_ref, xn_ref):
    tm = x_ref.shape[1]
    n_slab = D_MODEL // LANES

    def project(out_write, col0, width):
        for c0 in range(0, width, N_CHUNK):
            cw = min(N_CHUNK, width - c0)
            out_write(c0, cw, _dot(xs_ref[...], w_ref[:, col0 + c0:col0 + c0 + cw]).astype(BF16))

    x = x_ref[0]
    xn = x * _rms_scale(x) * ln_ref[...]
    xs_ref[...] = xn.astype(BF16)
    for j in range(n_slab):
        xn_ref[j] = xn[:, j * LANES:(j + 1) * LANES]

    def write_to(ref):
        def write(c0, cw, val):
            ref[0, :, c0:c0 + cw] = val
        return write

    project(write_to(pa0_ref), 0, QKV_W)
    project(write_to(pg_ref), 3 * QKV_W, GLA_W)
    project(write_to(pm_ref), 3 * QKV_W + GLA_W, MERGE_W)

    for gi, out_ref in ((1, pa1_ref), (2, pa2_ref)):
        d = DA_GROUPS[gi][1]
        n = tm // d
        for r in range(d):
            for j in range(n_slab):
                xs_ref[r * n:(r + 1) * n, j * LANES:(j + 1) * LANES] = (
                    xn_ref[j, pl.ds(r, n, stride=d), :].astype(BF16))

        def write(c0, cw, val, out_ref=out_ref, d=d, n=n):
            for r in range(d):
                out_ref[0, r, :, c0:c0 + cw] = val[r * n:(r + 1) * n]

        project(write, gi * QKV_W, QKV_W)


def _inproj(x, ln1_w, w_all):
    bsz, seq, _ = x.shape
    tm = TM_IN
    d1, d2 = DA_GROUPS[1][1], DA_GROUPS[2][1]
    grid = (bsz, seq // tm)
    out_shape = (
        jax.ShapeDtypeStruct((bsz, seq, QKV_W), BF16),
        jax.ShapeDtypeStruct((bsz, d1, seq // d1, QKV_W), BF16),
        jax.ShapeDtypeStruct((bsz, d2, seq // d2, QKV_W), BF16),
        jax.ShapeDtypeStruct((bsz, seq, GLA_W), BF16),
        jax.ShapeDtypeStruct((bsz, seq, MERGE_W), BF16),
    )
    return pl.pallas_call(
        _inproj_kernel,
        grid=grid,
        in_specs=[
            pl.BlockSpec((1, tm, D_MODEL), lambda b, i: (b, i, 0)),
            pl.BlockSpec((1, D_MODEL), lambda b, i: (0, 0)),
            pl.BlockSpec(w_all.shape, lambda b, i: (0, 0), pipeline_mode=pl.Buffered(1)),
        ],
        out_specs=(
            pl.BlockSpec((1, tm, QKV_W), lambda b, i: (b, i, 0)),
            pl.BlockSpec((1, d1, tm // d1, QKV_W), lambda b, i: (b, 0, i, 0)),
            pl.BlockSpec((1, d2, tm // d2, QKV_W), lambda b, i: (b, 0, i, 0)),
            pl.BlockSpec((1, tm, GLA_W), lambda b, i: (b, i, 0)),
            pl.BlockSpec((1, tm, MERGE_W), lambda b, i: (b, i, 0)),
        ),
        out_shape=out_shape,
        scratch_shapes=[pltpu.VMEM((tm, D_MODEL), BF16),
                        pltpu.VMEM((D_MODEL // LANES, tm, LANES), F32)],
        compiler_params=pltpu.CompilerParams(
            dimension_semantics=("parallel", "parallel"), vmem_limit_bytes=VMEM_LIMIT),
        name="inproj",
    )(x, ln1_w.reshape(1, D_MODEL), w_all)


def _dil_attn_kernel(q_ref, kp_ref, kc_ref, vp_ref, vc_ref, o_ref, l_ref, s_ref, p_ref, r_ref):
    n = pl.program_id(2)
    blk = DA_BLOCK
    qi = lax.broadcasted_iota(jnp.int32, (blk, 2 * blk), 0)
    kj = lax.broadcasted_iota(jnp.int32, (blk, 2 * blk), 1)
    band = (kj >= qi) & (kj <= qi + blk)
    band_first = (kj >= jnp.where(n > 0, qi, blk)) & (kj <= qi + blk)
    scale = DA_HEAD_DIM ** -0.5
    items = [(b, h) for b in range(DA_QB) for h in range(DA_HEADS)]

    def rows(b):
        return slice(b * blk, (b + 1) * blk)

    def cols(h):
        return slice(h * DA_HEAD_DIM, (h + 1) * DA_HEAD_DIM)

    def window(prev_ref, cur_ref, b, h):
        before = prev_ref[0, 0, :, cols(h)] if b == 0 else cur_ref[0, 0, rows(b - 1), cols(h)]
        return jnp.concatenate([before, cur_ref[0, 0, rows(b), cols(h)]], axis=0)

    for i, (b, h) in enumerate(items):
        s = lax.dot_general(q_ref[0, 0, rows(b), cols(h)], window(kp_ref, kc_ref, b, h), _NT,
                            preferred_element_type=F32) * scale
        s_ref[i] = jnp.where(band_first if b == 0 else band, s, NEG_INF)
    for i, (b, h) in enumerate(items):
        s = s_ref[i]
        m = jnp.max(s, axis=-1, keepdims=True)
        p = jnp.exp(s - m)
        l = jnp.sum(p, axis=-1, keepdims=True)
        p_ref[i] = p.astype(BF16)
        r_ref[i] = jnp.broadcast_to(1.0 / l, (blk, DA_HEAD_DIM))
        l_ref[0, 0, rows(b), cols(h)] = jnp.broadcast_to(m + jnp.log(l), (blk, DA_HEAD_DIM))
    for i, (b, h) in enumerate(items):
        acc = _dot(p_ref[i], window(vp_ref, vc_ref, b, h))
        o_ref[0, 0, rows(b), cols(h)] = (acc * r_ref[i]).astype(o_ref.dtype)


def _dil_attn(pa):
    bsz, d, sub_len, _ = pa.shape
    rows = DA_QB * DA_BLOCK
    n_items = DA_QB * DA_HEADS

    def cur(col):
        return pl.BlockSpec((1, 1, rows, DA_WIDTH), lambda b, r, n: (b, r, n, col))

    def prev(col):
        return pl.BlockSpec((1, 1, DA_BLOCK, DA_WIDTH),
                            lambda b, r, n: (b, r, jnp.maximum(n * DA_QB - 1, 0), col))

    return pl.pallas_call(
        _dil_attn_kernel,
        grid=(bsz, d, sub_len // rows),
        in_specs=[cur(0), prev(1), cur(1), prev(2), cur(2)],
        out_specs=(cur(0), cur(0)),
        out_shape=(jax.ShapeDtypeStruct((bsz, d, sub_len, DA_WIDTH), BF16),
                   jax.ShapeDtypeStruct((bsz, d, sub_len, DA_WIDTH), F32)),
        scratch_shapes=[pltpu.VMEM((n_items, DA_BLOCK, 2 * DA_BLOCK), F32),
                        pltpu.VMEM((n_items, DA_BLOCK, 2 * DA_BLOCK), BF16),
                        pltpu.VMEM((n_items, DA_BLOCK, DA_HEAD_DIM), F32)],
        compiler_params=pltpu.CompilerParams(
            dimension_semantics=("parallel", "parallel", "arbitrary"), vmem_limit_bytes=VMEM_LIMIT),
        name=f"dil_attn_d{d}",
    )(pa, pa, pa, pa, pa)


def _gla_kernel(q_ref, k_ref, v_ref, og_ref, lr_ref, w2_ref, b2_ref, nw_ref, o_ref, st_ref, mask_ref, keep_ref):
    t = pl.program_id(2)
    c = GLA_CHUNK
    tt = q_ref.shape[1]
    n_c = tt // c

    @pl.when(t == 0)
    def _():
        st_ref[...] = jnp.zeros_like(st_ref)
        row = lax.broadcasted_iota(jnp.int32, (tt, tt), 0)
        col = lax.broadcasted_iota(jnp.int32, (tt, tt), 1)
        keep = (col <= row) & (col >= row - row % c)
        keep_ref[...] = jnp.where(keep, 1.0, 0.0)
        mask_ref[...] = jnp.where(keep, 1.0, 0.0).astype(BF16)

    mask = mask_ref[...]
    gpre = _dot(lr_ref[0], w2_ref[...]) + b2_ref[...]
    forget = (jnp.minimum(gpre, 0.0) - jnp.log(1.0 + jnp.exp(-jnp.abs(gpre)))) / GLA_GATE_NORMALIZER
    g_hi = forget.astype(BF16)
    g_lo = (forget - g_hi.astype(F32)).astype(BF16)
    csum = _dot(mask, jnp.concatenate([g_hi, g_lo], axis=-1))
    b = csum[:, :GLA_DK] + csum[:, GLA_DK:]
    b_last = b.reshape(n_c, c, GLA_DK)[:, c - 1:c, :]
    b_to_end = (b_last - b.reshape(n_c, c, GLA_DK)).reshape(tt, GLA_DK)
    q = q_ref[0].astype(F32)
    k = k_ref[0].astype(F32)
    v = v_ref[0]
    q_e = (q * ((GLA_DK ** -0.5) * jnp.exp(b))).astype(BF16)
    k_e = (k * jnp.exp(-b)).astype(BF16)
    k_end = (k * jnp.exp(b_to_end)).astype(BF16)
    att = lax.dot_general(q_e, k_e, _NT, preferred_element_type=F32)
    att = jnp.where(keep_ref[...] > 0.0, att, 0.0).astype(BF16)
    o_intra = _dot(att, v)
    decay = jnp.exp(b_last)
    st = st_ref[...]
    outs = []
    for ci in range(n_c):
        rs = slice(ci * c, (ci + 1) * c)
        outs.append(o_intra[rs] + lax.dot_general(q_e[rs], st.astype(BF16), _NT, preferred_element_type=F32))
        st = decay[ci] * st + lax.dot_general(v[rs], k_end[rs], _TN, preferred_element_type=F32)
    st_ref[...] = st
    o = jnp.concatenate(outs, axis=0)
    o = o * _rms_scale(o) * nw_ref[...]
    gate = og_ref[0].astype(F32)
    o_ref[0] = (o * (gate * _sigmoid(gate))).astype(BF16)


def _gla(pg, w2_pad, b_gk2, gla_norm_w):
    bsz, seq, _ = pg.shape
    t = T_GLA
    kq = GLA_KEY_DIM // GLA_DK
    kv = 2 * GLA_KEY_DIM // GLA_DV
    kg = kv + GLA_VALUE_DIM // GLA_DV
    klr = (2 * GLA_KEY_DIM + 2 * GLA_VALUE_DIM) // LANES
    return pl.pallas_call(
        _gla_kernel,
        grid=(bsz, GLA_HEADS, seq // t),
        in_specs=[
            pl.BlockSpec((1, t, GLA_DK), lambda b, h, i: (b, i, h)),
            pl.BlockSpec((1, t, GLA_DK), lambda b, h, i: (b, i, kq + h)),
            pl.BlockSpec((1, t, GLA_DV), lambda b, h, i: (b, i, kv + h)),
            pl.BlockSpec((1, t, GLA_DV), lambda b, h, i: (b, i, kg + h)),
            pl.BlockSpec((1, t, LANES), lambda b, h, i: (b, i, klr)),
            pl.BlockSpec((LANES, GLA_DK), lambda b, h, i: (0, h)),
            pl.BlockSpec((1, GLA_DK), lambda b, h, i: (0, h)),
            pl.BlockSpec((1, GLA_DV), lambda b, h, i: (0, 0)),
        ],
        out_specs=pl.BlockSpec((1, t, GLA_DV), lambda b, h, i: (b, i, h)),
        out_shape=jax.ShapeDtypeStruct((bsz, seq, GLA_VALUE_DIM), BF16),
        scratch_shapes=[pltpu.VMEM((GLA_DV, GLA_DK), F32), pltpu.VMEM((t, t), BF16),
                        pltpu.VMEM((t, t), F32)],
        compiler_params=pltpu.CompilerParams(
            dimension_semantics=("parallel", "parallel", "arbitrary"), vmem_limit_bytes=VMEM_LIMIT),
        name="gla",
    )(pg, pg, pg, pg, pg, w2_pad, b_gk2.reshape(1, GLA_KEY_DIM), gla_norm_w.reshape(1, GLA_DV))


def _mix_kernel(x_ref, o0_ref, l0_ref, o1_ref, l1_ref, o2_ref, l2_ref, ob_ref, pm_ref,
                wa_ref, wb_ref, wo_ref, ln2_ref, wrh_ref, wrl_ref, br_ref,
                h1_ref, hn_ref, ri_ref, rg_ref, cnt_ref,
                po1_ref, pl1_ref, po2_ref, pl2_ref, carry_ref):
    step = pl.program_id(0)
    tm = x_ref.shape[0]

    @pl.when(step == 0)
    def _():
        carry_ref[...] = jnp.zeros_like(carry_ref)

    for o_ref, l_ref, po_ref, pl_ref, (_, d) in ((o1_ref, l1_ref, po1_ref, pl1_ref, DA_GROUPS[1]),
                                                 (o2_ref, l2_ref, po2_ref, pl2_ref, DA_GROUPS[2])):
        n = tm // d
        for r in range(d):
            for h in range(DA_HEADS):
                sl = slice(h * DA_HEAD_DIM, (h + 1) * DA_HEAD_DIM)
                po_ref[h, pl.ds(r, n, stride=d), :] = o_ref[0, r, :, sl].astype(F32)
                pl_ref[h, pl.ds(r, n, stride=d), :] = l_ref[0, r, :, sl]

    heads = []
    for h in range(DA_HEADS):
        sl = slice(h * DA_HEAD_DIM, (h + 1) * DA_HEAD_DIM)
        l0, l1, l2 = l0_ref[:, sl], pl1_ref[h], pl2_ref[h]
        mx = jnp.maximum(jnp.maximum(l0, l1), l2)
        e0, e1, e2 = jnp.exp(l0 - mx), jnp.exp(l1 - mx), jnp.exp(l2 - mx)
        o_h = (e0 * o0_ref[:, sl].astype(F32) + e1 * po1_ref[h] + e2 * po2_ref[h]) / (e0 + e1 + e2)
        heads.append(o_h.astype(BF16))
    o_a = jnp.concatenate(heads, axis=-1)

    gates = _sigmoid(pm_ref[...].astype(F32))
    mixed = (gates[:, :D_MODEL] * _dot(o_a, wa_ref[...])
             + gates[:, D_MODEL:] * _dot(ob_ref[...], wb_ref[...]))
    h1 = x_ref[...] + _dot(mixed.astype(BF16), wo_ref[...])
    h1_ref[...] = h1
    hn = h1 * _rms_scale(h1) * ln2_ref[...]
    hn_ref[...] = hn

    hn_hi = hn.astype(BF16)
    hn_lo = (hn - hn_hi.astype(F32)).astype(BF16)
    logits = (_dot(hn_hi, wrh_ref[...]) + _dot(hn_lo, wrh_ref[...]) + _dot(hn_hi, wrl_ref[...])
              + br_ref[...])
    lane = lax.broadcasted_iota(jnp.int32, (tm, LANES), 1).astype(F32)
    work = jnp.where(lane < N_EXPERTS, logits, -jnp.inf)
    vals, idxs = [], []
    for _ in range(TOP_K):
        m = jnp.max(work, axis=-1, keepdims=True)
        idx = jnp.min(jnp.where(work == m, lane, float(LANES)), axis=-1, keepdims=True)
        vals.append(m)
        idxs.append(idx)
        work = jnp.where(lane == idx, -jnp.inf, work)
    exps = [jnp.exp(v - vals[0]) for v in vals]
    denom = exps[0] + exps[1] + exps[2] + exps[3]

    onehot = jnp.zeros((tm, LANES), F32)
    for idx in idxs:
        onehot = onehot + jnp.where(lane == idx, 1.0, 0.0)
    row = lax.broadcasted_iota(jnp.int32, (tm, tm), 0)
    col = lax.broadcasted_iota(jnp.int32, (tm, tm), 1)
    below = jnp.where(col < row, 1.0, 0.0).astype(BF16)
    before = _dot(below, onehot.astype(BF16)) + carry_ref[0:1, :]
    ranks = [jnp.sum(jnp.where(lane == idx, before, 0.0), axis=-1, keepdims=True) for idx in idxs]
    carry = carry_ref[0:1, :] + jnp.sum(onehot, axis=0, keepdims=True)
    carry_ref[...] = jnp.broadcast_to(carry, carry_ref.shape)
    cnt_ref[...] = jnp.broadcast_to(carry, cnt_ref.shape)

    ri = jnp.zeros((tm, LANES), F32)
    for j, val in enumerate(idxs + ranks):
        ri = jnp.where(lane == float(j), val, ri)
    ri_ref[0] = ri.T[:2 * TOP_K].astype(jnp.int32)
    rg = jnp.zeros((tm, LANES), F32)
    for j, e in enumerate(exps):
        rg = jnp.where(lane == float(j), e / denom, rg)
    rg_ref[...] = rg


def _mix(x2, o0, l0, o1, l1, o2, l2, o_b, pm, wa, wb, wo, ln2_w, wr_hi, wr_lo, br_pad):
    n_tok = x2.shape[0]
    tm = TM_MIX
    bsz = o1.shape[0]
    d1, d2 = DA_GROUPS[1][1], DA_GROUPS[2][1]
    tiles_per_seq = (n_tok // bsz) // tm

    def rows(width):
        return pl.BlockSpec((tm, width), lambda i: (i, 0))

    def residue_major(d):
        return pl.BlockSpec((1, d, tm // d, DA_WIDTH),
                            lambda i: (i // tiles_per_seq, 0, i % tiles_per_seq, 0))

    def whole(arr):
        return pl.BlockSpec(arr.shape, lambda i: (0,) * arr.ndim)

    ln2 = ln2_w.reshape(1, D_MODEL)
    return pl.pallas_call(
        _mix_kernel,
        grid=(n_tok // tm,),
        in_specs=[rows(D_MODEL), rows(DA_WIDTH), rows(DA_WIDTH),
                  residue_major(d1), residue_major(d1), residue_major(d2), residue_major(d2),
                  rows(GLA_VALUE_DIM), rows(MERGE_W),
                  whole(wa), whole(wb), whole(wo), whole(ln2), whole(wr_hi), whole(wr_lo), whole(br_pad)],
        out_specs=(rows(D_MODEL), rows(D_MODEL),
                   pl.BlockSpec((1, 2 * TOP_K, tm), lambda i: (i, 0, 0)), rows(LANES),
                   pl.BlockSpec((8, LANES), lambda i: (0, 0))),
        out_shape=(jax.ShapeDtypeStruct((n_tok, D_MODEL), F32),
                   jax.ShapeDtypeStruct((n_tok, D_MODEL), F32),
                   jax.ShapeDtypeStruct((n_tok // tm, 2 * TOP_K, tm), jnp.int32),
                   jax.ShapeDtypeStruct((n_tok, LANES), F32),
                   jax.ShapeDtypeStruct((8, LANES), F32)),
        scratch_shapes=[pltpu.VMEM((DA_HEADS, tm, DA_HEAD_DIM), F32)] * 4 + [pltpu.VMEM((8, LANES), F32)],
        compiler_params=pltpu.CompilerParams(
            dimension_semantics=("arbitrary",), vmem_limit_bytes=VMEM_LIMIT),
        name="mix",
    )(x2, o0, l0, o1, l1, o2, l2, o_b, pm, wa, wb, wo, ln2, wr_hi, wr_lo, br_pad)


def _dispatch_kernel(zstart_ref, dest_ref, x_ref, xin_ref, zero_ref, sem, zsem):
    tm = x_ref.shape[0]

    @pl.when(pl.program_id(0) == 0)
    def _():
        zero_ref[...] = jnp.zeros_like(zero_ref)
        for j in range(zstart_ref.shape[0]):
            @pl.when(zstart_ref[j] >= 0)
            def _():
                start = pl.multiple_of(zstart_ref[j], TB)
                cp = pltpu.make_async_copy(zero_ref, xin_ref.at[pl.ds(start, TB), :], zsem)
                cp.start()
                cp.wait()

    def issue(t, carry):
        for k in range(TOP_K):
            row = dest_ref[0, 0, k * tm + t]
            pltpu.make_async_copy(x_ref.at[pl.ds(t, 1), :], xin_ref.at[pl.ds(row, 1), :], sem).start()
        return carry

    lax.fori_loop(0, tm, issue, 0)
    for _ in range(TOP_K):
        pltpu.make_async_copy(x_ref, xin_ref.at[pl.ds(0, tm), :], sem).wait()


def _dispatch(zero_start, dest, hn, n_rows):
    n_tok = hn.shape[0]
    tm = TM_ROW
    n_tiles = n_tok // tm
    return pl.pallas_call(
        _dispatch_kernel,
        grid_spec=pltpu.PrefetchScalarGridSpec(
            num_scalar_prefetch=1,
            grid=(n_tiles,),
            in_specs=[
                pl.BlockSpec((1, 1, tm * TOP_K), lambda i, *_: (i, 0, 0), memory_space=pltpu.SMEM),
                pl.BlockSpec((tm, D_MODEL), lambda i, *_: (i, 0)),
            ],
            out_specs=pl.BlockSpec(memory_space=pl.ANY),
            scratch_shapes=[pltpu.VMEM((TB, D_MODEL), F32),
                            pltpu.SemaphoreType.DMA(()), pltpu.SemaphoreType.DMA(())],
        ),
        out_shape=jax.ShapeDtypeStruct((n_rows, D_MODEL), F32),
        compiler_params=pltpu.CompilerParams(
            dimension_semantics=("arbitrary",), vmem_limit_bytes=VMEM_LIMIT),
        name="dispatch",
    )(zero_start, dest.reshape(n_tiles, 1, tm * TOP_K), hn)


def _expert_kernel(be_ref, nused_ref, x_ref, wu_ref, bu_ref, wd_ref, bd_ref, y_ref, wu16_ref, wd16_ref):
    i = pl.program_id(0)
    used = i < nused_ref[0]

    @pl.when((i == 0) | (be_ref[i] != be_ref[jnp.maximum(i - 1, 0)]))
    def _():
        wu16_ref[...] = wu_ref[0].astype(BF16)
        wd16_ref[...] = wd_ref[0].astype(BF16)

    @pl.when(jnp.logical_not(used))
    def _():
        y_ref[...] = jnp.zeros_like(y_ref)

    @pl.when(used)
    def _():
        hu = _dot(x_ref[...].astype(BF16), wu16_ref[...]) + bu_ref[0]
        x_glu = jnp.minimum(hu[:, :D_FF], SWIGLU_LIMIT)
        x_lin = jnp.clip(hu[:, D_FF:], -SWIGLU_LIMIT, SWIGLU_LIMIT)
        act = x_glu * _sigmoid(SWIGLU_ALPHA * x_glu) * (x_lin + 1.0)
        y_ref[...] = _dot(act.astype(BF16), wd16_ref[...]) + bd_ref[0]


def _experts(block_e, n_used, xin, w_up, b_up, w_down, b_down):
    n_rows = xin.shape[0]
    return pl.pallas_call(
        _expert_kernel,
        grid_spec=pltpu.PrefetchScalarGridSpec(
            num_scalar_prefetch=2,
            grid=(n_rows // TB,),
            in_specs=[
                pl.BlockSpec((TB, D_MODEL), lambda i, be, nu: (i, 0)),
                pl.BlockSpec((1, D_MODEL, 2 * D_FF), lambda i, be, nu: (be[i], 0, 0)),
                pl.BlockSpec((1, 1, 2 * D_FF), lambda i, be, nu: (be[i], 0, 0)),
                pl.BlockSpec((1, D_FF, D_MODEL), lambda i, be, nu: (be[i], 0, 0)),
                pl.BlockSpec((1, 1, D_MODEL), lambda i, be, nu: (be[i], 0, 0)),
            ],
            out_specs=pl.BlockSpec((TB, D_MODEL), lambda i, be, nu: (i, 0)),
            scratch_shapes=[pltpu.VMEM((D_MODEL, 2 * D_FF), BF16), pltpu.VMEM((D_FF, D_MODEL), BF16)],
        ),
        out_shape=jax.ShapeDtypeStruct((n_rows, D_MODEL), F32),
        compiler_params=pltpu.CompilerParams(
            dimension_semantics=("arbitrary",), vmem_limit_bytes=VMEM_LIMIT),
        name="experts",
    )(block_e, n_used, xin, w_up, b_up.reshape(N_EXPERTS, 1, 2 * D_FF),
      w_down, b_down.reshape(N_EXPERTS, 1, D_MODEL))


def _combine_kernel(dest_ref, yb_ref, h1_ref, rg_ref, lnf_ref, o_ref, buf_ref, sem):
    tm = h1_ref.shape[0]

    def issue(t, carry):
        for k in range(TOP_K):
            row = dest_ref[0, 0, k * tm + t]
            pltpu.make_async_copy(yb_ref.at[pl.ds(row, 1), :], buf_ref.at[k, pl.ds(t, 1), :], sem).start()
        return carry

    lax.fori_loop(0, tm, issue, 0)
    for k in range(TOP_K):
        pltpu.make_async_copy(yb_ref.at[pl.ds(0, tm), :], buf_ref.at[k], sem).wait()

    g = rg_ref[...]
    h2 = h1_ref[...]
    for k in range(TOP_K):
        h2 = h2 + g[:, k:k + 1] * buf_ref[k]
    o_ref[...] = h2 * _rms_scale(h2) * lnf_ref[...]


def _combine(dest, yb, h1, rg, lnf_w):
    n_tok = h1.shape[0]
    tm = TM_ROW
    n_tiles = n_tok // tm
    return pl.pallas_call(
        _combine_kernel,
        grid=(n_tiles,),
        in_specs=[
            pl.BlockSpec((1, 1, tm * TOP_K), lambda i: (i, 0, 0), memory_space=pltpu.SMEM),
            pl.BlockSpec(memory_space=pl.ANY),
            pl.BlockSpec((tm, D_MODEL), lambda i: (i, 0)),
            pl.BlockSpec((tm, LANES), lambda i: (i, 0)),
            pl.BlockSpec((1, D_MODEL), lambda i: (0, 0)),
        ],
        out_specs=pl.BlockSpec((tm, D_MODEL), lambda i: (i, 0)),
        out_shape=jax.ShapeDtypeStruct((n_tok, D_MODEL), F32),
        scratch_shapes=[pltpu.VMEM((TOP_K, tm, D_MODEL), F32), pltpu.SemaphoreType.DMA(())],
        compiler_params=pltpu.CompilerParams(
            dimension_semantics=("arbitrary",), vmem_limit_bytes=VMEM_LIMIT),
        name="combine",
    )(dest.reshape(n_tiles, 1, tm * TOP_K), yb, h1, rg, lnf_w.reshape(1, D_MODEL))


def _layer(h, ln1_w, w_in, w_gk2, b_gk2, gla_norm_w, w_proj_a, w_proj_b, w_out,
           ln2_w, w_router, b_router, w_up, b_up, w_down, b_down, lnf_w):
    bsz, seq, _ = h.shape
    n_tok = bsz * seq

    n_main = 3 * QKV_W + 2 * GLA_KEY_DIM + 2 * GLA_VALUE_DIM
    pad = LANES - GLA_GATE_RANK
    w_all = jnp.concatenate(
        [w_in[:, :n_main + GLA_GATE_RANK], jnp.zeros((D_MODEL, pad), F32), w_in[:, n_main + GLA_GATE_RANK:]],
        axis=1).astype(BF16)
    w2_pad = jnp.concatenate([w_gk2, jnp.zeros((pad, GLA_KEY_DIM), F32)], axis=0).astype(BF16)
    wr_pad = jnp.concatenate([w_router, jnp.zeros((D_MODEL, LANES - N_EXPERTS), F32)], axis=1)
    wr_hi = wr_pad.astype(BF16)
    wr_lo = (wr_pad - wr_hi.astype(F32)).astype(BF16)
    br_pad = jnp.concatenate([b_router, jnp.zeros((LANES - N_EXPERTS,), F32)]).reshape(1, LANES)

    pa0, pa1, pa2, pg, pm = _inproj(h, ln1_w, w_all)
    o0, l0 = _dil_attn(pa0.reshape(bsz, 1, seq, QKV_W))
    o1, l1 = _dil_attn(pa1)
    o2, l2 = _dil_attn(pa2)
    o_b = _gla(pg, w2_pad, b_gk2, gla_norm_w)

    h1, hn, ri, rg, cnt = _mix(
        h.reshape(n_tok, D_MODEL), o0.reshape(n_tok, DA_WIDTH), l0.reshape(n_tok, DA_WIDTH),
        o1, l1, o2, l2, o_b.reshape(n_tok, GLA_VALUE_DIM), pm.reshape(n_tok, MERGE_W),
        w_proj_a.astype(BF16), w_proj_b.astype(BF16), w_out.astype(BF16), ln2_w, wr_hi, wr_lo, br_pad)

    counts = cnt[0, :N_EXPERTS].astype(jnp.int32)
    padded = (counts + TB - 1) // TB * TB
    pad_end = jnp.cumsum(padded)
    pad_start = pad_end - padded
    n_asg = n_tok * TOP_K
    n_rows = (n_asg + N_EXPERTS * (TB - 1) + TB - 1) // TB * TB
    n_blocks = n_rows // TB
    experts_of = ri[:, :TOP_K, :, None] == jnp.arange(N_EXPERTS)
    dest = jnp.sum(jnp.where(experts_of, pad_start, 0), axis=-1) + ri[:, TOP_K:, :]
    block_starts = jnp.arange(n_blocks) * TB
    block_e = jnp.minimum(jnp.sum(pad_end[None, :] <= block_starts[:, None], axis=1),
                          N_EXPERTS - 1).astype(jnp.int32)
    n_used = (pad_end[-1:] // TB).astype(jnp.int32)

    tail = n_used[0] + jnp.arange(n_blocks - n_asg // TB)
    zero_start = jnp.concatenate([jnp.where(padded > 0, pad_end - TB, -1),
                                  jnp.where(tail < n_blocks, tail * TB, -1)]).astype(jnp.int32)

    xin = _dispatch(zero_start, dest, hn, n_rows)
    yb = _experts(block_e, n_used, xin, w_up, b_up, w_down, b_down)
    out = _combine(dest, yb, h1, rg, lnf_w)
    return out.reshape(bsz, seq, D_MODEL)


def kernel(x, ln1_w, w_in, w_gk2, b_gk2, gla_norm_w, w_proj_a, w_proj_b, w_out, ln2_w, w_router,
           b_router, w_up, b_up, w_down, b_down, lnf_w):
    assert x.shape[-1] == D_MODEL and ln1_w.shape[0] == 1, "one layer of width D_MODEL"
    return _layer(x, ln1_w[0], w_in[0], w_gk2[0], b_gk2[0], gla_norm_w[0], w_proj_a[0], w_proj_b[0],
                  w_out[0], ln2_w[0], w_router[0], b_router[0], w_up[0], b_up[0], w_down[0],
                  b_down[0], lnf_w)
```

```python
import functools

import jax
import jax.numpy as jnp
from jax import lax
from jax.experimental import pallas as pl
from jax.experimental.pallas import tpu as pltpu

F32 = jnp.float32
BF16 = jnp.bfloat16
U32 = jnp.uint32

D_MODEL = 1024
DA_GROUPS = ((128, 1), (512, 4), (2048, 16))
DA_HEADS = 4
DA_HEAD_DIM = 128
DA_WIDTH = DA_HEADS * DA_HEAD_DIM
DA_BLOCK = 128
GLA_HEADS = 4
GLA_KEY_DIM = D_MODEL // 2
GLA_VALUE_DIM = D_MODEL
GLA_DK = GLA_KEY_DIM // GLA_HEADS
GLA_DV = GLA_VALUE_DIM // GLA_HEADS
GLA_GATE_RANK = 16
GLA_GATE_NORMALIZER = 16.0
GLA_CHUNK = 64
N_EXPERTS = 32
TOP_K = 4
D_FF = D_MODEL
SWIGLU_ALPHA = 1.702
SWIGLU_LIMIT = 7.0
RMS_EPS = 1e-5
NEG_INF = -1e30

LANES = 128
QKV_W = 3 * DA_WIDTH
GLA_W = 2 * GLA_KEY_DIM + 2 * GLA_VALUE_DIM + LANES
MERGE_W = 2 * D_MODEL

DA_QB = 2
TM_IN = 256
N_CHUNK = 512
T_GLA = 512
TM_MIX = 256
TB = 512
TM_ROW = TM_MIX
ROW_ALIGN = 8
WIN = 32
N_SLOT = N_EXPERTS + TM_ROW * TOP_K // WIN
ZCHUNK = 256
ROW_W = D_MODEL // 2 + LANES
VMEM_LIMIT = 56 * 1024 * 1024

_NT = (((1,), (1,)), ((), ()))
_TN = (((0,), (0,)), ((), ()))


def _dot(a, b):
    return jnp.dot(a, b, preferred_element_type=F32)


def _sigmoid(x):
    return 1.0 / (1.0 + jnp.exp(-x))


def _rms_scale(x):
    return lax.rsqrt(jnp.mean(x * x, axis=-1, keepdims=True) + RMS_EPS)


def _inproj_kernel(x_ref, ln_ref, w_ref, pa0_ref, pa1_ref, pa2_ref, pg_ref, pm_ref, xs_ref, xn_ref):
    tm = x_ref.shape[1]
    n_slab = D_MODEL // LANES

    def project(out_write, col0, width):
        for c0 in range(0, width, N_CHUNK):
            cw = min(N_CHUNK, width - c0)
            out_write(c0, cw, _dot(xs_ref[...], w_ref[:, col0 + c0:col0 + c0 + cw]).astype(BF16))

    x = x_ref[0]
    xn = x * _rms_scale(x) * ln_ref[...]
    xs_ref[...] = xn.astype(BF16)
    for j in range(n_slab):
        xn_ref[j] = xn[:, j * LANES:(j + 1) * LANES]

    def write_to(ref):
        def write(c0, cw, val):
            ref[0, :, c0:c0 + cw] = val
        return write

    project(write_to(pa0_ref), 0, QKV_W)
    project(write_to(pg_ref), 3 * QKV_W, GLA_W)
    project(write_to(pm_ref), 3 * QKV_W + GLA_W, MERGE_W)

    for gi, out_ref in ((1, pa1_ref), (2, pa2_ref)):
        d = DA_GROUPS[gi][1]
        n = tm // d
        for r in range(d):
            for j in range(n_slab):
                xs_ref[r * n:(r + 1) * n, j * LANES:(j + 1) * LANES] = (
                    xn_ref[j, pl.ds(r, n, stride=d), :].astype(BF16))

        def write(c0, cw, val, out_ref=out_ref, d=d, n=n):
            for r in range(d):
                out_ref[0, r, :, c0:c0 + cw] = val[r * n:(r + 1) * n]

        project(write, gi * QKV_W, QKV_W)


def _inproj(x, ln1_w, w_all):
    bsz, seq, _ = x.shape
    tm = TM_IN
    d1, d2 = DA_GROUPS[1][1], DA_GROUPS[2][1]
    grid = (bsz, seq // tm)
    out_shape = (
        jax.ShapeDtypeStruct((bsz, seq, QKV_W), BF16),
        jax.ShapeDtypeStruct((bsz, d1, seq // d1, QKV_W), BF16),
        jax.ShapeDtypeStruct((bsz, d2, seq // d2, QKV_W), BF16),
        jax.ShapeDtypeStruct((bsz, seq, GLA_W), BF16),
        jax.ShapeDtypeStruct((bsz, seq, MERGE_W), BF16),
    )
    return pl.pallas_call(
        _inproj_kernel,
        grid=grid,
        in_specs=[
            pl.BlockSpec((1, tm, D_MODEL), lambda b, i: (b, i, 0)),
            pl.BlockSpec((1, D_MODEL), lambda b, i: (0, 0)),
            pl.BlockSpec(w_all.shape, lambda b, i: (0, 0), pipeline_mode=pl.Buffered(1)),
        ],
        out_specs=(
            pl.BlockSpec((1, tm, QKV_W), lambda b, i: (b, i, 0)),
            pl.BlockSpec((1, d1, tm // d1, QKV_W), lambda b, i: (b, 0, i, 0)),
            pl.BlockSpec((1, d2, tm // d2, QKV_W), lambda b, i: (b, 0, i, 0)),
            pl.BlockSpec((1, tm, GLA_W), lambda b, i: (b, i, 0)),
            pl.BlockSpec((1, tm, MERGE_W), lambda b, i: (b, i, 0)),
        ),
        out_shape=out_shape,
        scratch_shapes=[pltpu.VMEM((tm, D_MODEL), BF16),
                        pltpu.VMEM((D_MODEL // LANES, tm, LANES), F32)],
        compiler_params=pltpu.CompilerParams(
            dimension_semantics=("parallel", "parallel"), vmem_limit_bytes=VMEM_LIMIT),
        name="inproj",
    )(x, ln1_w.reshape(1, D_MODEL), w_all)


def _dil_attn_kernel(q_ref, kp_ref, kc_ref, vp_ref, vc_ref, o_ref, l_ref, s_ref, p_ref, r_ref):
    n = pl.program_id(2)
    blk = DA_BLOCK
    qi = lax.broadcasted_iota(jnp.int32, (blk, 2 * blk), 0)
    kj = lax.broadcasted_iota(jnp.int32, (blk, 2 * blk), 1)
    band = (kj >= qi) & (kj <= qi + blk)
    band_first = (kj >= jnp.where(n > 0, qi, blk)) & (kj <= qi + blk)
    scale = DA_HEAD_DIM ** -0.5
    items = [(b, h) for b in range(DA_QB) for h in range(DA_HEADS)]

    def rows(b):
        return slice(b * blk, (b + 1) * blk)

    def cols(h):
        return slice(h * DA_HEAD_DIM, (h + 1) * DA_HEAD_DIM)

    def window(prev_ref, cur_ref, b, h):
        before = prev_ref[0, 0, :, cols(h)] if b == 0 else cur_ref[0, 0, rows(b - 1), cols(h)]
        return jnp.concatenate([before, cur_ref[0, 0, rows(b), cols(h)]], axis=0)

    for i, (b, h) in enumerate(items):
        s = lax.dot_general(q_ref[0, 0, rows(b), cols(h)], window(kp_ref, kc_ref, b, h), _NT,
                            preferred_element_type=F32) * scale
        s_ref[i] = jnp.where(band_first if b == 0 else band, s, NEG_INF)
    for i, (b, h) in enumerate(items):
        s = s_ref[i]
        m = jnp.max(s, axis=-1, keepdims=True)
        p = jnp.exp(s - m)
        l = jnp.sum(p, axis=-1, keepdims=True)
        p_ref[i] = p.astype(BF16)
        r_ref[i] = jnp.broadcast_to(1.0 / l, (blk, DA_HEAD_DIM))
        l_ref[0, 0, rows(b), cols(h)] = jnp.broadcast_to(m + jnp.log(l), (blk, DA_HEAD_DIM))
    for i, (b, h) in enumerate(items):
        acc = _dot(p_ref[i], window(vp_ref, vc_ref, b, h))
        o_ref[0, 0, rows(b), cols(h)] = (acc * r_ref[i]).astype(o_ref.dtype)


def _dil_attn(pa):
    bsz, d, sub_len, _ = pa.shape
    rows = DA_QB * DA_BLOCK
    n_items = DA_QB * DA_HEADS

    def cur(col):
        return pl.BlockSpec((1, 1, rows, DA_WIDTH), lambda b, r, n: (b, r, n, col))

    def prev(col):
        return pl.BlockSpec((1, 1, DA_BLOCK, DA_WIDTH),
                            lambda b, r, n: (b, r, jnp.maximum(n * DA_QB - 1, 0), col))

    return pl.pallas_call(
        _dil_attn_kernel,
        grid=(bsz, d, sub_len // rows),
        in_specs=[cur(0), prev(1), cur(1), prev(2), cur(2)],
        out_specs=(cur(0), cur(0)),
        out_shape=(jax.ShapeDtypeStruct((bsz, d, sub_len, DA_WIDTH), BF16),
                   jax.ShapeDtypeStruct((bsz, d, sub_len, DA_WIDTH), F32)),
        scratch_shapes=[pltpu.VMEM((n_items, DA_BLOCK, 2 * DA_BLOCK), F32),
                        pltpu.VMEM((n_items, DA_BLOCK, 2 * DA_BLOCK), BF16),
                        pltpu.VMEM((n_items, DA_BLOCK, DA_HEAD_DIM), F32)],
        compiler_params=pltpu.CompilerParams(
            dimension_semantics=("parallel", "parallel", "arbitrary"), vmem_limit_bytes=VMEM_LIMIT),
        name=f"dil_attn_d{d}",
    )(pa, pa, pa, pa, pa)


def _gla_kernel(q_ref, k_ref, v_ref, og_ref, lr_ref, w2_ref, b2_ref, nw_ref, o_ref, st_ref, mask_ref, keep_ref):
    t = pl.program_id(2)
    c = GLA_CHUNK
    tt = q_ref.shape[1]
    n_c = tt // c

    @pl.when(t == 0)
    def _():
        st_ref[...] = jnp.zeros_like(st_ref)
        row = lax.broadcasted_iota(jnp.int32, (tt, tt), 0)
        col = lax.broadcasted_iota(jnp.int32, (tt, tt), 1)
        keep = (col <= row) & (col >= row - row % c)
        keep_ref[...] = jnp.where(keep, 1.0, 0.0)
        mask_ref[...] = jnp.where(keep, 1.0, 0.0).astype(BF16)

    mask = mask_ref[...]
    gpre = _dot(lr_ref[0], w2_ref[...]) + b2_ref[...]
    forget = (jnp.minimum(gpre, 0.0) - jnp.log(1.0 + jnp.exp(-jnp.abs(gpre)))) / GLA_GATE_NORMALIZER
    g_hi = forget.astype(BF16)
    g_lo = (forget - g_hi.astype(F32)).astype(BF16)
    csum = _dot(mask, jnp.concatenate([g_hi, g_lo], axis=-1))
    b = csum[:, :GLA_DK] + csum[:, GLA_DK:]
    b_last = b.reshape(n_c, c, GLA_DK)[:, c - 1:c, :]
    b_to_end = (b_last - b.reshape(n_c, c, GLA_DK)).reshape(tt, GLA_DK)
    q = q_ref[0].astype(F32)
    k = k_ref[0].astype(F32)
    v = v_ref[0]
    q_e = (q * ((GLA_DK ** -0.5) * jnp.exp(b))).astype(BF16)
    k_e = (k * jnp.exp(-b)).astype(BF16)
    k_end = (k * jnp.exp(b_to_end)).astype(BF16)
    att = lax.dot_general(q_e, k_e, _NT, preferred_element_type=F32)
    att = jnp.where(keep_ref[...] > 0.0, att, 0.0).astype(BF16)
    o_intra = _dot(att, v)
    decay = jnp.exp(b_last)
    st = st_ref[...]
    outs = []
    for ci in range(n_c):
        rs = slice(ci * c, (ci + 1) * c)
        outs.append(o_intra[rs] + lax.dot_general(q_e[rs], st.astype(BF16), _NT, preferred_element_type=F32))
        st = decay[ci] * st + lax.dot_general(v[rs], k_end[rs], _TN, preferred_element_type=F32)
    st_ref[...] = st
    o = jnp.concatenate(outs, axis=0)
    o = o * _rms_scale(o) * nw_ref[...]
    gate = og_ref[0].astype(F32)
    o_ref[0] = (o * (gate * _sigmoid(gate))).astype(BF16)


def _gla(pg, w2_pad, b_gk2, gla_norm_w):
    bsz, seq, _ = pg.shape
    t = T_GLA
    kq = GLA_KEY_DIM // GLA_DK
    kv = 2 * GLA_KEY_DIM // GLA_DV
    kg = kv + GLA_VALUE_DIM // GLA_DV
    klr = (2 * GLA_KEY_DIM + 2 * GLA_VALUE_DIM) // LANES
    return pl.pallas_call(
        _gla_kernel,
        grid=(bsz, GLA_HEADS, seq // t),
        in_specs=[
            pl.BlockSpec((1, t, GLA_DK), lambda b, h, i: (b, i, h)),
            pl.BlockSpec((1, t, GLA_DK), lambda b, h, i: (b, i, kq + h)),
            pl.BlockSpec((1, t, GLA_DV), lambda b, h, i: (b, i, kv + h)),
            pl.BlockSpec((1, t, GLA_DV), lambda b, h, i: (b, i, kg + h)),
            pl.BlockSpec((1, t, LANES), lambda b, h, i: (b, i, klr)),
            pl.BlockSpec((LANES, GLA_DK), lambda b, h, i: (0, h)),
            pl.BlockSpec((1, GLA_DK), lambda b, h, i: (0, h)),
            pl.BlockSpec((1, GLA_DV), lambda b, h, i: (0, 0)),
        ],
        out_specs=pl.BlockSpec((1, t, GLA_DV), lambda b, h, i: (b, i, h)),
        out_shape=jax.ShapeDtypeStruct((bsz, seq, GLA_VALUE_DIM), BF16),
        scratch_shapes=[pltpu.VMEM((GLA_DV, GLA_DK), F32), pltpu.VMEM((t, t), BF16),
                        pltpu.VMEM((t, t), F32)],
        compiler_params=pltpu.CompilerParams(
            dimension_semantics=("parallel", "parallel", "arbitrary"), vmem_limit_bytes=VMEM_LIMIT),
        name="gla",
    )(pg, pg, pg, pg, pg, w2_pad, b_gk2.reshape(1, GLA_KEY_DIM), gla_norm_w.reshape(1, GLA_DV))


def _mix_kernel(x_ref, o0_ref, l0_ref, o1_ref, l1_ref, o2_ref, l2_ref, ob_ref, pm_ref,
                wa_ref, wb_ref, wo_ref, ln2_ref, wrh_ref, wrl_ref, br_ref,
                h1_ref, hn_ref, ri_ref, ric_ref, rg_ref, cbefore_ref, cnt_ref,
                po1_ref, pl1_ref, po2_ref, pl2_ref, carry_ref):
    step = pl.program_id(0)
    tm = x_ref.shape[0]

    @pl.when(step == 0)
    def _():
        carry_ref[...] = jnp.zeros_like(carry_ref)

    for o_ref, l_ref, po_ref, pl_ref, (_, d) in ((o1_ref, l1_ref, po1_ref, pl1_ref, DA_GROUPS[1]),
                                                 (o2_ref, l2_ref, po2_ref, pl2_ref, DA_GROUPS[2])):
        n = tm // d
        for r in range(d):
            for h in range(DA_HEADS):
                sl = slice(h * DA_HEAD_DIM, (h + 1) * DA_HEAD_DIM)
                po_ref[h, pl.ds(r, n, stride=d), :] = o_ref[0, r, :, sl].astype(F32)
                pl_ref[h, pl.ds(r, n, stride=d), :] = l_ref[0, r, :, sl]

    heads = []
    for h in range(DA_HEADS):
        sl = slice(h * DA_HEAD_DIM, (h + 1) * DA_HEAD_DIM)
        l0, l1, l2 = l0_ref[:, sl], pl1_ref[h], pl2_ref[h]
        mx = jnp.maximum(jnp.maximum(l0, l1), l2)
        e0, e1, e2 = jnp.exp(l0 - mx), jnp.exp(l1 - mx), jnp.exp(l2 - mx)
        o_h = (e0 * o0_ref[:, sl].astype(F32) + e1 * po1_ref[h] + e2 * po2_ref[h]) / (e0 + e1 + e2)
        heads.append(o_h.astype(BF16))
    o_a = jnp.concatenate(heads, axis=-1)

    gates = _sigmoid(pm_ref[...].astype(F32))
    mixed = (gates[:, :D_MODEL] * _dot(o_a, wa_ref[...])
             + gates[:, D_MODEL:] * _dot(ob_ref[...], wb_ref[...]))
    h1 = x_ref[...] + _dot(mixed.astype(BF16), wo_ref[...])
    h1_ref[...] = h1
    hn = h1 * _rms_scale(h1) * ln2_ref[...]

    hn_hi = hn.astype(BF16)
    hn_ref[...] = hn_hi
    hn_lo = (hn - hn_hi.astype(F32)).astype(BF16)
    logits = (_dot(hn_hi, wrh_ref[...]) + _dot(hn_lo, wrh_ref[...]) + _dot(hn_hi, wrl_ref[...])
              + br_ref[...])
    lane = lax.broadcasted_iota(jnp.int32, (tm, LANES), 1).astype(F32)
    work = jnp.where(lane < N_EXPERTS, logits, -jnp.inf)
    vals, idxs = [], []
    for _ in range(TOP_K):
        m = jnp.max(work, axis=-1, keepdims=True)
        idx = jnp.min(jnp.where(work == m, lane, float(LANES)), axis=-1, keepdims=True)
        vals.append(m)
        idxs.append(idx)
        work = jnp.where(lane == idx, -jnp.inf, work)
    exps = [jnp.exp(v - vals[0]) for v in vals]
    denom = exps[0] + exps[1] + exps[2] + exps[3]

    onehot = jnp.zeros((tm, LANES), F32)
    for idx in idxs:
        onehot = onehot + jnp.where(lane == idx, 1.0, 0.0)
    row = lax.broadcasted_iota(jnp.int32, (tm, tm), 0)
    col = lax.broadcasted_iota(jnp.int32, (tm, tm), 1)
    below = jnp.where(col < row, 1.0, 0.0).astype(BF16)
    before = _dot(below, onehot.astype(BF16)) + carry_ref[0:1, :]
    ranks = [jnp.sum(jnp.where(lane == idx, before, 0.0), axis=-1, keepdims=True) for idx in idxs]
    cbefore_ref[0] = carry_ref[...]
    carry = carry_ref[0:1, :] + jnp.sum(onehot, axis=0, keepdims=True)
    carry_ref[...] = jnp.broadcast_to(carry, carry_ref.shape)
    cnt_ref[...] = jnp.broadcast_to(carry, cnt_ref.shape)

    ri = jnp.zeros((tm, LANES), F32)
    for j, val in enumerate(idxs + ranks):
        ri = jnp.where(lane == float(j), val, ri)
    ric_ref[...] = ri.astype(jnp.int32)
    ri_ref[0] = ri.T[:2 * TOP_K].astype(jnp.int32)
    rg = jnp.zeros((tm, LANES), F32)
    for j, e in enumerate(exps):
        rg = jnp.where(lane == float(j), e / denom, rg)
    rg_ref[0] = rg.T[:2 * TOP_K]


def _mix(x2, o0, l0, o1, l1, o2, l2, o_b, pm, wa, wb, wo, ln2_w, wr_hi, wr_lo, br_pad):
    n_tok = x2.shape[0]
    tm = TM_MIX
    bsz = o1.shape[0]
    d1, d2 = DA_GROUPS[1][1], DA_GROUPS[2][1]
    tiles_per_seq = (n_tok // bsz) // tm

    def rows(width):
        return pl.BlockSpec((tm, width), lambda i: (i, 0))

    def residue_major(d):
        return pl.BlockSpec((1, d, tm // d, DA_WIDTH),
                            lambda i: (i // tiles_per_seq, 0, i % tiles_per_seq, 0))

    def whole(arr):
        return pl.BlockSpec(arr.shape, lambda i: (0,) * arr.ndim)

    ln2 = ln2_w.reshape(1, D_MODEL)
    return pl.pallas_call(
        _mix_kernel,
        grid=(n_tok // tm,),
        in_specs=[rows(D_MODEL), rows(DA_WIDTH), rows(DA_WIDTH),
                  residue_major(d1), residue_major(d1), residue_major(d2), residue_major(d2),
                  rows(GLA_VALUE_DIM), rows(MERGE_W),
                  whole(wa), whole(wb), whole(wo), whole(ln2), whole(wr_hi), whole(wr_lo), whole(br_pad)],
        out_specs=(rows(D_MODEL), rows(D_MODEL),
                   pl.BlockSpec((1, 2 * TOP_K, tm), lambda i: (i, 0, 0)), rows(LANES),
                   pl.BlockSpec((1, 2 * TOP_K, tm), lambda i: (i, 0, 0)),
                   pl.BlockSpec((1, 8, LANES), lambda i: (i, 0, 0)),
                   pl.BlockSpec((8, LANES), lambda i: (0, 0))),
        out_shape=(jax.ShapeDtypeStruct((n_tok, D_MODEL), F32),
                   jax.ShapeDtypeStruct((n_tok, D_MODEL), BF16),
                   jax.ShapeDtypeStruct((n_tok // tm, 2 * TOP_K, tm), jnp.int32),
                   jax.ShapeDtypeStruct((n_tok, LANES), jnp.int32),
                   jax.ShapeDtypeStruct((n_tok // tm, 2 * TOP_K, tm), F32),
                   jax.ShapeDtypeStruct((n_tok // tm, 8, LANES), F32),
                   jax.ShapeDtypeStruct((8, LANES), F32)),
        scratch_shapes=[pltpu.VMEM((DA_HEADS, tm, DA_HEAD_DIM), F32)] * 4 + [pltpu.VMEM((8, LANES), F32)],
        compiler_params=pltpu.CompilerParams(
            dimension_semantics=("arbitrary",), vmem_limit_bytes=VMEM_LIMIT),
        name="mix",
    )(x2, o0, l0, o1, l1, o2, l2, o_b, pm, wa, wb, wo, ln2, wr_hi, wr_lo, br_pad)


def _pack_pairs(x):
    n = x.shape[1] // 2
    rounded = x.astype(BF16).astype(F32)
    lo = lax.bitcast_convert_type(rounded[:, :n], U32) >> 16
    hi = lax.bitcast_convert_type(rounded[:, n:], U32) & jnp.uint32(0xFFFF0000)
    return hi | lo


def _unpack_pairs(u):
    lo = lax.bitcast_convert_type(u << 16, F32).astype(BF16)
    hi = lax.bitcast_convert_type(u & jnp.uint32(0xFFFF0000), F32).astype(BF16)
    return lo, hi


def _dispatch_kernel(zstart_ref, nwin_ref, wsrc_ref, wdst_ref, pos_ref, g_ref, hn_ref, xin_ref,
                     buf_ref, zero_ref, sem, zsem):
    i = pl.program_id(0)
    tm = hn_ref.shape[0]
    n_buf_rows = buf_ref.shape[1]
    slot = i % 2

    def window_copy(step, s):
        src = pl.multiple_of(wsrc_ref[step * N_SLOT + s], ROW_ALIGN)
        dst = pl.multiple_of(wdst_ref[step * N_SLOT + s], ROW_ALIGN)
        return pltpu.make_async_copy(buf_ref.at[step % 2, pl.ds(src, WIN), :],
                                     xin_ref.at[pl.ds(dst, WIN), :], sem)

    def wait_windows(step):
        def body(s, carry):
            window_copy(step, s).wait()
            return carry
        lax.fori_loop(0, nwin_ref[step], body, 0)

    @pl.when(i == 0)
    def _():
        zero_ref[...] = jnp.zeros_like(zero_ref)
        for j in range(zstart_ref.shape[0]):
            @pl.when(zstart_ref[j] >= 0)
            def _():
                start = pl.multiple_of(zstart_ref[j], ZCHUNK)
                cp = pltpu.make_async_copy(zero_ref, xin_ref.at[pl.ds(start, ZCHUNK), :], zsem)
                cp.start()
                cp.wait()

    row = lax.broadcasted_iota(jnp.int32, (n_buf_rows, tm), 0)
    perm = jnp.zeros((n_buf_rows, tm), F32)
    gate = jnp.zeros((n_buf_rows, 1), F32)
    for k in range(TOP_K):
        hit = row == pos_ref[0, k:k + 1, :]
        perm = perm + jnp.where(hit, 1.0, 0.0)
        gate = gate + jnp.sum(jnp.where(hit, g_ref[0, k:k + 1, :], 0.0), axis=-1, keepdims=True)
    rows_sorted = _dot(perm.astype(BF16), hn_ref[...])
    buf_ref[slot, :, :D_MODEL // 2] = _pack_pairs(rows_sorted)
    buf_ref[slot, :, D_MODEL // 2:] = lax.bitcast_convert_type(
        jnp.broadcast_to(gate, (n_buf_rows, LANES)), U32)

    @pl.when(i > 0)
    def _():
        wait_windows(i - 1)

    def issue(s, carry):
        window_copy(i, s).start()
        return carry

    lax.fori_loop(0, nwin_ref[i], issue, 0)

    @pl.when(i == pl.num_programs(0) - 1)
    def _():
        wait_windows(i)


def _dispatch(zero_start, n_win, win_src, win_dst, pos, gates_t, hn, n_rows):
    n_tok = hn.shape[0]
    tm = TM_ROW
    n_tiles = n_tok // tm
    n_buf_rows = tm * TOP_K + N_EXPERTS * (ROW_ALIGN - 1) + WIN
    return pl.pallas_call(
        _dispatch_kernel,
        grid_spec=pltpu.PrefetchScalarGridSpec(
            num_scalar_prefetch=4,
            grid=(n_tiles,),
            in_specs=[
                pl.BlockSpec((1, TOP_K, tm), lambda i, *_: (i, 0, 0)),
                pl.BlockSpec((1, 2 * TOP_K, tm), lambda i, *_: (i, 0, 0)),
                pl.BlockSpec((tm, D_MODEL), lambda i, *_: (i, 0)),
            ],
            out_specs=pl.BlockSpec(memory_space=pl.ANY),
            scratch_shapes=[pltpu.VMEM((2, n_buf_rows, ROW_W), U32),
                            pltpu.VMEM((ZCHUNK, ROW_W), U32),
                            pltpu.SemaphoreType.DMA(()), pltpu.SemaphoreType.DMA(())],
        ),
        out_shape=jax.ShapeDtypeStruct((n_rows, ROW_W), U32),
        compiler_params=pltpu.CompilerParams(
            dimension_semantics=("arbitrary",), vmem_limit_bytes=VMEM_LIMIT),
        name="dispatch",
    )(zero_start, n_win, win_src.reshape(-1), win_dst.reshape(-1), pos, gates_t, hn)


def _expert_kernel(be_ref, nused_ref, x_ref, wu_ref, bu_ref, wd_ref, bd_ref, y_ref, wu16_ref, wd16_ref):
    i = pl.program_id(0)
    used = i < nused_ref[0]

    @pl.when((i == 0) | (be_ref[i] != be_ref[jnp.maximum(i - 1, 0)]))
    def _():
        wu16_ref[...] = wu_ref[0].astype(BF16)
        wd16_ref[...] = wd_ref[0].astype(BF16)

    @pl.when(jnp.logical_not(used))
    def _():
        y_ref[...] = jnp.zeros_like(y_ref)

    @pl.when(used)
    def _():
        half = D_MODEL // 2
        x_lo, x_hi = _unpack_pairs(x_ref[:, :half])
        gate = lax.bitcast_convert_type(x_ref[:, half:half + 1], F32)
        hu = _dot(x_lo, wu16_ref[:half, :]) + _dot(x_hi, wu16_ref[half:, :]) + bu_ref[0]
        x_glu = jnp.minimum(hu[:, :D_FF], SWIGLU_LIMIT)
        x_lin = jnp.clip(hu[:, D_FF:], -SWIGLU_LIMIT, SWIGLU_LIMIT)
        act = x_glu * _sigmoid(SWIGLU_ALPHA * x_glu) * (x_lin + 1.0)
        y_ref[...] = _pack_pairs((_dot(act.astype(BF16), wd16_ref[...]) + bd_ref[0]) * gate)


def _experts(block_e, n_used, xin, w_up, b_up, w_down, b_down):
    n_rows = xin.shape[0]
    return pl.pallas_call(
        _expert_kernel,
        grid_spec=pltpu.PrefetchScalarGridSpec(
            num_scalar_prefetch=2,
            grid=(n_rows // TB,),
            in_specs=[
                pl.BlockSpec((TB, ROW_W), lambda i, be, nu: (i, 0)),
                pl.BlockSpec((1, D_MODEL, 2 * D_FF), lambda i, be, nu: (be[i], 0, 0)),
                pl.BlockSpec((1, 1, 2 * D_FF), lambda i, be, nu: (be[i], 0, 0)),
                pl.BlockSpec((1, D_FF, D_MODEL), lambda i, be, nu: (be[i], 0, 0)),
                pl.BlockSpec((1, 1, D_MODEL), lambda i, be, nu: (be[i], 0, 0)),
            ],
            out_specs=pl.BlockSpec((TB, D_MODEL // 2), lambda i, be, nu: (i, 0)),
            scratch_shapes=[pltpu.VMEM((D_MODEL, 2 * D_FF), BF16), pltpu.VMEM((D_FF, D_MODEL), BF16)],
        ),
        out_shape=jax.ShapeDtypeStruct((n_rows, D_MODEL // 2), U32),
        compiler_params=pltpu.CompilerParams(
            dimension_semantics=("arbitrary",), vmem_limit_bytes=VMEM_LIMIT),
        name="experts",
    )(block_e, n_used, xin, w_up, b_up.reshape(N_EXPERTS, 1, 2 * D_FF),
      w_down, b_down.reshape(N_EXPERTS, 1, D_MODEL))


def _combine_kernel(nwin_ref, wdst_ref, ric_ref, delta_ref, yb_ref, h1_ref, lnf_ref, o_ref, buf_ref, sem):
    i = pl.program_id(0)
    tm = h1_ref.shape[0]
    n_buf_rows = buf_ref.shape[0]

    @pl.when(i == 0)
    def _():
        buf_ref[...] = jnp.zeros_like(buf_ref)

    def window_copy(s):
        dst = pl.multiple_of(wdst_ref[i * N_SLOT + s], ROW_ALIGN)
        return pltpu.make_async_copy(yb_ref.at[pl.ds(dst, WIN), :],
                                     buf_ref.at[pl.ds(pl.multiple_of(s * WIN, WIN), WIN), :], sem)

    def issue(s, carry):
        window_copy(s).start()
        return carry

    lax.fori_loop(0, nwin_ref[i], issue, 0)

    ric = ric_ref[...].astype(F32)
    lane = lax.broadcasted_iota(jnp.int32, (tm, LANES), 1).astype(F32)
    col = lax.broadcasted_iota(jnp.int32, (tm, n_buf_rows), 1)
    pick = jnp.zeros((tm, n_buf_rows), F32)
    for k in range(TOP_K):
        offset = jnp.sum(jnp.where(lane == ric[:, k:k + 1], delta_ref[0], 0.0), axis=-1, keepdims=True)
        pos = (ric[:, TOP_K + k:TOP_K + k + 1] + offset).astype(jnp.int32)
        pick = pick + jnp.where(col == pos, 1.0, 0.0)
    pick = pick.astype(BF16)

    def drain(s, carry):
        window_copy(s).wait()
        return carry

    lax.fori_loop(0, nwin_ref[i], drain, 0)

    y_lo, y_hi = _unpack_pairs(buf_ref[...])
    h2 = h1_ref[...] + jnp.concatenate([_dot(pick, y_lo), _dot(pick, y_hi)], axis=-1)
    o_ref[...] = h2 * _rms_scale(h2) * lnf_ref[...]


def _combine(n_win, win_dst, ric, delta, yb, h1, lnf_w):
    n_tok = h1.shape[0]
    tm = TM_ROW
    n_tiles = n_tok // tm
    return pl.pallas_call(
        _combine_kernel,
        grid_spec=pltpu.PrefetchScalarGridSpec(
            num_scalar_prefetch=2,
            grid=(n_tiles,),
            in_specs=[
                pl.BlockSpec((tm, LANES), lambda i, *_: (i, 0)),
                pl.BlockSpec((1, 1, LANES), lambda i, *_: (i, 0, 0)),
                pl.BlockSpec(memory_space=pl.ANY),
                pl.BlockSpec((tm, D_MODEL), lambda i, *_: (i, 0)),
                pl.BlockSpec((1, D_MODEL), lambda i, *_: (0, 0)),
            ],
            out_specs=pl.BlockSpec((tm, D_MODEL), lambda i, *_: (i, 0)),
            scratch_shapes=[pltpu.VMEM((N_SLOT * WIN, D_MODEL // 2), U32), pltpu.SemaphoreType.DMA(())],
        ),
        out_shape=jax.ShapeDtypeStruct((n_tok, D_MODEL), F32),
        compiler_params=pltpu.CompilerParams(
            dimension_semantics=("arbitrary",), vmem_limit_bytes=VMEM_LIMIT),
        name="combine",
    )(n_win, win_dst.reshape(-1), ric, delta, yb, h1, lnf_w.reshape(1, D_MODEL))


def _layer(h, ln1_w, w_in, w_gk2, b_gk2, gla_norm_w, w_proj_a, w_proj_b, w_out,
           ln2_w, w_router, b_router, w_up, b_up, w_down, b_down, lnf_w):
    bsz, seq, _ = h.shape
    n_tok = bsz * seq

    n_main = 3 * QKV_W + 2 * GLA_KEY_DIM + 2 * GLA_VALUE_DIM
    pad = LANES - GLA_GATE_RANK
    w_all = jnp.concatenate(
        [w_in[:, :n_main + GLA_GATE_RANK], jnp.zeros((D_MODEL, pad), F32), w_in[:, n_main + GLA_GATE_RANK:]],
        axis=1).astype(BF16)
    w2_pad = jnp.concatenate([w_gk2, jnp.zeros((pad, GLA_KEY_DIM), F32)], axis=0).astype(BF16)
    wr_pad = jnp.concatenate([w_router, jnp.zeros((D_MODEL, LANES - N_EXPERTS), F32)], axis=1)
    wr_hi = wr_pad.astype(BF16)
    wr_lo = (wr_pad - wr_hi.astype(F32)).astype(BF16)
    br_pad = jnp.concatenate([b_router, jnp.zeros((LANES - N_EXPERTS,), F32)]).reshape(1, LANES)

    pa0, pa1, pa2, pg, pm = _inproj(h, ln1_w, w_all)
    o0, l0 = _dil_attn(pa0.reshape(bsz, 1, seq, QKV_W))
    o1, l1 = _dil_attn(pa1)
    o2, l2 = _dil_attn(pa2)
    o_b = _gla(pg, w2_pad, b_gk2, gla_norm_w)

    h1, hn, ri, ric, gates_t, carry_f, cnt = _mix(
        h.reshape(n_tok, D_MODEL), o0.reshape(n_tok, DA_WIDTH), l0.reshape(n_tok, DA_WIDTH),
        o1, l1, o2, l2, o_b.reshape(n_tok, GLA_VALUE_DIM), pm.reshape(n_tok, MERGE_W),
        w_proj_a.astype(BF16), w_proj_b.astype(BF16), w_out.astype(BF16), ln2_w, wr_hi, wr_lo, br_pad)

    i32 = jnp.int32
    n_tiles = n_tok // TM_ROW
    experts = jnp.arange(N_EXPERTS)
    counts = cnt[0, :N_EXPERTS].astype(i32)
    before = carry_f[:, 0, :N_EXPERTS].astype(i32)
    run = jnp.concatenate([before[1:], counts[None]], axis=0) - before
    run_al = (run + ROW_ALIGN - 1) // ROW_ALIGN * ROW_ALIGN
    rows_end = jnp.cumsum(run_al, axis=0)
    rows_before = rows_end - run_al
    used = rows_end[-1]
    slack = WIN - ROW_ALIGN
    padded = (used + slack + TB - 1) // TB * TB
    pad_end = jnp.cumsum(padded)
    pad_start = pad_end - padded
    n_asg = n_tok * TOP_K
    n_rows = (n_asg + n_tiles * N_EXPERTS * (ROW_ALIGN - 1)
              + N_EXPERTS * (slack + TB - 1) + TB - 1) // TB * TB
    n_blocks = n_rows // TB
    block_starts = jnp.arange(n_blocks) * TB
    block_e = jnp.minimum(jnp.sum(pad_end[None, :] <= block_starts[:, None], axis=1),
                          N_EXPERTS - 1).astype(i32)
    n_used = (pad_end[-1:] // TB).astype(i32)

    run_end = jnp.cumsum(run_al, axis=1)
    run_start = run_end - run_al
    wins = (run + WIN - 1) // WIN
    wins_end = jnp.cumsum(wins, axis=1)
    wins_start = wins_end - wins
    n_win = wins_end[:, -1].astype(i32)
    slots = jnp.arange(N_SLOT)
    slot_e = jnp.sum(wins_end[:, None, :] <= slots[None, :, None], axis=-1)
    slot_is = slot_e[..., None] == experts

    def of_slot(table):
        return jnp.sum(jnp.where(slot_is, table[:, None, :], 0), axis=-1)

    win_off = (slots[None, :] - of_slot(wins_start)) * WIN
    win_src = (of_slot(run_start) + win_off).astype(i32)
    win_dst = (of_slot(pad_start[None, :] + rows_before) + win_off).astype(i32)
    asg_is = ri[:, :TOP_K, :, None] == experts
    pos = ri[:, TOP_K:, :] + jnp.sum(jnp.where(asg_is, (run_start - before)[:, None, None, :], 0), axis=-1)
    delta = jnp.pad((wins_start * WIN - before).astype(F32), ((0, 0), (0, LANES - N_EXPERTS)))

    chunk_back = ZCHUNK * (1 + jnp.arange((slack + TB - 1 + ZCHUNK - 1) // ZCHUNK + 1))
    region_chunks = pad_end[:, None] - chunk_back[None, :]
    region_ok = (region_chunks >= pad_start[:, None]) & (region_chunks + ZCHUNK > (pad_start + used)[:, None])
    tail_chunks = pad_end[-1] + ZCHUNK * jnp.arange((n_rows - n_asg) // ZCHUNK)
    zero_start = jnp.concatenate([jnp.where(region_ok, region_chunks, -1).reshape(-1),
                                  jnp.where(tail_chunks < n_rows, tail_chunks, -1)]).astype(i32)

    xin = _dispatch(zero_start, n_win, win_src, win_dst, pos.astype(i32), gates_t, hn, n_rows)
    yb = _experts(block_e, n_used, xin, w_up, b_up, w_down, b_down)
    out = _combine(n_win, win_dst, ric, delta.reshape(-1, 1, LANES), yb, h1, lnf_w)
    return out.reshape(bsz, seq, D_MODEL)


def kernel(x, ln1_w, w_in, w_gk2, b_gk2, gla_norm_w, w_proj_a, w_proj_b, w_out, ln2_w, w_router,
           b_router, w_up, b_up, w_down, b_down, lnf_w):
    assert x.shape[-1] == D_MODEL and ln1_w.shape[0] == 1, "one layer of width D_MODEL"
    return _layer(x, ln1_w[0], w_in[0], w_gk2[0], b_gk2[0], gla_norm_w[0], w_proj_a[0], w_proj_b[0],
                  w_out[0], ln2_w[0], w_router[0], b_router[0], w_up[0], b_up[0], w_down[0],
                  b_down[0], lnf_w)
```

```python
import functools

import jax
import jax.numpy as jnp
from jax import lax
from jax.experimental import pallas as pl
from jax.experimental.pallas import tpu as pltpu

F32 = jnp.float32
BF16 = jnp.bfloat16
U32 = jnp.uint32

D_MODEL = 1024
DA_GROUPS = ((128, 1), (512, 4), (2048, 16))
DA_HEADS = 4
DA_HEAD_DIM = 128
DA_WIDTH = DA_HEADS * DA_HEAD_DIM
DA_BLOCK = 128
GLA_HEADS = 4
GLA_KEY_DIM = D_MODEL // 2
GLA_VALUE_DIM = D_MODEL
GLA_DK = GLA_KEY_DIM // GLA_HEADS
GLA_DV = GLA_VALUE_DIM // GLA_HEADS
GLA_GATE_RANK = 16
GLA_GATE_NORMALIZER = 16.0
GLA_CHUNK = 64
N_EXPERTS = 32
TOP_K = 4
D_FF = D_MODEL
SWIGLU_ALPHA = 1.702
SWIGLU_LIMIT = 7.0
RMS_EPS = 1e-5
NEG_INF = -1e30

LANES = 128
QKV_W = 3 * DA_WIDTH
GLA_W = 2 * GLA_KEY_DIM + 2 * GLA_VALUE_DIM + LANES
MERGE_W = 2 * D_MODEL

DA_QB = 2
LSE_LANES = LANES // DA_HEADS
TM_IN = 512
N_CHUNK = 512
T_GLA = 512
TM_MIX = 256
TB = 512
TM_ROW = TM_MIX
ROW_ALIGN = 8
WIN = 32
N_SLOT = N_EXPERTS + TM_ROW * TOP_K // WIN
ZCHUNK = 256
ROW_W = D_MODEL // 2 + LANES
VMEM_LIMIT = 56 * 1024 * 1024
VMEM_LIMIT_INPROJ = 62 * 1024 * 1024

_NT = (((1,), (1,)), ((), ()))
_TN = (((0,), (0,)), ((), ()))


def _dot(a, b):
    return jnp.dot(a, b, preferred_element_type=F32)


def _sigmoid(x):
    return 1.0 / (1.0 + jnp.exp(-x))


def _rms_scale(x):
    return lax.rsqrt(jnp.mean(x * x, axis=-1, keepdims=True) + RMS_EPS)


def _inproj_kernel(x_ref, ln_ref, w_hbm, pa0_ref, pa1_ref, pa2_ref, pg_ref, pm_ref,
                   w_ref, xs_ref, xn_ref, wsem):
    tm = x_ref.shape[1]
    n_slab = D_MODEL // LANES

    @pl.when((pl.program_id(0) == 0) & (pl.program_id(1) == 0))
    def _():
        cp = pltpu.make_async_copy(w_hbm, w_ref, wsem)
        cp.start()
        cp.wait()

    def project(out_write, col0, width, post=None):
        for c0 in range(0, width, N_CHUNK):
            cw = min(N_CHUNK, width - c0)
            val = _dot(xs_ref[...], w_ref[:, col0 + c0:col0 + c0 + cw])
            out_write(c0, cw, (val if post is None else post(val)).astype(BF16))

    x = x_ref[0]
    xn = x * _rms_scale(x) * ln_ref[...]
    xs_ref[...] = xn.astype(BF16)
    for j in range(n_slab):
        xn_ref[j] = xn[:, j * LANES:(j + 1) * LANES]

    def write_to(ref):
        def write(c0, cw, val):
            ref[0, :, c0:c0 + cw] = val
        return write

    def write_qkv(out_ref, d, n):
        def write(c0, cw, val):
            for r in range(d):
                out_ref[0, r, c0 // DA_WIDTH] = val[r * n:(r + 1) * n]
        return write

    project(write_qkv(pa0_ref, 1, tm), 0, QKV_W)
    project(write_to(pg_ref), 3 * QKV_W, GLA_W)
    project(write_to(pm_ref), 3 * QKV_W + GLA_W, MERGE_W, post=_sigmoid)

    for gi, out_ref in ((1, pa1_ref), (2, pa2_ref)):
        d = DA_GROUPS[gi][1]
        n = tm // d
        for r in range(d):
            for j in range(n_slab):
                xs_ref[r * n:(r + 1) * n, j * LANES:(j + 1) * LANES] = (
                    xn_ref[j, pl.ds(r, n, stride=d), :].astype(BF16))
        project(write_qkv(out_ref, d, n), gi * QKV_W, QKV_W)


def _inproj(x, ln1_w, w_all):
    assert N_CHUNK == DA_WIDTH
    bsz, seq, _ = x.shape
    tm = TM_IN
    d1, d2 = DA_GROUPS[1][1], DA_GROUPS[2][1]

    def qkv_shape(d):
        return jax.ShapeDtypeStruct((bsz, d, 3, seq // d, DA_WIDTH), BF16)

    def qkv_spec(d):
        return pl.BlockSpec((1, d, 3, tm // d, DA_WIDTH), lambda b, i: (b, 0, 0, i, 0))

    return pl.pallas_call(
        _inproj_kernel,
        grid=(bsz, seq // tm),
        in_specs=[
            pl.BlockSpec((1, tm, D_MODEL), lambda b, i: (b, i, 0)),
            pl.BlockSpec((1, D_MODEL), lambda b, i: (0, 0)),
            pl.BlockSpec(memory_space=pl.ANY),
        ],
        out_specs=(
            qkv_spec(1), qkv_spec(d1), qkv_spec(d2),
            pl.BlockSpec((1, tm, GLA_W), lambda b, i: (b, i, 0)),
            pl.BlockSpec((1, tm, MERGE_W), lambda b, i: (b, i, 0)),
        ),
        out_shape=(qkv_shape(1), qkv_shape(d1), qkv_shape(d2),
                   jax.ShapeDtypeStruct((bsz, seq, GLA_W), BF16),
                   jax.ShapeDtypeStruct((bsz, seq, MERGE_W), BF16)),
        scratch_shapes=[pltpu.VMEM(w_all.shape, BF16),
                        pltpu.VMEM((tm, D_MODEL), BF16),
                        pltpu.VMEM((D_MODEL // LANES, tm, LANES), F32),
                        pltpu.SemaphoreType.DMA(())],
        compiler_params=pltpu.CompilerParams(
            dimension_semantics=("arbitrary", "arbitrary"), vmem_limit_bytes=VMEM_LIMIT_INPROJ),
        name="inproj",
    )(x, ln1_w.reshape(1, D_MODEL), w_all)


def _dil_attn_kernel(q_ref, kp_ref, kc_ref, vp_ref, vc_ref, o_ref, l_ref, s_ref, p_ref, r_ref):
    n = pl.program_id(2)
    blk = DA_BLOCK
    qi = lax.broadcasted_iota(jnp.int32, (blk, 2 * blk), 0)
    kj = lax.broadcasted_iota(jnp.int32, (blk, 2 * blk), 1)
    band = (kj >= qi) & (kj <= qi + blk)
    band_first = (kj >= jnp.where(n > 0, qi, blk)) & (kj <= qi + blk)
    scale = DA_HEAD_DIM ** -0.5
    items = [(b, h) for b in range(DA_QB) for h in range(DA_HEADS)]

    def rows(b):
        return slice(b * blk, (b + 1) * blk)

    def cols(h):
        return slice(h * DA_HEAD_DIM, (h + 1) * DA_HEAD_DIM)

    def window(prev_ref, cur_ref, b, h):
        before = prev_ref[0, 0, 0, :, cols(h)] if b == 0 else cur_ref[0, 0, 0, rows(b - 1), cols(h)]
        return jnp.concatenate([before, cur_ref[0, 0, 0, rows(b), cols(h)]], axis=0)

    for i, (b, h) in enumerate(items):
        s = lax.dot_general(q_ref[0, 0, 0, rows(b), cols(h)], window(kp_ref, kc_ref, b, h), _NT,
                            preferred_element_type=F32) * scale
        s_ref[i] = jnp.where(band_first if b == 0 else band, s, NEG_INF)
    for i, (b, h) in enumerate(items):
        s = s_ref[i]
        m = jnp.max(s, axis=-1, keepdims=True)
        p = jnp.exp(s - m)
        l = jnp.sum(p, axis=-1, keepdims=True)
        p_ref[i] = p.astype(BF16)
        r_ref[i] = jnp.broadcast_to(1.0 / l, (blk, DA_HEAD_DIM))
        l_ref[0, 0, rows(b), h * LSE_LANES:(h + 1) * LSE_LANES] = jnp.broadcast_to(
            m + jnp.log(l), (blk, LSE_LANES))
    for i, (b, h) in enumerate(items):
        acc = _dot(p_ref[i], window(vp_ref, vc_ref, b, h))
        o_ref[0, 0, rows(b), cols(h)] = (acc * r_ref[i]).astype(o_ref.dtype)


def _dil_attn(pa):
    bsz, d, _, sub_len, _ = pa.shape
    rows = DA_QB * DA_BLOCK
    n_items = DA_QB * DA_HEADS

    def cur(sec):
        return pl.BlockSpec((1, 1, 1, rows, DA_WIDTH), lambda b, r, n: (b, r, sec, n, 0))

    def prev(sec):
        return pl.BlockSpec((1, 1, 1, DA_BLOCK, DA_WIDTH),
                            lambda b, r, n: (b, r, sec, jnp.maximum(n * DA_QB - 1, 0), 0))

    def out(width):
        return pl.BlockSpec((1, 1, rows, width), lambda b, r, n: (b, r, n, 0))

    return pl.pallas_call(
        _dil_attn_kernel,
        grid=(bsz, d, sub_len // rows),
        in_specs=[cur(0), prev(1), cur(1), prev(2), cur(2)],
        out_specs=(out(DA_WIDTH), out(LANES)),
        out_shape=(jax.ShapeDtypeStruct((bsz, d, sub_len, DA_WIDTH), BF16),
                   jax.ShapeDtypeStruct((bsz, d, sub_len, LANES), F32)),
        scratch_shapes=[pltpu.VMEM((n_items, DA_BLOCK, 2 * DA_BLOCK), F32),
                        pltpu.VMEM((n_items, DA_BLOCK, 2 * DA_BLOCK), BF16),
                        pltpu.VMEM((n_items, DA_BLOCK, DA_HEAD_DIM), F32)],
        compiler_params=pltpu.CompilerParams(
            dimension_semantics=("parallel", "parallel", "arbitrary"), vmem_limit_bytes=VMEM_LIMIT),
        name=f"dil_attn_d{d}",
    )(pa, pa, pa, pa, pa)


def _gla_kernel(q_ref, k_ref, v_ref, og_ref, lr_ref, w2_ref, b2_ref, nw_ref, o_ref, st_ref, mask_ref, keep_ref):
    t = pl.program_id(2)
    c = GLA_CHUNK
    tt = q_ref.shape[1]
    n_c = tt // c

    @pl.when(t == 0)
    def _():
        st_ref[...] = jnp.zeros_like(st_ref)
        row = lax.broadcasted_iota(jnp.int32, (tt, tt), 0)
        col = lax.broadcasted_iota(jnp.int32, (tt, tt), 1)
        keep = (col <= row) & (col >= row - row % c)
        keep_ref[...] = jnp.where(keep, 1.0, 0.0)
        mask_ref[...] = jnp.where(keep, 1.0, 0.0).astype(BF16)

    mask = mask_ref[...]
    gpre = _dot(lr_ref[0], w2_ref[...]) + b2_ref[...]
    forget = (jnp.minimum(gpre, 0.0) - jnp.log(1.0 + jnp.exp(-jnp.abs(gpre)))) / GLA_GATE_NORMALIZER
    g_hi = forget.astype(BF16)
    g_lo = (forget - g_hi.astype(F32)).astype(BF16)
    csum = _dot(mask, jnp.concatenate([g_hi, g_lo], axis=-1))
    b = csum[:, :GLA_DK] + csum[:, GLA_DK:]
    b_last = b.reshape(n_c, c, GLA_DK)[:, c - 1:c, :]
    b_to_end = (b_last - b.reshape(n_c, c, GLA_DK)).reshape(tt, GLA_DK)
    q = q_ref[0].astype(F32)
    k = k_ref[0].astype(F32)
    v = v_ref[0]
    q_e = (q * ((GLA_DK ** -0.5) * jnp.exp(b))).astype(BF16)
    k_e = (k * jnp.exp(-b)).astype(BF16)
    k_end = (k * jnp.exp(b_to_end)).astype(BF16)
    att = lax.dot_general(q_e, k_e, _NT, preferred_element_type=F32)
    att = jnp.where(keep_ref[...] > 0.0, att, 0.0).astype(BF16)
    o_intra = _dot(att, v)
    decay = jnp.exp(b_last)
    st = st_ref[...]
    outs = []
    for ci in range(n_c):
        rs = slice(ci * c, (ci + 1) * c)
        outs.append(o_intra[rs] + lax.dot_general(q_e[rs], st.astype(BF16), _NT, preferred_element_type=F32))
        st = decay[ci] * st + lax.dot_general(v[rs], k_end[rs], _TN, preferred_element_type=F32)
    st_ref[...] = st
    o = jnp.concatenate(outs, axis=0)
    o = o * _rms_scale(o) * nw_ref[...]
    gate = og_ref[0].astype(F32)
    o_ref[0] = (o * (gate * _sigmoid(gate))).astype(BF16)


def _gla(pg, w2_pad, b_gk2, gla_norm_w):
    bsz, seq, _ = pg.shape
    t = T_GLA
    kq = GLA_KEY_DIM // GLA_DK
    kv = 2 * GLA_KEY_DIM // GLA_DV
    kg = kv + GLA_VALUE_DIM // GLA_DV
    klr = (2 * GLA_KEY_DIM + 2 * GLA_VALUE_DIM) // LANES
    return pl.pallas_call(
        _gla_kernel,
        grid=(bsz, GLA_HEADS, seq // t),
        in_specs=[
            pl.BlockSpec((1, t, GLA_DK), lambda b, h, i: (b, i, h)),
            pl.BlockSpec((1, t, GLA_DK), lambda b, h, i: (b, i, kq + h)),
            pl.BlockSpec((1, t, GLA_DV), lambda b, h, i: (b, i, kv + h)),
            pl.BlockSpec((1, t, GLA_DV), lambda b, h, i: (b, i, kg + h)),
            pl.BlockSpec((1, t, LANES), lambda b, h, i: (b, i, klr)),
            pl.BlockSpec((LANES, GLA_DK), lambda b, h, i: (0, h)),
            pl.BlockSpec((1, GLA_DK), lambda b, h, i: (0, h)),
            pl.BlockSpec((1, GLA_DV), lambda b, h, i: (0, 0)),
        ],
        out_specs=pl.BlockSpec((1, t, GLA_DV), lambda b, h, i: (b, i, h)),
        out_shape=jax.ShapeDtypeStruct((bsz, seq, GLA_VALUE_DIM), BF16),
        scratch_shapes=[pltpu.VMEM((GLA_DV, GLA_DK), F32), pltpu.VMEM((t, t), BF16),
                        pltpu.VMEM((t, t), F32)],
        compiler_params=pltpu.CompilerParams(
            dimension_semantics=("parallel", "parallel", "arbitrary"), vmem_limit_bytes=VMEM_LIMIT),
        name="gla",
    )(pg, pg, pg, pg, pg, w2_pad, b_gk2.reshape(1, GLA_KEY_DIM), gla_norm_w.reshape(1, GLA_DV))


def _mix_kernel(x_ref, o0_ref, l0_ref, o1_ref, l1_ref, o2_ref, l2_ref, ob_ref, pm_ref,
                wa_ref, wb_ref, wo_ref, ln2_ref, wrh_ref, wrl_ref, br_ref,
                h1_ref, hn_ref, ri_ref, ric_ref, meta_ref, cbefore_ref, cnt_ref,
                po1_ref, pl1_ref, po2_ref, pl2_ref, carry_ref):
    step = pl.program_id(0)
    tm = x_ref.shape[0]

    @pl.when(step == 0)
    def _():
        carry_ref[...] = jnp.zeros_like(carry_ref)

    for o_ref, l_ref, po_ref, pl_ref, (_, d) in ((o1_ref, l1_ref, po1_ref, pl1_ref, DA_GROUPS[1]),
                                                 (o2_ref, l2_ref, po2_ref, pl2_ref, DA_GROUPS[2])):
        n = tm // d
        for r in range(d):
            pl_ref[pl.ds(r, n, stride=d), :] = l_ref[0, r]
            for h in range(DA_HEADS):
                sl = slice(h * DA_HEAD_DIM, (h + 1) * DA_HEAD_DIM)
                po_ref[h, pl.ds(r, n, stride=d), :] = o_ref[0, r, :, sl].astype(F32)

    l0, l1, l2 = l0_ref[...], pl1_ref[...], pl2_ref[...]
    mx = jnp.maximum(jnp.maximum(l0, l1), l2)
    e0, e1, e2 = jnp.exp(l0 - mx), jnp.exp(l1 - mx), jnp.exp(l2 - mx)
    inv = 1.0 / (e0 + e1 + e2)
    w0, w1, w2 = e0 * inv, e1 * inv, e2 * inv
    heads = []
    for h in range(DA_HEADS):
        sl = slice(h * DA_HEAD_DIM, (h + 1) * DA_HEAD_DIM)
        at = slice(h * LSE_LANES, h * LSE_LANES + 1)
        o_h = w0[:, at] * o0_ref[:, sl].astype(F32) + w1[:, at] * po1_ref[h] + w2[:, at] * po2_ref[h]
        heads.append(o_h.astype(BF16))
    o_a = jnp.concatenate(heads, axis=-1)

    gates = pm_ref[...].astype(F32)
    mixed = (gates[:, :D_MODEL] * _dot(o_a, wa_ref[...])
             + gates[:, D_MODEL:] * _dot(ob_ref[...], wb_ref[...]))
    h1 = x_ref[...] + _dot(mixed.astype(BF16), wo_ref[...])
    h1_ref[...] = h1
    hn = h1 * _rms_scale(h1) * ln2_ref[...]

    hn_hi = hn.astype(BF16)
    hn_ref[...] = hn_hi
    hn_lo = (hn - hn_hi.astype(F32)).astype(BF16)
    logits = (_dot(hn_hi, wrh_ref[...]) + _dot(hn_lo, wrh_ref[...]) + _dot(hn_hi, wrl_ref[...])
              + br_ref[...])
    lane = lax.broadcasted_iota(jnp.int32, (tm, LANES), 1).astype(F32)
    work = jnp.where(lane < N_EXPERTS, logits, -jnp.inf)
    vals, idxs = [], []
    for _ in range(TOP_K):
        m = jnp.max(work, axis=-1, keepdims=True)
        idx = jnp.min(jnp.where(work == m, lane, float(LANES)), axis=-1, keepdims=True)
        vals.append(m)
        idxs.append(idx)
        work = jnp.where(lane == idx, -jnp.inf, work)
    exps = [jnp.exp(v - vals[0]) for v in vals]
    denom = exps[0] + exps[1] + exps[2] + exps[3]

    onehot = jnp.zeros((tm, LANES), F32)
    for idx in idxs:
        onehot = onehot + jnp.where(lane == idx, 1.0, 0.0)
    row = lax.broadcasted_iota(jnp.int32, (tm, tm), 0)
    col = lax.broadcasted_iota(jnp.int32, (tm, tm), 1)
    below = jnp.where(col < row, 1.0, 0.0).astype(BF16)
    before = _dot(below, onehot.astype(BF16)) + carry_ref[0:1, :]
    ranks = [jnp.sum(jnp.where(lane == idx, before, 0.0), axis=-1, keepdims=True) for idx in idxs]
    cbefore_ref[0] = carry_ref[...]
    carry = carry_ref[0:1, :] + jnp.sum(onehot, axis=0, keepdims=True)
    carry_ref[...] = jnp.broadcast_to(carry, carry_ref.shape)
    cnt_ref[...] = jnp.broadcast_to(carry, cnt_ref.shape)

    ri = jnp.zeros((tm, LANES), F32)
    for j, val in enumerate(idxs + ranks):
        ri = jnp.where(lane == float(j), val, ri)
    ric_ref[...] = ri.astype(jnp.int32)
    ri_ref[0] = ri.T[:2 * TOP_K].astype(jnp.int32)
    meta = jnp.zeros((tm, LANES), F32)
    for k in range(TOP_K):
        gate = exps[k] / denom
        gate_hi = gate.astype(BF16).astype(F32)
        meta = jnp.where(lane == float(k), idxs[k], meta)
        meta = jnp.where(lane == float(TOP_K + k), gate_hi, meta)
        meta = jnp.where(lane == float(2 * TOP_K + k), gate - gate_hi, meta)
    meta_ref[...] = meta.astype(BF16)


def _mix(x2, o0, l0, o1, l1, o2, l2, o_b, pm, wa, wb, wo, ln2_w, wr_hi, wr_lo, br_pad):
    n_tok = x2.shape[0]
    tm = TM_MIX
    bsz = o1.shape[0]
    d1, d2 = DA_GROUPS[1][1], DA_GROUPS[2][1]
    tiles_per_seq = (n_tok // bsz) // tm

    def rows(width):
        return pl.BlockSpec((tm, width), lambda i: (i, 0))

    def residue_major(d, width):
        return pl.BlockSpec((1, d, tm // d, width),
                            lambda i: (i // tiles_per_seq, 0, i % tiles_per_seq, 0))

    def whole(arr):
        return pl.BlockSpec(arr.shape, lambda i: (0,) * arr.ndim)

    ln2 = ln2_w.reshape(1, D_MODEL)
    return pl.pallas_call(
        _mix_kernel,
        grid=(n_tok // tm,),
        in_specs=[rows(D_MODEL), rows(DA_WIDTH), rows(LANES),
                  residue_major(d1, DA_WIDTH), residue_major(d1, LANES),
                  residue_major(d2, DA_WIDTH), residue_major(d2, LANES),
                  rows(GLA_VALUE_DIM), rows(MERGE_W),
                  whole(wa), whole(wb), whole(wo), whole(ln2), whole(wr_hi), whole(wr_lo), whole(br_pad)],
        out_specs=(rows(D_MODEL), rows(D_MODEL),
                   pl.BlockSpec((1, 2 * TOP_K, tm), lambda i: (i, 0, 0)), rows(LANES), rows(LANES),
                   pl.BlockSpec((1, 8, LANES), lambda i: (i, 0, 0)),
                   pl.BlockSpec((8, LANES), lambda i: (0, 0))),
        out_shape=(jax.ShapeDtypeStruct((n_tok, D_MODEL), F32),
                   jax.ShapeDtypeStruct((n_tok, D_MODEL), BF16),
                   jax.ShapeDtypeStruct((n_tok // tm, 2 * TOP_K, tm), jnp.int32),
                   jax.ShapeDtypeStruct((n_tok, LANES), jnp.int32),
                   jax.ShapeDtypeStruct((n_tok, LANES), BF16),
                   jax.ShapeDtypeStruct((n_tok // tm, 8, LANES), F32),
                   jax.ShapeDtypeStruct((8, LANES), F32)),
        scratch_shapes=[pltpu.VMEM((DA_HEADS, tm, DA_HEAD_DIM), F32), pltpu.VMEM((tm, LANES), F32),
                        pltpu.VMEM((DA_HEADS, tm, DA_HEAD_DIM), F32), pltpu.VMEM((tm, LANES), F32),
                        pltpu.VMEM((8, LANES), F32)],
        compiler_params=pltpu.CompilerParams(
            dimension_semantics=("arbitrary",), vmem_limit_bytes=VMEM_LIMIT),
        name="mix",
    )(x2, o0, l0, o1, l1, o2, l2, o_b, pm, wa, wb, wo, ln2, wr_hi, wr_lo, br_pad)


def _pack_pairs(x):
    n = x.shape[1] // 2
    rounded = x.astype(BF16).astype(F32)
    lo = lax.bitcast_convert_type(rounded[:, :n], U32) >> 16
    hi = lax.bitcast_convert_type(rounded[:, n:], U32) & jnp.uint32(0xFFFF0000)
    return hi | lo


def _unpack_pairs(u):
    lo = lax.bitcast_convert_type(u << 16, F32).astype(BF16)
    hi = lax.bitcast_convert_type(u & jnp.uint32(0xFFFF0000), F32).astype(BF16)
    return lo, hi


def _dispatch_kernel(zstart_ref, nwin_ref, wsrc_ref, wdst_ref, pos_ref, meta_ref, hn_ref, xin_ref,
                     buf_ref, zero_ref, sem, zsem):
    i = pl.program_id(0)
    tm = hn_ref.shape[0]
    n_buf_rows = buf_ref.shape[1]
    slot = i % 2

    def window_copy(step, s):
        src = pl.multiple_of(wsrc_ref[step * N_SLOT + s], ROW_ALIGN)
        dst = pl.multiple_of(wdst_ref[step * N_SLOT + s], ROW_ALIGN)
        return pltpu.make_async_copy(buf_ref.at[step % 2, pl.ds(src, WIN), :],
                                     xin_ref.at[pl.ds(dst, WIN), :], sem)

    def wait_windows(step):
        def body(s, carry):
            window_copy(step, s).wait()
            return carry
        lax.fori_loop(0, nwin_ref[step], body, 0)

    @pl.when(i == 0)
    def _():
        zero_ref[...] = jnp.zeros_like(zero_ref)
        for j in range(zstart_ref.shape[0]):
            @pl.when(zstart_ref[j] >= 0)
            def _():
                start = pl.multiple_of(zstart_ref[j], ZCHUNK)
                cp = pltpu.make_async_copy(zero_ref, xin_ref.at[pl.ds(start, ZCHUNK), :], zsem)
                cp.start()
                cp.wait()

    row = lax.broadcasted_iota(jnp.int32, (n_buf_rows, tm), 0)
    perm = jnp.zeros((n_buf_rows, tm), F32)
    for k in range(TOP_K):
        perm = perm + jnp.where(row == pos_ref[0, k:k + 1, :], 1.0, 0.0)
    perm = perm.astype(BF16)
    buf_ref[slot, :, :D_MODEL // 2] = _pack_pairs(_dot(perm, hn_ref[...]))
    buf_ref[slot, :, D_MODEL // 2:] = lax.bitcast_convert_type(_dot(perm, meta_ref[...]), U32)

    @pl.when(i > 0)
    def _():
        wait_windows(i - 1)

    def issue(s, carry):
        window_copy(i, s).start()
        return carry

    lax.fori_loop(0, nwin_ref[i], issue, 0)

    @pl.when(i == pl.num_programs(0) - 1)
    def _():
        wait_windows(i)


def _dispatch(zero_start, n_win, win_src, win_dst, pos, meta, hn, n_rows):
    n_tok = hn.shape[0]
    tm = TM_ROW
    n_tiles = n_tok // tm
    n_buf_rows = tm * TOP_K + N_EXPERTS * (ROW_ALIGN - 1) + WIN
    return pl.pallas_call(
        _dispatch_kernel,
        grid_spec=pltpu.PrefetchScalarGridSpec(
            num_scalar_prefetch=4,
            grid=(n_tiles,),
            in_specs=[
                pl.BlockSpec((1, TOP_K, tm), lambda i, *_: (i, 0, 0)),
                pl.BlockSpec((tm, LANES), lambda i, *_: (i, 0)),
                pl.BlockSpec((tm, D_MODEL), lambda i, *_: (i, 0)),
            ],
            out_specs=pl.BlockSpec(memory_space=pl.ANY),
            scratch_shapes=[pltpu.VMEM((2, n_buf_rows, ROW_W), U32),
                            pltpu.VMEM((ZCHUNK, ROW_W), U32),
                            pltpu.SemaphoreType.DMA(()), pltpu.SemaphoreType.DMA(())],
        ),
        out_shape=jax.ShapeDtypeStruct((n_rows, ROW_W), U32),
        compiler_params=pltpu.CompilerParams(
            dimension_semantics=("arbitrary",), vmem_limit_bytes=VMEM_LIMIT),
        name="dispatch",
    )(zero_start, n_win, win_src.reshape(-1), win_dst.reshape(-1), pos, meta, hn)


def _expert_kernel(be_ref, nused_ref, x_ref, wu_ref, bu_ref, wd_ref, bd_ref, y_ref, wu16_ref, wd16_ref):
    i = pl.program_id(0)
    used = i < nused_ref[0]

    @pl.when((i == 0) | (be_ref[i] != be_ref[jnp.maximum(i - 1, 0)]))
    def _():
        wu16_ref[...] = wu_ref[0].astype(BF16)
        wd16_ref[...] = wd_ref[0].astype(BF16)

    @pl.when(jnp.logical_not(used))
    def _():
        y_ref[...] = jnp.zeros_like(y_ref)

    @pl.when(used)
    def _():
        half = D_MODEL // 2
        x_lo, x_hi = _unpack_pairs(x_ref[:, :half])
        meta = lax.bitcast_convert_type(x_ref[:, half:half + 3 * TOP_K], F32)
        expert = be_ref[i].astype(F32)
        gate = jnp.zeros((x_ref.shape[0], 1), F32)
        for k in range(TOP_K):
            weight = meta[:, TOP_K + k:TOP_K + k + 1] + meta[:, 2 * TOP_K + k:2 * TOP_K + k + 1]
            gate = gate + jnp.where(meta[:, k:k + 1] == expert, weight, 0.0)
        hu = _dot(x_lo, wu16_ref[:half, :]) + _dot(x_hi, wu16_ref[half:, :]) + bu_ref[0]
        x_glu = jnp.minimum(hu[:, :D_FF], SWIGLU_LIMIT)
        x_lin = jnp.clip(hu[:, D_FF:], -SWIGLU_LIMIT, SWIGLU_LIMIT)
        act = x_glu * _sigmoid(SWIGLU_ALPHA * x_glu) * (x_lin + 1.0)
        y_ref[...] = _pack_pairs((_dot(act.astype(BF16), wd16_ref[...]) + bd_ref[0]) * gate)


def _experts(block_e, n_used, xin, w_up, b_up, w_down, b_down):
    n_rows = xin.shape[0]
    return pl.pallas_call(
        _expert_kernel,
        grid_spec=pltpu.PrefetchScalarGridSpec(
            num_scalar_prefetch=2,
            grid=(n_rows // TB,),
            in_specs=[
                pl.BlockSpec((TB, ROW_W), lambda i, be, nu: (i, 0)),
                pl.BlockSpec((1, D_MODEL, 2 * D_FF), lambda i, be, nu: (be[i], 0, 0)),
                pl.BlockSpec((1, 1, 2 * D_FF), lambda i, be, nu: (be[i], 0, 0)),
                pl.BlockSpec((1, D_FF, D_MODEL), lambda i, be, nu: (be[i], 0, 0)),
                pl.BlockSpec((1, 1, D_MODEL), lambda i, be, nu: (be[i], 0, 0)),
            ],
            out_specs=pl.BlockSpec((TB, D_MODEL // 2), lambda i, be, nu: (i, 0)),
            scratch_shapes=[pltpu.VMEM((D_MODEL, 2 * D_FF), BF16), pltpu.VMEM((D_FF, D_MODEL), BF16)],
        ),
        out_shape=jax.ShapeDtypeStruct((n_rows, D_MODEL // 2), U32),
        compiler_params=pltpu.CompilerParams(
            dimension_semantics=("arbitrary",), vmem_limit_bytes=VMEM_LIMIT),
        name="experts",
    )(block_e, n_used, xin, w_up, b_up.reshape(N_EXPERTS, 1, 2 * D_FF),
      w_down, b_down.reshape(N_EXPERTS, 1, D_MODEL))


def _combine_kernel(nwin_ref, wdst_ref, ric_ref, delta_ref, yb_ref, h1_ref, lnf_ref, o_ref, buf_ref, sem):
    i = pl.program_id(0)
    tm = h1_ref.shape[0]
    n_buf_rows = buf_ref.shape[0]

    @pl.when(i == 0)
    def _():
        buf_ref[...] = jnp.zeros_like(buf_ref)

    def window_copy(s):
        dst = pl.multiple_of(wdst_ref[i * N_SLOT + s], ROW_ALIGN)
        return pltpu.make_async_copy(yb_ref.at[pl.ds(dst, WIN), :],
                                     buf_ref.at[pl.ds(pl.multiple_of(s * WIN, WIN), WIN), :], sem)

    def issue(s, carry):
        window_copy(s).start()
        return carry

    lax.fori_loop(0, nwin_ref[i], issue, 0)

    ric = ric_ref[...].astype(F32)
    lane = lax.broadcasted_iota(jnp.int32, (tm, LANES), 1).astype(F32)
    col = lax.broadcasted_iota(jnp.int32, (tm, n_buf_rows), 1)
    pick = jnp.zeros((tm, n_buf_rows), F32)
    for k in range(TOP_K):
        offset = jnp.sum(jnp.where(lane == ric[:, k:k + 1], delta_ref[0], 0.0), axis=-1, keepdims=True)
        pos = (ric[:, TOP_K + k:TOP_K + k + 1] + offset).astype(jnp.int32)
        pick = pick + jnp.where(col == pos, 1.0, 0.0)
    pick = pick.astype(BF16)

    def drain(s, carry):
        window_copy(s).wait()
        return carry

    lax.fori_loop(0, nwin_ref[i], drain, 0)

    y_lo, y_hi = _unpack_pairs(buf_ref[...])
    h2 = h1_ref[...] + jnp.concatenate([_dot(pick, y_lo), _dot(pick, y_hi)], axis=-1)
    o_ref[...] = h2 * _rms_scale(h2) * lnf_ref[...]


def _combine(n_win, win_dst, ric, delta, yb, h1, lnf_w):
    n_tok = h1.shape[0]
    tm = TM_ROW
    n_tiles = n_tok // tm
    return pl.pallas_call(
        _combine_kernel,
        grid_spec=pltpu.PrefetchScalarGridSpec(
            num_scalar_prefetch=2,
            grid=(n_tiles,),
            in_specs=[
                pl.BlockSpec((tm, LANES), lambda i, *_: (i, 0)),
                pl.BlockSpec((1, 1, LANES), lambda i, *_: (i, 0, 0)),
                pl.BlockSpec(memory_space=pl.ANY),
                pl.BlockSpec((tm, D_MODEL), lambda i, *_: (i, 0)),
                pl.BlockSpec((1, D_MODEL), lambda i, *_: (0, 0)),
            ],
            out_specs=pl.BlockSpec((tm, D_MODEL), lambda i, *_: (i, 0)),
            scratch_shapes=[pltpu.VMEM((N_SLOT * WIN, D_MODEL // 2), U32), pltpu.SemaphoreType.DMA(())],
        ),
        out_shape=jax.ShapeDtypeStruct((n_tok, D_MODEL), F32),
        compiler_params=pltpu.CompilerParams(
            dimension_semantics=("arbitrary",), vmem_limit_bytes=VMEM_LIMIT),
        name="combine",
    )(n_win, win_dst.reshape(-1), ric, delta, yb, h1, lnf_w.reshape(1, D_MODEL))


def _layer(h, ln1_w, w_in, w_gk2, b_gk2, gla_norm_w, w_proj_a, w_proj_b, w_out,
           ln2_w, w_router, b_router, w_up, b_up, w_down, b_down, lnf_w):
    bsz, seq, _ = h.shape
    n_tok = bsz * seq

    n_main = 3 * QKV_W + 2 * GLA_KEY_DIM + 2 * GLA_VALUE_DIM
    pad = LANES - GLA_GATE_RANK
    w_all = jnp.concatenate(
        [w_in[:, :n_main + GLA_GATE_RANK], jnp.zeros((D_MODEL, pad), F32), w_in[:, n_main + GLA_GATE_RANK:]],
        axis=1).astype(BF16)
    w2_pad = jnp.concatenate([w_gk2, jnp.zeros((pad, GLA_KEY_DIM), F32)], axis=0).astype(BF16)
    wr_pad = jnp.concatenate([w_router, jnp.zeros((D_MODEL, LANES - N_EXPERTS), F32)], axis=1)
    wr_hi = wr_pad.astype(BF16)
    wr_lo = (wr_pad - wr_hi.astype(F32)).astype(BF16)
    br_pad = jnp.concatenate([b_router, jnp.zeros((LANES - N_EXPERTS,), F32)]).reshape(1, LANES)

    pa0, pa1, pa2, pg, pm = _inproj(h, ln1_w, w_all)
    o0, l0 = _dil_attn(pa0)
    o1, l1 = _dil_attn(pa1)
    o2, l2 = _dil_attn(pa2)
    o_b = _gla(pg, w2_pad, b_gk2, gla_norm_w)

    h1, hn, ri, ric, meta, carry_f, cnt = _mix(
        h.reshape(n_tok, D_MODEL), o0.reshape(n_tok, DA_WIDTH), l0.reshape(n_tok, LANES),
        o1, l1, o2, l2, o_b.reshape(n_tok, GLA_VALUE_DIM), pm.reshape(n_tok, MERGE_W),
        w_proj_a.astype(BF16), w_proj_b.astype(BF16), w_out.astype(BF16), ln2_w, wr_hi, wr_lo, br_pad)

    i32 = jnp.int32
    n_tiles = n_tok // TM_ROW
    experts = jnp.arange(N_EXPERTS)
    counts = cnt[0, :N_EXPERTS].astype(i32)
    before = carry_f[:, 0, :N_EXPERTS].astype(i32)
    run = jnp.concatenate([before[1:], counts[None]], axis=0) - before
    run_al = (run + ROW_ALIGN - 1) // ROW_ALIGN * ROW_ALIGN
    rows_end = jnp.cumsum(run_al, axis=0)
    rows_before = rows_end - run_al
    used = rows_end[-1]
    slack = WIN - ROW_ALIGN
    padded = (used + slack + TB - 1) // TB * TB
    pad_end = jnp.cumsum(padded)
    pad_start = pad_end - padded
    n_asg = n_tok * TOP_K
    n_rows = (n_asg + n_tiles * N_EXPERTS * (ROW_ALIGN - 1)
              + N_EXPERTS * (slack + TB - 1) + TB - 1) // TB * TB
    n_blocks = n_rows // TB
    block_starts = jnp.arange(n_blocks) * TB
    block_e = jnp.minimum(jnp.sum(pad_end[None, :] <= block_starts[:, None], axis=1),
                          N_EXPERTS - 1).astype(i32)
    n_used = (pad_end[-1:] // TB).astype(i32)

    run_end = jnp.cumsum(run_al, axis=1)
    run_start = run_end - run_al
    wins = (run + WIN - 1) // WIN
    wins_end = jnp.cumsum(wins, axis=1)
    wins_start = wins_end - wins
    n_win = wins_end[:, -1].astype(i32)
    slots = jnp.arange(N_SLOT)
    slot_e = jnp.sum(wins_end[:, None, :] <= slots[None, :, None], axis=-1)
    slot_is = slot_e[..., None] == experts

    def of_slot(table):
        return jnp.sum(jnp.where(slot_is, table[:, None, :], 0), axis=-1)

    win_off = (slots[None, :] - of_slot(wins_start)) * WIN
    win_src = (of_slot(run_start) + win_off).astype(i32)
    win_dst = (of_slot(pad_start[None, :] + rows_before) + win_off).astype(i32)
    asg_is = ri[:, :TOP_K, :, None] == experts
    pos = ri[:, TOP_K:, :] + jnp.sum(jnp.where(asg_is, (run_start - before)[:, None, None, :], 0), axis=-1)
    delta = jnp.pad((wins_start * WIN - before).astype(F32), ((0, 0), (0, LANES - N_EXPERTS)))

    chunk_back = ZCHUNK * (1 + jnp.arange((slack + TB - 1 + ZCHUNK - 1) // ZCHUNK + 1))
    region_chunks = pad_end[:, None] - chunk_back[None, :]
    region_ok = (region_chunks >= pad_start[:, None]) & (region_chunks + ZCHUNK > (pad_start + used)[:, None])
    tail_chunks = pad_end[-1] + ZCHUNK * jnp.arange((n_rows - n_asg) // ZCHUNK)
    zero_start = jnp.concatenate([jnp.where(region_ok, region_chunks, -1).reshape(-1),
                                  jnp.where(tail_chunks < n_rows, tail_chunks, -1)]).astype(i32)

    xin = _dispatch(zero_start, n_win, win_src, win_dst, pos.astype(i32), meta, hn, n_rows)
    yb = _experts(block_e, n_used, xin, w_up, b_up, w_down, b_down)
    out = _combine(n_win, win_dst, ric, delta.reshape(-1, 1, LANES), yb, h1, lnf_w)
    return out.reshape(bsz, seq, D_MODEL)


def kernel(x, ln1_w, w_in, w_gk2, b_gk2, gla_norm_w, w_proj_a, w_proj_b, w_out, ln2_w, w_router,
           b_router, w_up, b_up, w_down, b_down, lnf_w):
    assert x.shape[-1] == D_MODEL and ln1_w.shape[0] == 1, "one layer of width D_MODEL"
    return _layer(x, ln1_w[0], w_in[0], w_gk2[0], b_gk2[0], gla_norm_w[0], w_proj_a[0], w_proj_b[0],
                  w_out[0], ln2_w[0], w_router[0], b_router[0], w_up[0], b_up[0], w_down[0],
                  b_down[0], lnf_w)
```

```python
import functools

import jax
import jax.numpy as jnp
from jax import lax
from jax.experimental import pallas as pl
from jax.experimental.pallas import tpu as pltpu

F32 = jnp.float32
BF16 = jnp.bfloat16
U32 = jnp.uint32

D_MODEL = 1024
DA_GROUPS = ((128, 1), (512, 4), (2048, 16))
DA_HEADS = 4
DA_HEAD_DIM = 128
DA_WIDTH = DA_HEADS * DA_HEAD_DIM
DA_BLOCK = 128
GLA_HEADS = 4
GLA_KEY_DIM = D_MODEL // 2
GLA_VALUE_DIM = D_MODEL
GLA_DK = GLA_KEY_DIM // GLA_HEADS
GLA_DV = GLA_VALUE_DIM // GLA_HEADS
GLA_GATE_RANK = 16
GLA_GATE_NORMALIZER = 16.0
GLA_CHUNK = 64
N_EXPERTS = 32
TOP_K = 4
D_FF = D_MODEL
SWIGLU_ALPHA = 1.702
SWIGLU_LIMIT = 7.0
RMS_EPS = 1e-5
NEG_INF = -1e30

LANES = 128
QKV_W = 3 * DA_WIDTH
GLA_W = 2 * GLA_KEY_DIM + 2 * GLA_VALUE_DIM + LANES
MERGE_W = 2 * D_MODEL

DA_QB = 4
LSE_LANES = LANES // DA_HEADS
TM_IN = 512
N_CHUNK = 512
T_GLA = 512
GLA_SUB = 256
TM_MIX = 256
TB = 512
TM_ROW = TM_MIX
ROW_ALIGN = 8
WIN = 32
N_SLOT = N_EXPERTS + TM_ROW * TOP_K // WIN
SORT_ROWS = TM_ROW * TOP_K + N_EXPERTS * (ROW_ALIGN - 1) + WIN
ZCHUNK = 256
ROW_W = D_MODEL // 2 + LANES
VMEM_LIMIT = 56 * 1024 * 1024
VMEM_LIMIT_INPROJ = 62 * 1024 * 1024

_NT = (((1,), (1,)), ((), ()))
_TN = (((0,), (0,)), ((), ()))


def _dot(a, b):
    return jnp.dot(a, b, preferred_element_type=F32)


def _sigmoid(x):
    return 1.0 / (1.0 + jnp.exp(-x))


def _rms_scale(x):
    return lax.rsqrt(jnp.mean(x * x, axis=-1, keepdims=True) + RMS_EPS)


def _inproj_kernel(x_ref, ln_ref, w_hbm, pa0_ref, pa1_ref, pa2_ref, pg_ref, pm_ref,
                   w_ref, xs_ref, xn_ref, wsem):
    tm = x_ref.shape[1]
    n_slab = D_MODEL // LANES

    @pl.when((pl.program_id(0) == 0) & (pl.program_id(1) == 0))
    def _():
        cp = pltpu.make_async_copy(w_hbm, w_ref, wsem)
        cp.start()
        cp.wait()

    def project(out_write, col0, width, post=None):
        for c0 in range(0, width, N_CHUNK):
            cw = min(N_CHUNK, width - c0)
            val = _dot(xs_ref[...], w_ref[:, col0 + c0:col0 + c0 + cw])
            out_write(c0, cw, (val if post is None else post(val)).astype(BF16))

    x = x_ref[0]
    xn = x * _rms_scale(x) * ln_ref[...]
    xs_ref[...] = xn.astype(BF16)
    for j in range(n_slab):
        xn_ref[j] = xn[:, j * LANES:(j + 1) * LANES]

    def write_to(ref):
        def write(c0, cw, val):
            ref[0, :, c0:c0 + cw] = val
        return write

    def write_qkv(out_ref, d, n):
        def write(c0, cw, val):
            for r in range(d):
                out_ref[0, r, c0 // DA_WIDTH] = val[r * n:(r + 1) * n]
        return write

    project(write_qkv(pa0_ref, 1, tm), 0, QKV_W)
    project(write_to(pg_ref), 3 * QKV_W, GLA_W)
    project(write_to(pm_ref), 3 * QKV_W + GLA_W, MERGE_W, post=_sigmoid)

    for gi, out_ref in ((1, pa1_ref), (2, pa2_ref)):
        d = DA_GROUPS[gi][1]
        n = tm // d
        for r in range(d):
            for j in range(n_slab):
                xs_ref[r * n:(r + 1) * n, j * LANES:(j + 1) * LANES] = (
                    xn_ref[j, pl.ds(r, n, stride=d), :].astype(BF16))
        project(write_qkv(out_ref, d, n), gi * QKV_W, QKV_W)


def _inproj(x, ln1_w, w_all):
    assert N_CHUNK == DA_WIDTH
    bsz, seq, _ = x.shape
    tm = TM_IN
    d1, d2 = DA_GROUPS[1][1], DA_GROUPS[2][1]

    def qkv_shape(d):
        return jax.ShapeDtypeStruct((bsz, d, 3, seq // d, DA_WIDTH), BF16)

    def qkv_spec(d):
        return pl.BlockSpec((1, d, 3, tm // d, DA_WIDTH), lambda b, i: (b, 0, 0, i, 0))

    return pl.pallas_call(
        _inproj_kernel,
        grid=(bsz, seq // tm),
        in_specs=[
            pl.BlockSpec((1, tm, D_MODEL), lambda b, i: (b, i, 0)),
            pl.BlockSpec((1, D_MODEL), lambda b, i: (0, 0)),
            pl.BlockSpec(memory_space=pl.ANY),
        ],
        out_specs=(
            qkv_spec(1), qkv_spec(d1), qkv_spec(d2),
            pl.BlockSpec((1, tm, GLA_W), lambda b, i: (b, i, 0)),
            pl.BlockSpec((1, tm, MERGE_W), lambda b, i: (b, i, 0)),
        ),
        out_shape=(qkv_shape(1), qkv_shape(d1), qkv_shape(d2),
                   jax.ShapeDtypeStruct((bsz, seq, GLA_W), BF16),
                   jax.ShapeDtypeStruct((bsz, seq, MERGE_W), BF16)),
        scratch_shapes=[pltpu.VMEM(w_all.shape, BF16),
                        pltpu.VMEM((tm, D_MODEL), BF16),
                        pltpu.VMEM((D_MODEL // LANES, tm, LANES), F32),
                        pltpu.SemaphoreType.DMA(())],
        compiler_params=pltpu.CompilerParams(
            dimension_semantics=("arbitrary", "arbitrary"), vmem_limit_bytes=VMEM_LIMIT_INPROJ),
        name="inproj",
    )(x, ln1_w.reshape(1, D_MODEL), w_all)


def _dil_attn_kernel(q_ref, kp_ref, kc_ref, vp_ref, vc_ref, o_ref, l_ref, s_ref, p_ref, r_ref):
    n = pl.program_id(2)
    blk = DA_BLOCK
    qi = lax.broadcasted_iota(jnp.int32, (blk, 2 * blk), 0)
    kj = lax.broadcasted_iota(jnp.int32, (blk, 2 * blk), 1)
    band = (kj >= qi) & (kj <= qi + blk)
    band_first = (kj >= jnp.where(n > 0, qi, blk)) & (kj <= qi + blk)
    scale = DA_HEAD_DIM ** -0.5
    items = [(b, h) for b in range(DA_QB) for h in range(DA_HEADS)]

    def rows(b):
        return slice(b * blk, (b + 1) * blk)

    def cols(h):
        return slice(h * DA_HEAD_DIM, (h + 1) * DA_HEAD_DIM)

    def window(prev_ref, cur_ref, b, h):
        before = prev_ref[0, 0, 0, :, cols(h)] if b == 0 else cur_ref[0, 0, 0, rows(b - 1), cols(h)]
        return jnp.concatenate([before, cur_ref[0, 0, 0, rows(b), cols(h)]], axis=0)

    for i, (b, h) in enumerate(items):
        s = lax.dot_general(q_ref[0, 0, 0, rows(b), cols(h)], window(kp_ref, kc_ref, b, h), _NT,
                            preferred_element_type=F32) * scale
        s_ref[i] = jnp.where(band_first if b == 0 else band, s, NEG_INF)
    for i, (b, h) in enumerate(items):
        s = s_ref[i]
        m = jnp.max(s, axis=-1, keepdims=True)
        p = jnp.exp(s - m)
        l = jnp.sum(p, axis=-1, keepdims=True)
        p_ref[i] = p.astype(BF16)
        r_ref[i] = jnp.broadcast_to(1.0 / l, (blk, DA_HEAD_DIM))
        l_ref[0, 0, rows(b), h * LSE_LANES:(h + 1) * LSE_LANES] = jnp.broadcast_to(
            m + jnp.log(l), (blk, LSE_LANES))
    for i, (b, h) in enumerate(items):
        acc = _dot(p_ref[i], window(vp_ref, vc_ref, b, h))
        o_ref[0, 0, rows(b), cols(h)] = (acc * r_ref[i]).astype(o_ref.dtype)


def _dil_attn(pa):
    bsz, d, _, sub_len, _ = pa.shape
    rows = DA_QB * DA_BLOCK
    n_items = DA_QB * DA_HEADS

    def cur(sec):
        return pl.BlockSpec((1, 1, 1, rows, DA_WIDTH), lambda b, r, n: (b, r, sec, n, 0))

    def prev(sec):
        return pl.BlockSpec((1, 1, 1, DA_BLOCK, DA_WIDTH),
                            lambda b, r, n: (b, r, sec, jnp.maximum(n * DA_QB - 1, 0), 0))

    def out(width):
        return pl.BlockSpec((1, 1, rows, width), lambda b, r, n: (b, r, n, 0))

    return pl.pallas_call(
        _dil_attn_kernel,
        grid=(bsz, d, sub_len // rows),
        in_specs=[cur(0), prev(1), cur(1), prev(2), cur(2)],
        out_specs=(out(DA_WIDTH), out(LANES)),
        out_shape=(jax.ShapeDtypeStruct((bsz, d, sub_len, DA_WIDTH), BF16),
                   jax.ShapeDtypeStruct((bsz, d, sub_len, LANES), F32)),
        scratch_shapes=[pltpu.VMEM((n_items, DA_BLOCK, 2 * DA_BLOCK), F32),
                        pltpu.VMEM((n_items, DA_BLOCK, 2 * DA_BLOCK), BF16),
                        pltpu.VMEM((n_items, DA_BLOCK, DA_HEAD_DIM), F32)],
        compiler_params=pltpu.CompilerParams(
            dimension_semantics=("parallel", "parallel", "arbitrary"), vmem_limit_bytes=VMEM_LIMIT),
        name=f"dil_attn_d{d}",
    )(pa, pa, pa, pa, pa)


def _gla_kernel(q_ref, k_ref, v_ref, og_ref, lr_ref, w2_ref, b2_ref, nw_ref, o_ref, st_ref, mask_ref, keep_ref):
    t = pl.program_id(2)
    c = GLA_CHUNK
    tt = q_ref.shape[1]
    n_c = tt // c

    sub = mask_ref.shape[0]

    @pl.when(t == 0)
    def _():
        st_ref[...] = jnp.zeros_like(st_ref)
        row = lax.broadcasted_iota(jnp.int32, (sub, sub), 0)
        col = lax.broadcasted_iota(jnp.int32, (sub, sub), 1)
        keep = (col <= row) & (col >= row - row % c)
        keep_ref[...] = jnp.where(keep, 1.0, 0.0)
        mask_ref[...] = jnp.where(keep, 1.0, 0.0).astype(BF16)

    mask = mask_ref[...]
    gpre = _dot(lr_ref[0], w2_ref[...]) + b2_ref[...]
    forget = (jnp.minimum(gpre, 0.0) - jnp.log(1.0 + jnp.exp(-jnp.abs(gpre)))) / GLA_GATE_NORMALIZER
    g_hi = forget.astype(BF16)
    g_lo = (forget - g_hi.astype(F32)).astype(BF16)
    g_cat = jnp.concatenate([g_hi, g_lo], axis=-1)
    csum = jnp.concatenate([_dot(mask, g_cat[s0:s0 + sub]) for s0 in range(0, tt, sub)], axis=0)
    b = csum[:, :GLA_DK] + csum[:, GLA_DK:]
    b_last = b.reshape(n_c, c, GLA_DK)[:, c - 1:c, :]
    b_to_end = (b_last - b.reshape(n_c, c, GLA_DK)).reshape(tt, GLA_DK)
    q = q_ref[0].astype(F32)
    k = k_ref[0].astype(F32)
    v = v_ref[0]
    q_e = (q * ((GLA_DK ** -0.5) * jnp.exp(b))).astype(BF16)
    k_e = (k * jnp.exp(-b)).astype(BF16)
    k_end = (k * jnp.exp(b_to_end)).astype(BF16)
    o_intra = []
    for s0 in range(0, tt, sub):
        ss = slice(s0, s0 + sub)
        att = lax.dot_general(q_e[ss], k_e[ss], _NT, preferred_element_type=F32)
        att = jnp.where(keep_ref[...] > 0.0, att, 0.0).astype(BF16)
        o_intra.append(_dot(att, v[ss]))
    o_intra = jnp.concatenate(o_intra, axis=0)
    decay = jnp.exp(b_last.reshape(n_c, GLA_DK).T)
    st = st_ref[...]
    outs = []
    for ci in range(n_c):
        rs = slice(ci * c, (ci + 1) * c)
        outs.append(o_intra[rs] + _dot(q_e[rs], st.astype(BF16)))
        st = decay[:, ci:ci + 1] * st + lax.dot_general(k_end[rs], v[rs], _TN, preferred_element_type=F32)
    st_ref[...] = st
    o = jnp.concatenate(outs, axis=0)
    o = o * _rms_scale(o) * nw_ref[...]
    gate = og_ref[0].astype(F32)
    o_ref[0] = (o * (gate * _sigmoid(gate))).astype(BF16)


def _gla(pg, w2_pad, b_gk2, gla_norm_w):
    bsz, seq, _ = pg.shape
    t = T_GLA
    kq = GLA_KEY_DIM // GLA_DK
    kv = 2 * GLA_KEY_DIM // GLA_DV
    kg = kv + GLA_VALUE_DIM // GLA_DV
    klr = (2 * GLA_KEY_DIM + 2 * GLA_VALUE_DIM) // LANES
    return pl.pallas_call(
        _gla_kernel,
        grid=(bsz, GLA_HEADS, seq // t),
        in_specs=[
            pl.BlockSpec((1, t, GLA_DK), lambda b, h, i: (b, i, h)),
            pl.BlockSpec((1, t, GLA_DK), lambda b, h, i: (b, i, kq + h)),
            pl.BlockSpec((1, t, GLA_DV), lambda b, h, i: (b, i, kv + h)),
            pl.BlockSpec((1, t, GLA_DV), lambda b, h, i: (b, i, kg + h)),
            pl.BlockSpec((1, t, LANES), lambda b, h, i: (b, i, klr)),
            pl.BlockSpec((LANES, GLA_DK), lambda b, h, i: (0, h)),
            pl.BlockSpec((1, GLA_DK), lambda b, h, i: (0, h)),
            pl.BlockSpec((1, GLA_DV), lambda b, h, i: (0, 0)),
        ],
        out_specs=pl.BlockSpec((1, t, GLA_DV), lambda b, h, i: (b, i, h)),
        out_shape=jax.ShapeDtypeStruct((bsz, seq, GLA_VALUE_DIM), BF16),
        scratch_shapes=[pltpu.VMEM((GLA_DK, GLA_DV), F32), pltpu.VMEM((GLA_SUB, GLA_SUB), BF16),
                        pltpu.VMEM((GLA_SUB, GLA_SUB), F32)],
        compiler_params=pltpu.CompilerParams(
            dimension_semantics=("parallel", "parallel", "arbitrary"), vmem_limit_bytes=VMEM_LIMIT),
        name="gla",
    )(pg, pg, pg, pg, pg, w2_pad, b_gk2.reshape(1, GLA_KEY_DIM), gla_norm_w.reshape(1, GLA_DV))


def _mix_kernel(x_ref, o0_ref, l0_ref, o1_ref, l1_ref, o2_ref, l2_ref, ob_ref, pm_ref,
                wa_ref, wb_ref, wo_ref, ln2_ref, wrh_ref, wrl_ref, br_ref,
                h1_ref, hn_ref, ri_ref, ric_ref, meta_ref, cbefore_ref, cnt_ref,
                po1_ref, pl1_ref, po2_ref, pl2_ref, carry_ref):
    step = pl.program_id(0)
    tm = x_ref.shape[0]

    @pl.when(step == 0)
    def _():
        carry_ref[...] = jnp.zeros_like(carry_ref)

    for o_ref, l_ref, po_ref, pl_ref, (_, d) in ((o1_ref, l1_ref, po1_ref, pl1_ref, DA_GROUPS[1]),
                                                 (o2_ref, l2_ref, po2_ref, pl2_ref, DA_GROUPS[2])):
        n = tm // d
        for r in range(d):
            pl_ref[pl.ds(r, n, stride=d), :] = l_ref[0, r]
            for h in range(DA_HEADS):
                sl = slice(h * DA_HEAD_DIM, (h + 1) * DA_HEAD_DIM)
                po_ref[h, pl.ds(r, n, stride=d), :] = o_ref[0, r, :, sl].astype(F32)

    l0, l1, l2 = l0_ref[...], pl1_ref[...], pl2_ref[...]
    mx = jnp.maximum(jnp.maximum(l0, l1), l2)
    e0, e1, e2 = jnp.exp(l0 - mx), jnp.exp(l1 - mx), jnp.exp(l2 - mx)
    inv = 1.0 / (e0 + e1 + e2)
    w0, w1, w2 = e0 * inv, e1 * inv, e2 * inv
    heads = []
    for h in range(DA_HEADS):
        sl = slice(h * DA_HEAD_DIM, (h + 1) * DA_HEAD_DIM)
        at = slice(h * LSE_LANES, h * LSE_LANES + 1)
        o_h = w0[:, at] * o0_ref[:, sl].astype(F32) + w1[:, at] * po1_ref[h] + w2[:, at] * po2_ref[h]
        heads.append(o_h.astype(BF16))
    o_a = jnp.concatenate(heads, axis=-1)

    gates = pm_ref[...].astype(F32)
    mixed = (gates[:, :D_MODEL] * _dot(o_a, wa_ref[...])
             + gates[:, D_MODEL:] * _dot(ob_ref[...], wb_ref[...]))
    h1 = x_ref[...] + _dot(mixed.astype(BF16), wo_ref[...])
    h1_ref[...] = h1
    hn = h1 * _rms_scale(h1) * ln2_ref[...]

    hn_hi = hn.astype(BF16)
    hn_ref[...] = hn_hi
    hn_lo = (hn - hn_hi.astype(F32)).astype(BF16)
    logits = (_dot(hn_hi, wrh_ref[...]) + _dot(hn_lo, wrh_ref[...]) + _dot(hn_hi, wrl_ref[...])
              + br_ref[...])
    lane = lax.broadcasted_iota(jnp.int32, (tm, LANES), 1).astype(F32)
    work = jnp.where(lane < N_EXPERTS, logits, -jnp.inf)
    vals, idxs = [], []
    for _ in range(TOP_K):
        m = jnp.max(work, axis=-1, keepdims=True)
        idx = jnp.min(jnp.where(work == m, lane, float(LANES)), axis=-1, keepdims=True)
        vals.append(m)
        idxs.append(idx)
        work = jnp.where(lane == idx, -jnp.inf, work)
    exps = [jnp.exp(v - vals[0]) for v in vals]
    denom = exps[0] + exps[1] + exps[2] + exps[3]

    onehot = jnp.zeros((tm, LANES), F32)
    for idx in idxs:
        onehot = onehot + jnp.where(lane == idx, 1.0, 0.0)
    row = lax.broadcasted_iota(jnp.int32, (tm, tm), 0)
    col = lax.broadcasted_iota(jnp.int32, (tm, tm), 1)
    below = jnp.where(col < row, 1.0, 0.0).astype(BF16)
    before = _dot(below, onehot.astype(BF16)) + carry_ref[0:1, :]
    ranks = [jnp.sum(jnp.where(lane == idx, before, 0.0), axis=-1, keepdims=True) for idx in idxs]
    cbefore_ref[0] = carry_ref[...]
    carry = carry_ref[0:1, :] + jnp.sum(onehot, axis=0, keepdims=True)
    carry_ref[...] = jnp.broadcast_to(carry, carry_ref.shape)
    cnt_ref[...] = jnp.broadcast_to(carry, cnt_ref.shape)

    ri = jnp.zeros((tm, LANES), F32)
    for j, val in enumerate(idxs + ranks):
        ri = jnp.where(lane == float(j), val, ri)
    ric_ref[...] = ri.astype(jnp.int32)
    ri_ref[0] = ri.T[:2 * TOP_K].astype(jnp.int32)
    meta = jnp.zeros((tm, LANES), F32)
    for k in range(TOP_K):
        gate = exps[k] / denom
        gate_hi = gate.astype(BF16).astype(F32)
        meta = jnp.where(lane == float(k), idxs[k], meta)
        meta = jnp.where(lane == float(TOP_K + k), gate_hi, meta)
        meta = jnp.where(lane == float(2 * TOP_K + k), gate - gate_hi, meta)
    meta_ref[...] = meta.astype(BF16)


def _mix(x2, o0, l0, o1, l1, o2, l2, o_b, pm, wa, wb, wo, ln2_w, wr_hi, wr_lo, br_pad):
    n_tok = x2.shape[0]
    tm = TM_MIX
    bsz = o1.shape[0]
    d1, d2 = DA_GROUPS[1][1], DA_GROUPS[2][1]
    tiles_per_seq = (n_tok // bsz) // tm

    def rows(width):
        return pl.BlockSpec((tm, width), lambda i: (i, 0))

    def residue_major(d, width):
        return pl.BlockSpec((1, d, tm // d, width),
                            lambda i: (i // tiles_per_seq, 0, i % tiles_per_seq, 0))

    def whole(arr):
        return pl.BlockSpec(arr.shape, lambda i: (0,) * arr.ndim)

    ln2 = ln2_w.reshape(1, D_MODEL)
    return pl.pallas_call(
        _mix_kernel,
        grid=(n_tok // tm,),
        in_specs=[rows(D_MODEL), rows(DA_WIDTH), rows(LANES),
                  residue_major(d1, DA_WIDTH), residue_major(d1, LANES),
                  residue_major(d2, DA_WIDTH), residue_major(d2, LANES),
                  rows(GLA_VALUE_DIM), rows(MERGE_W),
                  whole(wa), whole(wb), whole(wo), whole(ln2), whole(wr_hi), whole(wr_lo), whole(br_pad)],
        out_specs=(rows(D_MODEL), rows(D_MODEL),
                   pl.BlockSpec((1, 2 * TOP_K, tm), lambda i: (i, 0, 0)), rows(LANES), rows(LANES),
                   pl.BlockSpec((1, 8, LANES), lambda i: (i, 0, 0)),
                   pl.BlockSpec((8, LANES), lambda i: (0, 0))),
        out_shape=(jax.ShapeDtypeStruct((n_tok, D_MODEL), F32),
                   jax.ShapeDtypeStruct((n_tok, D_MODEL), BF16),
                   jax.ShapeDtypeStruct((n_tok // tm, 2 * TOP_K, tm), jnp.int32),
                   jax.ShapeDtypeStruct((n_tok, LANES), jnp.int32),
                   jax.ShapeDtypeStruct((n_tok, LANES), BF16),
                   jax.ShapeDtypeStruct((n_tok // tm, 8, LANES), F32),
                   jax.ShapeDtypeStruct((8, LANES), F32)),
        scratch_shapes=[pltpu.VMEM((DA_HEADS, tm, DA_HEAD_DIM), F32), pltpu.VMEM((tm, LANES), F32),
                        pltpu.VMEM((DA_HEADS, tm, DA_HEAD_DIM), F32), pltpu.VMEM((tm, LANES), F32),
                        pltpu.VMEM((8, LANES), F32)],
        compiler_params=pltpu.CompilerParams(
            dimension_semantics=("arbitrary",), vmem_limit_bytes=VMEM_LIMIT),
        name="mix",
    )(x2, o0, l0, o1, l1, o2, l2, o_b, pm, wa, wb, wo, ln2, wr_hi, wr_lo, br_pad)


def _pack_pairs(x):
    n = x.shape[1] // 2
    rounded = x.astype(BF16).astype(F32)
    lo = lax.bitcast_convert_type(rounded[:, :n], U32) >> 16
    hi = lax.bitcast_convert_type(rounded[:, n:], U32) & jnp.uint32(0xFFFF0000)
    return hi | lo


def _unpack_pairs(u):
    lo = lax.bitcast_convert_type(u << 16, F32).astype(BF16)
    hi = lax.bitcast_convert_type(u & jnp.uint32(0xFFFF0000), F32).astype(BF16)
    return lo, hi


def _dispatch_kernel(zstart_ref, nwin_ref, wsrc_ref, wdst_ref, pos_ref, meta_ref, hn_ref, xin_ref,
                     buf_ref, zero_ref, sem, zsem):
    i = pl.program_id(0)
    tm = hn_ref.shape[0]
    n_buf_rows = buf_ref.shape[1]
    slot = i % 2

    def window_copy(step, s):
        src = pl.multiple_of(wsrc_ref[step * N_SLOT + s], ROW_ALIGN)
        dst = pl.multiple_of(wdst_ref[step * N_SLOT + s], ROW_ALIGN)
        return pltpu.make_async_copy(buf_ref.at[step % 2, pl.ds(src, WIN), :],
                                     xin_ref.at[pl.ds(dst, WIN), :], sem)

    def wait_windows(step):
        def body(s, carry):
            window_copy(step, s).wait()
            return carry
        lax.fori_loop(0, nwin_ref[step], body, 0)

    @pl.when(i == 0)
    def _():
        zero_ref[...] = jnp.zeros_like(zero_ref)
        for j in range(zstart_ref.shape[0]):
            @pl.when(zstart_ref[j] >= 0)
            def _():
                start = pl.multiple_of(zstart_ref[j], ZCHUNK)
                cp = pltpu.make_async_copy(zero_ref, xin_ref.at[pl.ds(start, ZCHUNK), :], zsem)
                cp.start()
                cp.wait()

    row = lax.broadcasted_iota(jnp.int32, (n_buf_rows, tm), 0)
    perm = jnp.zeros((n_buf_rows, tm), F32)
    for k in range(TOP_K):
        perm = perm + jnp.where(row == pos_ref[0, k:k + 1, :], 1.0, 0.0)
    perm = perm.astype(BF16)
    buf_ref[slot, :, :D_MODEL // 2] = _pack_pairs(_dot(perm, hn_ref[...]))
    buf_ref[slot, :, D_MODEL // 2:] = lax.bitcast_convert_type(_dot(perm, meta_ref[...]), U32)

    @pl.when(i > 0)
    def _():
        wait_windows(i - 1)

    def issue(s, carry):
        window_copy(i, s).start()
        return carry

    lax.fori_loop(0, nwin_ref[i], issue, 0)

    @pl.when(i == pl.num_programs(0) - 1)
    def _():
        wait_windows(i)


def _dispatch(zero_start, n_win, win_src, win_dst, pos, meta, hn, n_rows):
    n_tok = hn.shape[0]
    tm = TM_ROW
    n_tiles = n_tok // tm
    n_buf_rows = SORT_ROWS
    return pl.pallas_call(
        _dispatch_kernel,
        grid_spec=pltpu.PrefetchScalarGridSpec(
            num_scalar_prefetch=4,
            grid=(n_tiles,),
            in_specs=[
                pl.BlockSpec((1, TOP_K, tm), lambda i, *_: (i, 0, 0)),
                pl.BlockSpec((tm, LANES), lambda i, *_: (i, 0)),
                pl.BlockSpec((tm, D_MODEL), lambda i, *_: (i, 0)),
            ],
            out_specs=pl.BlockSpec(memory_space=pl.ANY),
            scratch_shapes=[pltpu.VMEM((2, n_buf_rows, ROW_W), U32),
                            pltpu.VMEM((ZCHUNK, ROW_W), U32),
                            pltpu.SemaphoreType.DMA(()), pltpu.SemaphoreType.DMA(())],
        ),
        out_shape=jax.ShapeDtypeStruct((n_rows, ROW_W), U32),
        compiler_params=pltpu.CompilerParams(
            dimension_semantics=("arbitrary",), vmem_limit_bytes=VMEM_LIMIT),
        name="dispatch",
    )(zero_start, n_win, win_src.reshape(-1), win_dst.reshape(-1), pos, meta, hn)


def _expert_kernel(be_ref, nused_ref, x_ref, wu_ref, bu_ref, wd_ref, bd_ref, y_ref, wu16_ref, wd16_ref):
    i = pl.program_id(0)
    used = i < nused_ref[0]

    @pl.when((i == 0) | (be_ref[i] != be_ref[jnp.maximum(i - 1, 0)]))
    def _():
        wu16_ref[...] = wu_ref[0].astype(BF16)
        wd16_ref[...] = wd_ref[0].astype(BF16)

    @pl.when(jnp.logical_not(used))
    def _():
        y_ref[...] = jnp.zeros_like(y_ref)

    @pl.when(used)
    def _():
        half = D_MODEL // 2
        x_lo, x_hi = _unpack_pairs(x_ref[:, :half])
        meta = lax.bitcast_convert_type(x_ref[:, half:half + 3 * TOP_K], F32)
        expert = be_ref[i].astype(F32)
        gate = jnp.zeros((x_ref.shape[0], 1), F32)
        for k in range(TOP_K):
            weight = meta[:, TOP_K + k:TOP_K + k + 1] + meta[:, 2 * TOP_K + k:2 * TOP_K + k + 1]
            gate = gate + jnp.where(meta[:, k:k + 1] == expert, weight, 0.0)
        hu = _dot(x_lo, wu16_ref[:half, :]) + _dot(x_hi, wu16_ref[half:, :]) + bu_ref[0]
        x_glu = jnp.minimum(hu[:, :D_FF], SWIGLU_LIMIT)
        x_lin = jnp.clip(hu[:, D_FF:], -SWIGLU_LIMIT, SWIGLU_LIMIT)
        act = x_glu * _sigmoid(SWIGLU_ALPHA * x_glu) * (x_lin + 1.0)
        y_ref[...] = _pack_pairs((_dot(act.astype(BF16), wd16_ref[...]) + bd_ref[0]) * gate)


def _experts(block_e, n_used, xin, w_up, b_up, w_down, b_down):
    n_rows = xin.shape[0]
    return pl.pallas_call(
        _expert_kernel,
        grid_spec=pltpu.PrefetchScalarGridSpec(
            num_scalar_prefetch=2,
            grid=(n_rows // TB,),
            in_specs=[
                pl.BlockSpec((TB, ROW_W), lambda i, be, nu: (i, 0)),
                pl.BlockSpec((1, D_MODEL, 2 * D_FF), lambda i, be, nu: (be[i], 0, 0)),
                pl.BlockSpec((1, 1, 2 * D_FF), lambda i, be, nu: (be[i], 0, 0)),
                pl.BlockSpec((1, D_FF, D_MODEL), lambda i, be, nu: (be[i], 0, 0)),
                pl.BlockSpec((1, 1, D_MODEL), lambda i, be, nu: (be[i], 0, 0)),
            ],
            out_specs=pl.BlockSpec((TB, D_MODEL // 2), lambda i, be, nu: (i, 0)),
            scratch_shapes=[pltpu.VMEM((D_MODEL, 2 * D_FF), BF16), pltpu.VMEM((D_FF, D_MODEL), BF16)],
        ),
        out_shape=jax.ShapeDtypeStruct((n_rows, D_MODEL // 2), U32),
        compiler_params=pltpu.CompilerParams(
            dimension_semantics=("arbitrary",), vmem_limit_bytes=VMEM_LIMIT),
        name="experts",
    )(block_e, n_used, xin, w_up, b_up.reshape(N_EXPERTS, 1, 2 * D_FF),
      w_down, b_down.reshape(N_EXPERTS, 1, D_MODEL))


def _combine_kernel(nwin_ref, wsrc_ref, wdst_ref, ric_ref, delta_ref, yb_ref, h1_ref, lnf_ref, o_ref,
                    stage_ref, buf_ref, sem):
    i = pl.program_id(0)
    tm = h1_ref.shape[0]
    n_buf_rows = buf_ref.shape[0]

    @pl.when(i == 0)
    def _():
        buf_ref[...] = jnp.zeros_like(buf_ref)

    def window_copy(s):
        dst = pl.multiple_of(wdst_ref[i * N_SLOT + s], ROW_ALIGN)
        return pltpu.make_async_copy(yb_ref.at[pl.ds(dst, WIN), :],
                                     stage_ref.at[pl.ds(pl.multiple_of(s * WIN, WIN), WIN), :], sem)

    def issue(s, carry):
        window_copy(s).start()
        return carry

    lax.fori_loop(0, nwin_ref[i], issue, 0)

    ric = ric_ref[...].astype(F32)
    lane = lax.broadcasted_iota(jnp.int32, (tm, LANES), 1).astype(F32)
    col = lax.broadcasted_iota(jnp.int32, (tm, n_buf_rows), 1)
    pick = jnp.zeros((tm, n_buf_rows), F32)
    for k in range(TOP_K):
        offset = jnp.sum(jnp.where(lane == ric[:, k:k + 1], delta_ref[0], 0.0), axis=-1, keepdims=True)
        pos = (ric[:, TOP_K + k:TOP_K + k + 1] + offset).astype(jnp.int32)
        pick = pick + jnp.where(col == pos, 1.0, 0.0)
    pick = pick.astype(BF16)

    def drain(s, carry):
        window_copy(s).wait()
        return carry

    lax.fori_loop(0, nwin_ref[i], drain, 0)

    def compact(s, carry):
        src = pl.multiple_of(wsrc_ref[i * N_SLOT + s], ROW_ALIGN)
        buf_ref[pl.ds(src, WIN), :] = stage_ref[pl.ds(pl.multiple_of(s * WIN, WIN), WIN), :]
        return carry

    lax.fori_loop(0, nwin_ref[i], compact, 0)

    y_lo, y_hi = _unpack_pairs(buf_ref[...])
    h2 = h1_ref[...] + jnp.concatenate([_dot(pick, y_lo), _dot(pick, y_hi)], axis=-1)
    o_ref[...] = h2 * _rms_scale(h2) * lnf_ref[...]


def _combine(n_win, win_src, win_dst, ric, delta, yb, h1, lnf_w):
    n_tok = h1.shape[0]
    tm = TM_ROW
    n_tiles = n_tok // tm
    return pl.pallas_call(
        _combine_kernel,
        grid_spec=pltpu.PrefetchScalarGridSpec(
            num_scalar_prefetch=3,
            grid=(n_tiles,),
            in_specs=[
                pl.BlockSpec((tm, LANES), lambda i, *_: (i, 0)),
                pl.BlockSpec((1, 1, LANES), lambda i, *_: (i, 0, 0)),
                pl.BlockSpec(memory_space=pl.ANY),
                pl.BlockSpec((tm, D_MODEL), lambda i, *_: (i, 0)),
                pl.BlockSpec((1, D_MODEL), lambda i, *_: (0, 0)),
            ],
            out_specs=pl.BlockSpec((tm, D_MODEL), lambda i, *_: (i, 0)),
            scratch_shapes=[pltpu.VMEM((N_SLOT * WIN, D_MODEL // 2), U32),
                            pltpu.VMEM((SORT_ROWS, D_MODEL // 2), U32), pltpu.SemaphoreType.DMA(())],
        ),
        out_shape=jax.ShapeDtypeStruct((n_tok, D_MODEL), F32),
        compiler_params=pltpu.CompilerParams(
            dimension_semantics=("arbitrary",), vmem_limit_bytes=VMEM_LIMIT),
        name="combine",
    )(n_win, win_src.reshape(-1), win_dst.reshape(-1), ric, delta, yb, h1, lnf_w.reshape(1, D_MODEL))


def _layer(h, ln1_w, w_in, w_gk2, b_gk2, gla_norm_w, w_proj_a, w_proj_b, w_out,
           ln2_w, w_router, b_router, w_up, b_up, w_down, b_down, lnf_w):
    bsz, seq, _ = h.shape
    n_tok = bsz * seq

    n_main = 3 * QKV_W + 2 * GLA_KEY_DIM + 2 * GLA_VALUE_DIM
    pad = LANES - GLA_GATE_RANK
    w_all = jnp.concatenate(
        [w_in[:, :n_main + GLA_GATE_RANK], jnp.zeros((D_MODEL, pad), F32), w_in[:, n_main + GLA_GATE_RANK:]],
        axis=1).astype(BF16)
    w2_pad = jnp.concatenate([w_gk2, jnp.zeros((pad, GLA_KEY_DIM), F32)], axis=0).astype(BF16)
    wr_pad = jnp.concatenate([w_router, jnp.zeros((D_MODEL, LANES - N_EXPERTS), F32)], axis=1)
    wr_hi = wr_pad.astype(BF16)
    wr_lo = (wr_pad - wr_hi.astype(F32)).astype(BF16)
    br_pad = jnp.concatenate([b_router, jnp.zeros((LANES - N_EXPERTS,), F32)]).reshape(1, LANES)

    pa0, pa1, pa2, pg, pm = _inproj(h, ln1_w, w_all)
    o0, l0 = _dil_attn(pa0)
    o1, l1 = _dil_attn(pa1)
    o2, l2 = _dil_attn(pa2)
    o_b = _gla(pg, w2_pad, b_gk2, gla_norm_w)

    h1, hn, ri, ric, meta, carry_f, cnt = _mix(
        h.reshape(n_tok, D_MODEL), o0.reshape(n_tok, DA_WIDTH), l0.reshape(n_tok, LANES),
        o1, l1, o2, l2, o_b.reshape(n_tok, GLA_VALUE_DIM), pm.reshape(n_tok, MERGE_W),
        w_proj_a.astype(BF16), w_proj_b.astype(BF16), w_out.astype(BF16), ln2_w, wr_hi, wr_lo, br_pad)

    i32 = jnp.int32
    n_tiles = n_tok // TM_ROW
    experts = jnp.arange(N_EXPERTS)
    counts = cnt[0, :N_EXPERTS].astype(i32)
    before = carry_f[:, 0, :N_EXPERTS].astype(i32)
    run = jnp.concatenate([before[1:], counts[None]], axis=0) - before
    run_al = (run + ROW_ALIGN - 1) // ROW_ALIGN * ROW_ALIGN
    rows_end = jnp.cumsum(run_al, axis=0)
    rows_before = rows_end - run_al
    used = rows_end[-1]
    slack = WIN - ROW_ALIGN
    padded = (used + slack + TB - 1) // TB * TB
    pad_end = jnp.cumsum(padded)
    pad_start = pad_end - padded
    n_asg = n_tok * TOP_K
    n_rows = (n_asg + n_tiles * N_EXPERTS * (ROW_ALIGN - 1)
              + N_EXPERTS * (slack + TB - 1) + TB - 1) // TB * TB
    n_blocks = n_rows // TB
    block_starts = jnp.arange(n_blocks) * TB
    block_e = jnp.minimum(jnp.sum(pad_end[None, :] <= block_starts[:, None], axis=1),
                          N_EXPERTS - 1).astype(i32)
    n_used = (pad_end[-1:] // TB).astype(i32)

    run_end = jnp.cumsum(run_al, axis=1)
    run_start = run_end - run_al
    wins = (run + WIN - 1) // WIN
    wins_end = jnp.cumsum(wins, axis=1)
    wins_start = wins_end - wins
    n_win = wins_end[:, -1].astype(i32)
    slots = jnp.arange(N_SLOT)
    slot_e = jnp.sum(wins_end[:, None, :] <= slots[None, :, None], axis=-1)
    slot_is = slot_e[..., None] == experts

    def of_slot(table):
        return jnp.sum(jnp.where(slot_is, table[:, None, :], 0), axis=-1)

    win_off = (slots[None, :] - of_slot(wins_start)) * WIN
    win_src = (of_slot(run_start) + win_off).astype(i32)
    win_dst = (of_slot(pad_start[None, :] + rows_before) + win_off).astype(i32)
    asg_is = ri[:, :TOP_K, :, None] == experts
    pos = ri[:, TOP_K:, :] + jnp.sum(jnp.where(asg_is, (run_start - before)[:, None, None, :], 0), axis=-1)
    delta = jnp.pad((run_start - before).astype(F32), ((0, 0), (0, LANES - N_EXPERTS)))

    chunk_back = ZCHUNK * (1 + jnp.arange((slack + TB - 1 + ZCHUNK - 1) // ZCHUNK + 1))
    region_chunks = pad_end[:, None] - chunk_back[None, :]
    region_ok = (region_chunks >= pad_start[:, None]) & (region_chunks + ZCHUNK > (pad_start + used)[:, None])
    tail_chunks = pad_end[-1] + ZCHUNK * jnp.arange((n_rows - n_asg) // ZCHUNK)
    zero_start = jnp.concatenate([jnp.where(region_ok, region_chunks, -1).reshape(-1),
                                  jnp.where(tail_chunks < n_rows, tail_chunks, -1)]).astype(i32)

    xin = _dispatch(zero_start, n_win, win_src, win_dst, pos.astype(i32), meta, hn, n_rows)
    yb = _experts(block_e, n_used, xin, w_up, b_up, w_down, b_down)
    out = _combine(n_win, win_src, win_dst, ric, delta.reshape(-1, 1, LANES), yb, h1, lnf_w)
    return out.reshape(bsz, seq, D_MODEL)


def kernel(x, ln1_w, w_in, w_gk2, b_gk2, gla_norm_w, w_proj_a, w_proj_b, w_out, ln2_w, w_router,
           b_router, w_up, b_up, w_down, b_down, lnf_w):
    assert x.shape[-1] == D_MODEL and ln1_w.shape[0] == 1, "one layer of width D_MODEL"
    return _layer(x, ln1_w[0], w_in[0], w_gk2[0], b_gk2[0], gla_norm_w[0], w_proj_a[0], w_proj_b[0],
                  w_out[0], ln2_w[0], w_router[0], b_router[0], w_up[0], b_up[0], w_down[0],
                  b_down[0], lnf_w)
```

```python
import functools

import jax
import jax.numpy as jnp
from jax import lax
from jax.experimental import pallas as pl
from jax.experimental.pallas import tpu as pltpu

F32 = jnp.float32
BF16 = jnp.bfloat16
U32 = jnp.uint32

D_MODEL = 1024
DA_GROUPS = ((128, 1), (512, 4), (2048, 16))
DA_HEADS = 4
DA_HEAD_DIM = 128
DA_WIDTH = DA_HEADS * DA_HEAD_DIM
DA_BLOCK = 128
GLA_HEADS = 4
GLA_KEY_DIM = D_MODEL // 2
GLA_VALUE_DIM = D_MODEL
GLA_DK = GLA_KEY_DIM // GLA_HEADS
GLA_DV = GLA_VALUE_DIM // GLA_HEADS
GLA_GATE_RANK = 16
GLA_GATE_NORMALIZER = 16.0
GLA_CHUNK = 64
N_EXPERTS = 32
TOP_K = 4
D_FF = D_MODEL
SWIGLU_ALPHA = 1.702
SWIGLU_LIMIT = 7.0
RMS_EPS = 1e-5
NEG_INF = -1e30

LANES = 128
QKV_W = 3 * DA_WIDTH
GLA_W = 2 * GLA_KEY_DIM + 2 * GLA_VALUE_DIM + LANES
MERGE_W = 2 * D_MODEL

DA_QB = 4
LSE_LANES = LANES // DA_HEADS
TM_IN = 512
N_CHUNK = 512
T_GLA = 512
GLA_SUB = 256
TM_MIX = 512
TB = 512
TB_SPLIT = 256
TM_ROW = 256
ROW_ALIGN = 8
WIN = 32
N_SLOT = N_EXPERTS + TM_ROW * TOP_K // WIN
SORT_ROWS = TM_ROW * TOP_K + N_EXPERTS * (ROW_ALIGN - 1) + WIN
ZCHUNK = 256
ROW_W = D_MODEL // 2 + LANES
VMEM_LIMIT = 56 * 1024 * 1024
VMEM_LIMIT_INPROJ = 62 * 1024 * 1024

_NT = (((1,), (1,)), ((), ()))
_TN = (((0,), (0,)), ((), ()))


def _dot(a, b):
    return jnp.dot(a, b, preferred_element_type=F32)


def _sigmoid(x):
    return 1.0 / (1.0 + jnp.exp(-x))


def _rms_scale(x):
    return lax.rsqrt(jnp.mean(x * x, axis=-1, keepdims=True) + RMS_EPS)


def _inproj_kernel(x_ref, ln_ref, w_hbm, pa0_ref, pa1_ref, pa2_ref, pg_ref, pm_ref,
                   w_ref, xs_ref, xn_ref, wsem):
    tm = x_ref.shape[1]
    n_slab = D_MODEL // LANES

    @pl.when((pl.program_id(0) == 0) & (pl.program_id(1) == 0))
    def _():
        cp = pltpu.make_async_copy(w_hbm, w_ref, wsem)
        cp.start()
        cp.wait()

    def project(out_write, col0, width, post=None):
        for c0 in range(0, width, N_CHUNK):
            cw = min(N_CHUNK, width - c0)
            val = _dot(xs_ref[...], w_ref[:, col0 + c0:col0 + c0 + cw])
            out_write(c0, cw, (val if post is None else post(val)).astype(BF16))

    x = x_ref[0]
    xn = x * _rms_scale(x) * ln_ref[...]
    xs_ref[...] = xn.astype(BF16)
    for j in range(n_slab):
        xn_ref[j] = xn[:, j * LANES:(j + 1) * LANES]

    def write_to(ref):
        def write(c0, cw, val):
            ref[0, :, c0:c0 + cw] = val
        return write

    def write_qkv(out_ref, d, n):
        def write(c0, cw, val):
            for r in range(d):
                out_ref[0, r, c0 // DA_WIDTH] = val[r * n:(r + 1) * n]
        return write

    project(write_qkv(pa0_ref, 1, tm), 0, QKV_W)
    project(write_to(pg_ref), 3 * QKV_W, GLA_W)
    project(write_to(pm_ref), 3 * QKV_W + GLA_W, MERGE_W, post=_sigmoid)

    for gi, out_ref in ((1, pa1_ref), (2, pa2_ref)):
        d = DA_GROUPS[gi][1]
        n = tm // d
        for r in range(d):
            for j in range(n_slab):
                xs_ref[r * n:(r + 1) * n, j * LANES:(j + 1) * LANES] = (
                    xn_ref[j, pl.ds(r, n, stride=d), :].astype(BF16))
        project(write_qkv(out_ref, d, n), gi * QKV_W, QKV_W)


def _inproj(x, ln1_w, w_all):
    assert N_CHUNK == DA_WIDTH
    bsz, seq, _ = x.shape
    tm = TM_IN
    d1, d2 = DA_GROUPS[1][1], DA_GROUPS[2][1]

    def qkv_shape(d):
        return jax.ShapeDtypeStruct((bsz, d, 3, seq // d, DA_WIDTH), BF16)

    def qkv_spec(d):
        return pl.BlockSpec((1, d, 3, tm // d, DA_WIDTH), lambda b, i: (b, 0, 0, i, 0))

    return pl.pallas_call(
        _inproj_kernel,
        grid=(bsz, seq // tm),
        in_specs=[
            pl.BlockSpec((1, tm, D_MODEL), lambda b, i: (b, i, 0)),
            pl.BlockSpec((1, D_MODEL), lambda b, i: (0, 0)),
            pl.BlockSpec(memory_space=pl.ANY),
        ],
        out_specs=(
            qkv_spec(1), qkv_spec(d1), qkv_spec(d2),
            pl.BlockSpec((1, tm, GLA_W), lambda b, i: (b, i, 0)),
            pl.BlockSpec((1, tm, MERGE_W), lambda b, i: (b, i, 0)),
        ),
        out_shape=(qkv_shape(1), qkv_shape(d1), qkv_shape(d2),
                   jax.ShapeDtypeStruct((bsz, seq, GLA_W), BF16),
                   jax.ShapeDtypeStruct((bsz, seq, MERGE_W), BF16)),
        scratch_shapes=[pltpu.VMEM(w_all.shape, BF16),
                        pltpu.VMEM((tm, D_MODEL), BF16),
                        pltpu.VMEM((D_MODEL // LANES, tm, LANES), F32),
                        pltpu.SemaphoreType.DMA(())],
        compiler_params=pltpu.CompilerParams(
            dimension_semantics=("arbitrary", "arbitrary"), vmem_limit_bytes=VMEM_LIMIT_INPROJ),
        name="inproj",
    )(x, ln1_w.reshape(1, D_MODEL), w_all)


def _dil_attn_kernel(q_ref, kp_ref, kc_ref, vp_ref, vc_ref, o_ref, l_ref, s_ref, p_ref, r_ref):
    n = pl.program_id(2)
    blk = DA_BLOCK
    qi = lax.broadcasted_iota(jnp.int32, (blk, 2 * blk), 0)
    kj = lax.broadcasted_iota(jnp.int32, (blk, 2 * blk), 1)
    band = (kj >= qi) & (kj <= qi + blk)
    band_first = (kj >= jnp.where(n > 0, qi, blk)) & (kj <= qi + blk)
    scale = DA_HEAD_DIM ** -0.5
    items = [(b, h) for b in range(DA_QB) for h in range(DA_HEADS)]

    def rows(b):
        return slice(b * blk, (b + 1) * blk)

    def cols(h):
        return slice(h * DA_HEAD_DIM, (h + 1) * DA_HEAD_DIM)

    def window(prev_ref, cur_ref, b, h):
        before = prev_ref[0, 0, 0, :, cols(h)] if b == 0 else cur_ref[0, 0, 0, rows(b - 1), cols(h)]
        return jnp.concatenate([before, cur_ref[0, 0, 0, rows(b), cols(h)]], axis=0)

    for i, (b, h) in enumerate(items):
        s = lax.dot_general(q_ref[0, 0, 0, rows(b), cols(h)], window(kp_ref, kc_ref, b, h), _NT,
                            preferred_element_type=F32) * scale
        s_ref[i] = jnp.where(band_first if b == 0 else band, s, NEG_INF)
    for i, (b, h) in enumerate(items):
        s = s_ref[i]
        m = jnp.max(s, axis=-1, keepdims=True)
        p = jnp.exp(s - m)
        l = jnp.sum(p, axis=-1, keepdims=True)
        p_ref[i] = p.astype(BF16)
        r_ref[i] = jnp.broadcast_to(1.0 / l, (blk, DA_HEAD_DIM))
        l_ref[0, 0, rows(b), h * LSE_LANES:(h + 1) * LSE_LANES] = jnp.broadcast_to(
            m + jnp.log(l), (blk, LSE_LANES))
    for i, (b, h) in enumerate(items):
        acc = _dot(p_ref[i], window(vp_ref, vc_ref, b, h))
        o_ref[0, 0, rows(b), cols(h)] = (acc * r_ref[i]).astype(o_ref.dtype)


def _dil_attn(pa):
    bsz, d, _, sub_len, _ = pa.shape
    rows = DA_QB * DA_BLOCK
    n_items = DA_QB * DA_HEADS

    def cur(sec):
        return pl.BlockSpec((1, 1, 1, rows, DA_WIDTH), lambda b, r, n: (b, r, sec, n, 0))

    def prev(sec):
        return pl.BlockSpec((1, 1, 1, DA_BLOCK, DA_WIDTH),
                            lambda b, r, n: (b, r, sec, jnp.maximum(n * DA_QB - 1, 0), 0))

    def out(width):
        return pl.BlockSpec((1, 1, rows, width), lambda b, r, n: (b, r, n, 0))

    return pl.pallas_call(
        _dil_attn_kernel,
        grid=(bsz, d, sub_len // rows),
        in_specs=[cur(0), prev(1), cur(1), prev(2), cur(2)],
        out_specs=(out(DA_WIDTH), out(LANES)),
        out_shape=(jax.ShapeDtypeStruct((bsz, d, sub_len, DA_WIDTH), BF16),
                   jax.ShapeDtypeStruct((bsz, d, sub_len, LANES), F32)),
        scratch_shapes=[pltpu.VMEM((n_items, DA_BLOCK, 2 * DA_BLOCK), F32),
                        pltpu.VMEM((n_items, DA_BLOCK, 2 * DA_BLOCK), BF16),
                        pltpu.VMEM((n_items, DA_BLOCK, DA_HEAD_DIM), F32)],
        compiler_params=pltpu.CompilerParams(
            dimension_semantics=("parallel", "parallel", "arbitrary"), vmem_limit_bytes=VMEM_LIMIT),
        name=f"dil_attn_d{d}",
    )(pa, pa, pa, pa, pa)


def _gla_kernel(q_ref, k_ref, v_ref, og_ref, lr_ref, w2_ref, b2_ref, nw_ref, o_ref, st_ref, mask_ref, keep_ref):
    t = pl.program_id(2)
    c = GLA_CHUNK
    tt = q_ref.shape[1]
    n_c = tt // c

    sub = mask_ref.shape[0]

    @pl.when(t == 0)
    def _():
        st_ref[...] = jnp.zeros_like(st_ref)
        row = lax.broadcasted_iota(jnp.int32, (sub, sub), 0)
        col = lax.broadcasted_iota(jnp.int32, (sub, sub), 1)
        keep = (col <= row) & (col >= row - row % c)
        keep_ref[...] = jnp.where(keep, 1.0, 0.0)
        mask_ref[...] = jnp.where(keep, 1.0, 0.0).astype(BF16)

    mask = mask_ref[...]
    gpre = _dot(lr_ref[0], w2_ref[...]) + b2_ref[...]
    forget = (jnp.minimum(gpre, 0.0) - jnp.log(1.0 + jnp.exp(-jnp.abs(gpre)))) / GLA_GATE_NORMALIZER
    g_hi = forget.astype(BF16)
    g_lo = (forget - g_hi.astype(F32)).astype(BF16)
    g_cat = jnp.concatenate([g_hi, g_lo], axis=-1)
    csum = jnp.concatenate([_dot(mask, g_cat[s0:s0 + sub]) for s0 in range(0, tt, sub)], axis=0)
    b = csum[:, :GLA_DK] + csum[:, GLA_DK:]
    b_last = b.reshape(n_c, c, GLA_DK)[:, c - 1:c, :]
    b_to_end = (b_last - b.reshape(n_c, c, GLA_DK)).reshape(tt, GLA_DK)
    q = q_ref[0].astype(F32)
    k = k_ref[0].astype(F32)
    v = v_ref[0]
    q_e = (q * ((GLA_DK ** -0.5) * jnp.exp(b))).astype(BF16)
    k_e = (k * jnp.exp(-b)).astype(BF16)
    k_end = (k * jnp.exp(b_to_end)).astype(BF16)
    o_intra = []
    for s0 in range(0, tt, sub):
        ss = slice(s0, s0 + sub)
        att = lax.dot_general(q_e[ss], k_e[ss], _NT, preferred_element_type=F32)
        att = jnp.where(keep_ref[...] > 0.0, att, 0.0).astype(BF16)
        o_intra.append(_dot(att, v[ss]))
    o_intra = jnp.concatenate(o_intra, axis=0)
    decay = jnp.exp(b_last.reshape(n_c, GLA_DK).T)
    st = st_ref[...]
    outs = []
    for ci in range(n_c):
        rs = slice(ci * c, (ci + 1) * c)
        outs.append(o_intra[rs] + _dot(q_e[rs], st.astype(BF16)))
        st = decay[:, ci:ci + 1] * st + lax.dot_general(k_end[rs], v[rs], _TN, preferred_element_type=F32)
    st_ref[...] = st
    o = jnp.concatenate(outs, axis=0)
    o = o * _rms_scale(o) * nw_ref[...]
    gate = og_ref[0].astype(F32)
    o_ref[0] = (o * (gate * _sigmoid(gate))).astype(BF16)


def _gla(pg, w2_pad, b_gk2, gla_norm_w):
    bsz, seq, _ = pg.shape
    t = T_GLA
    kq = GLA_KEY_DIM // GLA_DK
    kv = 2 * GLA_KEY_DIM // GLA_DV
    kg = kv + GLA_VALUE_DIM // GLA_DV
    klr = (2 * GLA_KEY_DIM + 2 * GLA_VALUE_DIM) // LANES
    return pl.pallas_call(
        _gla_kernel,
        grid=(bsz, GLA_HEADS, seq // t),
        in_specs=[
            pl.BlockSpec((1, t, GLA_DK), lambda b, h, i: (b, i, h)),
            pl.BlockSpec((1, t, GLA_DK), lambda b, h, i: (b, i, kq + h)),
            pl.BlockSpec((1, t, GLA_DV), lambda b, h, i: (b, i, kv + h)),
            pl.BlockSpec((1, t, GLA_DV), lambda b, h, i: (b, i, kg + h)),
            pl.BlockSpec((1, t, LANES), lambda b, h, i: (b, i, klr)),
            pl.BlockSpec((LANES, GLA_DK), lambda b, h, i: (0, h)),
            pl.BlockSpec((1, GLA_DK), lambda b, h, i: (0, h)),
            pl.BlockSpec((1, GLA_DV), lambda b, h, i: (0, 0)),
        ],
        out_specs=pl.BlockSpec((1, t, GLA_DV), lambda b, h, i: (b, i, h)),
        out_shape=jax.ShapeDtypeStruct((bsz, seq, GLA_VALUE_DIM), BF16),
        scratch_shapes=[pltpu.VMEM((GLA_DK, GLA_DV), F32), pltpu.VMEM((GLA_SUB, GLA_SUB), BF16),
                        pltpu.VMEM((GLA_SUB, GLA_SUB), F32)],
        compiler_params=pltpu.CompilerParams(
            dimension_semantics=("parallel", "parallel", "arbitrary"), vmem_limit_bytes=VMEM_LIMIT),
        name="gla",
    )(pg, pg, pg, pg, pg, w2_pad, b_gk2.reshape(1, GLA_KEY_DIM), gla_norm_w.reshape(1, GLA_DV))


def _mix_kernel(x_ref, o0_ref, l0_ref, o1_ref, l1_ref, o2_ref, l2_ref, ob_ref, pm_ref,
                wa_ref, wb_ref, wo_ref, ln2_ref, wrh_ref, wrl_ref, br_ref,
                h1_ref, hn_ref, ri_ref, ric_ref, meta_ref, cbefore_ref, cnt_ref,
                po1_ref, pl1_ref, po2_ref, pl2_ref, carry_ref):
    step = pl.program_id(0)
    tm = x_ref.shape[0]

    @pl.when(step == 0)
    def _():
        carry_ref[...] = jnp.zeros_like(carry_ref)

    for o_ref, l_ref, po_ref, pl_ref, (_, d) in ((o1_ref, l1_ref, po1_ref, pl1_ref, DA_GROUPS[1]),
                                                 (o2_ref, l2_ref, po2_ref, pl2_ref, DA_GROUPS[2])):
        n = tm // d
        for r in range(d):
            pl_ref[pl.ds(r, n, stride=d), :] = l_ref[0, r]
            for h in range(DA_HEADS):
                sl = slice(h * DA_HEAD_DIM, (h + 1) * DA_HEAD_DIM)
                po_ref[h, pl.ds(r, n, stride=d), :] = o_ref[0, r, :, sl].astype(F32)

    l0, l1, l2 = l0_ref[...], pl1_ref[...], pl2_ref[...]
    mx = jnp.maximum(jnp.maximum(l0, l1), l2)
    e0, e1, e2 = jnp.exp(l0 - mx), jnp.exp(l1 - mx), jnp.exp(l2 - mx)
    inv = 1.0 / (e0 + e1 + e2)
    w0, w1, w2 = e0 * inv, e1 * inv, e2 * inv
    heads = []
    for h in range(DA_HEADS):
        sl = slice(h * DA_HEAD_DIM, (h + 1) * DA_HEAD_DIM)
        at = slice(h * LSE_LANES, h * LSE_LANES + 1)
        o_h = w0[:, at] * o0_ref[:, sl].astype(F32) + w1[:, at] * po1_ref[h] + w2[:, at] * po2_ref[h]
        heads.append(o_h.astype(BF16))
    o_a = jnp.concatenate(heads, axis=-1)

    gates = pm_ref[...].astype(F32)
    mixed = (gates[:, :D_MODEL] * _dot(o_a, wa_ref[...])
             + gates[:, D_MODEL:] * _dot(ob_ref[...], wb_ref[...]))
    h1 = x_ref[...] + _dot(mixed.astype(BF16), wo_ref[...])
    h1_ref[...] = h1
    hn = h1 * _rms_scale(h1) * ln2_ref[...]

    hn_hi = hn.astype(BF16)
    hn_ref[...] = hn_hi
    hn_lo = (hn - hn_hi.astype(F32)).astype(BF16)
    logits = (_dot(hn_hi, wrh_ref[...]) + _dot(hn_lo, wrh_ref[...]) + _dot(hn_hi, wrl_ref[...])
              + br_ref[...])
    lane = lax.broadcasted_iota(jnp.int32, (tm, LANES), 1).astype(F32)
    work = jnp.where(lane < N_EXPERTS, logits, -jnp.inf)
    vals, idxs = [], []
    for _ in range(TOP_K):
        m = jnp.max(work, axis=-1, keepdims=True)
        idx = jnp.min(jnp.where(work == m, lane, float(LANES)), axis=-1, keepdims=True)
        vals.append(m)
        idxs.append(idx)
        work = jnp.where(lane == idx, -jnp.inf, work)
    exps = [jnp.exp(v - vals[0]) for v in vals]
    denom = exps[0] + exps[1] + exps[2] + exps[3]

    onehot = jnp.zeros((tm, LANES), F32)
    for idx in idxs:
        onehot = onehot + jnp.where(lane == idx, 1.0, 0.0)
    row = lax.broadcasted_iota(jnp.int32, (tm, tm), 0)
    col = lax.broadcasted_iota(jnp.int32, (tm, tm), 1)
    below = jnp.where(col < row, 1.0, 0.0).astype(BF16)
    before = _dot(below, onehot.astype(BF16)) + carry_ref[0:1, :]
    ranks = [jnp.sum(jnp.where(lane == idx, before, 0.0), axis=-1, keepdims=True) for idx in idxs]
    running = carry_ref[0:1, :]
    for j in range(tm // TM_ROW):
        cbefore_ref[j] = jnp.broadcast_to(running, cbefore_ref.shape[1:])
        running = running + jnp.sum(onehot[j * TM_ROW:(j + 1) * TM_ROW], axis=0, keepdims=True)
    carry = running
    carry_ref[...] = jnp.broadcast_to(carry, carry_ref.shape)
    cnt_ref[...] = jnp.broadcast_to(carry, cnt_ref.shape)

    ri = jnp.zeros((tm, LANES), F32)
    for j, val in enumerate(idxs + ranks):
        ri = jnp.where(lane == float(j), val, ri)
    ric_ref[...] = ri.astype(jnp.int32)
    ri_t = ri.T[:2 * TOP_K].astype(jnp.int32)
    for j in range(tm // TM_ROW):
        ri_ref[j] = ri_t[:, j * TM_ROW:(j + 1) * TM_ROW]
    meta = jnp.zeros((tm, LANES), F32)
    for k in range(TOP_K):
        gate = exps[k] / denom
        gate_hi = gate.astype(BF16).astype(F32)
        meta = jnp.where(lane == float(k), idxs[k], meta)
        meta = jnp.where(lane == float(TOP_K + k), gate_hi, meta)
        meta = jnp.where(lane == float(2 * TOP_K + k), gate - gate_hi, meta)
    meta_ref[...] = meta.astype(BF16)


def _mix(x2, o0, l0, o1, l1, o2, l2, o_b, pm, wa, wb, wo, ln2_w, wr_hi, wr_lo, br_pad):
    n_tok = x2.shape[0]
    tm = TM_MIX
    bsz = o1.shape[0]
    d1, d2 = DA_GROUPS[1][1], DA_GROUPS[2][1]
    tiles_per_seq = (n_tok // bsz) // tm

    def rows(width):
        return pl.BlockSpec((tm, width), lambda i: (i, 0))

    def residue_major(d, width):
        return pl.BlockSpec((1, d, tm // d, width),
                            lambda i: (i // tiles_per_seq, 0, i % tiles_per_seq, 0))

    def whole(arr):
        return pl.BlockSpec(arr.shape, lambda i: (0,) * arr.ndim)

    ln2 = ln2_w.reshape(1, D_MODEL)
    return pl.pallas_call(
        _mix_kernel,
        grid=(n_tok // tm,),
        in_specs=[rows(D_MODEL), rows(DA_WIDTH), rows(LANES),
                  residue_major(d1, DA_WIDTH), residue_major(d1, LANES),
                  residue_major(d2, DA_WIDTH), residue_major(d2, LANES),
                  rows(GLA_VALUE_DIM), rows(MERGE_W),
                  whole(wa), whole(wb), whole(wo), whole(ln2), whole(wr_hi), whole(wr_lo), whole(br_pad)],
        out_specs=(rows(D_MODEL), rows(D_MODEL),
                   pl.BlockSpec((tm // TM_ROW, 2 * TOP_K, TM_ROW), lambda i: (i, 0, 0)),
                   rows(LANES), rows(LANES),
                   pl.BlockSpec((tm // TM_ROW, 8, LANES), lambda i: (i, 0, 0)),
                   pl.BlockSpec((8, LANES), lambda i: (0, 0))),
        out_shape=(jax.ShapeDtypeStruct((n_tok, D_MODEL), F32),
                   jax.ShapeDtypeStruct((n_tok, D_MODEL), BF16),
                   jax.ShapeDtypeStruct((n_tok // TM_ROW, 2 * TOP_K, TM_ROW), jnp.int32),
                   jax.ShapeDtypeStruct((n_tok, LANES), jnp.int32),
                   jax.ShapeDtypeStruct((n_tok, LANES), BF16),
                   jax.ShapeDtypeStruct((n_tok // TM_ROW, 8, LANES), F32),
                   jax.ShapeDtypeStruct((8, LANES), F32)),
        scratch_shapes=[pltpu.VMEM((DA_HEADS, tm, DA_HEAD_DIM), F32), pltpu.VMEM((tm, LANES), F32),
                        pltpu.VMEM((DA_HEADS, tm, DA_HEAD_DIM), F32), pltpu.VMEM((tm, LANES), F32),
                        pltpu.VMEM((8, LANES), F32)],
        compiler_params=pltpu.CompilerParams(
            dimension_semantics=("arbitrary",), vmem_limit_bytes=VMEM_LIMIT),
        name="mix",
    )(x2, o0, l0, o1, l1, o2, l2, o_b, pm, wa, wb, wo, ln2, wr_hi, wr_lo, br_pad)


def _pack_pairs(x):
    n = x.shape[1] // 2
    rounded = x.astype(BF16).astype(F32)
    lo = lax.bitcast_convert_type(rounded[:, :n], U32) >> 16
    hi = lax.bitcast_convert_type(rounded[:, n:], U32) & jnp.uint32(0xFFFF0000)
    return hi | lo


def _unpack_pairs(u):
    lo = lax.bitcast_convert_type(u << 16, F32).astype(BF16)
    hi = lax.bitcast_convert_type(u & jnp.uint32(0xFFFF0000), F32).astype(BF16)
    return lo, hi


def _dispatch_kernel(zstart_ref, nwin_ref, wsrc_ref, wdst_ref, pos_ref, meta_ref, hn_ref, xin_ref,
                     buf_ref, zero_ref, sem, zsem):
    i = pl.program_id(0)
    tm = hn_ref.shape[0]
    n_buf_rows = buf_ref.shape[1]
    slot = i % 2

    def window_copy(step, s):
        src = pl.multiple_of(wsrc_ref[step * N_SLOT + s], ROW_ALIGN)
        dst = pl.multiple_of(wdst_ref[step * N_SLOT + s], ROW_ALIGN)
        return pltpu.make_async_copy(buf_ref.at[step % 2, pl.ds(src, WIN), :],
                                     xin_ref.at[pl.ds(dst, WIN), :], sem)

    def wait_windows(step):
        def body(s, carry):
            window_copy(step, s).wait()
            return carry
        lax.fori_loop(0, nwin_ref[step], body, 0)

    @pl.when(i == 0)
    def _():
        zero_ref[...] = jnp.zeros_like(zero_ref)
        for j in range(zstart_ref.shape[0]):
            @pl.when(zstart_ref[j] >= 0)
            def _():
                start = pl.multiple_of(zstart_ref[j], ZCHUNK)
                cp = pltpu.make_async_copy(zero_ref, xin_ref.at[pl.ds(start, ZCHUNK), :], zsem)
                cp.start()
                cp.wait()

    row = lax.broadcasted_iota(jnp.int32, (n_buf_rows, tm), 0)
    perm = jnp.zeros((n_buf_rows, tm), F32)
    for k in range(TOP_K):
        perm = perm + jnp.where(row == pos_ref[0, k:k + 1, :], 1.0, 0.0)
    perm = perm.astype(BF16)
    buf_ref[slot, :, :D_MODEL // 2] = _pack_pairs(_dot(perm, hn_ref[...]))
    buf_ref[slot, :, D_MODEL // 2:] = lax.bitcast_convert_type(_dot(perm, meta_ref[...]), U32)

    @pl.when(i > 0)
    def _():
        wait_windows(i - 1)

    def issue(s, carry):
        window_copy(i, s).start()
        return carry

    lax.fori_loop(0, nwin_ref[i], issue, 0)

    @pl.when(i == pl.num_programs(0) - 1)
    def _():
        wait_windows(i)


def _dispatch(zero_start, n_win, win_src, win_dst, pos, meta, hn, n_rows):
    n_tok = hn.shape[0]
    tm = TM_ROW
    n_tiles = n_tok // tm
    n_buf_rows = SORT_ROWS
    return pl.pallas_call(
        _dispatch_kernel,
        grid_spec=pltpu.PrefetchScalarGridSpec(
            num_scalar_prefetch=4,
            grid=(n_tiles,),
            in_specs=[
                pl.BlockSpec((1, TOP_K, tm), lambda i, *_: (i, 0, 0)),
                pl.BlockSpec((tm, LANES), lambda i, *_: (i, 0)),
                pl.BlockSpec((tm, D_MODEL), lambda i, *_: (i, 0)),
            ],
            out_specs=pl.BlockSpec(memory_space=pl.ANY),
            scratch_shapes=[pltpu.VMEM((2, n_buf_rows, ROW_W), U32),
                            pltpu.VMEM((ZCHUNK, ROW_W), U32),
                            pltpu.SemaphoreType.DMA(()), pltpu.SemaphoreType.DMA(())],
        ),
        out_shape=jax.ShapeDtypeStruct((n_rows, ROW_W), U32),
        compiler_params=pltpu.CompilerParams(
            dimension_semantics=("arbitrary",), vmem_limit_bytes=VMEM_LIMIT),
        name="dispatch",
    )(zero_start, n_win, win_src.reshape(-1), win_dst.reshape(-1), pos, meta, hn)


def _expert_kernel(be_ref, nused_ref, x_ref, wu_ref, bu_ref, wd_ref, bd_ref, y_ref, wu16_ref, wd16_ref):
    i = pl.program_id(0)
    used = i < nused_ref[0]

    @pl.when((i == 0) | (be_ref[i] != be_ref[jnp.maximum(i - 1, 0)]))
    def _():
        wu16_ref[...] = wu_ref[0].astype(BF16)
        wd16_ref[...] = wd_ref[0].astype(BF16)

    @pl.when(jnp.logical_not(used))
    def _():
        y_ref[...] = jnp.zeros_like(y_ref)

    @pl.when(used)
    def _():
        half = D_MODEL // 2
        expert = be_ref[i].astype(F32)
        for r0 in range(0, TB, TB_SPLIT):
            rows = slice(r0, r0 + TB_SPLIT)
            x_lo, x_hi = _unpack_pairs(x_ref[rows, :half])
            meta = lax.bitcast_convert_type(x_ref[rows, half:half + 3 * TOP_K], F32)
            gate = jnp.zeros((TB_SPLIT, 1), F32)
            for k in range(TOP_K):
                weight = meta[:, TOP_K + k:TOP_K + k + 1] + meta[:, 2 * TOP_K + k:2 * TOP_K + k + 1]
                gate = gate + jnp.where(meta[:, k:k + 1] == expert, weight, 0.0)
            hu = _dot(x_lo, wu16_ref[:half, :]) + _dot(x_hi, wu16_ref[half:, :]) + bu_ref[0]
            x_glu = jnp.minimum(hu[:, :D_FF], SWIGLU_LIMIT)
            x_lin = jnp.clip(hu[:, D_FF:], -SWIGLU_LIMIT, SWIGLU_LIMIT)
            act = x_glu * _sigmoid(SWIGLU_ALPHA * x_glu) * (x_lin + 1.0)
            y_ref[rows, :] = _pack_pairs((_dot(act.astype(BF16), wd16_ref[...]) + bd_ref[0]) * gate)


def _experts(block_e, n_used, xin, w_up, b_up, w_down, b_down):
    n_rows = xin.shape[0]
    return pl.pallas_call(
        _expert_kernel,
        grid_spec=pltpu.PrefetchScalarGridSpec(
            num_scalar_prefetch=2,
            grid=(n_rows // TB,),
            in_specs=[
                pl.BlockSpec((TB, ROW_W), lambda i, be, nu: (i, 0)),
                pl.BlockSpec((1, D_MODEL, 2 * D_FF), lambda i, be, nu: (be[i], 0, 0)),
                pl.BlockSpec((1, 1, 2 * D_FF), lambda i, be, nu: (be[i], 0, 0)),
                pl.BlockSpec((1, D_FF, D_MODEL), lambda i, be, nu: (be[i], 0, 0)),
                pl.BlockSpec((1, 1, D_MODEL), lambda i, be, nu: (be[i], 0, 0)),
            ],
            out_specs=pl.BlockSpec((TB, D_MODEL // 2), lambda i, be, nu: (i, 0)),
            scratch_shapes=[pltpu.VMEM((D_MODEL, 2 * D_FF), BF16), pltpu.VMEM((D_FF, D_MODEL), BF16)],
        ),
        out_shape=jax.ShapeDtypeStruct((n_rows, D_MODEL // 2), U32),
        compiler_params=pltpu.CompilerParams(
            dimension_semantics=("arbitrary",), vmem_limit_bytes=VMEM_LIMIT),
        name="experts",
    )(block_e, n_used, xin, w_up, b_up.reshape(N_EXPERTS, 1, 2 * D_FF),
      w_down, b_down.reshape(N_EXPERTS, 1, D_MODEL))


def _combine_kernel(nwin_ref, wsrc_ref, wdst_ref, ric_ref, delta_ref, yb_ref, h1_ref, lnf_ref, o_ref,
                    stage_ref, buf_ref, sem):
    i = pl.program_id(0)
    tm = h1_ref.shape[0]
    n_buf_rows = buf_ref.shape[0]

    def window_copy(step, s):
        dst = pl.multiple_of(wdst_ref[step * N_SLOT + s], ROW_ALIGN)
        return pltpu.make_async_copy(
            yb_ref.at[pl.ds(dst, WIN), :],
            stage_ref.at[step % 2, pl.ds(pl.multiple_of(s * WIN, WIN), WIN), :], sem.at[step % 2])

    def fetch(step):
        def body(s, carry):
            window_copy(step, s).start()
            return carry
        lax.fori_loop(0, nwin_ref[step], body, 0)

    @pl.when(i == 0)
    def _():
        buf_ref[...] = jnp.zeros_like(buf_ref)
        fetch(i)

    @pl.when(i + 1 < pl.num_programs(0))
    def _():
        fetch(i + 1)

    ric = ric_ref[...].astype(F32)
    lane = lax.broadcasted_iota(jnp.int32, (tm, LANES), 1).astype(F32)
    col = lax.broadcasted_iota(jnp.int32, (tm, n_buf_rows), 1)
    pick = jnp.zeros((tm, n_buf_rows), F32)
    for k in range(TOP_K):
        offset = jnp.sum(jnp.where(lane == ric[:, k:k + 1], delta_ref[0], 0.0), axis=-1, keepdims=True)
        pos = (ric[:, TOP_K + k:TOP_K + k + 1] + offset).astype(jnp.int32)
        pick = pick + jnp.where(col == pos, 1.0, 0.0)
    pick = pick.astype(BF16)

    def drain(s, carry):
        window_copy(i, s).wait()
        return carry

    lax.fori_loop(0, nwin_ref[i], drain, 0)

    def compact(s, carry):
        src = pl.multiple_of(wsrc_ref[i * N_SLOT + s], ROW_ALIGN)
        buf_ref[pl.ds(src, WIN), :] = stage_ref[i % 2, pl.ds(pl.multiple_of(s * WIN, WIN), WIN), :]
        return carry

    lax.fori_loop(0, nwin_ref[i], compact, 0)

    y_lo, y_hi = _unpack_pairs(buf_ref[...])
    h2 = h1_ref[...] + jnp.concatenate([_dot(pick, y_lo), _dot(pick, y_hi)], axis=-1)
    o_ref[...] = h2 * _rms_scale(h2) * lnf_ref[...]


def _combine(n_win, win_src, win_dst, ric, delta, yb, h1, lnf_w):
    n_tok = h1.shape[0]
    tm = TM_ROW
    n_tiles = n_tok // tm
    return pl.pallas_call(
        _combine_kernel,
        grid_spec=pltpu.PrefetchScalarGridSpec(
            num_scalar_prefetch=3,
            grid=(n_tiles,),
            in_specs=[
                pl.BlockSpec((tm, LANES), lambda i, *_: (i, 0)),
                pl.BlockSpec((1, 1, LANES), lambda i, *_: (i, 0, 0)),
                pl.BlockSpec(memory_space=pl.ANY),
                pl.BlockSpec((tm, D_MODEL), lambda i, *_: (i, 0)),
                pl.BlockSpec((1, D_MODEL), lambda i, *_: (0, 0)),
            ],
            out_specs=pl.BlockSpec((tm, D_MODEL), lambda i, *_: (i, 0)),
            scratch_shapes=[pltpu.VMEM((2, N_SLOT * WIN, D_MODEL // 2), U32),
                            pltpu.VMEM((SORT_ROWS, D_MODEL // 2), U32), pltpu.SemaphoreType.DMA((2,))],
        ),
        out_shape=jax.ShapeDtypeStruct((n_tok, D_MODEL), F32),
        compiler_params=pltpu.CompilerParams(
            dimension_semantics=("arbitrary",), vmem_limit_bytes=VMEM_LIMIT),
        name="combine",
    )(n_win, win_src.reshape(-1), win_dst.reshape(-1), ric, delta, yb, h1, lnf_w.reshape(1, D_MODEL))


def _layer(h, ln1_w, w_in, w_gk2, b_gk2, gla_norm_w, w_proj_a, w_proj_b, w_out,
           ln2_w, w_router, b_router, w_up, b_up, w_down, b_down, lnf_w):
    bsz, seq, _ = h.shape
    n_tok = bsz * seq

    n_main = 3 * QKV_W + 2 * GLA_KEY_DIM + 2 * GLA_VALUE_DIM
    pad = LANES - GLA_GATE_RANK
    w_all = jnp.concatenate(
        [w_in[:, :n_main + GLA_GATE_RANK], jnp.zeros((D_MODEL, pad), F32), w_in[:, n_main + GLA_GATE_RANK:]],
        axis=1).astype(BF16)
    w2_pad = jnp.concatenate([w_gk2, jnp.zeros((pad, GLA_KEY_DIM), F32)], axis=0).astype(BF16)
    wr_pad = jnp.concatenate([w_router, jnp.zeros((D_MODEL, LANES - N_EXPERTS), F32)], axis=1)
    wr_hi = wr_pad.astype(BF16)
    wr_lo = (wr_pad - wr_hi.astype(F32)).astype(BF16)
    br_pad = jnp.concatenate([b_router, jnp.zeros((LANES - N_EXPERTS,), F32)]).reshape(1, LANES)

    pa0, pa1, pa2, pg, pm = _inproj(h, ln1_w, w_all)
    o0, l0 = _dil_attn(pa0)
    o1, l1 = _dil_attn(pa1)
    o2, l2 = _dil_attn(pa2)
    o_b = _gla(pg, w2_pad, b_gk2, gla_norm_w)

    h1, hn, ri, ric, meta, carry_f, cnt = _mix(
        h.reshape(n_tok, D_MODEL), o0.reshape(n_tok, DA_WIDTH), l0.reshape(n_tok, LANES),
        o1, l1, o2, l2, o_b.reshape(n_tok, GLA_VALUE_DIM), pm.reshape(n_tok, MERGE_W),
        w_proj_a.astype(BF16), w_proj_b.astype(BF16), w_out.astype(BF16), ln2_w, wr_hi, wr_lo, br_pad)

    i32 = jnp.int32
    n_tiles = n_tok // TM_ROW
    experts = jnp.arange(N_EXPERTS)
    counts = cnt[0, :N_EXPERTS].astype(i32)
    before = carry_f[:, 0, :N_EXPERTS].astype(i32)
    run = jnp.concatenate([before[1:], counts[None]], axis=0) - before
    run_al = (run + ROW_ALIGN - 1) // ROW_ALIGN * ROW_ALIGN
    rows_end = jnp.cumsum(run_al, axis=0)
    rows_before = rows_end - run_al
    used = rows_end[-1]
    slack = WIN - ROW_ALIGN
    padded = (used + slack + TB - 1) // TB * TB
    pad_end = jnp.cumsum(padded)
    pad_start = pad_end - padded
    n_asg = n_tok * TOP_K
    n_rows = (n_asg + n_tiles * N_EXPERTS * (ROW_ALIGN - 1)
              + N_EXPERTS * (slack + TB - 1) + TB - 1) // TB * TB
    n_blocks = n_rows // TB
    block_starts = jnp.arange(n_blocks) * TB
    block_e = jnp.minimum(jnp.sum(pad_end[None, :] <= block_starts[:, None], axis=1),
                          N_EXPERTS - 1).astype(i32)
    n_used = (pad_end[-1:] // TB).astype(i32)

    run_end = jnp.cumsum(run_al, axis=1)
    run_start = run_end - run_al
    wins = (run + WIN - 1) // WIN
    wins_end = jnp.cumsum(wins, axis=1)
    wins_start = wins_end - wins
    n_win = wins_end[:, -1].astype(i32)
    slots = jnp.arange(N_SLOT)
    slot_e = jnp.sum(wins_end[:, None, :] <= slots[None, :, None], axis=-1)
    slot_is = slot_e[..., None] == experts

    def of_slot(table):
        return jnp.sum(jnp.where(slot_is, table[:, None, :], 0), axis=-1)

    win_off = (slots[None, :] - of_slot(wins_start)) * WIN
    win_src = (of_slot(run_start) + win_off).astype(i32)
    win_dst = (of_slot(pad_start[None, :] + rows_before) + win_off).astype(i32)
    asg_is = ri[:, :TOP_K, :, None] == experts
    pos = ri[:, TOP_K:, :] + jnp.sum(jnp.where(asg_is, (run_start - before)[:, None, None, :], 0), axis=-1)
    delta = jnp.pad((run_start - before).astype(F32), ((0, 0), (0, LANES - N_EXPERTS)))

    chunk_back = ZCHUNK * (1 + jnp.arange((slack + TB - 1 + ZCHUNK - 1) // ZCHUNK + 1))
    region_chunks = pad_end[:, None] - chunk_back[None, :]
    region_ok = (region_chunks >= pad_start[:, None]) & (region_chunks + ZCHUNK > (pad_start + used)[:, None])
    tail_chunks = pad_end[-1] + ZCHUNK * jnp.arange((n_rows - n_asg) // ZCHUNK)
    zero_start = jnp.concatenate([jnp.where(region_ok, region_chunks, -1).reshape(-1),
                                  jnp.where(tail_chunks < n_rows, tail_chunks, -1)]).astype(i32)

    xin = _dispatch(zero_start, n_win, win_src, win_dst, pos.astype(i32), meta, hn, n_rows)
    yb = _experts(block_e, n_used, xin, w_up, b_up, w_down, b_down)
    out = _combine(n_win, win_src, win_dst, ric, delta.reshape(-1, 1, LANES), yb, h1, lnf_w)
    return out.reshape(bsz, seq, D_MODEL)


def kernel(x, ln1_w, w_in, w_gk2, b_gk2, gla_norm_w, w_proj_a, w_proj_b, w_out, ln2_w, w_router,
           b_router, w_up, b_up, w_down, b_down, lnf_w):
    assert x.shape[-1] == D_MODEL and ln1_w.shape[0] == 1, "one layer of width D_MODEL"
    return _layer(x, ln1_w[0], w_in[0], w_gk2[0], b_gk2[0], gla_norm_w[0], w_proj_a[0], w_proj_b[0],
                  w_out[0], ln2_w[0], w_router[0], b_router[0], w_up[0], b_up[0], w_down[0],
                  b_down[0], lnf_w)
```

```python
import functools

import jax
import jax.numpy as jnp
from jax import lax
from jax.experimental import pallas as pl
from jax.experimental.pallas import tpu as pltpu

F32 = jnp.float32
BF16 = jnp.bfloat16
U32 = jnp.uint32

D_MODEL = 1024
DA_GROUPS = ((128, 1), (512, 4), (2048, 16))
DA_HEADS = 4
DA_HEAD_DIM = 128
DA_WIDTH = DA_HEADS * DA_HEAD_DIM
DA_BLOCK = 128
GLA_HEADS = 4
GLA_KEY_DIM = D_MODEL // 2
GLA_VALUE_DIM = D_MODEL
GLA_DK = GLA_KEY_DIM // GLA_HEADS
GLA_DV = GLA_VALUE_DIM // GLA_HEADS
GLA_GATE_RANK = 16
GLA_GATE_NORMALIZER = 16.0
GLA_CHUNK = 64
N_EXPERTS = 32
TOP_K = 4
D_FF = D_MODEL
SWIGLU_ALPHA = 1.702
SWIGLU_LIMIT = 7.0
RMS_EPS = 1e-5
NEG_INF = -1e30

LANES = 128
QKV_W = 3 * DA_WIDTH
GLA_W = 2 * GLA_KEY_DIM + 2 * GLA_VALUE_DIM + LANES
MERGE_W = 2 * D_MODEL

DA_QB = 4
LSE_LANES = LANES // DA_HEADS
TM_IN = 512
N_CHUNK = 512
T_GLA = 512
GLA_SUB = 256
TM_MIX = 512
TB = 512
TM_ROW = 256
ROW_ALIGN = 8
WIN = 16
N_SLOT = N_EXPERTS + TM_ROW * TOP_K // WIN
SORT_ROWS = TM_ROW * TOP_K + N_EXPERTS * (ROW_ALIGN - 1) + WIN
ZCHUNK = 256
ROW_W = D_MODEL // 2 + LANES
VMEM_LIMIT = 56 * 1024 * 1024
VMEM_LIMIT_INPROJ = 62 * 1024 * 1024

_NT = (((1,), (1,)), ((), ()))
_TN = (((0,), (0,)), ((), ()))


def _dot(a, b):
    return jnp.dot(a, b, preferred_element_type=F32)


def _sigmoid(x):
    return 1.0 / (1.0 + jnp.exp(-x))


def _rms_scale(x):
    return lax.rsqrt(jnp.mean(x * x, axis=-1, keepdims=True) + RMS_EPS)


def _inproj_kernel(x_ref, ln_ref, w_hbm, pa0_ref, pa1_ref, pa2_ref, pg_ref, pm_ref,
                   w_ref, xs_ref, xn_ref, wsem):
    tm = x_ref.shape[1]
    n_slab = D_MODEL // LANES

    @pl.when((pl.program_id(0) == 0) & (pl.program_id(1) == 0))
    def _():
        cp = pltpu.make_async_copy(w_hbm, w_ref, wsem)
        cp.start()
        cp.wait()

    def project(out_write, col0, width, post=None):
        for c0 in range(0, width, N_CHUNK):
            cw = min(N_CHUNK, width - c0)
            val = _dot(xs_ref[...], w_ref[:, col0 + c0:col0 + c0 + cw])
            out_write(c0, cw, (val if post is None else post(val)).astype(BF16))

    x = x_ref[0]
    xn = x * _rms_scale(x) * ln_ref[...]
    xs_ref[...] = xn.astype(BF16)
    for j in range(n_slab):
        xn_ref[j] = xn[:, j * LANES:(j + 1) * LANES]

    def write_to(ref):
        def write(c0, cw, val):
            ref[0, :, c0:c0 + cw] = val
        return write

    def write_qkv(out_ref, d, n):
        def write(c0, cw, val):
            for r in range(d):
                out_ref[0, r, c0 // DA_WIDTH] = val[r * n:(r + 1) * n]
        return write

    project(write_qkv(pa0_ref, 1, tm), 0, QKV_W)
    project(write_to(pg_ref), 3 * QKV_W, GLA_W)
    project(write_to(pm_ref), 3 * QKV_W + GLA_W, MERGE_W, post=_sigmoid)

    for gi, out_ref in ((1, pa1_ref), (2, pa2_ref)):
        d = DA_GROUPS[gi][1]
        n = tm // d
        for r in range(d):
            for j in range(n_slab):
                xs_ref[r * n:(r + 1) * n, j * LANES:(j + 1) * LANES] = (
                    xn_ref[j, pl.ds(r, n, stride=d), :].astype(BF16))
        project(write_qkv(out_ref, d, n), gi * QKV_W, QKV_W)


def _inproj(x, ln1_w, w_all):
    assert N_CHUNK == DA_WIDTH
    bsz, seq, _ = x.shape
    tm = TM_IN
    d1, d2 = DA_GROUPS[1][1], DA_GROUPS[2][1]

    def qkv_shape(d):
        return jax.ShapeDtypeStruct((bsz, d, 3, seq // d, DA_WIDTH), BF16)

    def qkv_spec(d):
        return pl.BlockSpec((1, d, 3, tm // d, DA_WIDTH), lambda b, i: (b, 0, 0, i, 0))

    return pl.pallas_call(
        _inproj_kernel,
        grid=(bsz, seq // tm),
        in_specs=[
            pl.BlockSpec((1, tm, D_MODEL), lambda b, i: (b, i, 0)),
            pl.BlockSpec((1, D_MODEL), lambda b, i: (0, 0)),
            pl.BlockSpec(memory_space=pl.ANY),
        ],
        out_specs=(
            qkv_spec(1), qkv_spec(d1), qkv_spec(d2),
            pl.BlockSpec((1, tm, GLA_W), lambda b, i: (b, i, 0)),
            pl.BlockSpec((1, tm, MERGE_W), lambda b, i: (b, i, 0)),
        ),
        out_shape=(qkv_shape(1), qkv_shape(d1), qkv_shape(d2),
                   jax.ShapeDtypeStruct((bsz, seq, GLA_W), BF16),
                   jax.ShapeDtypeStruct((bsz, seq, MERGE_W), BF16)),
        scratch_shapes=[pltpu.VMEM(w_all.shape, BF16),
                        pltpu.VMEM((tm, D_MODEL), BF16),
                        pltpu.VMEM((D_MODEL // LANES, tm, LANES), F32),
                        pltpu.SemaphoreType.DMA(())],
        compiler_params=pltpu.CompilerParams(
            dimension_semantics=("arbitrary", "arbitrary"), vmem_limit_bytes=VMEM_LIMIT_INPROJ),
        name="inproj",
    )(x, ln1_w.reshape(1, D_MODEL), w_all)


def _dil_attn_kernel(q_ref, kp_ref, kc_ref, vp_ref, vc_ref, o_ref, l_ref, s_ref, p_ref, r_ref):
    n = pl.program_id(2)
    blk = DA_BLOCK
    qi = lax.broadcasted_iota(jnp.int32, (blk, 2 * blk), 0)
    kj = lax.broadcasted_iota(jnp.int32, (blk, 2 * blk), 1)
    band = (kj >= qi) & (kj <= qi + blk)
    band_first = (kj >= jnp.where(n > 0, qi, blk)) & (kj <= qi + blk)
    scale = DA_HEAD_DIM ** -0.5
    items = [(b, h) for b in range(DA_QB) for h in range(DA_HEADS)]

    def rows(b):
        return slice(b * blk, (b + 1) * blk)

    def cols(h):
        return slice(h * DA_HEAD_DIM, (h + 1) * DA_HEAD_DIM)

    def window(prev_ref, cur_ref, b, h):
        before = prev_ref[0, 0, 0, :, cols(h)] if b == 0 else cur_ref[0, 0, 0, rows(b - 1), cols(h)]
        return jnp.concatenate([before, cur_ref[0, 0, 0, rows(b), cols(h)]], axis=0)

    for i, (b, h) in enumerate(items):
        s = lax.dot_general(q_ref[0, 0, 0, rows(b), cols(h)], window(kp_ref, kc_ref, b, h), _NT,
                            preferred_element_type=F32) * scale
        s_ref[i] = jnp.where(band_first if b == 0 else band, s, NEG_INF)
    for i, (b, h) in enumerate(items):
        s = s_ref[i]
        m = jnp.max(s, axis=-1, keepdims=True)
        p = jnp.exp(s - m)
        l = jnp.sum(p, axis=-1, keepdims=True)
        p_ref[i] = p.astype(BF16)
        r_ref[i] = jnp.broadcast_to(1.0 / l, (blk, DA_HEAD_DIM))
        l_ref[0, 0, rows(b), h * LSE_LANES:(h + 1) * LSE_LANES] = jnp.broadcast_to(
            m + jnp.log(l), (blk, LSE_LANES))
    for i, (b, h) in enumerate(items):
        acc = _dot(p_ref[i], window(vp_ref, vc_ref, b, h))
        o_ref[0, 0, rows(b), cols(h)] = (acc * r_ref[i]).astype(o_ref.dtype)


def _dil_attn(pa):
    bsz, d, _, sub_len, _ = pa.shape
    rows = DA_QB * DA_BLOCK
    n_items = DA_QB * DA_HEADS

    def cur(sec):
        return pl.BlockSpec((1, 1, 1, rows, DA_WIDTH), lambda b, r, n: (b, r, sec, n, 0))

    def prev(sec):
        return pl.BlockSpec((1, 1, 1, DA_BLOCK, DA_WIDTH),
                            lambda b, r, n: (b, r, sec, jnp.maximum(n * DA_QB - 1, 0), 0))

    def out(width):
        return pl.BlockSpec((1, 1, rows, width), lambda b, r, n: (b, r, n, 0))

    return pl.pallas_call(
        _dil_attn_kernel,
        grid=(bsz, d, sub_len // rows),
        in_specs=[cur(0), prev(1), cur(1), prev(2), cur(2)],
        out_specs=(out(DA_WIDTH), out(LANES)),
        out_shape=(jax.ShapeDtypeStruct((bsz, d, sub_len, DA_WIDTH), BF16),
                   jax.ShapeDtypeStruct((bsz, d, sub_len, LANES), F32)),
        scratch_shapes=[pltpu.VMEM((n_items, DA_BLOCK, 2 * DA_BLOCK), F32),
                        pltpu.VMEM((n_items, DA_BLOCK, 2 * DA_BLOCK), BF16),
                        pltpu.VMEM((n_items, DA_BLOCK, DA_HEAD_DIM), F32)],
        compiler_params=pltpu.CompilerParams(
            dimension_semantics=("parallel", "parallel", "arbitrary"), vmem_limit_bytes=VMEM_LIMIT),
        name=f"dil_attn_d{d}",
    )(pa, pa, pa, pa, pa)


def _gla_kernel(q_ref, k_ref, v_ref, og_ref, lr_ref, w2_ref, b2_ref, nw_ref, o_ref, st_ref, mask_ref, keep_ref):
    t = pl.program_id(2)
    c = GLA_CHUNK
    tt = q_ref.shape[1]
    n_c = tt // c

    sub = mask_ref.shape[0]

    @pl.when(t == 0)
    def _():
        st_ref[...] = jnp.zeros_like(st_ref)
        row = lax.broadcasted_iota(jnp.int32, (sub, sub), 0)
        col = lax.broadcasted_iota(jnp.int32, (sub, sub), 1)
        keep = (col <= row) & (col >= row - row % c)
        keep_ref[...] = jnp.where(keep, 1.0, 0.0)
        mask_ref[...] = jnp.where(keep, 1.0, 0.0).astype(BF16)

    mask = mask_ref[...]
    gpre = _dot(lr_ref[0], w2_ref[...]) + b2_ref[...]
    forget = (jnp.minimum(gpre, 0.0) - jnp.log(1.0 + jnp.exp(-jnp.abs(gpre)))) / GLA_GATE_NORMALIZER
    g_hi = forget.astype(BF16)
    g_lo = (forget - g_hi.astype(F32)).astype(BF16)
    g_cat = jnp.concatenate([g_hi, g_lo], axis=-1)
    csum = jnp.concatenate([_dot(mask, g_cat[s0:s0 + sub]) for s0 in range(0, tt, sub)], axis=0)
    b = csum[:, :GLA_DK] + csum[:, GLA_DK:]
    b_last = b.reshape(n_c, c, GLA_DK)[:, c - 1:c, :]
    b_to_end = (b_last - b.reshape(n_c, c, GLA_DK)).reshape(tt, GLA_DK)
    q = q_ref[0].astype(F32)
    k = k_ref[0].astype(F32)
    v = v_ref[0]
    q_e = (q * ((GLA_DK ** -0.5) * jnp.exp(b))).astype(BF16)
    k_e = (k * jnp.exp(-b)).astype(BF16)
    k_end = (k * jnp.exp(b_to_end)).astype(BF16)
    o_intra = []
    for s0 in range(0, tt, sub):
        ss = slice(s0, s0 + sub)
        att = lax.dot_general(q_e[ss], k_e[ss], _NT, preferred_element_type=F32)
        att = jnp.where(keep_ref[...] > 0.0, att, 0.0).astype(BF16)
        o_intra.append(_dot(att, v[ss]))
    o_intra = jnp.concatenate(o_intra, axis=0)
    decay = jnp.exp(b_last.reshape(n_c, GLA_DK).T)
    st = st_ref[...]
    outs = []
    for ci in range(n_c):
        rs = slice(ci * c, (ci + 1) * c)
        outs.append(o_intra[rs] + _dot(q_e[rs], st.astype(BF16)))
        st = decay[:, ci:ci + 1] * st + lax.dot_general(k_end[rs], v[rs], _TN, preferred_element_type=F32)
    st_ref[...] = st
    o = jnp.concatenate(outs, axis=0)
    o = o * _rms_scale(o) * nw_ref[...]
    gate = og_ref[0].astype(F32)
    o_ref[0] = (o * (gate * _sigmoid(gate))).astype(BF16)


def _gla(pg, w2_pad, b_gk2, gla_norm_w):
    bsz, seq, _ = pg.shape
    t = T_GLA
    kq = GLA_KEY_DIM // GLA_DK
    kv = 2 * GLA_KEY_DIM // GLA_DV
    kg = kv + GLA_VALUE_DIM // GLA_DV
    klr = (2 * GLA_KEY_DIM + 2 * GLA_VALUE_DIM) // LANES
    return pl.pallas_call(
        _gla_kernel,
        grid=(bsz, GLA_HEADS, seq // t),
        in_specs=[
            pl.BlockSpec((1, t, GLA_DK), lambda b, h, i: (b, i, h)),
            pl.BlockSpec((1, t, GLA_DK), lambda b, h, i: (b, i, kq + h)),
            pl.BlockSpec((1, t, GLA_DV), lambda b, h, i: (b, i, kv + h)),
            pl.BlockSpec((1, t, GLA_DV), lambda b, h, i: (b, i, kg + h)),
            pl.BlockSpec((1, t, LANES), lambda b, h, i: (b, i, klr)),
            pl.BlockSpec((LANES, GLA_DK), lambda b, h, i: (0, h)),
            pl.BlockSpec((1, GLA_DK), lambda b, h, i: (0, h)),
            pl.BlockSpec((1, GLA_DV), lambda b, h, i: (0, 0)),
        ],
        out_specs=pl.BlockSpec((1, t, GLA_DV), lambda b, h, i: (b, i, h)),
        out_shape=jax.ShapeDtypeStruct((bsz, seq, GLA_VALUE_DIM), BF16),
        scratch_shapes=[pltpu.VMEM((GLA_DK, GLA_DV), F32), pltpu.VMEM((GLA_SUB, GLA_SUB), BF16),
                        pltpu.VMEM((GLA_SUB, GLA_SUB), F32)],
        compiler_params=pltpu.CompilerParams(
            dimension_semantics=("parallel", "parallel", "arbitrary"), vmem_limit_bytes=VMEM_LIMIT),
        name="gla",
    )(pg, pg, pg, pg, pg, w2_pad, b_gk2.reshape(1, GLA_KEY_DIM), gla_norm_w.reshape(1, GLA_DV))


def _mix_kernel(x_ref, o0_ref, l0_ref, o1_ref, l1_ref, o2_ref, l2_ref, ob_ref, pm_ref,
                wa_ref, wb_ref, wo_ref, ln2_ref, wrh_ref, wrl_ref, br_ref,
                h1_ref, hn_ref, ri_ref, ric_ref, meta_ref, cbefore_ref, cnt_ref,
                po1_ref, pl1_ref, po2_ref, pl2_ref, carry_ref):
    step = pl.program_id(0)
    tm = x_ref.shape[0]

    @pl.when(step == 0)
    def _():
        carry_ref[...] = jnp.zeros_like(carry_ref)

    for o_ref, l_ref, po_ref, pl_ref, (_, d) in ((o1_ref, l1_ref, po1_ref, pl1_ref, DA_GROUPS[1]),
                                                 (o2_ref, l2_ref, po2_ref, pl2_ref, DA_GROUPS[2])):
        n = tm // d
        for r in range(d):
            pl_ref[pl.ds(r, n, stride=d), :] = l_ref[0, r]
            for h in range(DA_HEADS):
                sl = slice(h * DA_HEAD_DIM, (h + 1) * DA_HEAD_DIM)
                po_ref[h, pl.ds(r, n, stride=d), :] = o_ref[0, r, :, sl].astype(F32)

    l0, l1, l2 = l0_ref[...], pl1_ref[...], pl2_ref[...]
    mx = jnp.maximum(jnp.maximum(l0, l1), l2)
    e0, e1, e2 = jnp.exp(l0 - mx), jnp.exp(l1 - mx), jnp.exp(l2 - mx)
    inv = 1.0 / (e0 + e1 + e2)
    w0, w1, w2 = e0 * inv, e1 * inv, e2 * inv
    heads = []
    for h in range(DA_HEADS):
        sl = slice(h * DA_HEAD_DIM, (h + 1) * DA_HEAD_DIM)
        at = slice(h * LSE_LANES, h * LSE_LANES + 1)
        o_h = w0[:, at] * o0_ref[:, sl].astype(F32) + w1[:, at] * po1_ref[h] + w2[:, at] * po2_ref[h]
        heads.append(o_h.astype(BF16))
    o_a = jnp.concatenate(heads, axis=-1)

    gates = pm_ref[...].astype(F32)
    mixed = (gates[:, :D_MODEL] * _dot(o_a, wa_ref[...])
             + gates[:, D_MODEL:] * _dot(ob_ref[...], wb_ref[...]))
    h1 = x_ref[...] + _dot(mixed.astype(BF16), wo_ref[...])
    h1_ref[...] = h1
    hn = h1 * _rms_scale(h1) * ln2_ref[...]

    hn_hi = hn.astype(BF16)
    hn_ref[...] = hn_hi
    hn_lo = (hn - hn_hi.astype(F32)).astype(BF16)
    logits = (_dot(hn_hi, wrh_ref[...]) + _dot(hn_lo, wrh_ref[...]) + _dot(hn_hi, wrl_ref[...])
              + br_ref[...])
    lane = lax.broadcasted_iota(jnp.int32, (tm, LANES), 1).astype(F32)
    work = jnp.where(lane < N_EXPERTS, logits, -jnp.inf)
    vals, idxs = [], []
    for _ in range(TOP_K):
        m = jnp.max(work, axis=-1, keepdims=True)
        idx = jnp.min(jnp.where(work == m, lane, float(LANES)), axis=-1, keepdims=True)
        vals.append(m)
        idxs.append(idx)
        work = jnp.where(lane == idx, -jnp.inf, work)
    exps = [jnp.exp(v - vals[0]) for v in vals]
    denom = exps[0] + exps[1] + exps[2] + exps[3]

    onehot = jnp.zeros((tm, LANES), F32)
    for idx in idxs:
        onehot = onehot + jnp.where(lane == idx, 1.0, 0.0)
    row = lax.broadcasted_iota(jnp.int32, (tm, tm), 0)
    col = lax.broadcasted_iota(jnp.int32, (tm, tm), 1)
    below = jnp.where(col < row, 1.0, 0.0).astype(BF16)
    before = _dot(below, onehot.astype(BF16)) + carry_ref[0:1, :]
    ranks = [jnp.sum(jnp.where(lane == idx, before, 0.0), axis=-1, keepdims=True) for idx in idxs]
    running = carry_ref[0:1, :]
    for j in range(tm // TM_ROW):
        cbefore_ref[j] = jnp.broadcast_to(running, cbefore_ref.shape[1:])
        running = running + jnp.sum(onehot[j * TM_ROW:(j + 1) * TM_ROW], axis=0, keepdims=True)
    carry = running
    carry_ref[...] = jnp.broadcast_to(carry, carry_ref.shape)
    cnt_ref[...] = jnp.broadcast_to(carry, cnt_ref.shape)

    ri = jnp.zeros((tm, LANES), F32)
    for j, val in enumerate(idxs + ranks):
        ri = jnp.where(lane == float(j), val, ri)
    ric_ref[...] = ri.astype(jnp.int32)
    ri_t = ri.T[:2 * TOP_K].astype(jnp.int32)
    for j in range(tm // TM_ROW):
        ri_ref[j] = ri_t[:, j * TM_ROW:(j + 1) * TM_ROW]
    meta = jnp.zeros((tm, LANES), F32)
    for k in range(TOP_K):
        gate = exps[k] / denom
        gate_hi = gate.astype(BF16).astype(F32)
        meta = jnp.where(lane == float(k), idxs[k], meta)
        meta = jnp.where(lane == float(TOP_K + k), gate_hi, meta)
        meta = jnp.where(lane == float(2 * TOP_K + k), gate - gate_hi, meta)
    meta_ref[...] = meta.astype(BF16)


def _mix(x2, o0, l0, o1, l1, o2, l2, o_b, pm, wa, wb, wo, ln2_w, wr_hi, wr_lo, br_pad):
    n_tok = x2.shape[0]
    tm = TM_MIX
    bsz = o1.shape[0]
    d1, d2 = DA_GROUPS[1][1], DA_GROUPS[2][1]
    tiles_per_seq = (n_tok // bsz) // tm

    def rows(width):
        return pl.BlockSpec((tm, width), lambda i: (i, 0))

    def residue_major(d, width):
        return pl.BlockSpec((1, d, tm // d, width),
                            lambda i: (i // tiles_per_seq, 0, i % tiles_per_seq, 0))

    def whole(arr):
        return pl.BlockSpec(arr.shape, lambda i: (0,) * arr.ndim)

    ln2 = ln2_w.reshape(1, D_MODEL)
    return pl.pallas_call(
        _mix_kernel,
        grid=(n_tok // tm,),
        in_specs=[rows(D_MODEL), rows(DA_WIDTH), rows(LANES),
                  residue_major(d1, DA_WIDTH), residue_major(d1, LANES),
                  residue_major(d2, DA_WIDTH), residue_major(d2, LANES),
                  rows(GLA_VALUE_DIM), rows(MERGE_W),
                  whole(wa), whole(wb), whole(wo), whole(ln2), whole(wr_hi), whole(wr_lo), whole(br_pad)],
        out_specs=(rows(D_MODEL), rows(D_MODEL),
                   pl.BlockSpec((tm // TM_ROW, 2 * TOP_K, TM_ROW), lambda i: (i, 0, 0)),
                   rows(LANES), rows(LANES),
                   pl.BlockSpec((tm // TM_ROW, 8, LANES), lambda i: (i, 0, 0)),
                   pl.BlockSpec((8, LANES), lambda i: (0, 0))),
        out_shape=(jax.ShapeDtypeStruct((n_tok, D_MODEL), F32),
                   jax.ShapeDtypeStruct((n_tok, D_MODEL), BF16),
                   jax.ShapeDtypeStruct((n_tok // TM_ROW, 2 * TOP_K, TM_ROW), jnp.int32),
                   jax.ShapeDtypeStruct((n_tok, LANES), jnp.int32),
                   jax.ShapeDtypeStruct((n_tok, LANES), BF16),
                   jax.ShapeDtypeStruct((n_tok // TM_ROW, 8, LANES), F32),
                   jax.ShapeDtypeStruct((8, LANES), F32)),
        scratch_shapes=[pltpu.VMEM((DA_HEADS, tm, DA_HEAD_DIM), F32), pltpu.VMEM((tm, LANES), F32),
                        pltpu.VMEM((DA_HEADS, tm, DA_HEAD_DIM), F32), pltpu.VMEM((tm, LANES), F32),
                        pltpu.VMEM((8, LANES), F32)],
        compiler_params=pltpu.CompilerParams(
            dimension_semantics=("arbitrary",), vmem_limit_bytes=VMEM_LIMIT),
        name="mix",
    )(x2, o0, l0, o1, l1, o2, l2, o_b, pm, wa, wb, wo, ln2, wr_hi, wr_lo, br_pad)


def _pack_pairs(x):
    n = x.shape[1] // 2
    rounded = x.astype(BF16).astype(F32)
    lo = lax.bitcast_convert_type(rounded[:, :n], U32) >> 16
    hi = lax.bitcast_convert_type(rounded[:, n:], U32) & jnp.uint32(0xFFFF0000)
    return hi | lo


def _unpack_pairs(u):
    lo = lax.bitcast_convert_type(u << 16, F32).astype(BF16)
    hi = lax.bitcast_convert_type(u & jnp.uint32(0xFFFF0000), F32).astype(BF16)
    return lo, hi


def _dispatch_kernel(zstart_ref, nwin_ref, wsrc_ref, wdst_ref, pos_ref, meta_ref, hn_ref, xin_ref,
                     buf_ref, zero_ref, sem, zsem):
    i = pl.program_id(0)
    tm = hn_ref.shape[0]
    n_buf_rows = buf_ref.shape[1]
    slot = i % 2

    def window_copy(s, buf_slot):
        src = pl.multiple_of(wsrc_ref[i * N_SLOT + s], ROW_ALIGN)
        dst = pl.multiple_of(wdst_ref[i * N_SLOT + s], ROW_ALIGN)
        return pltpu.make_async_copy(buf_ref.at[buf_slot, pl.ds(src, WIN), :],
                                     xin_ref.at[pl.ds(dst, WIN), :], sem)

    def wait_windows(step):
        def body(s, carry):
            pltpu.make_async_copy(buf_ref.at[0, pl.ds(0, WIN), :], xin_ref.at[pl.ds(0, WIN), :], sem).wait()
            return carry
        lax.fori_loop(0, nwin_ref[step], body, 0)

    @pl.when(i == 0)
    def _():
        zero_ref[...] = jnp.zeros_like(zero_ref)
        for j in range(zstart_ref.shape[0]):
            @pl.when(zstart_ref[j] >= 0)
            def _():
                start = pl.multiple_of(zstart_ref[j], ZCHUNK)
                cp = pltpu.make_async_copy(zero_ref, xin_ref.at[pl.ds(start, ZCHUNK), :], zsem)
                cp.start()
                cp.wait()

    row = lax.broadcasted_iota(jnp.int32, (n_buf_rows, tm), 0)
    perm = jnp.zeros((n_buf_rows, tm), F32)
    for k in range(TOP_K):
        perm = perm + jnp.where(row == pos_ref[0, k:k + 1, :], 1.0, 0.0)
    perm = perm.astype(BF16)
    buf_ref[slot, :, :D_MODEL // 2] = _pack_pairs(_dot(perm, hn_ref[...]))
    buf_ref[slot, :, D_MODEL // 2:] = lax.bitcast_convert_type(_dot(perm, meta_ref[...]), U32)

    @pl.when(i > 0)
    def _():
        wait_windows(i - 1)

    for buf_slot in range(2):
        @pl.when(slot == buf_slot)
        def _():
            def issue(s, carry):
                window_copy(s, buf_slot).start()
                return carry
            lax.fori_loop(0, nwin_ref[i], issue, 0)

    @pl.when(i == pl.num_programs(0) - 1)
    def _():
        wait_windows(i)


def _dispatch(zero_start, n_win, win_src, win_dst, pos, meta, hn, n_rows):
    n_tok = hn.shape[0]
    tm = TM_ROW
    n_tiles = n_tok // tm
    n_buf_rows = SORT_ROWS
    return pl.pallas_call(
        _dispatch_kernel,
        grid_spec=pltpu.PrefetchScalarGridSpec(
            num_scalar_prefetch=4,
            grid=(n_tiles,),
            in_specs=[
                pl.BlockSpec((1, TOP_K, tm), lambda i, *_: (i, 0, 0)),
                pl.BlockSpec((tm, LANES), lambda i, *_: (i, 0)),
                pl.BlockSpec((tm, D_MODEL), lambda i, *_: (i, 0)),
            ],
            out_specs=pl.BlockSpec(memory_space=pl.ANY),
            scratch_shapes=[pltpu.VMEM((2, n_buf_rows, ROW_W), U32),
                            pltpu.VMEM((ZCHUNK, ROW_W), U32),
                            pltpu.SemaphoreType.DMA(()), pltpu.SemaphoreType.DMA(())],
        ),
        out_shape=jax.ShapeDtypeStruct((n_rows, ROW_W), U32),
        compiler_params=pltpu.CompilerParams(
            dimension_semantics=("arbitrary",), vmem_limit_bytes=VMEM_LIMIT),
        name="dispatch",
    )(zero_start, n_win, win_src.reshape(-1), win_dst.reshape(-1), pos, meta, hn)


def _expert_kernel(be_ref, nused_ref, x_ref, wu_ref, bu_ref, wd_ref, bd_ref, y_ref, wu16_ref, wd16_ref):
    i = pl.program_id(0)
    used = i < nused_ref[0]

    @pl.when((i == 0) | (be_ref[i] != be_ref[jnp.maximum(i - 1, 0)]))
    def _():
        wu16_ref[...] = wu_ref[0].astype(BF16)
        wd16_ref[...] = wd_ref[0].astype(BF16)

    @pl.when(jnp.logical_not(used))
    def _():
        y_ref[...] = jnp.zeros_like(y_ref)

    @pl.when(used)
    def _():
        half = D_MODEL // 2
        x_lo, x_hi = _unpack_pairs(x_ref[:, :half])
        meta = lax.bitcast_convert_type(x_ref[:, half:half + 3 * TOP_K], F32)
        expert = be_ref[i].astype(F32)
        gate = jnp.zeros((x_ref.shape[0], 1), F32)
        for k in range(TOP_K):
            weight = meta[:, TOP_K + k:TOP_K + k + 1] + meta[:, 2 * TOP_K + k:2 * TOP_K + k + 1]
            gate = gate + jnp.where(meta[:, k:k + 1] == expert, weight, 0.0)
        hu = _dot(x_lo, wu16_ref[:half, :]) + _dot(x_hi, wu16_ref[half:, :]) + bu_ref[0]
        x_glu = jnp.minimum(hu[:, :D_FF], SWIGLU_LIMIT)
        x_lin = jnp.clip(hu[:, D_FF:], -SWIGLU_LIMIT, SWIGLU_LIMIT)
        act = x_glu * _sigmoid(SWIGLU_ALPHA * x_glu) * (x_lin + 1.0)
        y_ref[...] = _pack_pairs((_dot(act.astype(BF16), wd16_ref[...]) + bd_ref[0]) * gate)


def _experts(block_e, n_used, xin, w_up, b_up, w_down, b_down):
    n_rows = xin.shape[0]
    return pl.pallas_call(
        _expert_kernel,
        grid_spec=pltpu.PrefetchScalarGridSpec(
            num_scalar_prefetch=2,
            grid=(n_rows // TB,),
            in_specs=[
                pl.BlockSpec((TB, ROW_W), lambda i, be, nu: (i, 0)),
                pl.BlockSpec((1, D_MODEL, 2 * D_FF), lambda i, be, nu: (be[i], 0, 0)),
                pl.BlockSpec((1, 1, 2 * D_FF), lambda i, be, nu: (be[i], 0, 0)),
                pl.BlockSpec((1, D_FF, D_MODEL), lambda i, be, nu: (be[i], 0, 0)),
                pl.BlockSpec((1, 1, D_MODEL), lambda i, be, nu: (be[i], 0, 0)),
            ],
            out_specs=pl.BlockSpec((TB, D_MODEL // 2), lambda i, be, nu: (i, 0)),
            scratch_shapes=[pltpu.VMEM((D_MODEL, 2 * D_FF), BF16), pltpu.VMEM((D_FF, D_MODEL), BF16)],
        ),
        out_shape=jax.ShapeDtypeStruct((n_rows, D_MODEL // 2), U32),
        compiler_params=pltpu.CompilerParams(
            dimension_semantics=("arbitrary",), vmem_limit_bytes=VMEM_LIMIT),
        name="experts",
    )(block_e, n_used, xin, w_up, b_up.reshape(N_EXPERTS, 1, 2 * D_FF),
      w_down, b_down.reshape(N_EXPERTS, 1, D_MODEL))


def _combine_kernel(nwin_ref, wsrc_ref, wdst_ref, ric_ref, delta_ref, yb_ref, h1_ref, lnf_ref, o_ref,
                    stage_ref, buf_ref, sem):
    i = pl.program_id(0)
    tm = h1_ref.shape[0]
    n_buf_rows = buf_ref.shape[0]

    parity = i % 2

    def fetch(step, stage_slot):
        def body(s, carry):
            dst = pl.multiple_of(wdst_ref[step * N_SLOT + s], ROW_ALIGN)
            pltpu.make_async_copy(
                yb_ref.at[pl.ds(dst, WIN), :],
                stage_ref.at[stage_slot, pl.ds(pl.multiple_of(s * WIN, WIN), WIN), :],
                sem.at[stage_slot]).start()
            return carry
        lax.fori_loop(0, nwin_ref[step], body, 0)

    @pl.when(i == 0)
    def _():
        buf_ref[...] = jnp.zeros_like(buf_ref)
        fetch(i, 0)

    for stage_slot in range(2):
        @pl.when((i + 1 < pl.num_programs(0)) & (parity != stage_slot))
        def _():
            fetch(i + 1, stage_slot)

    ric = ric_ref[...].astype(F32)
    lane = lax.broadcasted_iota(jnp.int32, (tm, LANES), 1).astype(F32)
    col = lax.broadcasted_iota(jnp.int32, (tm, n_buf_rows), 1)
    pick = jnp.zeros((tm, n_buf_rows), F32)
    for k in range(TOP_K):
        offset = jnp.sum(jnp.where(lane == ric[:, k:k + 1], delta_ref[0], 0.0), axis=-1, keepdims=True)
        pos = (ric[:, TOP_K + k:TOP_K + k + 1] + offset).astype(jnp.int32)
        pick = pick + jnp.where(col == pos, 1.0, 0.0)
    pick = pick.astype(BF16)

    for stage_slot in range(2):
        @pl.when(parity == stage_slot)
        def _():
            def drain(s, carry):
                pltpu.make_async_copy(yb_ref.at[pl.ds(0, WIN), :], stage_ref.at[stage_slot, pl.ds(0, WIN), :],
                                      sem.at[stage_slot]).wait()
                return carry

            def compact(s, carry):
                src = pl.multiple_of(wsrc_ref[i * N_SLOT + s], ROW_ALIGN)
                buf_ref[pl.ds(src, WIN), :] = stage_ref[stage_slot, pl.ds(pl.multiple_of(s * WIN, WIN), WIN), :]
                return carry

            lax.fori_loop(0, nwin_ref[i], drain, 0)
            lax.fori_loop(0, nwin_ref[i], compact, 0)

    y_lo, y_hi = _unpack_pairs(buf_ref[...])
    h2 = h1_ref[...] + jnp.concatenate([_dot(pick, y_lo), _dot(pick, y_hi)], axis=-1)
    o_ref[...] = h2 * _rms_scale(h2) * lnf_ref[...]


def _combine(n_win, win_src, win_dst, ric, delta, yb, h1, lnf_w):
    n_tok = h1.shape[0]
    tm = TM_ROW
    n_tiles = n_tok // tm
    return pl.pallas_call(
        _combine_kernel,
        grid_spec=pltpu.PrefetchScalarGridSpec(
            num_scalar_prefetch=3,
            grid=(n_tiles,),
            in_specs=[
                pl.BlockSpec((tm, LANES), lambda i, *_: (i, 0)),
                pl.BlockSpec((1, 1, LANES), lambda i, *_: (i, 0, 0)),
                pl.BlockSpec(memory_space=pl.ANY),
                pl.BlockSpec((tm, D_MODEL), lambda i, *_: (i, 0)),
                pl.BlockSpec((1, D_MODEL), lambda i, *_: (0, 0)),
            ],
            out_specs=pl.BlockSpec((tm, D_MODEL), lambda i, *_: (i, 0)),
            scratch_shapes=[pltpu.VMEM((2, N_SLOT * WIN, D_MODEL // 2), U32),
                            pltpu.VMEM((SORT_ROWS, D_MODEL // 2), U32), pltpu.SemaphoreType.DMA((2,))],
        ),
        out_shape=jax.ShapeDtypeStruct((n_tok, D_MODEL), F32),
        compiler_params=pltpu.CompilerParams(
            dimension_semantics=("arbitrary",), vmem_limit_bytes=VMEM_LIMIT),
        name="combine",
    )(n_win, win_src.reshape(-1), win_dst.reshape(-1), ric, delta, yb, h1, lnf_w.reshape(1, D_MODEL))


def _layer(h, ln1_w, w_in, w_gk2, b_gk2, gla_norm_w, w_proj_a, w_proj_b, w_out,
           ln2_w, w_router, b_router, w_up, b_up, w_down, b_down, lnf_w):
    bsz, seq, _ = h.shape
    n_tok = bsz * seq

    n_main = 3 * QKV_W + 2 * GLA_KEY_DIM + 2 * GLA_VALUE_DIM
    pad = LANES - GLA_GATE_RANK
    w_all = jnp.concatenate(
        [w_in[:, :n_main + GLA_GATE_RANK], jnp.zeros((D_MODEL, pad), F32), w_in[:, n_main + GLA_GATE_RANK:]],
        axis=1).astype(BF16)
    w2_pad = jnp.concatenate([w_gk2, jnp.zeros((pad, GLA_KEY_DIM), F32)], axis=0).astype(BF16)
    wr_pad = jnp.concatenate([w_router, jnp.zeros((D_MODEL, LANES - N_EXPERTS), F32)], axis=1)
    wr_hi = wr_pad.astype(BF16)
    wr_lo = (wr_pad - wr_hi.astype(F32)).astype(BF16)
    br_pad = jnp.concatenate([b_router, jnp.zeros((LANES - N_EXPERTS,), F32)]).reshape(1, LANES)

    pa0, pa1, pa2, pg, pm = _inproj(h, ln1_w, w_all)
    o0, l0 = _dil_attn(pa0)
    o1, l1 = _dil_attn(pa1)
    o2, l2 = _dil_attn(pa2)
    o_b = _gla(pg, w2_pad, b_gk2, gla_norm_w)

    h1, hn, ri, ric, meta, carry_f, cnt = _mix(
        h.reshape(n_tok, D_MODEL), o0.reshape(n_tok, DA_WIDTH), l0.reshape(n_tok, LANES),
        o1, l1, o2, l2, o_b.reshape(n_tok, GLA_VALUE_DIM), pm.reshape(n_tok, MERGE_W),
        w_proj_a.astype(BF16), w_proj_b.astype(BF16), w_out.astype(BF16), ln2_w, wr_hi, wr_lo, br_pad)

    i32 = jnp.int32
    n_tiles = n_tok // TM_ROW
    experts = jnp.arange(N_EXPERTS)
    counts = cnt[0, :N_EXPERTS].astype(i32)
    before = carry_f[:, 0, :N_EXPERTS].astype(i32)
    run = jnp.concatenate([before[1:], counts[None]], axis=0) - before
    run_al = (run + ROW_ALIGN - 1) // ROW_ALIGN * ROW_ALIGN
    rows_end = jnp.cumsum(run_al, axis=0)
    rows_before = rows_end - run_al
    used = rows_end[-1]
    slack = WIN - ROW_ALIGN
    padded = (used + slack + TB - 1) // TB * TB
    pad_end = jnp.cumsum(padded)
    pad_start = pad_end - padded
    n_asg = n_tok * TOP_K
    n_rows = (n_asg + n_tiles * N_EXPERTS * (ROW_ALIGN - 1)
              + N_EXPERTS * (slack + TB - 1) + TB - 1) // TB * TB
    n_blocks = n_rows // TB
    block_starts = jnp.arange(n_blocks) * TB
    block_e = jnp.minimum(jnp.sum(pad_end[None, :] <= block_starts[:, None], axis=1),
                          N_EXPERTS - 1).astype(i32)
    n_used = (pad_end[-1:] // TB).astype(i32)

    run_end = jnp.cumsum(run_al, axis=1)
    run_start = run_end - run_al
    wins = (run + WIN - 1) // WIN
    wins_end = jnp.cumsum(wins, axis=1)
    wins_start = wins_end - wins
    n_win = wins_end[:, -1].astype(i32)
    slots = jnp.arange(N_SLOT)
    slot_e = jnp.sum(wins_end[:, None, :] <= slots[None, :, None], axis=-1)
    slot_is = slot_e[..., None] == experts

    def of_slot(table):
        return jnp.sum(jnp.where(slot_is, table[:, None, :], 0), axis=-1)

    win_off = (slots[None, :] - of_slot(wins_start)) * WIN
    win_src = (of_slot(run_start) + win_off).astype(i32)
    win_dst = (of_slot(pad_start[None, :] + rows_before) + win_off).astype(i32)
    asg_is = ri[:, :TOP_K, :, None] == experts
    pos = ri[:, TOP_K:, :] + jnp.sum(jnp.where(asg_is, (run_start - before)[:, None, None, :], 0), axis=-1)
    delta = jnp.pad((run_start - before).astype(F32), ((0, 0), (0, LANES - N_EXPERTS)))

    chunk_back = ZCHUNK * (1 + jnp.arange((slack + TB - 1 + ZCHUNK - 1) // ZCHUNK + 1))
    region_chunks = pad_end[:, None] - chunk_back[None, :]
    region_ok = (region_chunks >= pad_start[:, None]) & (region_chunks + ZCHUNK > (pad_start + used)[:, None])
    tail_chunks = pad_end[-1] + ZCHUNK * jnp.arange((n_rows - n_asg) // ZCHUNK)
    zero_start = jnp.concatenate([jnp.where(region_ok, region_chunks, -1).reshape(-1),
                                  jnp.where(tail_chunks < n_rows, tail_chunks, -1)]).astype(i32)

    xin = _dispatch(zero_start, n_win, win_src, win_dst, pos.astype(i32), meta, hn, n_rows)
    yb = _experts(block_e, n_used, xin, w_up, b_up, w_down, b_down)
    out = _combine(n_win, win_src, win_dst, ric, delta.reshape(-1, 1, LANES), yb, h1, lnf_w)
    return out.reshape(bsz, seq, D_MODEL)


def kernel(x, ln1_w, w_in, w_gk2, b_gk2, gla_norm_w, w_proj_a, w_proj_b, w_out, ln2_w, w_router,
           b_router, w_up, b_up, w_down, b_down, lnf_w):
    assert x.shape[-1] == D_MODEL and ln1_w.shape[0] == 1, "one layer of width D_MODEL"
    return _layer(x, ln1_w[0], w_in[0], w_gk2[0], b_gk2[0], gla_norm_w[0], w_proj_a[0], w_proj_b[0],
                  w_out[0], ln2_w[0], w_router[0], b_router[0], w_up[0], b_up[0], w_down[0],
                  b_down[0], lnf_w)
```

```python
import functools

import jax
import jax.numpy as jnp
from jax import lax
from jax.experimental import pallas as pl
from jax.experimental.pallas import tpu as pltpu

F32 = jnp.float32
BF16 = jnp.bfloat16
U32 = jnp.uint32

D_MODEL = 1024
DA_GROUPS = ((128, 1), (512, 4), (2048, 16))
DA_HEADS = 4
DA_HEAD_DIM = 128
DA_WIDTH = DA_HEADS * DA_HEAD_DIM
DA_BLOCK = 128
GLA_HEADS = 4
GLA_KEY_DIM = D_MODEL // 2
GLA_VALUE_DIM = D_MODEL
GLA_DK = GLA_KEY_DIM // GLA_HEADS
GLA_DV = GLA_VALUE_DIM // GLA_HEADS
GLA_GATE_RANK = 16
GLA_GATE_NORMALIZER = 16.0
GLA_CHUNK = 64
N_EXPERTS = 32
TOP_K = 4
D_FF = D_MODEL
SWIGLU_ALPHA = 1.702
SWIGLU_LIMIT = 7.0
RMS_EPS = 1e-5
NEG_INF = -1e30

LANES = 128
QKV_W = 3 * DA_WIDTH
GLA_W = 2 * GLA_KEY_DIM + 2 * GLA_VALUE_DIM + LANES
MERGE_W = 2 * D_MODEL

DA_QB = 4
LSE_LANES = LANES // DA_HEADS
TM_IN = 512
N_CHUNK = 512
T_GLA = 512
GLA_SUB = 256
GLA_HEADS_PER_STEP = 4
TM_MIX = 512
TB = 512
TM_ROW = 256
ROW_ALIGN = 8
WIN = 32
N_SLOT = N_EXPERTS + TM_ROW * TOP_K // WIN
SORT_ROWS = TM_ROW * TOP_K + N_EXPERTS * (ROW_ALIGN - 1) + WIN
ZCHUNK = 256
ROW_W = D_MODEL // 2 + LANES
VMEM_LIMIT = 56 * 1024 * 1024
VMEM_LIMIT_INPROJ = 62 * 1024 * 1024

_NT = (((1,), (1,)), ((), ()))
_TN = (((0,), (0,)), ((), ()))


def _dot(a, b):
    return jnp.dot(a, b, preferred_element_type=F32)


def _sigmoid(x):
    return 1.0 / (1.0 + jnp.exp(-x))


def _rms_scale(x):
    return lax.rsqrt(jnp.mean(x * x, axis=-1, keepdims=True) + RMS_EPS)


def _inproj_kernel(x_ref, ln_ref, w_hbm, pa0_ref, pa1_ref, pa2_ref, pg_ref, pm_ref,
                   w_ref, xs_ref, xn_ref, wsem):
    tm = x_ref.shape[1]
    n_slab = D_MODEL // LANES

    @pl.when((pl.program_id(0) == 0) & (pl.program_id(1) == 0))
    def _():
        cp = pltpu.make_async_copy(w_hbm, w_ref, wsem)
        cp.start()
        cp.wait()

    def project(out_write, col0, width, post=None):
        for c0 in range(0, width, N_CHUNK):
            cw = min(N_CHUNK, width - c0)
            val = _dot(xs_ref[...], w_ref[:, col0 + c0:col0 + c0 + cw])
            out_write(c0, cw, (val if post is None else post(val)).astype(BF16))

    x = x_ref[0]
    xn = x * _rms_scale(x) * ln_ref[...]
    xs_ref[...] = xn.astype(BF16)
    for j in range(n_slab):
        xn_ref[j] = xn[:, j * LANES:(j + 1) * LANES]

    def write_to(ref):
        def write(c0, cw, val):
            ref[0, :, c0:c0 + cw] = val
        return write

    def write_qkv(out_ref, d, n):
        def write(c0, cw, val):
            for r in range(d):
                out_ref[0, r, c0 // DA_WIDTH] = val[r * n:(r + 1) * n]
        return write

    project(write_qkv(pa0_ref, 1, tm), 0, QKV_W)
    project(write_to(pg_ref), 3 * QKV_W, GLA_W)
    project(write_to(pm_ref), 3 * QKV_W + GLA_W, MERGE_W, post=_sigmoid)

    for gi, out_ref in ((1, pa1_ref), (2, pa2_ref)):
        d = DA_GROUPS[gi][1]
        n = tm // d
        for r in range(d):
            for j in range(n_slab):
                xs_ref[r * n:(r + 1) * n, j * LANES:(j + 1) * LANES] = (
                    xn_ref[j, pl.ds(r, n, stride=d), :].astype(BF16))
        project(write_qkv(out_ref, d, n), gi * QKV_W, QKV_W)


def _inproj(x, ln1_w, w_all):
    assert N_CHUNK == DA_WIDTH
    bsz, seq, _ = x.shape
    tm = TM_IN
    d1, d2 = DA_GROUPS[1][1], DA_GROUPS[2][1]

    def qkv_shape(d):
        return jax.ShapeDtypeStruct((bsz, d, 3, seq // d, DA_WIDTH), BF16)

    def qkv_spec(d):
        return pl.BlockSpec((1, d, 3, tm // d, DA_WIDTH), lambda b, i: (b, 0, 0, i, 0))

    return pl.pallas_call(
        _inproj_kernel,
        grid=(bsz, seq // tm),
        in_specs=[
            pl.BlockSpec((1, tm, D_MODEL), lambda b, i: (b, i, 0)),
            pl.BlockSpec((1, D_MODEL), lambda b, i: (0, 0)),
            pl.BlockSpec(memory_space=pl.ANY),
        ],
        out_specs=(
            qkv_spec(1), qkv_spec(d1), qkv_spec(d2),
            pl.BlockSpec((1, tm, GLA_W), lambda b, i: (b, i, 0)),
            pl.BlockSpec((1, tm, MERGE_W), lambda b, i: (b, i, 0)),
        ),
        out_shape=(qkv_shape(1), qkv_shape(d1), qkv_shape(d2),
                   jax.ShapeDtypeStruct((bsz, seq, GLA_W), BF16),
                   jax.ShapeDtypeStruct((bsz, seq, MERGE_W), BF16)),
        scratch_shapes=[pltpu.VMEM(w_all.shape, BF16),
                        pltpu.VMEM((tm, D_MODEL), BF16),
                        pltpu.VMEM((D_MODEL // LANES, tm, LANES), F32),
                        pltpu.SemaphoreType.DMA(())],
        compiler_params=pltpu.CompilerParams(
            dimension_semantics=("arbitrary", "arbitrary"), vmem_limit_bytes=VMEM_LIMIT_INPROJ),
        name="inproj",
    )(x, ln1_w.reshape(1, D_MODEL), w_all)


def _dil_attn_kernel(q_ref, kp_ref, kc_ref, vp_ref, vc_ref, o_ref, l_ref, s_ref, p_ref, r_ref):
    n = pl.program_id(2)
    blk = DA_BLOCK
    qi = lax.broadcasted_iota(jnp.int32, (blk, 2 * blk), 0)
    kj = lax.broadcasted_iota(jnp.int32, (blk, 2 * blk), 1)
    band = (kj >= qi) & (kj <= qi + blk)
    band_first = (kj >= jnp.where(n > 0, qi, blk)) & (kj <= qi + blk)
    scale = DA_HEAD_DIM ** -0.5
    items = [(b, h) for b in range(DA_QB) for h in range(DA_HEADS)]

    def rows(b):
        return slice(b * blk, (b + 1) * blk)

    def cols(h):
        return slice(h * DA_HEAD_DIM, (h + 1) * DA_HEAD_DIM)

    def window(prev_ref, cur_ref, b, h):
        before = prev_ref[0, 0, 0, :, cols(h)] if b == 0 else cur_ref[0, 0, 0, rows(b - 1), cols(h)]
        return jnp.concatenate([before, cur_ref[0, 0, 0, rows(b), cols(h)]], axis=0)

    for i, (b, h) in enumerate(items):
        s = lax.dot_general(q_ref[0, 0, 0, rows(b), cols(h)], window(kp_ref, kc_ref, b, h), _NT,
                            preferred_element_type=F32) * scale
        s_ref[i] = jnp.where(band_first if b == 0 else band, s, NEG_INF)
    for i, (b, h) in enumerate(items):
        s = s_ref[i]
        m = jnp.max(s, axis=-1, keepdims=True)
        p = jnp.exp(s - m)
        l = jnp.sum(p, axis=-1, keepdims=True)
        p_ref[i] = p.astype(BF16)
        r_ref[i] = jnp.broadcast_to(1.0 / l, (blk, DA_HEAD_DIM))
        l_ref[0, 0, rows(b), h * LSE_LANES:(h + 1) * LSE_LANES] = jnp.broadcast_to(
            m + jnp.log(l), (blk, LSE_LANES))
    for i, (b, h) in enumerate(items):
        acc = _dot(p_ref[i], window(vp_ref, vc_ref, b, h))
        o_ref[0, 0, rows(b), cols(h)] = (acc * r_ref[i]).astype(o_ref.dtype)


def _dil_attn(pa):
    bsz, d, _, sub_len, _ = pa.shape
    rows = DA_QB * DA_BLOCK
    n_items = DA_QB * DA_HEADS

    def cur(sec):
        return pl.BlockSpec((1, 1, 1, rows, DA_WIDTH), lambda b, r, n: (b, r, sec, n, 0))

    def prev(sec):
        return pl.BlockSpec((1, 1, 1, DA_BLOCK, DA_WIDTH),
                            lambda b, r, n: (b, r, sec, jnp.maximum(n * DA_QB - 1, 0), 0))

    def out(width):
        return pl.BlockSpec((1, 1, rows, width), lambda b, r, n: (b, r, n, 0))

    return pl.pallas_call(
        _dil_attn_kernel,
        grid=(bsz, d, sub_len // rows),
        in_specs=[cur(0), prev(1), cur(1), prev(2), cur(2)],
        out_specs=(out(DA_WIDTH), out(LANES)),
        out_shape=(jax.ShapeDtypeStruct((bsz, d, sub_len, DA_WIDTH), BF16),
                   jax.ShapeDtypeStruct((bsz, d, sub_len, LANES), F32)),
        scratch_shapes=[pltpu.VMEM((n_items, DA_BLOCK, 2 * DA_BLOCK), F32),
                        pltpu.VMEM((n_items, DA_BLOCK, 2 * DA_BLOCK), BF16),
                        pltpu.VMEM((n_items, DA_BLOCK, DA_HEAD_DIM), F32)],
        compiler_params=pltpu.CompilerParams(
            dimension_semantics=("parallel", "parallel", "arbitrary"), vmem_limit_bytes=VMEM_LIMIT),
        name=f"dil_attn_d{d}",
    )(pa, pa, pa, pa, pa)


def _gla_kernel(q_ref, k_ref, v_ref, og_ref, lr_ref, w2_ref, b2_ref, nw_ref, o_ref, st_ref, mask_ref, keep_ref):
    t = pl.program_id(2)
    c = GLA_CHUNK
    tt = q_ref.shape[1]
    n_c = tt // c

    sub = mask_ref.shape[0]

    @pl.when(t == 0)
    def _():
        st_ref[...] = jnp.zeros_like(st_ref)
        row = lax.broadcasted_iota(jnp.int32, (sub, sub), 0)
        col = lax.broadcasted_iota(jnp.int32, (sub, sub), 1)
        keep = (col <= row) & (col >= row - row % c)
        keep_ref[...] = jnp.where(keep, 1.0, 0.0)
        mask_ref[...] = jnp.where(keep, 1.0, 0.0).astype(BF16)

    heads = range(q_ref.shape[2] // GLA_DK)
    mask = mask_ref[...]

    def kcols(h):
        return slice(h * GLA_DK, (h + 1) * GLA_DK)

    def vcols(h):
        return slice(h * GLA_DV, (h + 1) * GLA_DV)

    gpre = _dot(lr_ref[0], w2_ref[...]) + b2_ref[...]
    forget = (jnp.minimum(gpre, 0.0) - jnp.log(1.0 + jnp.exp(-jnp.abs(gpre)))) / GLA_GATE_NORMALIZER
    g_hi = forget.astype(BF16)
    g_lo = (forget - g_hi.astype(F32)).astype(BF16)
    b, b_last, q_e, k_e, k_end = [], [], [], [], []
    for h in heads:
        g_cat = jnp.concatenate([g_hi[:, kcols(h)], g_lo[:, kcols(h)]], axis=-1)
        csum = jnp.concatenate([_dot(mask, g_cat[s0:s0 + sub]) for s0 in range(0, tt, sub)], axis=0)
        b.append(csum[:, :GLA_DK] + csum[:, GLA_DK:])
    for h in heads:
        b_last.append(b[h].reshape(n_c, c, GLA_DK)[:, c - 1:c, :])
        b_to_end = (b_last[h] - b[h].reshape(n_c, c, GLA_DK)).reshape(tt, GLA_DK)
        q = q_ref[0, :, kcols(h)].astype(F32)
        k = k_ref[0, :, kcols(h)].astype(F32)
        q_e.append((q * ((GLA_DK ** -0.5) * jnp.exp(b[h]))).astype(BF16))
        k_e.append((k * jnp.exp(-b[h])).astype(BF16))
        k_end.append((k * jnp.exp(b_to_end)).astype(BF16))
    o_intra = []
    for h in heads:
        parts = []
        for s0 in range(0, tt, sub):
            ss = slice(s0, s0 + sub)
            att = lax.dot_general(q_e[h][ss], k_e[h][ss], _NT, preferred_element_type=F32)
            att = jnp.where(keep_ref[...] > 0.0, att, 0.0).astype(BF16)
            parts.append(_dot(att, v_ref[0, ss, vcols(h)]))
        o_intra.append(jnp.concatenate(parts, axis=0))
    decay = [jnp.exp(b_last[h].reshape(n_c, GLA_DK).T) for h in heads]
    st = [st_ref[h] for h in heads]
    outs = [[] for _ in heads]
    for ci in range(n_c):
        rs = slice(ci * c, (ci + 1) * c)
        for h in heads:
            outs[h].append(o_intra[h][rs] + _dot(q_e[h][rs], st[h].astype(BF16)))
            st[h] = decay[h][:, ci:ci + 1] * st[h] + lax.dot_general(
                k_end[h][rs], v_ref[0, rs, vcols(h)], _TN, preferred_element_type=F32)
    for h in heads:
        st_ref[h] = st[h]
        o = jnp.concatenate(outs[h], axis=0)
        o = o * _rms_scale(o) * nw_ref[...]
        gate = og_ref[0, :, vcols(h)].astype(F32)
        o_ref[0, :, vcols(h)] = (o * (gate * _sigmoid(gate))).astype(BF16)


def _gla(pg, w2_pad, b_gk2, gla_norm_w):
    bsz, seq, _ = pg.shape
    t = T_GLA
    hps = GLA_HEADS_PER_STEP
    wk, wv = hps * GLA_DK, hps * GLA_DV
    kq = GLA_KEY_DIM // wk
    kv = 2 * GLA_KEY_DIM // wv
    kg = kv + GLA_VALUE_DIM // wv
    klr = (2 * GLA_KEY_DIM + 2 * GLA_VALUE_DIM) // LANES
    return pl.pallas_call(
        _gla_kernel,
        grid=(bsz, GLA_HEADS // hps, seq // t),
        in_specs=[
            pl.BlockSpec((1, t, wk), lambda b, h, i: (b, i, h)),
            pl.BlockSpec((1, t, wk), lambda b, h, i: (b, i, kq + h)),
            pl.BlockSpec((1, t, wv), lambda b, h, i: (b, i, kv + h)),
            pl.BlockSpec((1, t, wv), lambda b, h, i: (b, i, kg + h)),
            pl.BlockSpec((1, t, LANES), lambda b, h, i: (b, i, klr)),
            pl.BlockSpec((LANES, wk), lambda b, h, i: (0, h)),
            pl.BlockSpec((1, wk), lambda b, h, i: (0, h)),
            pl.BlockSpec((1, GLA_DV), lambda b, h, i: (0, 0)),
        ],
        out_specs=pl.BlockSpec((1, t, wv), lambda b, h, i: (b, i, h)),
        out_shape=jax.ShapeDtypeStruct((bsz, seq, GLA_VALUE_DIM), BF16),
        scratch_shapes=[pltpu.VMEM((hps, GLA_DK, GLA_DV), F32), pltpu.VMEM((GLA_SUB, GLA_SUB), BF16),
                        pltpu.VMEM((GLA_SUB, GLA_SUB), F32)],
        compiler_params=pltpu.CompilerParams(
            dimension_semantics=("parallel", "parallel", "arbitrary"), vmem_limit_bytes=VMEM_LIMIT),
        name="gla",
    )(pg, pg, pg, pg, pg, w2_pad, b_gk2.reshape(1, GLA_KEY_DIM), gla_norm_w.reshape(1, GLA_DV))


def _mix_kernel(x_ref, o0_ref, l0_ref, o1_ref, l1_ref, o2_ref, l2_ref, ob_ref, pm_ref,
                wa_ref, wb_ref, wo_ref, ln2_ref, wrh_ref, wrl_ref, br_ref,
                h1_ref, hn_ref, ri_ref, ric_ref, meta_ref, cbefore_ref, cnt_ref,
                po1_ref, pl1_ref, po2_ref, pl2_ref, carry_ref):
    step = pl.program_id(0)
    tm = x_ref.shape[0]

    @pl.when(step == 0)
    def _():
        carry_ref[...] = jnp.zeros_like(carry_ref)

    for o_ref, l_ref, po_ref, pl_ref, (_, d) in ((o1_ref, l1_ref, po1_ref, pl1_ref, DA_GROUPS[1]),
                                                 (o2_ref, l2_ref, po2_ref, pl2_ref, DA_GROUPS[2])):
        n = tm // d
        for r in range(d):
            pl_ref[pl.ds(r, n, stride=d), :] = l_ref[0, r]
            for h in range(DA_HEADS):
                sl = slice(h * DA_HEAD_DIM, (h + 1) * DA_HEAD_DIM)
                po_ref[h, pl.ds(r, n, stride=d), :] = o_ref[0, r, :, sl].astype(F32)

    l0, l1, l2 = l0_ref[...], pl1_ref[...], pl2_ref[...]
    mx = jnp.maximum(jnp.maximum(l0, l1), l2)
    e0, e1, e2 = jnp.exp(l0 - mx), jnp.exp(l1 - mx), jnp.exp(l2 - mx)
    inv = 1.0 / (e0 + e1 + e2)
    w0, w1, w2 = e0 * inv, e1 * inv, e2 * inv
    heads = []
    for h in range(DA_HEADS):
        sl = slice(h * DA_HEAD_DIM, (h + 1) * DA_HEAD_DIM)
        at = slice(h * LSE_LANES, h * LSE_LANES + 1)
        o_h = w0[:, at] * o0_ref[:, sl].astype(F32) + w1[:, at] * po1_ref[h] + w2[:, at] * po2_ref[h]
        heads.append(o_h.astype(BF16))
    o_a = jnp.concatenate(heads, axis=-1)

    gates = pm_ref[...].astype(F32)
    mixed = (gates[:, :D_MODEL] * _dot(o_a, wa_ref[...])
             + gates[:, D_MODEL:] * _dot(ob_ref[...], wb_ref[...]))
    h1 = x_ref[...] + _dot(mixed.astype(BF16), wo_ref[...])
    h1_ref[...] = h1
    hn = h1 * _rms_scale(h1) * ln2_ref[...]

    hn_hi = hn.astype(BF16)
    hn_ref[...] = hn_hi
    hn_lo = (hn - hn_hi.astype(F32)).astype(BF16)
    logits = (_dot(hn_hi, wrh_ref[...]) + _dot(hn_lo, wrh_ref[...]) + _dot(hn_hi, wrl_ref[...])
              + br_ref[...])
    lane = lax.broadcasted_iota(jnp.int32, (tm, LANES), 1).astype(F32)
    work = jnp.where(lane < N_EXPERTS, logits, -jnp.inf)
    vals, idxs = [], []
    for _ in range(TOP_K):
        m = jnp.max(work, axis=-1, keepdims=True)
        idx = jnp.min(jnp.where(work == m, lane, float(LANES)), axis=-1, keepdims=True)
        vals.append(m)
        idxs.append(idx)
        work = jnp.where(lane == idx, -jnp.inf, work)
    exps = [jnp.exp(v - vals[0]) for v in vals]
    denom = exps[0] + exps[1] + exps[2] + exps[3]

    onehot = jnp.zeros((tm, LANES), F32)
    for idx in idxs:
        onehot = onehot + jnp.where(lane == idx, 1.0, 0.0)
    row = lax.broadcasted_iota(jnp.int32, (tm, tm), 0)
    col = lax.broadcasted_iota(jnp.int32, (tm, tm), 1)
    below = jnp.where(col < row, 1.0, 0.0).astype(BF16)
    before = _dot(below, onehot.astype(BF16)) + carry_ref[0:1, :]
    ranks = [jnp.sum(jnp.where(lane == idx, before, 0.0), axis=-1, keepdims=True) for idx in idxs]
    running = carry_ref[0:1, :]
    for j in range(tm // TM_ROW):
        cbefore_ref[j] = jnp.broadcast_to(running, cbefore_ref.shape[1:])
        running = running + jnp.sum(onehot[j * TM_ROW:(j + 1) * TM_ROW], axis=0, keepdims=True)
    carry = running
    carry_ref[...] = jnp.broadcast_to(carry, carry_ref.shape)
    cnt_ref[...] = jnp.broadcast_to(carry, cnt_ref.shape)

    ri = jnp.zeros((tm, LANES), F32)
    for j, val in enumerate(idxs + ranks):
        ri = jnp.where(lane == float(j), val, ri)
    ric_ref[...] = ri.astype(jnp.int32)
    ri_t = ri.T[:2 * TOP_K].astype(jnp.int32)
    for j in range(tm // TM_ROW):
        ri_ref[j] = ri_t[:, j * TM_ROW:(j + 1) * TM_ROW]
    meta = jnp.zeros((tm, LANES), F32)
    for k in range(TOP_K):
        gate = exps[k] / denom
        gate_hi = gate.astype(BF16).astype(F32)
        meta = jnp.where(lane == float(k), idxs[k], meta)
        meta = jnp.where(lane == float(TOP_K + k), gate_hi, meta)
        meta = jnp.where(lane == float(2 * TOP_K + k), gate - gate_hi, meta)
    meta_ref[...] = meta.astype(BF16)


def _mix(x2, o0, l0, o1, l1, o2, l2, o_b, pm, wa, wb, wo, ln2_w, wr_hi, wr_lo, br_pad):
    n_tok = x2.shape[0]
    tm = TM_MIX
    bsz = o1.shape[0]
    d1, d2 = DA_GROUPS[1][1], DA_GROUPS[2][1]
    tiles_per_seq = (n_tok // bsz) // tm

    def rows(width):
        return pl.BlockSpec((tm, width), lambda i: (i, 0))

    def residue_major(d, width):
        return pl.BlockSpec((1, d, tm // d, width),
                            lambda i: (i // tiles_per_seq, 0, i % tiles_per_seq, 0))

    def whole(arr):
        return pl.BlockSpec(arr.shape, lambda i: (0,) * arr.ndim)

    ln2 = ln2_w.reshape(1, D_MODEL)
    return pl.pallas_call(
        _mix_kernel,
        grid=(n_tok // tm,),
        in_specs=[rows(D_MODEL), rows(DA_WIDTH), rows(LANES),
                  residue_major(d1, DA_WIDTH), residue_major(d1, LANES),
                  residue_major(d2, DA_WIDTH), residue_major(d2, LANES),
                  rows(GLA_VALUE_DIM), rows(MERGE_W),
                  whole(wa), whole(wb), whole(wo), whole(ln2), whole(wr_hi), whole(wr_lo), whole(br_pad)],
        out_specs=(rows(D_MODEL), rows(D_MODEL),
                   pl.BlockSpec((tm // TM_ROW, 2 * TOP_K, TM_ROW), lambda i: (i, 0, 0)),
                   rows(LANES), rows(LANES),
                   pl.BlockSpec((tm // TM_ROW, 8, LANES), lambda i: (i, 0, 0)),
                   pl.BlockSpec((8, LANES), lambda i: (0, 0))),
        out_shape=(jax.ShapeDtypeStruct((n_tok, D_MODEL), F32),
                   jax.ShapeDtypeStruct((n_tok, D_MODEL), BF16),
                   jax.ShapeDtypeStruct((n_tok // TM_ROW, 2 * TOP_K, TM_ROW), jnp.int32),
                   jax.ShapeDtypeStruct((n_tok, LANES), jnp.int32),
                   jax.ShapeDtypeStruct((n_tok, LANES), BF16),
                   jax.ShapeDtypeStruct((n_tok // TM_ROW, 8, LANES), F32),
                   jax.ShapeDtypeStruct((8, LANES), F32)),
        scratch_shapes=[pltpu.VMEM((DA_HEADS, tm, DA_HEAD_DIM), F32), pltpu.VMEM((tm, LANES), F32),
                        pltpu.VMEM((DA_HEADS, tm, DA_HEAD_DIM), F32), pltpu.VMEM((tm, LANES), F32),
                        pltpu.VMEM((8, LANES), F32)],
        compiler_params=pltpu.CompilerParams(
            dimension_semantics=("arbitrary",), vmem_limit_bytes=VMEM_LIMIT),
        name="mix",
    )(x2, o0, l0, o1, l1, o2, l2, o_b, pm, wa, wb, wo, ln2, wr_hi, wr_lo, br_pad)


def _pack_pairs(x):
    n = x.shape[1] // 2
    rounded = x.astype(BF16).astype(F32)
    lo = lax.bitcast_convert_type(rounded[:, :n], U32) >> 16
    hi = lax.bitcast_convert_type(rounded[:, n:], U32) & jnp.uint32(0xFFFF0000)
    return hi | lo


def _unpack_pairs(u):
    lo = lax.bitcast_convert_type(u << 16, F32).astype(BF16)
    hi = lax.bitcast_convert_type(u & jnp.uint32(0xFFFF0000), F32).astype(BF16)
    return lo, hi


def _dispatch_kernel(zstart_ref, nwin_ref, wsrc_ref, wdst_ref, pos_ref, meta_ref, hn_ref, xin_ref,
                     buf_ref, zero_ref, sem, zsem):
    i = pl.program_id(0)
    tm = hn_ref.shape[0]
    n_buf_rows = buf_ref.shape[1]
    slot = i % 2

    def window_copy(s, buf_slot):
        src = pl.multiple_of(wsrc_ref[i * N_SLOT + s], ROW_ALIGN)
        dst = pl.multiple_of(wdst_ref[i * N_SLOT + s], ROW_ALIGN)
        return pltpu.make_async_copy(buf_ref.at[buf_slot, pl.ds(src, WIN), :],
                                     xin_ref.at[pl.ds(dst, WIN), :], sem)

    def wait_windows(step):
        def body(s, carry):
            pltpu.make_async_copy(buf_ref.at[0, pl.ds(0, WIN), :], xin_ref.at[pl.ds(0, WIN), :], sem).wait()
            return carry
        lax.fori_loop(0, nwin_ref[step], body, 0)

    @pl.when(i == 0)
    def _():
        zero_ref[...] = jnp.zeros_like(zero_ref)
        for j in range(zstart_ref.shape[0]):
            @pl.when(zstart_ref[j] >= 0)
            def _():
                start = pl.multiple_of(zstart_ref[j], ZCHUNK)
                cp = pltpu.make_async_copy(zero_ref, xin_ref.at[pl.ds(start, ZCHUNK), :], zsem)
                cp.start()
                cp.wait()

    row = lax.broadcasted_iota(jnp.int32, (n_buf_rows, tm), 0)
    perm = jnp.zeros((n_buf_rows, tm), F32)
    for k in range(TOP_K):
        perm = perm + jnp.where(row == pos_ref[0, k:k + 1, :], 1.0, 0.0)
    perm = perm.astype(BF16)
    buf_ref[slot, :, :D_MODEL // 2] = _pack_pairs(_dot(perm, hn_ref[...]))
    buf_ref[slot, :, D_MODEL // 2:] = lax.bitcast_convert_type(_dot(perm, meta_ref[...]), U32)

    @pl.when(i > 0)
    def _():
        wait_windows(i - 1)

    for buf_slot in range(2):
        @pl.when(slot == buf_slot)
        def _():
            def issue(s, carry):
                window_copy(s, buf_slot).start()
                return carry
            lax.fori_loop(0, nwin_ref[i], issue, 0)

    @pl.when(i == pl.num_programs(0) - 1)
    def _():
        wait_windows(i)


def _dispatch(zero_start, n_win, win_src, win_dst, pos, meta, hn, n_rows):
    n_tok = hn.shape[0]
    tm = TM_ROW
    n_tiles = n_tok // tm
    n_buf_rows = SORT_ROWS
    return pl.pallas_call(
        _dispatch_kernel,
        grid_spec=pltpu.PrefetchScalarGridSpec(
            num_scalar_prefetch=4,
            grid=(n_tiles,),
            in_specs=[
                pl.BlockSpec((1, TOP_K, tm), lambda i, *_: (i, 0, 0)),
                pl.BlockSpec((tm, LANES), lambda i, *_: (i, 0)),
                pl.BlockSpec((tm, D_MODEL), lambda i, *_: (i, 0)),
            ],
            out_specs=pl.BlockSpec(memory_space=pl.ANY),
            scratch_shapes=[pltpu.VMEM((2, n_buf_rows, ROW_W), U32),
                            pltpu.VMEM((ZCHUNK, ROW_W), U32),
                            pltpu.SemaphoreType.DMA(()), pltpu.SemaphoreType.DMA(())],
        ),
        out_shape=jax.ShapeDtypeStruct((n_rows, ROW_W), U32),
        compiler_params=pltpu.CompilerParams(
            dimension_semantics=("arbitrary",), vmem_limit_bytes=VMEM_LIMIT),
        name="dispatch",
    )(zero_start, n_win, win_src.reshape(-1), win_dst.reshape(-1), pos, meta, hn)


def _expert_kernel(be_ref, nused_ref, x_ref, wu_ref, bu_ref, wd_ref, bd_ref, y_ref, wu16_ref, wd16_ref):
    i = pl.program_id(0)
    used = i < nused_ref[0]

    @pl.when((i == 0) | (be_ref[i] != be_ref[jnp.maximum(i - 1, 0)]))
    def _():
        wu16_ref[...] = wu_ref[0].astype(BF16)
        wd16_ref[...] = wd_ref[0].astype(BF16)

    @pl.when(jnp.logical_not(used))
    def _():
        y_ref[...] = jnp.zeros_like(y_ref)

    @pl.when(used)
    def _():
        half = D_MODEL // 2
        x_lo, x_hi = _unpack_pairs(x_ref[:, :half])
        meta = lax.bitcast_convert_type(x_ref[:, half:half + 3 * TOP_K], F32)
        expert = be_ref[i].astype(F32)
        gate = jnp.zeros((x_ref.shape[0], 1), F32)
        for k in range(TOP_K):
            weight = meta[:, TOP_K + k:TOP_K + k + 1] + meta[:, 2 * TOP_K + k:2 * TOP_K + k + 1]
            gate = gate + jnp.where(meta[:, k:k + 1] == expert, weight, 0.0)
        hu = _dot(x_lo, wu16_ref[:half, :]) + _dot(x_hi, wu16_ref[half:, :]) + bu_ref[0]
        x_glu = jnp.minimum(hu[:, :D_FF], SWIGLU_LIMIT)
        x_lin = jnp.clip(hu[:, D_FF:], -SWIGLU_LIMIT, SWIGLU_LIMIT)
        act = x_glu * _sigmoid(SWIGLU_ALPHA * x_glu) * (x_lin + 1.0)
        y_ref[...] = _pack_pairs((_dot(act.astype(BF16), wd16_ref[...]) + bd_ref[0]) * gate)


def _experts(block_e, n_used, xin, w_up, b_up, w_down, b_down):
    n_rows = xin.shape[0]
    return pl.pallas_call(
        _expert_kernel,
        grid_spec=pltpu.PrefetchScalarGridSpec(
            num_scalar_prefetch=2,
            grid=(n_rows // TB,),
            in_specs=[
                pl.BlockSpec((TB, ROW_W), lambda i, be, nu: (i, 0)),
                pl.BlockSpec((1, D_MODEL, 2 * D_FF), lambda i, be, nu: (be[i], 0, 0)),
                pl.BlockSpec((1, 1, 2 * D_FF), lambda i, be, nu: (be[i], 0, 0)),
                pl.BlockSpec((1, D_FF, D_MODEL), lambda i, be, nu: (be[i], 0, 0)),
                pl.BlockSpec((1, 1, D_MODEL), lambda i, be, nu: (be[i], 0, 0)),
            ],
            out_specs=pl.BlockSpec((TB, D_MODEL // 2), lambda i, be, nu: (i, 0)),
            scratch_shapes=[pltpu.VMEM((D_MODEL, 2 * D_FF), BF16), pltpu.VMEM((D_FF, D_MODEL), BF16)],
        ),
        out_shape=jax.ShapeDtypeStruct((n_rows, D_MODEL // 2), U32),
        compiler_params=pltpu.CompilerParams(
            dimension_semantics=("arbitrary",), vmem_limit_bytes=VMEM_LIMIT),
        name="experts",
    )(block_e, n_used, xin, w_up, b_up.reshape(N_EXPERTS, 1, 2 * D_FF),
      w_down, b_down.reshape(N_EXPERTS, 1, D_MODEL))


def _combine_kernel(nwin_ref, wsrc_ref, wdst_ref, ric_ref, delta_ref, yb_ref, h1_ref, lnf_ref, o_ref,
                    stage_ref, buf_ref, sem):
    i = pl.program_id(0)
    tm = h1_ref.shape[0]
    n_buf_rows = buf_ref.shape[0]

    parity = i % 2

    def fetch(step, stage_slot):
        def body(s, carry):
            dst = pl.multiple_of(wdst_ref[step * N_SLOT + s], ROW_ALIGN)
            pltpu.make_async_copy(
                yb_ref.at[pl.ds(dst, WIN), :],
                stage_ref.at[stage_slot, pl.ds(pl.multiple_of(s * WIN, WIN), WIN), :],
                sem.at[stage_slot]).start()
            return carry
        lax.fori_loop(0, nwin_ref[step], body, 0)

    @pl.when(i == 0)
    def _():
        buf_ref[...] = jnp.zeros_like(buf_ref)
        fetch(i, 0)

    for stage_slot in range(2):
        @pl.when((i + 1 < pl.num_programs(0)) & (parity != stage_slot))
        def _():
            fetch(i + 1, stage_slot)

    ric = ric_ref[...].astype(F32)
    lane = lax.broadcasted_iota(jnp.int32, (tm, LANES), 1).astype(F32)
    col = lax.broadcasted_iota(jnp.int32, (tm, n_buf_rows), 1)
    pick = jnp.zeros((tm, n_buf_rows), F32)
    for k in range(TOP_K):
        offset = jnp.sum(jnp.where(lane == ric[:, k:k + 1], delta_ref[0], 0.0), axis=-1, keepdims=True)
        pos = (ric[:, TOP_K + k:TOP_K + k + 1] + offset).astype(jnp.int32)
        pick = pick + jnp.where(col == pos, 1.0, 0.0)
    pick = pick.astype(BF16)

    for stage_slot in range(2):
        @pl.when(parity == stage_slot)
        def _():
            def drain(s, carry):
                pltpu.make_async_copy(yb_ref.at[pl.ds(0, WIN), :], stage_ref.at[stage_slot, pl.ds(0, WIN), :],
                                      sem.at[stage_slot]).wait()
                return carry

            def compact(s, carry):
                src = pl.multiple_of(wsrc_ref[i * N_SLOT + s], ROW_ALIGN)
                buf_ref[pl.ds(src, WIN), :] = stage_ref[stage_slot, pl.ds(pl.multiple_of(s * WIN, WIN), WIN), :]
                return carry

            lax.fori_loop(0, nwin_ref[i], drain, 0)
            lax.fori_loop(0, nwin_ref[i], compact, 0)

    y_lo, y_hi = _unpack_pairs(buf_ref[...])
    h2 = h1_ref[...] + jnp.concatenate([_dot(pick, y_lo), _dot(pick, y_hi)], axis=-1)
    o_ref[...] = h2 * _rms_scale(h2) * lnf_ref[...]


def _combine(n_win, win_src, win_dst, ric, delta, yb, h1, lnf_w):
    n_tok = h1.shape[0]
    tm = TM_ROW
    n_tiles = n_tok // tm
    return pl.pallas_call(
        _combine_kernel,
        grid_spec=pltpu.PrefetchScalarGridSpec(
            num_scalar_prefetch=3,
            grid=(n_tiles,),
            in_specs=[
                pl.BlockSpec((tm, LANES), lambda i, *_: (i, 0)),
                pl.BlockSpec((1, 1, LANES), lambda i, *_: (i, 0, 0)),
                pl.BlockSpec(memory_space=pl.ANY),
                pl.BlockSpec((tm, D_MODEL), lambda i, *_: (i, 0)),
                pl.BlockSpec((1, D_MODEL), lambda i, *_: (0, 0)),
            ],
            out_specs=pl.BlockSpec((tm, D_MODEL), lambda i, *_: (i, 0)),
            scratch_shapes=[pltpu.VMEM((2, N_SLOT * WIN, D_MODEL // 2), U32),
                            pltpu.VMEM((SORT_ROWS, D_MODEL // 2), U32), pltpu.SemaphoreType.DMA((2,))],
        ),
        out_shape=jax.ShapeDtypeStruct((n_tok, D_MODEL), F32),
        compiler_params=pltpu.CompilerParams(
            dimension_semantics=("arbitrary",), vmem_limit_bytes=VMEM_LIMIT),
        name="combine",
    )(n_win, win_src.reshape(-1), win_dst.reshape(-1), ric, delta, yb, h1, lnf_w.reshape(1, D_MODEL))


def _layer(h, ln1_w, w_in, w_gk2, b_gk2, gla_norm_w, w_proj_a, w_proj_b, w_out,
           ln2_w, w_router, b_router, w_up, b_up, w_down, b_down, lnf_w):
    bsz, seq, _ = h.shape
    n_tok = bsz * seq

    n_main = 3 * QKV_W + 2 * GLA_KEY_DIM + 2 * GLA_VALUE_DIM
    pad = LANES - GLA_GATE_RANK
    w_all = jnp.concatenate(
        [w_in[:, :n_main + GLA_GATE_RANK], jnp.zeros((D_MODEL, pad), F32), w_in[:, n_main + GLA_GATE_RANK:]],
        axis=1).astype(BF16)
    w2_pad = jnp.concatenate([w_gk2, jnp.zeros((pad, GLA_KEY_DIM), F32)], axis=0).astype(BF16)
    wr_pad = jnp.concatenate([w_router, jnp.zeros((D_MODEL, LANES - N_EXPERTS), F32)], axis=1)
    wr_hi = wr_pad.astype(BF16)
    wr_lo = (wr_pad - wr_hi.astype(F32)).astype(BF16)
    br_pad = jnp.concatenate([b_router, jnp.zeros((LANES - N_EXPERTS,), F32)]).reshape(1, LANES)

    pa0, pa1, pa2, pg, pm = _inproj(h, ln1_w, w_all)
    o0, l0 = _dil_attn(pa0)
    o1, l1 = _dil_attn(pa1)
    o2, l2 = _dil_attn(pa2)
    o_b = _gla(pg, w2_pad, b_gk2, gla_norm_w)

    h1, hn, ri, ric, meta, carry_f, cnt = _mix(
        h.reshape(n_tok, D_MODEL), o0.reshape(n_tok, DA_WIDTH), l0.reshape(n_tok, LANES),
        o1, l1, o2, l2, o_b.reshape(n_tok, GLA_VALUE_DIM), pm.reshape(n_tok, MERGE_W),
        w_proj_a.astype(BF16), w_proj_b.astype(BF16), w_out.astype(BF16), ln2_w, wr_hi, wr_lo, br_pad)

    i32 = jnp.int32
    n_tiles = n_tok // TM_ROW
    experts = jnp.arange(N_EXPERTS)
    counts = cnt[0, :N_EXPERTS].astype(i32)
    before = carry_f[:, 0, :N_EXPERTS].astype(i32)
    run = jnp.concatenate([before[1:], counts[None]], axis=0) - before
    run_al = (run + ROW_ALIGN - 1) // ROW_ALIGN * ROW_ALIGN
    rows_end = jnp.cumsum(run_al, axis=0)
    rows_before = rows_end - run_al
    used = rows_end[-1]
    slack = WIN - ROW_ALIGN
    padded = (used + slack + TB - 1) // TB * TB
    pad_end = jnp.cumsum(padded)
    pad_start = pad_end - padded
    n_asg = n_tok * TOP_K
    n_rows = (n_asg + n_tiles * N_EXPERTS * (ROW_ALIGN - 1)
              + N_EXPERTS * (slack + TB - 1) + TB - 1) // TB * TB
    n_blocks = n_rows // TB
    block_starts = jnp.arange(n_blocks) * TB
    block_e = jnp.minimum(jnp.sum(pad_end[None, :] <= block_starts[:, None], axis=1),
                          N_EXPERTS - 1).astype(i32)
    n_used = (pad_end[-1:] // TB).astype(i32)

    run_end = jnp.cumsum(run_al, axis=1)
    run_start = run_end - run_al
    wins = (run + WIN - 1) // WIN
    wins_end = jnp.cumsum(wins, axis=1)
    wins_start = wins_end - wins
    n_win = wins_end[:, -1].astype(i32)
    slots = jnp.arange(N_SLOT)
    slot_e = jnp.sum(wins_end[:, None, :] <= slots[None, :, None], axis=-1)
    slot_is = slot_e[..., None] == experts

    def of_slot(table):
        return jnp.sum(jnp.where(slot_is, table[:, None, :], 0), axis=-1)

    win_off = (slots[None, :] - of_slot(wins_start)) * WIN
    win_src = (of_slot(run_start) + win_off).astype(i32)
    win_dst = (of_slot(pad_start[None, :] + rows_before) + win_off).astype(i32)
    asg_is = ri[:, :TOP_K, :, None] == experts
    pos = ri[:, TOP_K:, :] + jnp.sum(jnp.where(asg_is, (run_start - before)[:, None, None, :], 0), axis=-1)
    delta = jnp.pad((run_start - before).astype(F32), ((0, 0), (0, LANES - N_EXPERTS)))

    chunk_back = ZCHUNK * (1 + jnp.arange((slack + TB - 1 + ZCHUNK - 1) // ZCHUNK + 1))
    region_chunks = pad_end[:, None] - chunk_back[None, :]
    region_ok = (region_chunks >= pad_start[:, None]) & (region_chunks + ZCHUNK > (pad_start + used)[:, None])
    tail_chunks = pad_end[-1] + ZCHUNK * jnp.arange((n_rows - n_asg) // ZCHUNK)
    zero_start = jnp.concatenate([jnp.where(region_ok, region_chunks, -1).reshape(-1),
                                  jnp.where(tail_chunks < n_rows, tail_chunks, -1)]).astype(i32)

    xin = _dispatch(zero_start, n_win, win_src, win_dst, pos.astype(i32), meta, hn, n_rows)
    yb = _experts(block_e, n_used, xin, w_up, b_up, w_down, b_down)
    out = _combine(n_win, win_src, win_dst, ric, delta.reshape(-1, 1, LANES), yb, h1, lnf_w)
    return out.reshape(bsz, seq, D_MODEL)


def kernel(x, ln1_w, w_in, w_gk2, b_gk2, gla_norm_w, w_proj_a, w_proj_b, w_out, ln2_w, w_router,
           b_router, w_up, b_up, w_down, b_down, lnf_w):
    assert x.shape[-1] == D_MODEL and ln1_w.shape[0] == 1, "one layer of width D_MODEL"
    return _layer(x, ln1_w[0], w_in[0], w_gk2[0], b_gk2[0], gla_norm_w[0], w_proj_a[0], w_proj_b[0],
                  w_out[0], ln2_w[0], w_router[0], b_router[0], w_up[0], b_up[0], w_down[0],
                  b_down[0], lnf_w)
```

```python
import functools

import jax
import jax.numpy as jnp
from jax import lax
from jax.experimental import pallas as pl
from jax.experimental.pallas import tpu as pltpu

F32 = jnp.float32
BF16 = jnp.bfloat16
U32 = jnp.uint32

D_MODEL = 1024
DA_GROUPS = ((128, 1), (512, 4), (2048, 16))
DA_HEADS = 4
DA_HEAD_DIM = 128
DA_WIDTH = DA_HEADS * DA_HEAD_DIM
DA_BLOCK = 128
GLA_HEADS = 4
GLA_KEY_DIM = D_MODEL // 2
GLA_VALUE_DIM = D_MODEL
GLA_DK = GLA_KEY_DIM // GLA_HEADS
GLA_DV = GLA_VALUE_DIM // GLA_HEADS
GLA_GATE_RANK = 16
GLA_GATE_NORMALIZER = 16.0
GLA_CHUNK = 64
N_EXPERTS = 32
TOP_K = 4
D_FF = D_MODEL
SWIGLU_ALPHA = 1.702
SWIGLU_LIMIT = 7.0
RMS_EPS = 1e-5
NEG_INF = -1e30

LANES = 128
QKV_W = 3 * DA_WIDTH
GLA_W = 2 * GLA_KEY_DIM + 2 * GLA_VALUE_DIM + LANES
MERGE_W = 2 * D_MODEL

DA_QB = 4
LSE_LANES = LANES // DA_HEADS
TM_IN = 512
N_CHUNK = 512
T_GLA = 512
GLA_SUB = 256
GLA_HEADS_PER_STEP = 4
TM_MIX = 512
TB = 512
TM_ROW = 256
ROW_ALIGN = 8
WIN = 32
N_SLOT = N_EXPERTS + TM_ROW * TOP_K // WIN
SORT_ROWS = TM_ROW * TOP_K + N_EXPERTS * (ROW_ALIGN - 1) + WIN
ZCHUNK = 256
ROW_W = D_MODEL // 2 + LANES
VMEM_LIMIT = 56 * 1024 * 1024
VMEM_LIMIT_INPROJ = 62 * 1024 * 1024

_NT = (((1,), (1,)), ((), ()))
_TN = (((0,), (0,)), ((), ()))


def _dot(a, b):
    return jnp.dot(a, b, preferred_element_type=F32)


def _sigmoid(x):
    return 1.0 / (1.0 + jnp.exp(-x))


def _rms_scale(x):
    return lax.rsqrt(jnp.mean(x * x, axis=-1, keepdims=True) + RMS_EPS)


def _inproj_kernel(x_ref, ln_ref, w_hbm, pa0_ref, pa1_ref, pa2_ref, pg_ref, pm_ref,
                   w_ref, xs_ref, xn_ref, wsem):
    tm = x_ref.shape[1]
    n_slab = D_MODEL // LANES

    @pl.when((pl.program_id(0) == 0) & (pl.program_id(1) == 0))
    def _():
        cp = pltpu.make_async_copy(w_hbm, w_ref, wsem)
        cp.start()
        cp.wait()

    def project(out_write, col0, width, post=None):
        for c0 in range(0, width, N_CHUNK):
            cw = min(N_CHUNK, width - c0)
            val = _dot(xs_ref[...], w_ref[:, col0 + c0:col0 + c0 + cw])
            out_write(c0, cw, (val if post is None else post(val)).astype(BF16))

    x = x_ref[0]
    xn = x * _rms_scale(x) * ln_ref[...]
    xs_ref[...] = xn.astype(BF16)
    for j in range(n_slab):
        xn_ref[j] = xn[:, j * LANES:(j + 1) * LANES]

    def write_to(ref):
        def write(c0, cw, val):
            ref[0, :, c0:c0 + cw] = val
        return write

    def write_qkv(out_ref, d, n):
        def write(c0, cw, val):
            for r in range(d):
                out_ref[0, r, c0 // DA_WIDTH] = val[r * n:(r + 1) * n]
        return write

    project(write_qkv(pa0_ref, 1, tm), 0, QKV_W)
    project(write_to(pg_ref), 3 * QKV_W, GLA_W)
    project(write_to(pm_ref), 3 * QKV_W + GLA_W, MERGE_W, post=_sigmoid)

    for gi, out_ref in ((1, pa1_ref), (2, pa2_ref)):
        d = DA_GROUPS[gi][1]
        n = tm // d
        for r in range(d):
            for j in range(n_slab):
                xs_ref[r * n:(r + 1) * n, j * LANES:(j + 1) * LANES] = (
                    xn_ref[j, pl.ds(r, n, stride=d), :].astype(BF16))
        project(write_qkv(out_ref, d, n), gi * QKV_W, QKV_W)


def _inproj(x, ln1_w, w_all):
    assert N_CHUNK == DA_WIDTH
    bsz, seq, _ = x.shape
    tm = TM_IN
    d1, d2 = DA_GROUPS[1][1], DA_GROUPS[2][1]

    def qkv_shape(d):
        return jax.ShapeDtypeStruct((bsz, d, 3, seq // d, DA_WIDTH), BF16)

    def qkv_spec(d):
        return pl.BlockSpec((1, d, 3, tm // d, DA_WIDTH), lambda b, i: (b, 0, 0, i, 0))

    return pl.pallas_call(
        _inproj_kernel,
        grid=(bsz, seq // tm),
        in_specs=[
            pl.BlockSpec((1, tm, D_MODEL), lambda b, i: (b, i, 0)),
            pl.BlockSpec((1, D_MODEL), lambda b, i: (0, 0)),
            pl.BlockSpec(memory_space=pl.ANY),
        ],
        out_specs=(
            qkv_spec(1), qkv_spec(d1), qkv_spec(d2),
            pl.BlockSpec((1, tm, GLA_W), lambda b, i: (b, i, 0)),
            pl.BlockSpec((1, tm, MERGE_W), lambda b, i: (b, i, 0)),
        ),
        out_shape=(qkv_shape(1), qkv_shape(d1), qkv_shape(d2),
                   jax.ShapeDtypeStruct((bsz, seq, GLA_W), BF16),
                   jax.ShapeDtypeStruct((bsz, seq, MERGE_W), BF16)),
        scratch_shapes=[pltpu.VMEM(w_all.shape, BF16),
                        pltpu.VMEM((tm, D_MODEL), BF16),
                        pltpu.VMEM((D_MODEL // LANES, tm, LANES), F32),
                        pltpu.SemaphoreType.DMA(())],
        compiler_params=pltpu.CompilerParams(
            dimension_semantics=("arbitrary", "arbitrary"), vmem_limit_bytes=VMEM_LIMIT_INPROJ),
        name="inproj",
    )(x, ln1_w.reshape(1, D_MODEL), w_all)


def _dil_attn_kernel(q_ref, kp_ref, kc_ref, vp_ref, vc_ref, o_ref, l_ref, s_ref, p_ref, r_ref):
    n = pl.program_id(2)
    blk = DA_BLOCK
    qi = lax.broadcasted_iota(jnp.int32, (blk, 2 * blk), 0)
    kj = lax.broadcasted_iota(jnp.int32, (blk, 2 * blk), 1)
    band = (kj >= qi) & (kj <= qi + blk)
    band_first = (kj >= jnp.where(n > 0, qi, blk)) & (kj <= qi + blk)
    scale = DA_HEAD_DIM ** -0.5
    items = [(b, h) for b in range(DA_QB) for h in range(DA_HEADS)]

    def rows(b):
        return slice(b * blk, (b + 1) * blk)

    def cols(h):
        return slice(h * DA_HEAD_DIM, (h + 1) * DA_HEAD_DIM)

    def window(prev_ref, cur_ref, b, h):
        before = prev_ref[0, 0, 0, :, cols(h)] if b == 0 else cur_ref[0, 0, 0, rows(b - 1), cols(h)]
        return jnp.concatenate([before, cur_ref[0, 0, 0, rows(b), cols(h)]], axis=0)

    for i, (b, h) in enumerate(items):
        s = lax.dot_general(q_ref[0, 0, 0, rows(b), cols(h)], window(kp_ref, kc_ref, b, h), _NT,
                            preferred_element_type=F32) * scale
        s_ref[i] = jnp.where(band_first if b == 0 else band, s, NEG_INF)
    for i, (b, h) in enumerate(items):
        s = s_ref[i]
        m = jnp.max(s, axis=-1, keepdims=True)
        p = jnp.exp(s - m)
        l = jnp.sum(p, axis=-1, keepdims=True)
        p_ref[i] = p.astype(BF16)
        r_ref[i] = jnp.broadcast_to(1.0 / l, (blk, DA_HEAD_DIM))
        l_ref[0, 0, rows(b), h * LSE_LANES:(h + 1) * LSE_LANES] = jnp.broadcast_to(
            m + jnp.log(l), (blk, LSE_LANES))
    for i, (b, h) in enumerate(items):
        acc = _dot(p_ref[i], window(vp_ref, vc_ref, b, h))
        o_ref[0, 0, rows(b), cols(h)] = (acc * r_ref[i]).astype(o_ref.dtype)


def _dil_attn(pa):
    bsz, d, _, sub_len, _ = pa.shape
    rows = DA_QB * DA_BLOCK
    n_items = DA_QB * DA_HEADS

    def cur(sec):
        return pl.BlockSpec((1, 1, 1, rows, DA_WIDTH), lambda b, r, n: (b, r, sec, n, 0))

    def prev(sec):
        return pl.BlockSpec((1, 1, 1, DA_BLOCK, DA_WIDTH),
                            lambda b, r, n: (b, r, sec, jnp.maximum(n * DA_QB - 1, 0), 0))

    def out(width):
        return pl.BlockSpec((1, 1, rows, width), lambda b, r, n: (b, r, n, 0))

    return pl.pallas_call(
        _dil_attn_kernel,
        grid=(bsz, d, sub_len // rows),
        in_specs=[cur(0), prev(1), cur(1), prev(2), cur(2)],
        out_specs=(out(DA_WIDTH), out(LANES)),
        out_shape=(jax.ShapeDtypeStruct((bsz, d, sub_len, DA_WIDTH), BF16),
                   jax.ShapeDtypeStruct((bsz, d, sub_len, LANES), F32)),
        scratch_shapes=[pltpu.VMEM((n_items, DA_BLOCK, 2 * DA_BLOCK), F32),
                        pltpu.VMEM((n_items, DA_BLOCK, 2 * DA_BLOCK), BF16),
                        pltpu.VMEM((n_items, DA_BLOCK, DA_HEAD_DIM), F32)],
        compiler_params=pltpu.CompilerParams(
            dimension_semantics=("parallel", "parallel", "arbitrary"), vmem_limit_bytes=VMEM_LIMIT),
        name=f"dil_attn_d{d}",
    )(pa, pa, pa, pa, pa)


def _gla_kernel(q_ref, k_ref, v_ref, og_ref, lr_ref, w2_ref, b2_ref, nw_ref, o_ref, st_ref, mask_ref, keep_ref):
    t = pl.program_id(2)
    c = GLA_CHUNK
    tt = q_ref.shape[1]
    n_c = tt // c

    sub = mask_ref.shape[0]

    @pl.when(t == 0)
    def _():
        st_ref[...] = jnp.zeros_like(st_ref)
        row = lax.broadcasted_iota(jnp.int32, (sub, sub), 0)
        col = lax.broadcasted_iota(jnp.int32, (sub, sub), 1)
        keep = (col <= row) & (col >= row - row % c)
        keep_ref[...] = jnp.where(keep, 1.0, 0.0)
        mask_ref[...] = jnp.where(keep, 1.0, 0.0).astype(BF16)

    heads = range(q_ref.shape[2] // GLA_DK)
    mask = mask_ref[...]

    def kcols(h):
        return slice(h * GLA_DK, (h + 1) * GLA_DK)

    def vcols(h):
        return slice(h * GLA_DV, (h + 1) * GLA_DV)

    gpre = _dot(lr_ref[0], w2_ref[...]) + b2_ref[...]
    forget = (jnp.minimum(gpre, 0.0) - jnp.log(1.0 + jnp.exp(-jnp.abs(gpre)))) / GLA_GATE_NORMALIZER
    g_hi = forget.astype(BF16)
    g_lo = (forget - g_hi.astype(F32)).astype(BF16)
    b, b_last, q_e, k_e, k_end = [], [], [], [], []
    for h in heads:
        g_cat = jnp.concatenate([g_hi[:, kcols(h)], g_lo[:, kcols(h)]], axis=-1)
        csum = jnp.concatenate([_dot(mask, g_cat[s0:s0 + sub]) for s0 in range(0, tt, sub)], axis=0)
        b.append(csum[:, :GLA_DK] + csum[:, GLA_DK:])
    for h in heads:
        b_last.append(b[h].reshape(n_c, c, GLA_DK)[:, c - 1:c, :])
        b_to_end = (b_last[h] - b[h].reshape(n_c, c, GLA_DK)).reshape(tt, GLA_DK)
        q = q_ref[0, :, kcols(h)].astype(F32)
        k = k_ref[0, :, kcols(h)].astype(F32)
        q_e.append((q * ((GLA_DK ** -0.5) * jnp.exp(b[h]))).astype(BF16))
        k_e.append((k * jnp.exp(-b[h])).astype(BF16))
        k_end.append((k * jnp.exp(b_to_end)).astype(BF16))
    o_intra = []
    for h in heads:
        parts = []
        for s0 in range(0, tt, sub):
            ss = slice(s0, s0 + sub)
            att = lax.dot_general(q_e[h][ss], k_e[h][ss], _NT, preferred_element_type=F32)
            att = jnp.where(keep_ref[...] > 0.0, att, 0.0).astype(BF16)
            parts.append(_dot(att, v_ref[0, ss, vcols(h)]))
        o_intra.append(jnp.concatenate(parts, axis=0))
    decay = [jnp.exp(b_last[h].reshape(n_c, GLA_DK).T) for h in heads]
    st = [st_ref[h] for h in heads]
    outs = [[] for _ in heads]
    for ci in range(n_c):
        rs = slice(ci * c, (ci + 1) * c)
        for h in heads:
            outs[h].append(o_intra[h][rs] + _dot(q_e[h][rs], st[h].astype(BF16)))
            st[h] = decay[h][:, ci:ci + 1] * st[h] + lax.dot_general(
                k_end[h][rs], v_ref[0, rs, vcols(h)], _TN, preferred_element_type=F32)
    for h in heads:
        st_ref[h] = st[h]
        o = jnp.concatenate(outs[h], axis=0)
        o = o * _rms_scale(o) * nw_ref[...]
        gate = og_ref[0, :, vcols(h)].astype(F32)
        o_ref[0, :, vcols(h)] = (o * (gate * _sigmoid(gate))).astype(BF16)


def _gla(pg, w2_pad, b_gk2, gla_norm_w):
    bsz, seq, _ = pg.shape
    t = T_GLA
    hps = GLA_HEADS_PER_STEP
    wk, wv = hps * GLA_DK, hps * GLA_DV
    kq = GLA_KEY_DIM // wk
    kv = 2 * GLA_KEY_DIM // wv
    kg = kv + GLA_VALUE_DIM // wv
    klr = (2 * GLA_KEY_DIM + 2 * GLA_VALUE_DIM) // LANES
    return pl.pallas_call(
        _gla_kernel,
        grid=(bsz, GLA_HEADS // hps, seq // t),
        in_specs=[
            pl.BlockSpec((1, t, wk), lambda b, h, i: (b, i, h)),
            pl.BlockSpec((1, t, wk), lambda b, h, i: (b, i, kq + h)),
            pl.BlockSpec((1, t, wv), lambda b, h, i: (b, i, kv + h)),
            pl.BlockSpec((1, t, wv), lambda b, h, i: (b, i, kg + h)),
            pl.BlockSpec((1, t, LANES), lambda b, h, i: (b, i, klr)),
            pl.BlockSpec((LANES, wk), lambda b, h, i: (0, h)),
            pl.BlockSpec((1, wk), lambda b, h, i: (0, h)),
            pl.BlockSpec((1, GLA_DV), lambda b, h, i: (0, 0)),
        ],
        out_specs=pl.BlockSpec((1, t, wv), lambda b, h, i: (b, i, h)),
        out_shape=jax.ShapeDtypeStruct((bsz, seq, GLA_VALUE_DIM), BF16),
        scratch_shapes=[pltpu.VMEM((hps, GLA_DK, GLA_DV), F32), pltpu.VMEM((GLA_SUB, GLA_SUB), BF16),
                        pltpu.VMEM((GLA_SUB, GLA_SUB), F32)],
        compiler_params=pltpu.CompilerParams(
            dimension_semantics=("parallel", "parallel", "arbitrary"), vmem_limit_bytes=VMEM_LIMIT),
        name="gla",
    )(pg, pg, pg, pg, pg, w2_pad, b_gk2.reshape(1, GLA_KEY_DIM), gla_norm_w.reshape(1, GLA_DV))


def _mix_kernel(x_ref, o0_ref, l0_ref, o1_ref, l1_ref, o2_ref, l2_ref, ob_ref, pm_ref,
                wa_ref, wb_ref, wo_ref, ln2_ref, wrh_ref, wrl_ref, br_ref,
                h1_ref, hn_ref, ri_ref, ric_ref, meta_ref, cbefore_ref, cnt_ref,
                po1_ref, pl1_ref, po2_ref, pl2_ref, carry_ref, logit_ref):
    step = pl.program_id(0)
    tm = x_ref.shape[0]

    @pl.when(step == 0)
    def _():
        carry_ref[...] = jnp.zeros_like(carry_ref)
        logit_ref[...] = jnp.zeros_like(logit_ref)

    for o_ref, l_ref, po_ref, pl_ref, (_, d) in ((o1_ref, l1_ref, po1_ref, pl1_ref, DA_GROUPS[1]),
                                                 (o2_ref, l2_ref, po2_ref, pl2_ref, DA_GROUPS[2])):
        n = tm // d
        for r in range(d):
            pl_ref[pl.ds(r, n, stride=d), :] = l_ref[0, r]
            for h in range(DA_HEADS):
                sl = slice(h * DA_HEAD_DIM, (h + 1) * DA_HEAD_DIM)
                po_ref[h, pl.ds(r, n, stride=d), :] = o_ref[0, r, :, sl].astype(F32)

    routed = step > 0
    lane = lax.broadcasted_iota(jnp.int32, (tm, LANES), 1).astype(F32)
    route = {"work": None, "vals": [], "idxs": []}

    def topk_round():
        work = route["work"]
        m = jnp.max(work, axis=-1, keepdims=True)
        idx = jnp.min(jnp.where(work == m, lane, float(LANES)), axis=-1, keepdims=True)
        route["vals"].append(m)
        route["idxs"].append(idx)
        route["work"] = jnp.where(lane == idx, -jnp.inf, work)

    route["work"] = jnp.where(lane < N_EXPERTS, logit_ref[(step + 1) % 2], -jnp.inf)
    topk_round()

    l0, l1, l2 = l0_ref[...], pl1_ref[...], pl2_ref[...]
    mx = jnp.maximum(jnp.maximum(l0, l1), l2)
    e0, e1, e2 = jnp.exp(l0 - mx), jnp.exp(l1 - mx), jnp.exp(l2 - mx)
    inv = 1.0 / (e0 + e1 + e2)
    w0, w1, w2 = e0 * inv, e1 * inv, e2 * inv
    topk_round()
    heads = []
    for h in range(DA_HEADS):
        sl = slice(h * DA_HEAD_DIM, (h + 1) * DA_HEAD_DIM)
        at = slice(h * LSE_LANES, h * LSE_LANES + 1)
        o_h = w0[:, at] * o0_ref[:, sl].astype(F32) + w1[:, at] * po1_ref[h] + w2[:, at] * po2_ref[h]
        heads.append(o_h.astype(BF16))
    o_a = jnp.concatenate(heads, axis=-1)
    topk_round()

    gates = pm_ref[...].astype(F32)
    mixed = (gates[:, :D_MODEL] * _dot(o_a, wa_ref[...])
             + gates[:, D_MODEL:] * _dot(ob_ref[...], wb_ref[...]))
    topk_round()
    vals, idxs = route["vals"], route["idxs"]
    exps = [jnp.exp(v - vals[0]) for v in vals]
    denom = exps[0] + exps[1] + exps[2] + exps[3]
    onehot = jnp.zeros((tm, LANES), F32)
    for idx in idxs:
        onehot = onehot + jnp.where(lane == idx, 1.0, 0.0)

    h1 = x_ref[...] + _dot(mixed.astype(BF16), wo_ref[...])
    h1_ref[...] = h1
    hn = h1 * _rms_scale(h1) * ln2_ref[...]

    row = lax.broadcasted_iota(jnp.int32, (tm, tm), 0)
    col = lax.broadcasted_iota(jnp.int32, (tm, tm), 1)
    below = jnp.where(col < row, 1.0, 0.0).astype(BF16)
    before = _dot(below, onehot.astype(BF16)) + carry_ref[0:1, :]
    ranks = [jnp.sum(jnp.where(lane == idx, before, 0.0), axis=-1, keepdims=True) for idx in idxs]
    running = carry_ref[0:1, :]
    for j in range(tm // TM_ROW):
        cbefore_ref[j] = jnp.broadcast_to(running, cbefore_ref.shape[1:])
        running = running + jnp.sum(onehot[j * TM_ROW:(j + 1) * TM_ROW], axis=0, keepdims=True)
    carry = jnp.where(routed, running, carry_ref[0:1, :])
    carry_ref[...] = jnp.broadcast_to(carry, carry_ref.shape)
    cnt_ref[...] = jnp.broadcast_to(carry, cnt_ref.shape)

    hn_hi = hn.astype(BF16)
    hn_ref[...] = hn_hi

    ri = jnp.zeros((tm, LANES), F32)
    for j, val in enumerate(idxs + ranks):
        ri = jnp.where(lane == float(j), val, ri)
    ric_ref[...] = ri.astype(jnp.int32)
    ri_t = ri.T[:2 * TOP_K].astype(jnp.int32)
    for j in range(tm // TM_ROW):
        ri_ref[j] = ri_t[:, j * TM_ROW:(j + 1) * TM_ROW]
    meta = jnp.zeros((tm, LANES), F32)
    for k in range(TOP_K):
        gate = exps[k] / denom
        gate_hi = gate.astype(BF16).astype(F32)
        meta = jnp.where(lane == float(k), idxs[k], meta)
        meta = jnp.where(lane == float(TOP_K + k), gate_hi, meta)
        meta = jnp.where(lane == float(2 * TOP_K + k), gate - gate_hi, meta)
    meta_ref[...] = meta.astype(BF16)

    hn_lo = (hn - hn_hi.astype(F32)).astype(BF16)
    logit_ref[step % 2] = (_dot(hn_hi, wrh_ref[...]) + _dot(hn_lo, wrh_ref[...])
                           + _dot(hn_hi, wrl_ref[...]) + br_ref[...])


def _mix(x2, o0, l0, o1, l1, o2, l2, o_b, pm, wa, wb, wo, ln2_w, wr_hi, wr_lo, br_pad):
    n_tok = x2.shape[0]
    tm = TM_MIX
    bsz = o1.shape[0]
    d1, d2 = DA_GROUPS[1][1], DA_GROUPS[2][1]
    tiles_per_seq = (n_tok // bsz) // tm
    n_tiles = n_tok // tm

    def tile(i):
        return jnp.minimum(i, n_tiles - 1)

    def routed(i):
        return jnp.maximum(i - 1, 0)

    def rows(width, which=tile):
        return pl.BlockSpec((tm, width), lambda i: (which(i), 0))

    def residue_major(d, width):
        return pl.BlockSpec((1, d, tm // d, width),
                            lambda i: (tile(i) // tiles_per_seq, 0, tile(i) % tiles_per_seq, 0))

    def whole(arr):
        return pl.BlockSpec(arr.shape, lambda i: (0,) * arr.ndim)

    ln2 = ln2_w.reshape(1, D_MODEL)
    return pl.pallas_call(
        _mix_kernel,
        grid=(n_tiles + 1,),
        in_specs=[rows(D_MODEL), rows(DA_WIDTH), rows(LANES),
                  residue_major(d1, DA_WIDTH), residue_major(d1, LANES),
                  residue_major(d2, DA_WIDTH), residue_major(d2, LANES),
                  rows(GLA_VALUE_DIM), rows(MERGE_W),
                  whole(wa), whole(wb), whole(wo), whole(ln2), whole(wr_hi), whole(wr_lo), whole(br_pad)],
        out_specs=(rows(D_MODEL), rows(D_MODEL),
                   pl.BlockSpec((tm // TM_ROW, 2 * TOP_K, TM_ROW), lambda i: (routed(i), 0, 0)),
                   rows(LANES, routed), rows(LANES, routed),
                   pl.BlockSpec((tm // TM_ROW, 8, LANES), lambda i: (routed(i), 0, 0)),
                   pl.BlockSpec((8, LANES), lambda i: (0, 0))),
        out_shape=(jax.ShapeDtypeStruct((n_tok, D_MODEL), F32),
                   jax.ShapeDtypeStruct((n_tok, D_MODEL), BF16),
                   jax.ShapeDtypeStruct((n_tok // TM_ROW, 2 * TOP_K, TM_ROW), jnp.int32),
                   jax.ShapeDtypeStruct((n_tok, LANES), jnp.int32),
                   jax.ShapeDtypeStruct((n_tok, LANES), BF16),
                   jax.ShapeDtypeStruct((n_tok // TM_ROW, 8, LANES), F32),
                   jax.ShapeDtypeStruct((8, LANES), F32)),
        scratch_shapes=[pltpu.VMEM((DA_HEADS, tm, DA_HEAD_DIM), F32), pltpu.VMEM((tm, LANES), F32),
                        pltpu.VMEM((DA_HEADS, tm, DA_HEAD_DIM), F32), pltpu.VMEM((tm, LANES), F32),
                        pltpu.VMEM((8, LANES), F32), pltpu.VMEM((2, tm, LANES), F32)],
        compiler_params=pltpu.CompilerParams(
            dimension_semantics=("arbitrary",), vmem_limit_bytes=VMEM_LIMIT),
        name="mix",
    )(x2, o0, l0, o1, l1, o2, l2, o_b, pm, wa, wb, wo, ln2, wr_hi, wr_lo, br_pad)


def _pack_pairs(x):
    n = x.shape[1] // 2
    rounded = x.astype(BF16).astype(F32)
    lo = lax.bitcast_convert_type(rounded[:, :n], U32) >> 16
    hi = lax.bitcast_convert_type(rounded[:, n:], U32) & jnp.uint32(0xFFFF0000)
    return hi | lo


def _unpack_pairs(u):
    lo = lax.bitcast_convert_type(u << 16, F32).astype(BF16)
    hi = lax.bitcast_convert_type(u & jnp.uint32(0xFFFF0000), F32).astype(BF16)
    return lo, hi


def _dispatch_kernel(zstart_ref, nwin_ref, wsrc_ref, wdst_ref, pos_ref, meta_ref, hn_ref, xin_ref,
                     buf_ref, zero_ref, sem, zsem):
    i = pl.program_id(0)
    tm = hn_ref.shape[0]
    n_buf_rows = buf_ref.shape[1]
    slot = i % 2

    def window_copy(s, buf_slot):
        src = pl.multiple_of(wsrc_ref[i * N_SLOT + s], ROW_ALIGN)
        dst = pl.multiple_of(wdst_ref[i * N_SLOT + s], ROW_ALIGN)
        return pltpu.make_async_copy(buf_ref.at[buf_slot, pl.ds(src, WIN), :],
                                     xin_ref.at[pl.ds(dst, WIN), :], sem)

    def wait_windows(step):
        def body(s, carry):
            pltpu.make_async_copy(buf_ref.at[0, pl.ds(0, WIN), :], xin_ref.at[pl.ds(0, WIN), :], sem).wait()
            return carry
        lax.fori_loop(0, nwin_ref[step], body, 0)

    @pl.when(i == 0)
    def _():
        zero_ref[...] = jnp.zeros_like(zero_ref)
        for j in range(zstart_ref.shape[0]):
            @pl.when(zstart_ref[j] >= 0)
            def _():
                start = pl.multiple_of(zstart_ref[j], ZCHUNK)
                cp = pltpu.make_async_copy(zero_ref, xin_ref.at[pl.ds(start, ZCHUNK), :], zsem)
                cp.start()
                cp.wait()

    row = lax.broadcasted_iota(jnp.int32, (n_buf_rows, tm), 0)
    perm = jnp.zeros((n_buf_rows, tm), F32)
    for k in range(TOP_K):
        perm = perm + jnp.where(row == pos_ref[0, k:k + 1, :], 1.0, 0.0)
    perm = perm.astype(BF16)
    buf_ref[slot, :, :D_MODEL // 2] = _pack_pairs(_dot(perm, hn_ref[...]))
    buf_ref[slot, :, D_MODEL // 2:] = lax.bitcast_convert_type(_dot(perm, meta_ref[...]), U32)

    @pl.when(i > 0)
    def _():
        wait_windows(i - 1)

    for buf_slot in range(2):
        @pl.when(slot == buf_slot)
        def _():
            def issue(s, carry):
                window_copy(s, buf_slot).start()
                return carry
            lax.fori_loop(0, nwin_ref[i], issue, 0)

    @pl.when(i == pl.num_programs(0) - 1)
    def _():
        wait_windows(i)


def _dispatch(zero_start, n_win, win_src, win_dst, pos, meta, hn, n_rows):
    n_tok = hn.shape[0]
    tm = TM_ROW
    n_tiles = n_tok // tm
    n_buf_rows = SORT_ROWS
    return pl.pallas_call(
        _dispatch_kernel,
        grid_spec=pltpu.PrefetchScalarGridSpec(
            num_scalar_prefetch=4,
            grid=(n_tiles,),
            in_specs=[
                pl.BlockSpec((1, TOP_K, tm), lambda i, *_: (i, 0, 0)),
                pl.BlockSpec((tm, LANES), lambda i, *_: (i, 0)),
                pl.BlockSpec((tm, D_MODEL), lambda i, *_: (i, 0)),
            ],
            out_specs=pl.BlockSpec(memory_space=pl.ANY),
            scratch_shapes=[pltpu.VMEM((2, n_buf_rows, ROW_W), U32),
                            pltpu.VMEM((ZCHUNK, ROW_W), U32),
                            pltpu.SemaphoreType.DMA(()), pltpu.SemaphoreType.DMA(())],
        ),
        out_shape=jax.ShapeDtypeStruct((n_rows, ROW_W), U32),
        compiler_params=pltpu.CompilerParams(
            dimension_semantics=("arbitrary",), vmem_limit_bytes=VMEM_LIMIT),
        name="dispatch",
    )(zero_start, n_win, win_src.reshape(-1), win_dst.reshape(-1), pos, meta, hn)


def _expert_kernel(be_ref, nused_ref, x_ref, wu_ref, bu_ref, wd_ref, bd_ref, y_ref, wu16_ref, wd16_ref):
    i = pl.program_id(0)
    used = i < nused_ref[0]

    @pl.when((i == 0) | (be_ref[i] != be_ref[jnp.maximum(i - 1, 0)]))
    def _():
        wu16_ref[...] = wu_ref[0].astype(BF16)
        wd16_ref[...] = wd_ref[0].astype(BF16)

    @pl.when(jnp.logical_not(used))
    def _():
        y_ref[...] = jnp.zeros_like(y_ref)

    @pl.when(used)
    def _():
        half = D_MODEL // 2
        x_lo, x_hi = _unpack_pairs(x_ref[:, :half])
        meta = lax.bitcast_convert_type(x_ref[:, half:half + 3 * TOP_K], F32)
        expert = be_ref[i].astype(F32)
        gate = jnp.zeros((x_ref.shape[0], 1), F32)
        for k in range(TOP_K):
            weight = meta[:, TOP_K + k:TOP_K + k + 1] + meta[:, 2 * TOP_K + k:2 * TOP_K + k + 1]
            gate = gate + jnp.where(meta[:, k:k + 1] == expert, weight, 0.0)
        hu = _dot(x_lo, wu16_ref[:half, :]) + _dot(x_hi, wu16_ref[half:, :]) + bu_ref[0]
        x_glu = jnp.minimum(hu[:, :D_FF], SWIGLU_LIMIT)
        x_lin = jnp.clip(hu[:, D_FF:], -SWIGLU_LIMIT, SWIGLU_LIMIT)
        act = x_glu * _sigmoid(SWIGLU_ALPHA * x_glu) * (x_lin + 1.0)
        y_ref[...] = _pack_pairs((_dot(act.astype(BF16), wd16_ref[...]) + bd_ref[0]) * gate)


def _experts(block_e, n_used, xin, w_up, b_up, w_down, b_down):
    n_rows = xin.shape[0]
    return pl.pallas_call(
        _expert_kernel,
        grid_spec=pltpu.PrefetchScalarGridSpec(
            num_scalar_prefetch=2,
            grid=(n_rows // TB,),
            in_specs=[
                pl.BlockSpec((TB, ROW_W), lambda i, be, nu: (i, 0)),
                pl.BlockSpec((1, D_MODEL, 2 * D_FF), lambda i, be, nu: (be[i], 0, 0)),
                pl.BlockSpec((1, 1, 2 * D_FF), lambda i, be, nu: (be[i], 0, 0)),
                pl.BlockSpec((1, D_FF, D_MODEL), lambda i, be, nu: (be[i], 0, 0)),
                pl.BlockSpec((1, 1, D_MODEL), lambda i, be, nu: (be[i], 0, 0)),
            ],
            out_specs=pl.BlockSpec((TB, D_MODEL // 2), lambda i, be, nu: (i, 0)),
            scratch_shapes=[pltpu.VMEM((D_MODEL, 2 * D_FF), BF16), pltpu.VMEM((D_FF, D_MODEL), BF16)],
        ),
        out_shape=jax.ShapeDtypeStruct((n_rows, D_MODEL // 2), U32),
        compiler_params=pltpu.CompilerParams(
            dimension_semantics=("arbitrary",), vmem_limit_bytes=VMEM_LIMIT),
        name="experts",
    )(block_e, n_used, xin, w_up, b_up.reshape(N_EXPERTS, 1, 2 * D_FF),
      w_down, b_down.reshape(N_EXPERTS, 1, D_MODEL))


def _combine_kernel(nwin_ref, wsrc_ref, wdst_ref, ric_ref, delta_ref, yb_ref, h1_ref, lnf_ref, o_ref,
                    stage_ref, buf_ref, sem):
    i = pl.program_id(0)
    tm = h1_ref.shape[0]
    n_buf_rows = buf_ref.shape[0]

    parity = i % 2

    def fetch(step, stage_slot):
        def body(s, carry):
            dst = pl.multiple_of(wdst_ref[step * N_SLOT + s], ROW_ALIGN)
            pltpu.make_async_copy(
                yb_ref.at[pl.ds(dst, WIN), :],
                stage_ref.at[stage_slot, pl.ds(pl.multiple_of(s * WIN, WIN), WIN), :],
                sem.at[stage_slot]).start()
            return carry
        lax.fori_loop(0, nwin_ref[step], body, 0)

    @pl.when(i == 0)
    def _():
        buf_ref[...] = jnp.zeros_like(buf_ref)
        fetch(i, 0)

    for stage_slot in range(2):
        @pl.when((i + 1 < pl.num_programs(0)) & (parity != stage_slot))
        def _():
            fetch(i + 1, stage_slot)

    ric = ric_ref[...].astype(F32)
    lane = lax.broadcasted_iota(jnp.int32, (tm, LANES), 1).astype(F32)
    col = lax.broadcasted_iota(jnp.int32, (tm, n_buf_rows), 1)
    pick = jnp.zeros((tm, n_buf_rows), F32)
    for k in range(TOP_K):
        offset = jnp.sum(jnp.where(lane == ric[:, k:k + 1], delta_ref[0], 0.0), axis=-1, keepdims=True)
        pos = (ric[:, TOP_K + k:TOP_K + k + 1] + offset).astype(jnp.int32)
        pick = pick + jnp.where(col == pos, 1.0, 0.0)
    pick = pick.astype(BF16)

    for stage_slot in range(2):
        @pl.when(parity == stage_slot)
        def _():
            def drain(s, carry):
                pltpu.make_async_copy(yb_ref.at[pl.ds(0, WIN), :], stage_ref.at[stage_slot, pl.ds(0, WIN), :],
                                      sem.at[stage_slot]).wait()
                return carry

            def compact(s, carry):
                src = pl.multiple_of(wsrc_ref[i * N_SLOT + s], ROW_ALIGN)
                buf_ref[pl.ds(src, WIN), :] = stage_ref[stage_slot, pl.ds(pl.multiple_of(s * WIN, WIN), WIN), :]
                return carry

            lax.fori_loop(0, nwin_ref[i], drain, 0)
            lax.fori_loop(0, nwin_ref[i], compact, 0)

    y_lo, y_hi = _unpack_pairs(buf_ref[...])
    h2 = h1_ref[...] + jnp.concatenate([_dot(pick, y_lo), _dot(pick, y_hi)], axis=-1)
    o_ref[...] = h2 * _rms_scale(h2) * lnf_ref[...]


def _combine(n_win, win_src, win_dst, ric, delta, yb, h1, lnf_w):
    n_tok = h1.shape[0]
    tm = TM_ROW
    n_tiles = n_tok // tm
    return pl.pallas_call(
        _combine_kernel,
        grid_spec=pltpu.PrefetchScalarGridSpec(
            num_scalar_prefetch=3,
            grid=(n_tiles,),
            in_specs=[
                pl.BlockSpec((tm, LANES), lambda i, *_: (i, 0)),
                pl.BlockSpec((1, 1, LANES), lambda i, *_: (i, 0, 0)),
                pl.BlockSpec(memory_space=pl.ANY),
                pl.BlockSpec((tm, D_MODEL), lambda i, *_: (i, 0)),
                pl.BlockSpec((1, D_MODEL), lambda i, *_: (0, 0)),
            ],
            out_specs=pl.BlockSpec((tm, D_MODEL), lambda i, *_: (i, 0)),
            scratch_shapes=[pltpu.VMEM((2, N_SLOT * WIN, D_MODEL // 2), U32),
                            pltpu.VMEM((SORT_ROWS, D_MODEL // 2), U32), pltpu.SemaphoreType.DMA((2,))],
        ),
        out_shape=jax.ShapeDtypeStruct((n_tok, D_MODEL), F32),
        compiler_params=pltpu.CompilerParams(
            dimension_semantics=("arbitrary",), vmem_limit_bytes=VMEM_LIMIT),
        name="combine",
    )(n_win, win_src.reshape(-1), win_dst.reshape(-1), ric, delta, yb, h1, lnf_w.reshape(1, D_MODEL))


def _layer(h, ln1_w, w_in, w_gk2, b_gk2, gla_norm_w, w_proj_a, w_proj_b, w_out,
           ln2_w, w_router, b_router, w_up, b_up, w_down, b_down, lnf_w):
    bsz, seq, _ = h.shape
    n_tok = bsz * seq

    n_main = 3 * QKV_W + 2 * GLA_KEY_DIM + 2 * GLA_VALUE_DIM
    pad = LANES - GLA_GATE_RANK
    w_all = jnp.concatenate(
        [w_in[:, :n_main + GLA_GATE_RANK], jnp.zeros((D_MODEL, pad), F32), w_in[:, n_main + GLA_GATE_RANK:]],
        axis=1).astype(BF16)
    w2_pad = jnp.concatenate([w_gk2, jnp.zeros((pad, GLA_KEY_DIM), F32)], axis=0).astype(BF16)
    wr_pad = jnp.concatenate([w_router, jnp.zeros((D_MODEL, LANES - N_EXPERTS), F32)], axis=1)
    wr_hi = wr_pad.astype(BF16)
    wr_lo = (wr_pad - wr_hi.astype(F32)).astype(BF16)
    br_pad = jnp.concatenate([b_router, jnp.zeros((LANES - N_EXPERTS,), F32)]).reshape(1, LANES)

    pa0, pa1, pa2, pg, pm = _inproj(h, ln1_w, w_all)
    o0, l0 = _dil_attn(pa0)
    o1, l1 = _dil_attn(pa1)
    o2, l2 = _dil_attn(pa2)
    o_b = _gla(pg, w2_pad, b_gk2, gla_norm_w)

    h1, hn, ri, ric, meta, carry_f, cnt = _mix(
        h.reshape(n_tok, D_MODEL), o0.reshape(n_tok, DA_WIDTH), l0.reshape(n_tok, LANES),
        o1, l1, o2, l2, o_b.reshape(n_tok, GLA_VALUE_DIM), pm.reshape(n_tok, MERGE_W),
        w_proj_a.astype(BF16), w_proj_b.astype(BF16), w_out.astype(BF16), ln2_w, wr_hi, wr_lo, br_pad)

    i32 = jnp.int32
    n_tiles = n_tok // TM_ROW
    experts = jnp.arange(N_EXPERTS)
    counts = cnt[0, :N_EXPERTS].astype(i32)
    before = carry_f[:, 0, :N_EXPERTS].astype(i32)
    run = jnp.concatenate([before[1:], counts[None]], axis=0) - before
    run_al = (run + ROW_ALIGN - 1) // ROW_ALIGN * ROW_ALIGN
    rows_end = jnp.cumsum(run_al, axis=0)
    rows_before = rows_end - run_al
    used = rows_end[-1]
    slack = WIN - ROW_ALIGN
    padded = (used + slack + TB - 1) // TB * TB
    pad_end = jnp.cumsum(padded)
    pad_start = pad_end - padded
    n_asg = n_tok * TOP_K
    n_rows = (n_asg + n_tiles * N_EXPERTS * (ROW_ALIGN - 1)
              + N_EXPERTS * (slack + TB - 1) + TB - 1) // TB * TB
    n_blocks = n_rows // TB
    block_starts = jnp.arange(n_blocks) * TB
    block_e = jnp.minimum(jnp.sum(pad_end[None, :] <= block_starts[:, None], axis=1),
                          N_EXPERTS - 1).astype(i32)
    n_used = (pad_end[-1:] // TB).astype(i32)

    run_end = jnp.cumsum(run_al, axis=1)
    run_start = run_end - run_al
    wins = (run + WIN - 1) // WIN
    wins_end = jnp.cumsum(wins, axis=1)
    wins_start = wins_end - wins
    n_win = wins_end[:, -1].astype(i32)
    slots = jnp.arange(N_SLOT)
    slot_e = jnp.sum(wins_end[:, None, :] <= slots[None, :, None], axis=-1)
    slot_is = slot_e[..., None] == experts

    def of_slot(table):
        return jnp.sum(jnp.where(slot_is, table[:, None, :], 0), axis=-1)

    win_off = (slots[None, :] - of_slot(wins_start)) * WIN
    win_src = (of_slot(run_start) + win_off).astype(i32)
    win_dst = (of_slot(pad_start[None, :] + rows_before) + win_off).astype(i32)
    asg_is = ri[:, :TOP_K, :, None] == experts
    pos = ri[:, TOP_K:, :] + jnp.sum(jnp.where(asg_is, (run_start - before)[:, None, None, :], 0), axis=-1)
    delta = jnp.pad((run_start - before).astype(F32), ((0, 0), (0, LANES - N_EXPERTS)))

    chunk_back = ZCHUNK * (1 + jnp.arange((slack + TB - 1 + ZCHUNK - 1) // ZCHUNK + 1))
    region_chunks = pad_end[:, None] - chunk_back[None, :]
    region_ok = (region_chunks >= pad_start[:, None]) & (region_chunks + ZCHUNK > (pad_start + used)[:, None])
    tail_chunks = pad_end[-1] + ZCHUNK * jnp.arange((n_rows - n_asg) // ZCHUNK)
    zero_start = jnp.concatenate([jnp.where(region_ok, region_chunks, -1).reshape(-1),
                                  jnp.where(tail_chunks < n_rows, tail_chunks, -1)]).astype(i32)

    xin = _dispatch(zero_start, n_win, win_src, win_dst, pos.astype(i32), meta, hn, n_rows)
    yb = _experts(block_e, n_used, xin, w_up, b_up, w_down, b_down)
    out = _combine(n_win, win_src, win_dst, ric, delta.reshape(-1, 1, LANES), yb, h1, lnf_w)
    return out.reshape(bsz, seq, D_MODEL)


def kernel(x, ln1_w, w_in, w_gk2, b_gk2, gla_norm_w, w_proj_a, w_proj_b, w_out, ln2_w, w_router,
           b_router, w_up, b_up, w_down, b_down, lnf_w):
    assert x.shape[-1] == D_MODEL and ln1_w.shape[0] == 1, "one layer of width D_MODEL"
    return _layer(x, ln1_w[0], w_in[0], w_gk2[0], b_gk2[0], gla_norm_w[0], w_proj_a[0], w_proj_b[0],
                  w_out[0], ln2_w[0], w_router[0], b_router[0], w_up[0], b_up[0], w_down[0],
                  b_down[0], lnf_w)
```

```python
import functools

import jax
import jax.numpy as jnp
from jax import lax
from jax.experimental import pallas as pl
from jax.experimental.pallas import tpu as pltpu

F32 = jnp.float32
BF16 = jnp.bfloat16
U32 = jnp.uint32

D_MODEL = 1024
DA_GROUPS = ((128, 1), (512, 4), (2048, 16))
DA_HEADS = 4
DA_HEAD_DIM = 128
DA_WIDTH = DA_HEADS * DA_HEAD_DIM
DA_BLOCK = 128
GLA_HEADS = 4
GLA_KEY_DIM = D_MODEL // 2
GLA_VALUE_DIM = D_MODEL
GLA_DK = GLA_KEY_DIM // GLA_HEADS
GLA_DV = GLA_VALUE_DIM // GLA_HEADS
GLA_GATE_RANK = 16
GLA_GATE_NORMALIZER = 16.0
GLA_CHUNK = 64
N_EXPERTS = 32
TOP_K = 4
D_FF = D_MODEL
SWIGLU_ALPHA = 1.702
SWIGLU_LIMIT = 7.0
RMS_EPS = 1e-5
NEG_INF = -1e30

LANES = 128
QKV_W = 3 * DA_WIDTH
GLA_W = 2 * GLA_KEY_DIM + 2 * GLA_VALUE_DIM + LANES
MERGE_W = 2 * D_MODEL

DA_QB = 4
LSE_LANES = LANES // DA_HEADS
TM_IN = 512
N_CHUNK = 512
T_GLA = 512
GLA_SUB = 256
GLA_HEADS_PER_STEP = 4
TM_MIX = 512
TB = 512
TM_ROW = 256
ROW_ALIGN = 8
WIN = 32
N_SLOT = N_EXPERTS + TM_ROW * TOP_K // WIN
SORT_ROWS = TM_ROW * TOP_K + N_EXPERTS * (ROW_ALIGN - 1) + WIN
ZCHUNK = 256
ROW_W = D_MODEL // 2 + LANES
VMEM_LIMIT = 56 * 1024 * 1024
VMEM_LIMIT_INPROJ = 62 * 1024 * 1024

_NT = (((1,), (1,)), ((), ()))
_TN = (((0,), (0,)), ((), ()))


def _dot(a, b):
    return jnp.dot(a, b, preferred_element_type=F32)


def _sigmoid(x):
    return 1.0 / (1.0 + jnp.exp(-x))


def _rms_scale(x):
    return lax.rsqrt(jnp.mean(x * x, axis=-1, keepdims=True) + RMS_EPS)


def _inproj_kernel(x_ref, ln_ref, w_hbm, pa0_ref, pa1_ref, pa2_ref, pg_ref, pm_ref,
                   w_ref, xs_ref, xn_ref, wsem):
    tm = x_ref.shape[1]
    n_slab = D_MODEL // LANES

    @pl.when((pl.program_id(0) == 0) & (pl.program_id(1) == 0))
    def _():
        cp = pltpu.make_async_copy(w_hbm, w_ref, wsem)
        cp.start()
        cp.wait()

    def project(out_write, col0, width, post=None):
        for c0 in range(0, width, N_CHUNK):
            cw = min(N_CHUNK, width - c0)
            val = _dot(xs_ref[...], w_ref[:, col0 + c0:col0 + c0 + cw])
            out_write(c0, cw, (val if post is None else post(val)).astype(BF16))

    x = x_ref[0]
    xn = x * _rms_scale(x) * ln_ref[...]
    xs_ref[...] = xn.astype(BF16)
    for j in range(n_slab):
        xn_ref[j] = xn[:, j * LANES:(j + 1) * LANES]

    def write_to(ref):
        def write(c0, cw, val):
            ref[0, :, c0:c0 + cw] = val
        return write

    def write_qkv(out_ref, d, n):
        def write(c0, cw, val):
            for r in range(d):
                out_ref[0, r, c0 // DA_WIDTH] = val[r * n:(r + 1) * n]
        return write

    project(write_qkv(pa0_ref, 1, tm), 0, QKV_W)
    project(write_to(pg_ref), 3 * QKV_W, GLA_W)
    project(write_to(pm_ref), 3 * QKV_W + GLA_W, MERGE_W, post=_sigmoid)

    for gi, out_ref in ((1, pa1_ref), (2, pa2_ref)):
        d = DA_GROUPS[gi][1]
        n = tm // d
        for r in range(d):
            for j in range(n_slab):
                xs_ref[r * n:(r + 1) * n, j * LANES:(j + 1) * LANES] = (
                    xn_ref[j, pl.ds(r, n, stride=d), :].astype(BF16))
        project(write_qkv(out_ref, d, n), gi * QKV_W, QKV_W)


def _inproj(x, ln1_w, w_all):
    assert N_CHUNK == DA_WIDTH
    bsz, seq, _ = x.shape
    tm = TM_IN
    d1, d2 = DA_GROUPS[1][1], DA_GROUPS[2][1]

    def qkv_shape(d):
        return jax.ShapeDtypeStruct((bsz, d, 3, seq // d, DA_WIDTH), BF16)

    def qkv_spec(d):
        return pl.BlockSpec((1, d, 3, tm // d, DA_WIDTH), lambda b, i: (b, 0, 0, i, 0))

    return pl.pallas_call(
        _inproj_kernel,
        grid=(bsz, seq // tm),
        in_specs=[
            pl.BlockSpec((1, tm, D_MODEL), lambda b, i: (b, i, 0)),
            pl.BlockSpec((1, D_MODEL), lambda b, i: (0, 0)),
            pl.BlockSpec(memory_space=pl.ANY),
        ],
        out_specs=(
            qkv_spec(1), qkv_spec(d1), qkv_spec(d2),
            pl.BlockSpec((1, tm, GLA_W), lambda b, i: (b, i, 0)),
            pl.BlockSpec((1, tm, MERGE_W), lambda b, i: (b, i, 0)),
        ),
        out_shape=(qkv_shape(1), qkv_shape(d1), qkv_shape(d2),
                   jax.ShapeDtypeStruct((bsz, seq, GLA_W), BF16),
                   jax.ShapeDtypeStruct((bsz, seq, MERGE_W), BF16)),
        scratch_shapes=[pltpu.VMEM(w_all.shape, BF16),
                        pltpu.VMEM((tm, D_MODEL), BF16),
                        pltpu.VMEM((D_MODEL // LANES, tm, LANES), F32),
                        pltpu.SemaphoreType.DMA(())],
        compiler_params=pltpu.CompilerParams(
            dimension_semantics=("arbitrary", "arbitrary"), vmem_limit_bytes=VMEM_LIMIT_INPROJ),
        name="inproj",
    )(x, ln1_w.reshape(1, D_MODEL), w_all)


def _dil_attn_kernel(q_ref, kp_ref, kc_ref, vp_ref, vc_ref, o_ref, l_ref, s_ref, p_ref, r_ref):
    n = pl.program_id(2)
    blk = DA_BLOCK
    qi = lax.broadcasted_iota(jnp.int32, (blk, 2 * blk), 0)
    kj = lax.broadcasted_iota(jnp.int32, (blk, 2 * blk), 1)
    band = (kj >= qi) & (kj <= qi + blk)
    band_first = (kj >= jnp.where(n > 0, qi, blk)) & (kj <= qi + blk)
    scale = DA_HEAD_DIM ** -0.5
    items = [(b, h) for b in range(DA_QB) for h in range(DA_HEADS)]

    def rows(b):
        return slice(b * blk, (b + 1) * blk)

    def cols(h):
        return slice(h * DA_HEAD_DIM, (h + 1) * DA_HEAD_DIM)

    def window(prev_ref, cur_ref, b, h):
        before = prev_ref[0, 0, 0, :, cols(h)] if b == 0 else cur_ref[0, 0, 0, rows(b - 1), cols(h)]
        return jnp.concatenate([before, cur_ref[0, 0, 0, rows(b), cols(h)]], axis=0)

    for i, (b, h) in enumerate(items):
        s = lax.dot_general(q_ref[0, 0, 0, rows(b), cols(h)], window(kp_ref, kc_ref, b, h), _NT,
                            preferred_element_type=F32) * scale
        s_ref[i] = jnp.where(band_first if b == 0 else band, s, NEG_INF)
    for i, (b, h) in enumerate(items):
        s = s_ref[i]
        m = jnp.max(s, axis=-1, keepdims=True)
        p = jnp.exp(s - m)
        l = jnp.sum(p, axis=-1, keepdims=True)
        p_ref[i] = p.astype(BF16)
        r_ref[i] = jnp.broadcast_to(1.0 / l, (blk, DA_HEAD_DIM))
        l_ref[0, 0, rows(b), h * LSE_LANES:(h + 1) * LSE_LANES] = jnp.broadcast_to(
            m + jnp.log(l), (blk, LSE_LANES))
    for i, (b, h) in enumerate(items):
        acc = _dot(p_ref[i], window(vp_ref, vc_ref, b, h))
        o_ref[0, 0, rows(b), cols(h)] = (acc * r_ref[i]).astype(o_ref.dtype)


def _dil_attn(pa):
    bsz, d, _, sub_len, _ = pa.shape
    rows = DA_QB * DA_BLOCK
    n_items = DA_QB * DA_HEADS

    def cur(sec):
        return pl.BlockSpec((1, 1, 1, rows, DA_WIDTH), lambda b, r, n: (b, r, sec, n, 0))

    def prev(sec):
        return pl.BlockSpec((1, 1, 1, DA_BLOCK, DA_WIDTH),
                            lambda b, r, n: (b, r, sec, jnp.maximum(n * DA_QB - 1, 0), 0))

    def out(width):
        return pl.BlockSpec((1, 1, rows, width), lambda b, r, n: (b, r, n, 0))

    return pl.pallas_call(
        _dil_attn_kernel,
        grid=(bsz, d, sub_len // rows),
        in_specs=[cur(0), prev(1), cur(1), prev(2), cur(2)],
        out_specs=(out(DA_WIDTH), out(LANES)),
        out_shape=(jax.ShapeDtypeStruct((bsz, d, sub_len, DA_WIDTH), BF16),
                   jax.ShapeDtypeStruct((bsz, d, sub_len, LANES), F32)),
        scratch_shapes=[pltpu.VMEM((n_items, DA_BLOCK, 2 * DA_BLOCK), F32),
                        pltpu.VMEM((n_items, DA_BLOCK, 2 * DA_BLOCK), BF16),
                        pltpu.VMEM((n_items, DA_BLOCK, DA_HEAD_DIM), F32)],
        compiler_params=pltpu.CompilerParams(
            dimension_semantics=("parallel", "parallel", "arbitrary"), vmem_limit_bytes=VMEM_LIMIT),
        name=f"dil_attn_d{d}",
    )(pa, pa, pa, pa, pa)


def _gla_kernel(q_ref, k_ref, v_ref, og_ref, lr_ref, w2_ref, b2_ref, nw_ref, o_ref, st_ref, mask_ref, keep_ref):
    t = pl.program_id(2)
    c = GLA_CHUNK
    tt = q_ref.shape[1]
    n_c = tt // c

    sub = mask_ref.shape[0]

    @pl.when(t == 0)
    def _():
        st_ref[...] = jnp.zeros_like(st_ref)
        row = lax.broadcasted_iota(jnp.int32, (sub, sub), 0)
        col = lax.broadcasted_iota(jnp.int32, (sub, sub), 1)
        keep = (col <= row) & (col >= row - row % c)
        keep_ref[...] = jnp.where(keep, 1.0, 0.0)
        mask_ref[...] = jnp.where(keep, 1.0, 0.0).astype(BF16)

    heads = range(q_ref.shape[2] // GLA_DK)
    mask = mask_ref[...]

    def kcols(h):
        return slice(h * GLA_DK, (h + 1) * GLA_DK)

    def vcols(h):
        return slice(h * GLA_DV, (h + 1) * GLA_DV)

    gpre = _dot(lr_ref[0], w2_ref[...]) + b2_ref[...]
    forget = (jnp.minimum(gpre, 0.0) - jnp.log(1.0 + jnp.exp(-jnp.abs(gpre)))) / GLA_GATE_NORMALIZER
    g_hi = forget.astype(BF16)
    g_lo = (forget - g_hi.astype(F32)).astype(BF16)
    b, b_last, q_e, k_e, k_end = [], [], [], [], []
    for h in heads:
        g_cat = jnp.concatenate([g_hi[:, kcols(h)], g_lo[:, kcols(h)]], axis=-1)
        csum = jnp.concatenate([_dot(mask, g_cat[s0:s0 + sub]) for s0 in range(0, tt, sub)], axis=0)
        b.append(csum[:, :GLA_DK] + csum[:, GLA_DK:])
    for h in heads:
        b_last.append(b[h].reshape(n_c, c, GLA_DK)[:, c - 1:c, :])
        b_to_end = (b_last[h] - b[h].reshape(n_c, c, GLA_DK)).reshape(tt, GLA_DK)
        q = q_ref[0, :, kcols(h)].astype(F32)
        k = k_ref[0, :, kcols(h)].astype(F32)
        q_e.append((q * ((GLA_DK ** -0.5) * jnp.exp(b[h]))).astype(BF16))
        k_e.append((k * jnp.exp(-b[h])).astype(BF16))
        k_end.append((k * jnp.exp(b_to_end)).astype(BF16))
    o_intra = []
    for h in heads:
        parts = []
        for s0 in range(0, tt, sub):
            ss = slice(s0, s0 + sub)
            att = lax.dot_general(q_e[h][ss], k_e[h][ss], _NT, preferred_element_type=F32)
            att = jnp.where(keep_ref[...] > 0.0, att, 0.0).astype(BF16)
            parts.append(_dot(att, v_ref[0, ss, vcols(h)]))
        o_intra.append(jnp.concatenate(parts, axis=0))
    decay = [jnp.exp(b_last[h].reshape(n_c, GLA_DK).T) for h in heads]
    st = [st_ref[h] for h in heads]
    outs = [[] for _ in heads]
    for ci in range(n_c):
        rs = slice(ci * c, (ci + 1) * c)
        for h in heads:
            outs[h].append(o_intra[h][rs] + _dot(q_e[h][rs], st[h].astype(BF16)))
            st[h] = decay[h][:, ci:ci + 1] * st[h] + lax.dot_general(
                k_end[h][rs], v_ref[0, rs, vcols(h)], _TN, preferred_element_type=F32)
    for h in heads:
        st_ref[h] = st[h]
        o = jnp.concatenate(outs[h], axis=0)
        o = o * _rms_scale(o) * nw_ref[...]
        gate = og_ref[0, :, vcols(h)].astype(F32)
        o_ref[0, :, vcols(h)] = (o * (gate * _sigmoid(gate))).astype(BF16)


def _gla(pg, w2_pad, b_gk2, gla_norm_w):
    bsz, seq, _ = pg.shape
    t = T_GLA
    hps = GLA_HEADS_PER_STEP
    wk, wv = hps * GLA_DK, hps * GLA_DV
    kq = GLA_KEY_DIM // wk
    kv = 2 * GLA_KEY_DIM // wv
    kg = kv + GLA_VALUE_DIM // wv
    klr = (2 * GLA_KEY_DIM + 2 * GLA_VALUE_DIM) // LANES
    return pl.pallas_call(
        _gla_kernel,
        grid=(bsz, GLA_HEADS // hps, seq // t),
        in_specs=[
            pl.BlockSpec((1, t, wk), lambda b, h, i: (b, i, h)),
            pl.BlockSpec((1, t, wk), lambda b, h, i: (b, i, kq + h)),
            pl.BlockSpec((1, t, wv), lambda b, h, i: (b, i, kv + h)),
            pl.BlockSpec((1, t, wv), lambda b, h, i: (b, i, kg + h)),
            pl.BlockSpec((1, t, LANES), lambda b, h, i: (b, i, klr)),
            pl.BlockSpec((LANES, wk), lambda b, h, i: (0, h)),
            pl.BlockSpec((1, wk), lambda b, h, i: (0, h)),
            pl.BlockSpec((1, GLA_DV), lambda b, h, i: (0, 0)),
        ],
        out_specs=pl.BlockSpec((1, t, wv), lambda b, h, i: (b, i, h)),
        out_shape=jax.ShapeDtypeStruct((bsz, seq, GLA_VALUE_DIM), BF16),
        scratch_shapes=[pltpu.VMEM((hps, GLA_DK, GLA_DV), F32), pltpu.VMEM((GLA_SUB, GLA_SUB), BF16),
                        pltpu.VMEM((GLA_SUB, GLA_SUB), F32)],
        compiler_params=pltpu.CompilerParams(
            dimension_semantics=("parallel", "parallel", "arbitrary"), vmem_limit_bytes=VMEM_LIMIT),
        name="gla",
    )(pg, pg, pg, pg, pg, w2_pad, b_gk2.reshape(1, GLA_KEY_DIM), gla_norm_w.reshape(1, GLA_DV))


def _mix_kernel(x_ref, o0_ref, l0_ref, o1_ref, l1_ref, o2_ref, l2_ref, ob_ref, pm_ref,
                wa_ref, wb_ref, wo_ref, ln2_ref, wrh_ref, wrl_ref, br_ref,
                h1_ref, hn_ref, ri_ref, ric_ref, meta_ref, cbefore_ref, cnt_ref,
                po1_ref, pl1_ref, po2_ref, pl2_ref, carry_ref, logit_ref):
    step = pl.program_id(0)
    tm = x_ref.shape[0]

    @pl.when(step == 0)
    def _():
        carry_ref[...] = jnp.zeros_like(carry_ref)
        logit_ref[...] = jnp.zeros_like(logit_ref)

    for o_ref, l_ref, po_ref, pl_ref, (_, d) in ((o1_ref, l1_ref, po1_ref, pl1_ref, DA_GROUPS[1]),
                                                 (o2_ref, l2_ref, po2_ref, pl2_ref, DA_GROUPS[2])):
        n = tm // d
        for r in range(d):
            pl_ref[pl.ds(r, n, stride=d), :] = l_ref[0, r]
            for h in range(DA_HEADS):
                sl = slice(h * DA_HEAD_DIM, (h + 1) * DA_HEAD_DIM)
                po_ref[h, pl.ds(r, n, stride=d), :] = o_ref[0, r, :, sl].astype(F32)

    routed = step > 0
    lane = lax.broadcasted_iota(jnp.int32, (tm, LANES), 1).astype(F32)
    route = {"work": None, "vals": [], "idxs": []}

    def topk_round():
        work = route["work"]
        m = jnp.max(work, axis=-1, keepdims=True)
        idx = jnp.min(jnp.where(work == m, lane, float(LANES)), axis=-1, keepdims=True)
        route["vals"].append(m)
        route["idxs"].append(idx)
        route["work"] = jnp.where(lane == idx, -jnp.inf, work)

    route["work"] = jnp.where(lane < N_EXPERTS, logit_ref[(step + 1) % 2], -jnp.inf)
    topk_round()

    l0, l1, l2 = l0_ref[...], pl1_ref[...], pl2_ref[...]
    mx = jnp.maximum(jnp.maximum(l0, l1), l2)
    e0, e1, e2 = jnp.exp(l0 - mx), jnp.exp(l1 - mx), jnp.exp(l2 - mx)
    inv = 1.0 / (e0 + e1 + e2)
    w0, w1, w2 = e0 * inv, e1 * inv, e2 * inv
    topk_round()
    heads = []
    for h in range(DA_HEADS):
        sl = slice(h * DA_HEAD_DIM, (h + 1) * DA_HEAD_DIM)
        at = slice(h * LSE_LANES, h * LSE_LANES + 1)
        o_h = w0[:, at] * o0_ref[:, sl].astype(F32) + w1[:, at] * po1_ref[h] + w2[:, at] * po2_ref[h]
        heads.append(o_h.astype(BF16))
    o_a = jnp.concatenate(heads, axis=-1)
    topk_round()

    gates = pm_ref[...].astype(F32)
    mixed = (gates[:, :D_MODEL] * _dot(o_a, wa_ref[...])
             + gates[:, D_MODEL:] * _dot(ob_ref[...], wb_ref[...]))
    topk_round()
    vals, idxs = route["vals"], route["idxs"]
    exps = [jnp.exp(v - vals[0]) for v in vals]
    denom = exps[0] + exps[1] + exps[2] + exps[3]
    onehot = jnp.zeros((tm, LANES), F32)
    for idx in idxs:
        onehot = onehot + jnp.where(lane == idx, 1.0, 0.0)

    h1 = x_ref[...] + _dot(mixed.astype(BF16), wo_ref[...])
    h1_ref[...] = h1
    hn = h1 * _rms_scale(h1) * ln2_ref[...]

    row = lax.broadcasted_iota(jnp.int32, (tm, tm), 0)
    col = lax.broadcasted_iota(jnp.int32, (tm, tm), 1)
    below = jnp.where(col < row, 1.0, 0.0).astype(BF16)
    before = _dot(below, onehot.astype(BF16)) + carry_ref[0:1, :]
    ranks = [jnp.sum(jnp.where(lane == idx, before, 0.0), axis=-1, keepdims=True) for idx in idxs]
    running = carry_ref[0:1, :]
    for j in range(tm // TM_ROW):
        cbefore_ref[j] = jnp.broadcast_to(running, cbefore_ref.shape[1:])
        running = running + jnp.sum(onehot[j * TM_ROW:(j + 1) * TM_ROW], axis=0, keepdims=True)
    carry = jnp.where(routed, running, carry_ref[0:1, :])
    carry_ref[...] = jnp.broadcast_to(carry, carry_ref.shape)
    cnt_ref[...] = jnp.broadcast_to(carry, cnt_ref.shape)

    hn_hi = hn.astype(BF16)
    hn_ref[...] = hn_hi

    ri = jnp.zeros((tm, LANES), F32)
    for j, val in enumerate(idxs + ranks):
        ri = jnp.where(lane == float(j), val, ri)
    ric_ref[...] = ri.astype(jnp.int32)
    ri_t = ri.T[:2 * TOP_K].astype(jnp.int32)
    for j in range(tm // TM_ROW):
        ri_ref[j] = ri_t[:, j * TM_ROW:(j + 1) * TM_ROW]
    meta = jnp.zeros((tm, LANES), F32)
    for k in range(TOP_K):
        gate = exps[k] / denom
        gate_hi = gate.astype(BF16).astype(F32)
        meta = jnp.where(lane == float(k), idxs[k], meta)
        meta = jnp.where(lane == float(TOP_K + k), gate_hi, meta)
        meta = jnp.where(lane == float(2 * TOP_K + k), gate - gate_hi, meta)
    meta_ref[...] = meta.astype(BF16)

    hn_lo = (hn - hn_hi.astype(F32)).astype(BF16)
    logit_ref[step % 2] = (_dot(hn_hi, wrh_ref[...]) + _dot(hn_lo, wrh_ref[...])
                           + _dot(hn_hi, wrl_ref[...]) + br_ref[...])


def _mix(x2, o0, l0, o1, l1, o2, l2, o_b, pm, wa, wb, wo, ln2_w, wr_hi, wr_lo, br_pad):
    n_tok = x2.shape[0]
    tm = TM_MIX
    bsz = o1.shape[0]
    d1, d2 = DA_GROUPS[1][1], DA_GROUPS[2][1]
    tiles_per_seq = (n_tok // bsz) // tm
    n_tiles = n_tok // tm

    def tile(i):
        return jnp.minimum(i, n_tiles - 1)

    def routed(i):
        return jnp.maximum(i - 1, 0)

    def rows(width, which=tile):
        return pl.BlockSpec((tm, width), lambda i: (which(i), 0))

    def residue_major(d, width):
        return pl.BlockSpec((1, d, tm // d, width),
                            lambda i: (tile(i) // tiles_per_seq, 0, tile(i) % tiles_per_seq, 0))

    def whole(arr):
        return pl.BlockSpec(arr.shape, lambda i: (0,) * arr.ndim)

    ln2 = ln2_w.reshape(1, D_MODEL)
    return pl.pallas_call(
        _mix_kernel,
        grid=(n_tiles + 1,),
        in_specs=[rows(D_MODEL), rows(DA_WIDTH), rows(LANES),
                  residue_major(d1, DA_WIDTH), residue_major(d1, LANES),
                  residue_major(d2, DA_WIDTH), residue_major(d2, LANES),
                  rows(GLA_VALUE_DIM), rows(MERGE_W),
                  whole(wa), whole(wb), whole(wo), whole(ln2), whole(wr_hi), whole(wr_lo), whole(br_pad)],
        out_specs=(rows(D_MODEL), rows(D_MODEL),
                   pl.BlockSpec((tm // TM_ROW, 2 * TOP_K, TM_ROW), lambda i: (routed(i), 0, 0)),
                   rows(LANES, routed), rows(LANES, routed),
                   pl.BlockSpec((tm // TM_ROW, 8, LANES), lambda i: (routed(i), 0, 0)),
                   pl.BlockSpec((8, LANES), lambda i: (0, 0))),
        out_shape=(jax.ShapeDtypeStruct((n_tok, D_MODEL), F32),
                   jax.ShapeDtypeStruct((n_tok, D_MODEL), BF16),
                   jax.ShapeDtypeStruct((n_tok // TM_ROW, 2 * TOP_K, TM_ROW), jnp.int32),
                   jax.ShapeDtypeStruct((n_tok, LANES), jnp.int32),
                   jax.ShapeDtypeStruct((n_tok, LANES), BF16),
                   jax.ShapeDtypeStruct((n_tok // TM_ROW, 8, LANES), F32),
                   jax.ShapeDtypeStruct((8, LANES), F32)),
        scratch_shapes=[pltpu.VMEM((DA_HEADS, tm, DA_HEAD_DIM), F32), pltpu.VMEM((tm, LANES), F32),
                        pltpu.VMEM((DA_HEADS, tm, DA_HEAD_DIM), F32), pltpu.VMEM((tm, LANES), F32),
                        pltpu.VMEM((8, LANES), F32), pltpu.VMEM((2, tm, LANES), F32)],
        compiler_params=pltpu.CompilerParams(
            dimension_semantics=("arbitrary",), vmem_limit_bytes=VMEM_LIMIT),
        name="mix",
    )(x2, o0, l0, o1, l1, o2, l2, o_b, pm, wa, wb, wo, ln2, wr_hi, wr_lo, br_pad)


def _pack_pairs(x):
    n = x.shape[1] // 2
    rounded = x.astype(BF16).astype(F32)
    lo = lax.bitcast_convert_type(rounded[:, :n], U32) >> 16
    hi = lax.bitcast_convert_type(rounded[:, n:], U32) & jnp.uint32(0xFFFF0000)
    return hi | lo


def _unpack_pairs(u):
    lo = lax.bitcast_convert_type(u << 16, F32).astype(BF16)
    hi = lax.bitcast_convert_type(u & jnp.uint32(0xFFFF0000), F32).astype(BF16)
    return lo, hi


def _dispatch_kernel(zstart_ref, nwin_ref, wsrc_ref, wdst_ref, pos_ref, meta_ref, hn_ref, xin_ref,
                     buf_ref, zero_ref, sem, zsem):
    i = pl.program_id(0)
    tm = hn_ref.shape[0]
    n_buf_rows = buf_ref.shape[1]
    slot = i % 2

    def window_copy(s, buf_slot):
        src = pl.multiple_of(wsrc_ref[i * N_SLOT + s], ROW_ALIGN)
        dst = pl.multiple_of(wdst_ref[i * N_SLOT + s], ROW_ALIGN)
        return pltpu.make_async_copy(buf_ref.at[buf_slot, pl.ds(src, WIN), :],
                                     xin_ref.at[pl.ds(dst, WIN), :], sem)

    def wait_windows(step):
        def body(s, carry):
            pltpu.make_async_copy(buf_ref.at[0, pl.ds(0, WIN), :], xin_ref.at[pl.ds(0, WIN), :], sem).wait()
            return carry
        lax.fori_loop(0, nwin_ref[step], body, 0)

    @pl.when(i == 0)
    def _():
        zero_ref[...] = jnp.zeros_like(zero_ref)
        for j in range(zstart_ref.shape[0]):
            @pl.when(zstart_ref[j] >= 0)
            def _():
                start = pl.multiple_of(zstart_ref[j], ZCHUNK)
                cp = pltpu.make_async_copy(zero_ref, xin_ref.at[pl.ds(start, ZCHUNK), :], zsem)
                cp.start()
                cp.wait()

    row = lax.broadcasted_iota(jnp.int32, (n_buf_rows, tm), 0)

    perm = jnp.zeros((n_buf_rows, tm), F32)
    for k in range(TOP_K):
        perm = perm + jnp.where(row == pos_ref[0, k:k + 1, :], 1.0, 0.0)
    perm = perm.astype(BF16)
    buf_ref[slot, :, :D_MODEL // 2] = _pack_pairs(_dot(perm, hn_ref[...]))
    buf_ref[slot, :, D_MODEL // 2:] = lax.bitcast_convert_type(_dot(perm, meta_ref[...]), U32)

    @pl.when(i > 0)
    def _():
        wait_windows(i - 1)

    for buf_slot in range(2):
        @pl.when(slot == buf_slot)
        def _():
            def issue(s, carry):
                window_copy(s, buf_slot).start()
                return carry
            lax.fori_loop(0, nwin_ref[i], issue, 0)

    @pl.when(i == pl.num_programs(0) - 1)
    def _():
        wait_windows(i)


def _dispatch(zero_start, n_win, win_src, win_dst, pos, meta, hn, n_rows):
    n_tok = hn.shape[0]
    tm = TM_ROW
    n_tiles = n_tok // tm
    n_buf_rows = SORT_ROWS
    return pl.pallas_call(
        _dispatch_kernel,
        grid_spec=pltpu.PrefetchScalarGridSpec(
            num_scalar_prefetch=4,
            grid=(n_tiles,),
            in_specs=[
                pl.BlockSpec((1, TOP_K, tm), lambda i, *_: (i, 0, 0)),
                pl.BlockSpec((tm, LANES), lambda i, *_: (i, 0)),
                pl.BlockSpec((tm, D_MODEL), lambda i, *_: (i, 0)),
            ],
            out_specs=pl.BlockSpec(memory_space=pl.ANY),
            scratch_shapes=[pltpu.VMEM((2, n_buf_rows, ROW_W), U32),
                            pltpu.VMEM((ZCHUNK, ROW_W), U32),
                            pltpu.SemaphoreType.DMA(()), pltpu.SemaphoreType.DMA(())],
        ),
        out_shape=jax.ShapeDtypeStruct((n_rows, ROW_W), U32),
        compiler_params=pltpu.CompilerParams(
            dimension_semantics=("arbitrary",), vmem_limit_bytes=VMEM_LIMIT),
        name="dispatch",
    )(zero_start, n_win, win_src.reshape(-1), win_dst.reshape(-1), pos, meta, hn)


def _expert_kernel(be_ref, nused_ref, x_ref, wu_ref, bu_ref, wd_ref, bd_ref, y_ref, wu16_ref, wd16_ref):
    i = pl.program_id(0)
    used = i < nused_ref[0]

    @pl.when((i == 0) | (be_ref[i] != be_ref[jnp.maximum(i - 1, 0)]))
    def _():
        wu16_ref[...] = wu_ref[0].astype(BF16)
        wd16_ref[...] = wd_ref[0].astype(BF16)

    @pl.when(jnp.logical_not(used))
    def _():
        y_ref[...] = jnp.zeros_like(y_ref)

    @pl.when(used)
    def _():
        half = D_MODEL // 2
        x_lo, x_hi = _unpack_pairs(x_ref[:, :half])
        meta = lax.bitcast_convert_type(x_ref[:, half:half + 3 * TOP_K], F32)
        expert = be_ref[i].astype(F32)
        gate = jnp.zeros((x_ref.shape[0], 1), F32)
        for k in range(TOP_K):
            weight = meta[:, TOP_K + k:TOP_K + k + 1] + meta[:, 2 * TOP_K + k:2 * TOP_K + k + 1]
            gate = gate + jnp.where(meta[:, k:k + 1] == expert, weight, 0.0)
        hu = _dot(x_lo, wu16_ref[:half, :]) + _dot(x_hi, wu16_ref[half:, :]) + bu_ref[0]
        x_glu = jnp.minimum(hu[:, :D_FF], SWIGLU_LIMIT)
        x_lin = jnp.clip(hu[:, D_FF:], -SWIGLU_LIMIT, SWIGLU_LIMIT)
        act = x_glu * _sigmoid(SWIGLU_ALPHA * x_glu) * (x_lin + 1.0)
        y_ref[...] = _pack_pairs((_dot(act.astype(BF16), wd16_ref[...]) + bd_ref[0]) * gate)


def _experts(block_e, n_used, xin, w_up, b_up, w_down, b_down):
    n_rows = xin.shape[0]
    return pl.pallas_call(
        _expert_kernel,
        grid_spec=pltpu.PrefetchScalarGridSpec(
            num_scalar_prefetch=2,
            grid=(n_rows // TB,),
            in_specs=[
                pl.BlockSpec((TB, ROW_W), lambda i, be, nu: (i, 0)),
                pl.BlockSpec((1, D_MODEL, 2 * D_FF), lambda i, be, nu: (be[i], 0, 0)),
                pl.BlockSpec((1, 1, 2 * D_FF), lambda i, be, nu: (be[i], 0, 0)),
                pl.BlockSpec((1, D_FF, D_MODEL), lambda i, be, nu: (be[i], 0, 0)),
                pl.BlockSpec((1, 1, D_MODEL), lambda i, be, nu: (be[i], 0, 0)),
            ],
            out_specs=pl.BlockSpec((TB, D_MODEL // 2), lambda i, be, nu: (i, 0)),
            scratch_shapes=[pltpu.VMEM((D_MODEL, 2 * D_FF), BF16), pltpu.VMEM((D_FF, D_MODEL), BF16)],
        ),
        out_shape=jax.ShapeDtypeStruct((n_rows, D_MODEL // 2), U32),
        compiler_params=pltpu.CompilerParams(
            dimension_semantics=("arbitrary",), vmem_limit_bytes=VMEM_LIMIT),
        name="experts",
    )(block_e, n_used, xin, w_up, b_up.reshape(N_EXPERTS, 1, 2 * D_FF),
      w_down, b_down.reshape(N_EXPERTS, 1, D_MODEL))


def _combine_kernel(nwin_ref, wsrc_ref, wdst_ref, ric_ref, delta_ref, ricn_ref, deltan_ref, yb_ref, h1_ref,
                    lnf_ref, o_ref, stage_ref, buf_ref, pick_ref, sem):
    i = pl.program_id(0)
    tm = h1_ref.shape[0]
    n_buf_rows = buf_ref.shape[0]

    parity = i % 2

    def fetch(step, stage_slot):
        def body(s, carry):
            dst = pl.multiple_of(wdst_ref[step * N_SLOT + s], ROW_ALIGN)
            pltpu.make_async_copy(
                yb_ref.at[pl.ds(dst, WIN), :],
                stage_ref.at[stage_slot, pl.ds(pl.multiple_of(s * WIN, WIN), WIN), :],
                sem.at[stage_slot]).start()
            return carry
        lax.fori_loop(0, nwin_ref[step], body, 0)

    lane = lax.broadcasted_iota(jnp.int32, (tm, LANES), 1).astype(F32)
    col = lax.broadcasted_iota(jnp.int32, (tm, n_buf_rows), 1)

    def pick_round(pick, k, ric, delta):
        offset = jnp.sum(jnp.where(lane == ric[:, k:k + 1], delta, 0.0), axis=-1, keepdims=True)
        pos = (ric[:, TOP_K + k:TOP_K + k + 1] + offset).astype(jnp.int32)
        return pick + jnp.where(col == pos, 1.0, 0.0)

    @pl.when(i == 0)
    def _():
        buf_ref[...] = jnp.zeros_like(buf_ref)
        fetch(i, 0)
        pick = jnp.zeros((tm, n_buf_rows), F32)
        ric = ric_ref[...].astype(F32)
        for k in range(TOP_K):
            pick = pick_round(pick, k, ric, delta_ref[0])
        pick_ref[0] = pick.astype(BF16)

    for stage_slot in range(2):
        @pl.when((i + 1 < pl.num_programs(0)) & (parity != stage_slot))
        def _():
            fetch(i + 1, stage_slot)

    for stage_slot in range(2):
        @pl.when(parity == stage_slot)
        def _():
            def drain(s, carry):
                pltpu.make_async_copy(yb_ref.at[pl.ds(0, WIN), :], stage_ref.at[stage_slot, pl.ds(0, WIN), :],
                                      sem.at[stage_slot]).wait()
                return carry

            def compact(s, carry):
                src = pl.multiple_of(wsrc_ref[i * N_SLOT + s], ROW_ALIGN)
                buf_ref[pl.ds(src, WIN), :] = stage_ref[stage_slot, pl.ds(pl.multiple_of(s * WIN, WIN), WIN), :]
                return carry

            lax.fori_loop(0, nwin_ref[i], drain, 0)
            lax.fori_loop(0, nwin_ref[i], compact, 0)

    ric_next = ricn_ref[...].astype(F32)
    delta_next = deltan_ref[0]
    pick = pick_ref[parity]
    nxt = jnp.zeros((tm, n_buf_rows), F32)
    y_lo, y_hi = _unpack_pairs(buf_ref[...])
    nxt = pick_round(nxt, 0, ric_next, delta_next)
    left = _dot(pick, y_lo)
    nxt = pick_round(nxt, 1, ric_next, delta_next)
    right = _dot(pick, y_hi)
    nxt = pick_round(nxt, 2, ric_next, delta_next)
    h2 = h1_ref[...] + jnp.concatenate([left, right], axis=-1)
    nxt = pick_round(nxt, 3, ric_next, delta_next)
    o_ref[...] = h2 * _rms_scale(h2) * lnf_ref[...]
    pick_ref[1 - parity] = nxt.astype(BF16)


def _combine(n_win, win_src, win_dst, ric, delta, yb, h1, lnf_w):
    n_tok = h1.shape[0]
    tm = TM_ROW
    n_tiles = n_tok // tm
    return pl.pallas_call(
        _combine_kernel,
        grid_spec=pltpu.PrefetchScalarGridSpec(
            num_scalar_prefetch=3,
            grid=(n_tiles,),
            in_specs=[
                pl.BlockSpec((tm, LANES), lambda i, *_: (i, 0)),
                pl.BlockSpec((1, 1, LANES), lambda i, *_: (i, 0, 0)),
                pl.BlockSpec((tm, LANES), lambda i, *_: (jnp.minimum(i + 1, n_tiles - 1), 0)),
                pl.BlockSpec((1, 1, LANES), lambda i, *_: (jnp.minimum(i + 1, n_tiles - 1), 0, 0)),
                pl.BlockSpec(memory_space=pl.ANY),
                pl.BlockSpec((tm, D_MODEL), lambda i, *_: (i, 0)),
                pl.BlockSpec((1, D_MODEL), lambda i, *_: (0, 0)),
            ],
            out_specs=pl.BlockSpec((tm, D_MODEL), lambda i, *_: (i, 0)),
            scratch_shapes=[pltpu.VMEM((2, N_SLOT * WIN, D_MODEL // 2), U32),
                            pltpu.VMEM((SORT_ROWS, D_MODEL // 2), U32),
                            pltpu.VMEM((2, tm, SORT_ROWS), BF16), pltpu.SemaphoreType.DMA((2,))],
        ),
        out_shape=jax.ShapeDtypeStruct((n_tok, D_MODEL), F32),
        compiler_params=pltpu.CompilerParams(
            dimension_semantics=("arbitrary",), vmem_limit_bytes=VMEM_LIMIT),
        name="combine",
    )(n_win, win_src.reshape(-1), win_dst.reshape(-1), ric, delta, ric, delta, yb, h1,
      lnf_w.reshape(1, D_MODEL))


def _layer(h, ln1_w, w_in, w_gk2, b_gk2, gla_norm_w, w_proj_a, w_proj_b, w_out,
           ln2_w, w_router, b_router, w_up, b_up, w_down, b_down, lnf_w):
    bsz, seq, _ = h.shape
    n_tok = bsz * seq

    n_main = 3 * QKV_W + 2 * GLA_KEY_DIM + 2 * GLA_VALUE_DIM
    pad = LANES - GLA_GATE_RANK
    w_all = jnp.concatenate(
        [w_in[:, :n_main + GLA_GATE_RANK], jnp.zeros((D_MODEL, pad), F32), w_in[:, n_main + GLA_GATE_RANK:]],
        axis=1).astype(BF16)
    w2_pad = jnp.concatenate([w_gk2, jnp.zeros((pad, GLA_KEY_DIM), F32)], axis=0).astype(BF16)
    wr_pad = jnp.concatenate([w_router, jnp.zeros((D_MODEL, LANES - N_EXPERTS), F32)], axis=1)
    wr_hi = wr_pad.astype(BF16)
    wr_lo = (wr_pad - wr_hi.astype(F32)).astype(BF16)
    br_pad = jnp.concatenate([b_router, jnp.zeros((LANES - N_EXPERTS,), F32)]).reshape(1, LANES)

    pa0, pa1, pa2, pg, pm = _inproj(h, ln1_w, w_all)
    o0, l0 = _dil_attn(pa0)
    o1, l1 = _dil_attn(pa1)
    o2, l2 = _dil_attn(pa2)
    o_b = _gla(pg, w2_pad, b_gk2, gla_norm_w)

    h1, hn, ri, ric, meta, carry_f, cnt = _mix(
        h.reshape(n_tok, D_MODEL), o0.reshape(n_tok, DA_WIDTH), l0.reshape(n_tok, LANES),
        o1, l1, o2, l2, o_b.reshape(n_tok, GLA_VALUE_DIM), pm.reshape(n_tok, MERGE_W),
        w_proj_a.astype(BF16), w_proj_b.astype(BF16), w_out.astype(BF16), ln2_w, wr_hi, wr_lo, br_pad)

    i32 = jnp.int32
    n_tiles = n_tok // TM_ROW
    experts = jnp.arange(N_EXPERTS)
    counts = cnt[0, :N_EXPERTS].astype(i32)
    before = carry_f[:, 0, :N_EXPERTS].astype(i32)
    run = jnp.concatenate([before[1:], counts[None]], axis=0) - before
    run_al = (run + ROW_ALIGN - 1) // ROW_ALIGN * ROW_ALIGN
    rows_end = jnp.cumsum(run_al, axis=0)
    rows_before = rows_end - run_al
    used = rows_end[-1]
    slack = WIN - ROW_ALIGN
    padded = (used + slack + TB - 1) // TB * TB
    pad_end = jnp.cumsum(padded)
    pad_start = pad_end - padded
    n_asg = n_tok * TOP_K
    n_rows = (n_asg + n_tiles * N_EXPERTS * (ROW_ALIGN - 1)
              + N_EXPERTS * (slack + TB - 1) + TB - 1) // TB * TB
    n_blocks = n_rows // TB
    block_starts = jnp.arange(n_blocks) * TB
    block_e = jnp.minimum(jnp.sum(pad_end[None, :] <= block_starts[:, None], axis=1),
                          N_EXPERTS - 1).astype(i32)
    n_used = (pad_end[-1:] // TB).astype(i32)

    run_end = jnp.cumsum(run_al, axis=1)
    run_start = run_end - run_al
    wins = (run + WIN - 1) // WIN
    wins_end = jnp.cumsum(wins, axis=1)
    wins_start = wins_end - wins
    n_win = wins_end[:, -1].astype(i32)
    slots = jnp.arange(N_SLOT)
    slot_e = jnp.sum(wins_end[:, None, :] <= slots[None, :, None], axis=-1)
    slot_is = slot_e[..., None] == experts

    def of_slot(table):
        return jnp.sum(jnp.where(slot_is, table[:, None, :], 0), axis=-1)

    win_off = (slots[None, :] - of_slot(wins_start)) * WIN
    win_src = (of_slot(run_start) + win_off).astype(i32)
    win_dst = (of_slot(pad_start[None, :] + rows_before) + win_off).astype(i32)
    asg_is = ri[:, :TOP_K, :, None] == experts
    pos = ri[:, TOP_K:, :] + jnp.sum(jnp.where(asg_is, (run_start - before)[:, None, None, :], 0), axis=-1)
    delta = jnp.pad((run_start - before).astype(F32), ((0, 0), (0, LANES - N_EXPERTS)))

    chunk_back = ZCHUNK * (1 + jnp.arange((slack + TB - 1 + ZCHUNK - 1) // ZCHUNK + 1))
    region_chunks = pad_end[:, None] - chunk_back[None, :]
    region_ok = (region_chunks >= pad_start[:, None]) & (region_chunks + ZCHUNK > (pad_start + used)[:, None])
    tail_chunks = pad_end[-1] + ZCHUNK * jnp.arange((n_rows - n_asg) // ZCHUNK)
    zero_start = jnp.concatenate([jnp.where(region_ok, region_chunks, -1).reshape(-1),
                                  jnp.where(tail_chunks < n_rows, tail_chunks, -1)]).astype(i32)

    xin = _dispatch(zero_start, n_win, win_src, win_dst, pos.astype(i32), meta, hn, n_rows)
    yb = _experts(block_e, n_used, xin, w_up, b_up, w_down, b_down)
    out = _combine(n_win, win_src, win_dst, ric, delta.reshape(-1, 1, LANES), yb, h1, lnf_w)
    return out.reshape(bsz, seq, D_MODEL)


def kernel(x, ln1_w, w_in, w_gk2, b_gk2, gla_norm_w, w_proj_a, w_proj_b, w_out, ln2_w, w_router,
           b_router, w_up, b_up, w_down, b_down, lnf_w):
    assert x.shape[-1] == D_MODEL and ln1_w.shape[0] == 1, "one layer of width D_MODEL"
    return _layer(x, ln1_w[0], w_in[0], w_gk2[0], b_gk2[0], gla_norm_w[0], w_proj_a[0], w_proj_b[0],
                  w_out[0], ln2_w[0], w_router[0], b_router[0], w_up[0], b_up[0], w_down[0],
                  b_down[0], lnf_w)
```

```python
import functools

import jax
import jax.numpy as jnp
from jax import lax
from jax.experimental import pallas as pl
from jax.experimental.pallas import tpu as pltpu

F32 = jnp.float32
BF16 = jnp.bfloat16
U32 = jnp.uint32

D_MODEL = 1024
DA_GROUPS = ((128, 1), (512, 4), (2048, 16))
DA_HEADS = 4
DA_HEAD_DIM = 128
DA_WIDTH = DA_HEADS * DA_HEAD_DIM
DA_BLOCK = 128
GLA_HEADS = 4
GLA_KEY_DIM = D_MODEL // 2
GLA_VALUE_DIM = D_MODEL
GLA_DK = GLA_KEY_DIM // GLA_HEADS
GLA_DV = GLA_VALUE_DIM // GLA_HEADS
GLA_GATE_RANK = 16
GLA_GATE_NORMALIZER = 16.0
GLA_CHUNK = 64
N_EXPERTS = 32
TOP_K = 4
D_FF = D_MODEL
SWIGLU_ALPHA = 1.702
SWIGLU_LIMIT = 7.0
RMS_EPS = 1e-5
NEG_INF = -1e30

LANES = 128
QKV_W = 3 * DA_WIDTH
GLA_W = 2 * GLA_KEY_DIM + 2 * GLA_VALUE_DIM + LANES
MERGE_W = 2 * D_MODEL

DA_QB = 4
LSE_LANES = LANES // DA_HEADS
TM_IN = 512
N_CHUNK = 512
T_GLA = 512
GLA_SUB = 256
GLA_HEADS_PER_STEP = 4
TM_MIX = 512
TB = 512
TM_ROW = 256
ROW_ALIGN = 8
WIN = 32
N_SLOT = N_EXPERTS + TM_ROW * TOP_K // WIN
SORT_ROWS = TM_ROW * TOP_K + N_EXPERTS * (ROW_ALIGN - 1) + WIN
ZCHUNK = 256
ROW_W = D_MODEL // 2 + LANES
VMEM_LIMIT = 56 * 1024 * 1024
VMEM_LIMIT_INPROJ = 62 * 1024 * 1024

_NT = (((1,), (1,)), ((), ()))
_TN = (((0,), (0,)), ((), ()))


def _dot(a, b):
    return jnp.dot(a, b, preferred_element_type=F32)


def _sigmoid(x):
    return 1.0 / (1.0 + jnp.exp(-x))


def _rms_scale(x):
    return lax.rsqrt(jnp.mean(x * x, axis=-1, keepdims=True) + RMS_EPS)


def _inproj_kernel(x_ref, ln_ref, wmain_hbm, wtail_hbm, pa0_ref, pa1_ref, pa2_ref, pg_ref, pm_ref,
                   w_ref, xs_ref, xn_ref, wsem):
    tm = x_ref.shape[1]
    n_slab = D_MODEL // LANES

    @pl.when((pl.program_id(0) == 0) & (pl.program_id(1) == 0))
    def _():
        n_main = wmain_hbm.shape[1]
        main = pltpu.make_async_copy(wmain_hbm, w_ref.at[:, :n_main], wsem.at[0])
        tail = pltpu.make_async_copy(wtail_hbm, w_ref.at[:, n_main:], wsem.at[1])
        main.start()
        tail.start()
        main.wait()
        tail.wait()

    def project(out_write, col0, width, post=None):
        for c0 in range(0, width, N_CHUNK):
            cw = min(N_CHUNK, width - c0)
            val = _dot(xs_ref[...], w_ref[:, col0 + c0:col0 + c0 + cw])
            out_write(c0, cw, (val if post is None else post(val)).astype(BF16))

    x = x_ref[0]
    xn = x * _rms_scale(x) * ln_ref[...]
    xs_ref[...] = xn.astype(BF16)
    for j in range(n_slab):
        xn_ref[j] = xn[:, j * LANES:(j + 1) * LANES]

    def write_to(ref):
        def write(c0, cw, val):
            ref[0, :, c0:c0 + cw] = val
        return write

    def write_qkv(out_ref, d, n):
        def write(c0, cw, val):
            for r in range(d):
                out_ref[0, r, c0 // DA_WIDTH] = val[r * n:(r + 1) * n]
        return write

    project(write_qkv(pa0_ref, 1, tm), 0, QKV_W)
    project(write_to(pg_ref), 3 * QKV_W, GLA_W)
    project(write_to(pm_ref), 3 * QKV_W + GLA_W, MERGE_W, post=_sigmoid)

    for gi, out_ref in ((1, pa1_ref), (2, pa2_ref)):
        d = DA_GROUPS[gi][1]
        n = tm // d
        for r in range(d):
            for j in range(n_slab):
                xs_ref[r * n:(r + 1) * n, j * LANES:(j + 1) * LANES] = (
                    xn_ref[j, pl.ds(r, n, stride=d), :].astype(BF16))
        project(write_qkv(out_ref, d, n), gi * QKV_W, QKV_W)


def _inproj(x, ln1_w, w_main, w_tail):
    assert N_CHUNK == DA_WIDTH and w_main.shape[1] % LANES == 0
    bsz, seq, _ = x.shape
    tm = TM_IN
    d1, d2 = DA_GROUPS[1][1], DA_GROUPS[2][1]

    def qkv_shape(d):
        return jax.ShapeDtypeStruct((bsz, d, 3, seq // d, DA_WIDTH), BF16)

    def qkv_spec(d):
        return pl.BlockSpec((1, d, 3, tm // d, DA_WIDTH), lambda b, i: (b, 0, 0, i, 0))

    return pl.pallas_call(
        _inproj_kernel,
        grid=(bsz, seq // tm),
        in_specs=[
            pl.BlockSpec((1, tm, D_MODEL), lambda b, i: (b, i, 0)),
            pl.BlockSpec((1, D_MODEL), lambda b, i: (0, 0)),
            pl.BlockSpec(memory_space=pl.ANY),
            pl.BlockSpec(memory_space=pl.ANY),
        ],
        out_specs=(
            qkv_spec(1), qkv_spec(d1), qkv_spec(d2),
            pl.BlockSpec((1, tm, GLA_W), lambda b, i: (b, i, 0)),
            pl.BlockSpec((1, tm, MERGE_W), lambda b, i: (b, i, 0)),
        ),
        out_shape=(qkv_shape(1), qkv_shape(d1), qkv_shape(d2),
                   jax.ShapeDtypeStruct((bsz, seq, GLA_W), BF16),
                   jax.ShapeDtypeStruct((bsz, seq, MERGE_W), BF16)),
        scratch_shapes=[pltpu.VMEM((D_MODEL, w_main.shape[1] + w_tail.shape[1]), BF16),
                        pltpu.VMEM((tm, D_MODEL), BF16),
                        pltpu.VMEM((D_MODEL // LANES, tm, LANES), F32),
                        pltpu.SemaphoreType.DMA((2,))],
        compiler_params=pltpu.CompilerParams(
            dimension_semantics=("arbitrary", "arbitrary"), vmem_limit_bytes=VMEM_LIMIT_INPROJ),
        name="inproj",
    )(x, ln1_w.reshape(1, D_MODEL), w_main, w_tail)


def _dil_attn_kernel(q_ref, kp_ref, kc_ref, vp_ref, vc_ref, o_ref, l_ref, s_ref, p_ref, r_ref):
    n = pl.program_id(2)
    blk = DA_BLOCK
    qi = lax.broadcasted_iota(jnp.int32, (blk, 2 * blk), 0)
    kj = lax.broadcasted_iota(jnp.int32, (blk, 2 * blk), 1)
    band = (kj >= qi) & (kj <= qi + blk)
    band_first = (kj >= jnp.where(n > 0, qi, blk)) & (kj <= qi + blk)
    scale = DA_HEAD_DIM ** -0.5
    items = [(b, h) for b in range(DA_QB) for h in range(DA_HEADS)]

    def rows(b):
        return slice(b * blk, (b + 1) * blk)

    def cols(h):
        return slice(h * DA_HEAD_DIM, (h + 1) * DA_HEAD_DIM)

    def window(prev_ref, cur_ref, b, h):
        before = prev_ref[0, 0, 0, :, cols(h)] if b == 0 else cur_ref[0, 0, 0, rows(b - 1), cols(h)]
        return jnp.concatenate([before, cur_ref[0, 0, 0, rows(b), cols(h)]], axis=0)

    for i, (b, h) in enumerate(items):
        s = lax.dot_general(q_ref[0, 0, 0, rows(b), cols(h)], window(kp_ref, kc_ref, b, h), _NT,
                            preferred_element_type=F32) * scale
        s_ref[i] = jnp.where(band_first if b == 0 else band, s, NEG_INF)
    for i, (b, h) in enumerate(items):
        s = s_ref[i]
        m = jnp.max(s, axis=-1, keepdims=True)
        p = jnp.exp(s - m)
        l = jnp.sum(p, axis=-1, keepdims=True)
        p_ref[i] = p.astype(BF16)
        r_ref[i] = jnp.broadcast_to(1.0 / l, (blk, DA_HEAD_DIM))
        l_ref[0, 0, rows(b), h * LSE_LANES:(h + 1) * LSE_LANES] = jnp.broadcast_to(
            m + jnp.log(l), (blk, LSE_LANES))
    for i, (b, h) in enumerate(items):
        acc = _dot(p_ref[i], window(vp_ref, vc_ref, b, h))
        o_ref[0, 0, rows(b), cols(h)] = (acc * r_ref[i]).astype(o_ref.dtype)


def _dil_attn(pa):
    bsz, d, _, sub_len, _ = pa.shape
    rows = DA_QB * DA_BLOCK
    n_items = DA_QB * DA_HEADS

    def cur(sec):
        return pl.BlockSpec((1, 1, 1, rows, DA_WIDTH), lambda b, r, n: (b, r, sec, n, 0))

    def prev(sec):
        return pl.BlockSpec((1, 1, 1, DA_BLOCK, DA_WIDTH),
                            lambda b, r, n: (b, r, sec, jnp.maximum(n * DA_QB - 1, 0), 0))

    def out(width):
        return pl.BlockSpec((1, 1, rows, width), lambda b, r, n: (b, r, n, 0))

    return pl.pallas_call(
        _dil_attn_kernel,
        grid=(bsz, d, sub_len // rows),
        in_specs=[cur(0), prev(1), cur(1), prev(2), cur(2)],
        out_specs=(out(DA_WIDTH), out(LANES)),
        out_shape=(jax.ShapeDtypeStruct((bsz, d, sub_len, DA_WIDTH), BF16),
                   jax.ShapeDtypeStruct((bsz, d, sub_len, LANES), F32)),
        scratch_shapes=[pltpu.VMEM((n_items, DA_BLOCK, 2 * DA_BLOCK), F32),
                        pltpu.VMEM((n_items, DA_BLOCK, 2 * DA_BLOCK), BF16),
                        pltpu.VMEM((n_items, DA_BLOCK, DA_HEAD_DIM), F32)],
        compiler_params=pltpu.CompilerParams(
            dimension_semantics=("parallel", "parallel", "arbitrary"), vmem_limit_bytes=VMEM_LIMIT),
        name=f"dil_attn_d{d}",
    )(pa, pa, pa, pa, pa)


def _gla_kernel(q_ref, k_ref, v_ref, og_ref, lr_ref, w2_ref, b2_ref, nw_ref, o_ref, st_ref, mask_ref, keep_ref):
    t = pl.program_id(2)
    c = GLA_CHUNK
    tt = q_ref.shape[1]
    n_c = tt // c

    sub = mask_ref.shape[0]

    @pl.when(t == 0)
    def _():
        st_ref[...] = jnp.zeros_like(st_ref)
        row = lax.broadcasted_iota(jnp.int32, (sub, sub), 0)
        col = lax.broadcasted_iota(jnp.int32, (sub, sub), 1)
        keep = (col <= row) & (col >= row - row % c)
        keep_ref[...] = jnp.where(keep, 1.0, 0.0)
        mask_ref[...] = jnp.where(keep, 1.0, 0.0).astype(BF16)

    heads = range(q_ref.shape[2] // GLA_DK)
    mask = mask_ref[...]

    def kcols(h):
        return slice(h * GLA_DK, (h + 1) * GLA_DK)

    def vcols(h):
        return slice(h * GLA_DV, (h + 1) * GLA_DV)

    gpre = _dot(lr_ref[0], w2_ref[...]) + b2_ref[...]
    forget = (jnp.minimum(gpre, 0.0) - jnp.log(1.0 + jnp.exp(-jnp.abs(gpre)))) / GLA_GATE_NORMALIZER
    g_hi = forget.astype(BF16)
    g_lo = (forget - g_hi.astype(F32)).astype(BF16)
    b, b_last, q_e, k_e, k_end = [], [], [], [], []
    for h in heads:
        g_cat = jnp.concatenate([g_hi[:, kcols(h)], g_lo[:, kcols(h)]], axis=-1)
        csum = jnp.concatenate([_dot(mask, g_cat[s0:s0 + sub]) for s0 in range(0, tt, sub)], axis=0)
        b.append(csum[:, :GLA_DK] + csum[:, GLA_DK:])
    for h in heads:
        b_last.append(b[h].reshape(n_c, c, GLA_DK)[:, c - 1:c, :])
        b_to_end = (b_last[h] - b[h].reshape(n_c, c, GLA_DK)).reshape(tt, GLA_DK)
        q = q_ref[0, :, kcols(h)].astype(F32)
        k = k_ref[0, :, kcols(h)].astype(F32)
        q_e.append((q * ((GLA_DK ** -0.5) * jnp.exp(b[h]))).astype(BF16))
        k_e.append((k * jnp.exp(-b[h])).astype(BF16))
        k_end.append((k * jnp.exp(b_to_end)).astype(BF16))
    o_intra = []
    for h in heads:
        parts = []
        for s0 in range(0, tt, sub):
            ss = slice(s0, s0 + sub)
            att = lax.dot_general(q_e[h][ss], k_e[h][ss], _NT, preferred_element_type=F32)
            att = jnp.where(keep_ref[...] > 0.0, att, 0.0).astype(BF16)
            parts.append(_dot(att, v_ref[0, ss, vcols(h)]))
        o_intra.append(jnp.concatenate(parts, axis=0))
    decay = [jnp.exp(b_last[h].reshape(n_c, GLA_DK).T) for h in heads]
    st = [st_ref[h] for h in heads]
    outs = [[] for _ in heads]
    for ci in range(n_c):
        rs = slice(ci * c, (ci + 1) * c)
        for h in heads:
            outs[h].append(o_intra[h][rs] + _dot(q_e[h][rs], st[h].astype(BF16)))
            st[h] = decay[h][:, ci:ci + 1] * st[h] + lax.dot_general(
                k_end[h][rs], v_ref[0, rs, vcols(h)], _TN, preferred_element_type=F32)
    for h in heads:
        st_ref[h] = st[h]
        o = jnp.concatenate(outs[h], axis=0)
        o = o * _rms_scale(o) * nw_ref[...]
        gate = og_ref[0, :, vcols(h)].astype(F32)
        o_ref[0, :, vcols(h)] = (o * (gate * _sigmoid(gate))).astype(BF16)


def _gla(pg, w2_pad, b_gk2, gla_norm_w):
    bsz, seq, _ = pg.shape
    t = T_GLA
    hps = GLA_HEADS_PER_STEP
    wk, wv = hps * GLA_DK, hps * GLA_DV
    kq = GLA_KEY_DIM // wk
    kv = 2 * GLA_KEY_DIM // wv
    kg = kv + GLA_VALUE_DIM // wv
    klr = (2 * GLA_KEY_DIM + 2 * GLA_VALUE_DIM) // LANES
    return pl.pallas_call(
        _gla_kernel,
        grid=(bsz, GLA_HEADS // hps, seq // t),
        in_specs=[
            pl.BlockSpec((1, t, wk), lambda b, h, i: (b, i, h)),
            pl.BlockSpec((1, t, wk), lambda b, h, i: (b, i, kq + h)),
            pl.BlockSpec((1, t, wv), lambda b, h, i: (b, i, kv + h)),
            pl.BlockSpec((1, t, wv), lambda b, h, i: (b, i, kg + h)),
            pl.BlockSpec((1, t, LANES), lambda b, h, i: (b, i, klr)),
            pl.BlockSpec((LANES, wk), lambda b, h, i: (0, h)),
            pl.BlockSpec((1, wk), lambda b, h, i: (0, h)),
            pl.BlockSpec((1, GLA_DV), lambda b, h, i: (0, 0)),
        ],
        out_specs=pl.BlockSpec((1, t, wv), lambda b, h, i: (b, i, h)),
        out_shape=jax.ShapeDtypeStruct((bsz, seq, GLA_VALUE_DIM), BF16),
        scratch_shapes=[pltpu.VMEM((hps, GLA_DK, GLA_DV), F32), pltpu.VMEM((GLA_SUB, GLA_SUB), BF16),
                        pltpu.VMEM((GLA_SUB, GLA_SUB), F32)],
        compiler_params=pltpu.CompilerParams(
            dimension_semantics=("parallel", "parallel", "arbitrary"), vmem_limit_bytes=VMEM_LIMIT),
        name="gla",
    )(pg, pg, pg, pg, pg, w2_pad, b_gk2.reshape(1, GLA_KEY_DIM), gla_norm_w.reshape(1, GLA_DV))


def _mix_kernel(x_ref, o0_ref, l0_ref, o1_ref, l1_ref, o2_ref, l2_ref, ob_ref, pm_ref,
                wa_ref, wb_ref, wo_ref, ln2_ref, wrh_ref, wrl_ref, br_ref,
                h1_ref, hn_ref, ri_ref, ric_ref, meta_ref, cbefore_ref, cnt_ref,
                po1_ref, pl1_ref, po2_ref, pl2_ref, carry_ref, logit_ref):
    step = pl.program_id(0)
    tm = x_ref.shape[0]

    @pl.when(step == 0)
    def _():
        carry_ref[...] = jnp.zeros_like(carry_ref)
        logit_ref[...] = jnp.zeros_like(logit_ref)

    routed = step > 0
    lane = lax.broadcasted_iota(jnp.int32, (tm, LANES), 1).astype(F32)
    route = {"work": None, "vals": [], "idxs": []}

    def topk_round():
        work = route["work"]
        m = jnp.max(work, axis=-1, keepdims=True)
        idx = jnp.min(jnp.where(work == m, lane, float(LANES)), axis=-1, keepdims=True)
        route["vals"].append(m)
        route["idxs"].append(idx)
        route["work"] = jnp.where(lane == idx, -jnp.inf, work)

    route["work"] = jnp.where(lane < N_EXPERTS, logit_ref[(step + 1) % 2], -jnp.inf)

    for o_ref, l_ref, po_ref, pl_ref, (_, d) in ((o1_ref, l1_ref, po1_ref, pl1_ref, DA_GROUPS[1]),
                                                 (o2_ref, l2_ref, po2_ref, pl2_ref, DA_GROUPS[2])):
        topk_round()
        n = tm // d
        for r in range(d):
            pl_ref[pl.ds(r, n, stride=d), :] = l_ref[0, r]
            for h in range(DA_HEADS):
                sl = slice(h * DA_HEAD_DIM, (h + 1) * DA_HEAD_DIM)
                po_ref[h, pl.ds(r, n, stride=d), :] = o_ref[0, r, :, sl].astype(F32)

    l0, l1, l2 = l0_ref[...], pl1_ref[...], pl2_ref[...]
    mx = jnp.maximum(jnp.maximum(l0, l1), l2)
    e0, e1, e2 = jnp.exp(l0 - mx), jnp.exp(l1 - mx), jnp.exp(l2 - mx)
    inv = 1.0 / (e0 + e1 + e2)
    w0, w1, w2 = e0 * inv, e1 * inv, e2 * inv
    topk_round()
    heads = []
    for h in range(DA_HEADS):
        sl = slice(h * DA_HEAD_DIM, (h + 1) * DA_HEAD_DIM)
        at = slice(h * LSE_LANES, h * LSE_LANES + 1)
        o_h = w0[:, at] * o0_ref[:, sl].astype(F32) + w1[:, at] * po1_ref[h] + w2[:, at] * po2_ref[h]
        heads.append(o_h.astype(BF16))
    o_a = jnp.concatenate(heads, axis=-1)
    topk_round()

    gates = pm_ref[...].astype(F32)
    mixed = (gates[:, :D_MODEL] * _dot(o_a, wa_ref[...])
             + gates[:, D_MODEL:] * _dot(ob_ref[...], wb_ref[...]))
    vals, idxs = route["vals"], route["idxs"]
    assert len(vals) == TOP_K
    exps = [jnp.exp(v - vals[0]) for v in vals]
    denom = exps[0] + exps[1] + exps[2] + exps[3]
    onehot = jnp.zeros((tm, LANES), F32)
    for idx in idxs:
        onehot = onehot + jnp.where(lane == idx, 1.0, 0.0)

    h1 = x_ref[...] + _dot(mixed.astype(BF16), wo_ref[...])
    h1_ref[...] = h1
    hn = h1 * _rms_scale(h1) * ln2_ref[...]

    row = lax.broadcasted_iota(jnp.int32, (tm, tm), 0)
    col = lax.broadcasted_iota(jnp.int32, (tm, tm), 1)
    below = jnp.where(col < row, 1.0, 0.0).astype(BF16)
    before = _dot(below, onehot.astype(BF16)) + carry_ref[0:1, :]
    ranks = [jnp.sum(jnp.where(lane == idx, before, 0.0), axis=-1, keepdims=True) for idx in idxs]
    running = carry_ref[0:1, :]
    for j in range(tm // TM_ROW):
        cbefore_ref[j] = jnp.broadcast_to(running, cbefore_ref.shape[1:])
        running = running + jnp.sum(onehot[j * TM_ROW:(j + 1) * TM_ROW], axis=0, keepdims=True)
    carry = jnp.where(routed, running, carry_ref[0:1, :])
    carry_ref[...] = jnp.broadcast_to(carry, carry_ref.shape)
    cnt_ref[...] = jnp.broadcast_to(carry, cnt_ref.shape)

    hn_hi = hn.astype(BF16)
    hn_ref[...] = hn_hi

    ri = jnp.zeros((tm, LANES), F32)
    for j, val in enumerate(idxs + ranks):
        ri = jnp.where(lane == float(j), val, ri)
    ric_ref[...] = ri.astype(jnp.int32)
    ri_t = ri.T[:2 * TOP_K].astype(jnp.int32)
    for j in range(tm // TM_ROW):
        ri_ref[j] = ri_t[:, j * TM_ROW:(j + 1) * TM_ROW]
    meta = jnp.zeros((tm, LANES), F32)
    for k in range(TOP_K):
        gate = exps[k] / denom
        gate_hi = gate.astype(BF16).astype(F32)
        meta = jnp.where(lane == float(k), idxs[k], meta)
        meta = jnp.where(lane == float(TOP_K + k), gate_hi, meta)
        meta = jnp.where(lane == float(2 * TOP_K + k), gate - gate_hi, meta)
    meta_ref[...] = meta.astype(BF16)

    hn_lo = (hn - hn_hi.astype(F32)).astype(BF16)
    logit_ref[step % 2] = (_dot(hn_hi, wrh_ref[...]) + _dot(hn_lo, wrh_ref[...])
                           + _dot(hn_hi, wrl_ref[...]) + br_ref[...])


def _mix(x2, o0, l0, o1, l1, o2, l2, o_b, pm, wa, wb, wo, ln2_w, wr_hi, wr_lo, br_pad):
    n_tok = x2.shape[0]
    tm = TM_MIX
    bsz = o1.shape[0]
    d1, d2 = DA_GROUPS[1][1], DA_GROUPS[2][1]
    tiles_per_seq = (n_tok // bsz) // tm
    n_tiles = n_tok // tm

    def tile(i):
        return jnp.minimum(i, n_tiles - 1)

    def routed(i):
        return jnp.maximum(i - 1, 0)

    def rows(width, which=tile):
        return pl.BlockSpec((tm, width), lambda i: (which(i), 0))

    def residue_major(d, width):
        return pl.BlockSpec((1, d, tm // d, width),
                            lambda i: (tile(i) // tiles_per_seq, 0, tile(i) % tiles_per_seq, 0))

    def whole(arr):
        return pl.BlockSpec(arr.shape, lambda i: (0,) * arr.ndim)

    ln2 = ln2_w.reshape(1, D_MODEL)
    return pl.pallas_call(
        _mix_kernel,
        grid=(n_tiles + 1,),
        in_specs=[rows(D_MODEL), rows(DA_WIDTH), rows(LANES),
                  residue_major(d1, DA_WIDTH), residue_major(d1, LANES),
                  residue_major(d2, DA_WIDTH), residue_major(d2, LANES),
                  rows(GLA_VALUE_DIM), rows(MERGE_W),
                  whole(wa), whole(wb), whole(wo), whole(ln2), whole(wr_hi), whole(wr_lo), whole(br_pad)],
        out_specs=(rows(D_MODEL), rows(D_MODEL),
                   pl.BlockSpec((tm // TM_ROW, 2 * TOP_K, TM_ROW), lambda i: (routed(i), 0, 0)),
                   rows(LANES, routed), rows(LANES, routed),
                   pl.BlockSpec((tm // TM_ROW, 8, LANES), lambda i: (routed(i), 0, 0)),
                   pl.BlockSpec((8, LANES), lambda i: (0, 0))),
        out_shape=(jax.ShapeDtypeStruct((n_tok, D_MODEL), F32),
                   jax.ShapeDtypeStruct((n_tok, D_MODEL), BF16),
                   jax.ShapeDtypeStruct((n_tok // TM_ROW, 2 * TOP_K, TM_ROW), jnp.int32),
                   jax.ShapeDtypeStruct((n_tok, LANES), jnp.int32),
                   jax.ShapeDtypeStruct((n_tok, LANES), BF16),
                   jax.ShapeDtypeStruct((n_tok // TM_ROW, 8, LANES), F32),
                   jax.ShapeDtypeStruct((8, LANES), F32)),
        scratch_shapes=[pltpu.VMEM((DA_HEADS, tm, DA_HEAD_DIM), F32), pltpu.VMEM((tm, LANES), F32),
                        pltpu.VMEM((DA_HEADS, tm, DA_HEAD_DIM), F32), pltpu.VMEM((tm, LANES), F32),
                        pltpu.VMEM((8, LANES), F32), pltpu.VMEM((2, tm, LANES), F32)],
        compiler_params=pltpu.CompilerParams(
            dimension_semantics=("arbitrary",), vmem_limit_bytes=VMEM_LIMIT),
        name="mix",
    )(x2, o0, l0, o1, l1, o2, l2, o_b, pm, wa, wb, wo, ln2, wr_hi, wr_lo, br_pad)


def _pack_pairs(x):
    n = x.shape[1] // 2
    rounded = x.astype(BF16).astype(F32)
    lo = lax.bitcast_convert_type(rounded[:, :n], U32) >> 16
    hi = lax.bitcast_convert_type(rounded[:, n:], U32) & jnp.uint32(0xFFFF0000)
    return hi | lo


def _unpack_pairs(u):
    lo = lax.bitcast_convert_type(u << 16, F32).astype(BF16)
    hi = lax.bitcast_convert_type(u & jnp.uint32(0xFFFF0000), F32).astype(BF16)
    return lo, hi


def _dispatch_kernel(zstart_ref, nwin_ref, wsrc_ref, wdst_ref, pos_ref, meta_ref, hn_ref, xin_ref,
                     buf_ref, zero_ref, sem, zsem):
    i = pl.program_id(0)
    tm = hn_ref.shape[0]
    n_buf_rows = buf_ref.shape[1]
    slot = i % 2

    def window_copy(s, buf_slot):
        src = pl.multiple_of(wsrc_ref[i * N_SLOT + s], ROW_ALIGN)
        dst = pl.multiple_of(wdst_ref[i * N_SLOT + s], ROW_ALIGN)
        return pltpu.make_async_copy(buf_ref.at[buf_slot, pl.ds(src, WIN), :],
                                     xin_ref.at[pl.ds(dst, WIN), :], sem)

    def wait_windows(step):
        def body(s, carry):
            pltpu.make_async_copy(buf_ref.at[0, pl.ds(0, WIN), :], xin_ref.at[pl.ds(0, WIN), :], sem).wait()
            return carry
        lax.fori_loop(0, nwin_ref[step], body, 0)

    @pl.when(i == 0)
    def _():
        zero_ref[...] = jnp.zeros_like(zero_ref)
        for j in range(zstart_ref.shape[0]):
            @pl.when(zstart_ref[j] >= 0)
            def _():
                start = pl.multiple_of(zstart_ref[j], ZCHUNK)
                cp = pltpu.make_async_copy(zero_ref, xin_ref.at[pl.ds(start, ZCHUNK), :], zsem)
                cp.start()
                cp.wait()

    row = lax.broadcasted_iota(jnp.int32, (n_buf_rows, tm), 0)

    perm = jnp.zeros((n_buf_rows, tm), F32)
    for k in range(TOP_K):
        perm = perm + jnp.where(row == pos_ref[0, k:k + 1, :], 1.0, 0.0)
    perm = perm.astype(BF16)
    buf_ref[slot, :, :D_MODEL // 2] = _pack_pairs(_dot(perm, hn_ref[...]))
    buf_ref[slot, :, D_MODEL // 2:] = lax.bitcast_convert_type(_dot(perm, meta_ref[...]), U32)

    @pl.when(i > 0)
    def _():
        wait_windows(i - 1)

    for buf_slot in range(2):
        @pl.when(slot == buf_slot)
        def _():
            def issue(s, carry):
                window_copy(s, buf_slot).start()
                return carry
            lax.fori_loop(0, nwin_ref[i], issue, 0)

    @pl.when(i == pl.num_programs(0) - 1)
    def _():
        wait_windows(i)


def _dispatch(zero_start, n_win, win_src, win_dst, pos, meta, hn, n_rows):
    n_tok = hn.shape[0]
    tm = TM_ROW
    n_tiles = n_tok // tm
    n_buf_rows = SORT_ROWS
    return pl.pallas_call(
        _dispatch_kernel,
        grid_spec=pltpu.PrefetchScalarGridSpec(
            num_scalar_prefetch=4,
            grid=(n_tiles,),
            in_specs=[
                pl.BlockSpec((1, TOP_K, tm), lambda i, *_: (i, 0, 0)),
                pl.BlockSpec((tm, LANES), lambda i, *_: (i, 0)),
                pl.BlockSpec((tm, D_MODEL), lambda i, *_: (i, 0)),
            ],
            out_specs=pl.BlockSpec(memory_space=pl.ANY),
            scratch_shapes=[pltpu.VMEM((2, n_buf_rows, ROW_W), U32),
                            pltpu.VMEM((ZCHUNK, ROW_W), U32),
                            pltpu.SemaphoreType.DMA(()), pltpu.SemaphoreType.DMA(())],
        ),
        out_shape=jax.ShapeDtypeStruct((n_rows, ROW_W), U32),
        compiler_params=pltpu.CompilerParams(
            dimension_semantics=("arbitrary",), vmem_limit_bytes=VMEM_LIMIT),
        name="dispatch",
    )(zero_start, n_win, win_src.reshape(-1), win_dst.reshape(-1), pos, meta, hn)


def _expert_kernel(be_ref, nused_ref, x_ref, wu_ref, bu_ref, wd_ref, bd_ref, y_ref, wu16_ref, wd16_ref):
    i = pl.program_id(0)
    used = i < nused_ref[0]

    @pl.when((i == 0) | (be_ref[i] != be_ref[jnp.maximum(i - 1, 0)]))
    def _():
        wu16_ref[...] = wu_ref[0].astype(BF16)
        wd16_ref[...] = wd_ref[0].astype(BF16)

    @pl.when(jnp.logical_not(used))
    def _():
        y_ref[...] = jnp.zeros_like(y_ref)

    @pl.when(used)
    def _():
        half = D_MODEL // 2
        x_lo, x_hi = _unpack_pairs(x_ref[:, :half])
        meta = lax.bitcast_convert_type(x_ref[:, half:half + 3 * TOP_K], F32)
        expert = be_ref[i].astype(F32)
        gate = jnp.zeros((x_ref.shape[0], 1), F32)
        for k in range(TOP_K):
            weight = meta[:, TOP_K + k:TOP_K + k + 1] + meta[:, 2 * TOP_K + k:2 * TOP_K + k + 1]
            gate = gate + jnp.where(meta[:, k:k + 1] == expert, weight, 0.0)
        hu = _dot(x_lo, wu16_ref[:half, :]) + _dot(x_hi, wu16_ref[half:, :]) + bu_ref[0]
        x_glu = jnp.minimum(hu[:, :D_FF], SWIGLU_LIMIT)
        x_lin = jnp.clip(hu[:, D_FF:], -SWIGLU_LIMIT, SWIGLU_LIMIT)
        act = x_glu * _sigmoid(SWIGLU_ALPHA * x_glu) * (x_lin + 1.0)
        y_ref[...] = _pack_pairs((_dot(act.astype(BF16), wd16_ref[...]) + bd_ref[0]) * gate)


def _experts(block_e, n_used, xin, w_up, b_up, w_down, b_down):
    n_rows = xin.shape[0]
    return pl.pallas_call(
        _expert_kernel,
        grid_spec=pltpu.PrefetchScalarGridSpec(
            num_scalar_prefetch=2,
            grid=(n_rows // TB,),
            in_specs=[
                pl.BlockSpec((TB, ROW_W), lambda i, be, nu: (i, 0)),
                pl.BlockSpec((1, D_MODEL, 2 * D_FF), lambda i, be, nu: (be[i], 0, 0)),
                pl.BlockSpec((1, 1, 2 * D_FF), lambda i, be, nu: (be[i], 0, 0)),
                pl.BlockSpec((1, D_FF, D_MODEL), lambda i, be, nu: (be[i], 0, 0)),
                pl.BlockSpec((1, 1, D_MODEL), lambda i, be, nu: (be[i], 0, 0)),
            ],
            out_specs=pl.BlockSpec((TB, D_MODEL // 2), lambda i, be, nu: (i, 0)),
            scratch_shapes=[pltpu.VMEM((D_MODEL, 2 * D_FF), BF16), pltpu.VMEM((D_FF, D_MODEL), BF16)],
        ),
        out_shape=jax.ShapeDtypeStruct((n_rows, D_MODEL // 2), U32),
        compiler_params=pltpu.CompilerParams(
            dimension_semantics=("arbitrary",), vmem_limit_bytes=VMEM_LIMIT),
        name="experts",
    )(block_e, n_used, xin, w_up, b_up.reshape(N_EXPERTS, 1, 2 * D_FF),
      w_down, b_down.reshape(N_EXPERTS, 1, D_MODEL))


def _combine_kernel(nwin_ref, wsrc_ref, wdst_ref, ric_ref, delta_ref, ricn_ref, deltan_ref, yb_ref, h1_ref,
                    lnf_ref, o_ref, stage_ref, buf_ref, pick_ref, sem):
    i = pl.program_id(0)
    tm = h1_ref.shape[0]
    n_buf_rows = buf_ref.shape[0]

    parity = i % 2

    def fetch(step, stage_slot):
        def body(s, carry):
            dst = pl.multiple_of(wdst_ref[step * N_SLOT + s], ROW_ALIGN)
            pltpu.make_async_copy(
                yb_ref.at[pl.ds(dst, WIN), :],
                stage_ref.at[stage_slot, pl.ds(pl.multiple_of(s * WIN, WIN), WIN), :],
                sem.at[stage_slot]).start()
            return carry
        lax.fori_loop(0, nwin_ref[step], body, 0)

    lane = lax.broadcasted_iota(jnp.int32, (tm, LANES), 1).astype(F32)
    col = lax.broadcasted_iota(jnp.int32, (tm, n_buf_rows), 1)

    def pick_round(pick, k, ric, delta):
        offset = jnp.sum(jnp.where(lane == ric[:, k:k + 1], delta, 0.0), axis=-1, keepdims=True)
        pos = (ric[:, TOP_K + k:TOP_K + k + 1] + offset).astype(jnp.int32)
        return pick + jnp.where(col == pos, 1.0, 0.0)

    @pl.when(i == 0)
    def _():
        buf_ref[...] = jnp.zeros_like(buf_ref)
        fetch(i, 0)
        pick = jnp.zeros((tm, n_buf_rows), F32)
        ric = ric_ref[...].astype(F32)
        for k in range(TOP_K):
            pick = pick_round(pick, k, ric, delta_ref[0])
        pick_ref[0] = pick.astype(BF16)

    for stage_slot in range(2):
        @pl.when((i + 1 < pl.num_programs(0)) & (parity != stage_slot))
        def _():
            fetch(i + 1, stage_slot)

    for stage_slot in range(2):
        @pl.when(parity == stage_slot)
        def _():
            def drain(s, carry):
                pltpu.make_async_copy(yb_ref.at[pl.ds(0, WIN), :], stage_ref.at[stage_slot, pl.ds(0, WIN), :],
                                      sem.at[stage_slot]).wait()
                return carry

            def compact(s, carry):
                src = pl.multiple_of(wsrc_ref[i * N_SLOT + s], ROW_ALIGN)
                buf_ref[pl.ds(src, WIN), :] = stage_ref[stage_slot, pl.ds(pl.multiple_of(s * WIN, WIN), WIN), :]
                return carry

            lax.fori_loop(0, nwin_ref[i], drain, 0)
            lax.fori_loop(0, nwin_ref[i], compact, 0)

    ric_next = ricn_ref[...].astype(F32)
    delta_next = deltan_ref[0]
    pick = pick_ref[parity]
    nxt = jnp.zeros((tm, n_buf_rows), F32)
    y_lo, y_hi = _unpack_pairs(buf_ref[...])
    nxt = pick_round(nxt, 0, ric_next, delta_next)
    left = _dot(pick, y_lo)
    nxt = pick_round(nxt, 1, ric_next, delta_next)
    right = _dot(pick, y_hi)
    nxt = pick_round(nxt, 2, ric_next, delta_next)
    h2 = h1_ref[...] + jnp.concatenate([left, right], axis=-1)
    nxt = pick_round(nxt, 3, ric_next, delta_next)
    o_ref[...] = h2 * _rms_scale(h2) * lnf_ref[...]
    pick_ref[1 - parity] = nxt.astype(BF16)


def _combine(n_win, win_src, win_dst, ric, delta, yb, h1, lnf_w):
    n_tok = h1.shape[0]
    tm = TM_ROW
    n_tiles = n_tok // tm
    return pl.pallas_call(
        _combine_kernel,
        grid_spec=pltpu.PrefetchScalarGridSpec(
            num_scalar_prefetch=3,
            grid=(n_tiles,),
            in_specs=[
                pl.BlockSpec((tm, LANES), lambda i, *_: (i, 0)),
                pl.BlockSpec((1, 1, LANES), lambda i, *_: (i, 0, 0)),
                pl.BlockSpec((tm, LANES), lambda i, *_: (jnp.minimum(i + 1, n_tiles - 1), 0)),
                pl.BlockSpec((1, 1, LANES), lambda i, *_: (jnp.minimum(i + 1, n_tiles - 1), 0, 0)),
                pl.BlockSpec(memory_space=pl.ANY),
                pl.BlockSpec((tm, D_MODEL), lambda i, *_: (i, 0)),
                pl.BlockSpec((1, D_MODEL), lambda i, *_: (0, 0)),
            ],
            out_specs=pl.BlockSpec((tm, D_MODEL), lambda i, *_: (i, 0)),
            scratch_shapes=[pltpu.VMEM((2, N_SLOT * WIN, D_MODEL // 2), U32),
                            pltpu.VMEM((SORT_ROWS, D_MODEL // 2), U32),
                            pltpu.VMEM((2, tm, SORT_ROWS), BF16), pltpu.SemaphoreType.DMA((2,))],
        ),
        out_shape=jax.ShapeDtypeStruct((n_tok, D_MODEL), F32),
        compiler_params=pltpu.CompilerParams(
            dimension_semantics=("arbitrary",), vmem_limit_bytes=VMEM_LIMIT),
        name="combine",
    )(n_win, win_src.reshape(-1), win_dst.reshape(-1), ric, delta, ric, delta, yb, h1,
      lnf_w.reshape(1, D_MODEL))


def _layer(h, ln1_w, w_in, w_gk2, b_gk2, gla_norm_w, w_proj_a, w_proj_b, w_out,
           ln2_w, w_router, b_router, w_up, b_up, w_down, b_down, lnf_w):
    bsz, seq, _ = h.shape
    n_tok = bsz * seq

    n_main = 3 * QKV_W + 2 * GLA_KEY_DIM + 2 * GLA_VALUE_DIM
    pad = LANES - GLA_GATE_RANK
    w_main = w_in[:, :n_main].astype(BF16)
    w_tail = jnp.concatenate(
        [w_in[:, n_main:n_main + GLA_GATE_RANK], jnp.zeros((D_MODEL, pad), F32),
         w_in[:, n_main + GLA_GATE_RANK:]], axis=1).astype(BF16)
    w2_pad = jnp.concatenate([w_gk2, jnp.zeros((pad, GLA_KEY_DIM), F32)], axis=0).astype(BF16)
    wr_pad = jnp.concatenate([w_router, jnp.zeros((D_MODEL, LANES - N_EXPERTS), F32)], axis=1)
    wr_hi = wr_pad.astype(BF16)
    wr_lo = (wr_pad - wr_hi.astype(F32)).astype(BF16)
    br_pad = jnp.concatenate([b_router, jnp.zeros((LANES - N_EXPERTS,), F32)]).reshape(1, LANES)

    pa0, pa1, pa2, pg, pm = _inproj(h, ln1_w, w_main, w_tail)
    o0, l0 = _dil_attn(pa0)
    o1, l1 = _dil_attn(pa1)
    o2, l2 = _dil_attn(pa2)
    o_b = _gla(pg, w2_pad, b_gk2, gla_norm_w)

    h1, hn, ri, ric, meta, carry_f, cnt = _mix(
        h.reshape(n_tok, D_MODEL), o0.reshape(n_tok, DA_WIDTH), l0.reshape(n_tok, LANES),
        o1, l1, o2, l2, o_b.reshape(n_tok, GLA_VALUE_DIM), pm.reshape(n_tok, MERGE_W),
        w_proj_a.astype(BF16), w_proj_b.astype(BF16), w_out.astype(BF16), ln2_w, wr_hi, wr_lo, br_pad)

    i32 = jnp.int32
    n_tiles = n_tok // TM_ROW
    experts = jnp.arange(N_EXPERTS)
    counts = cnt[0, :N_EXPERTS].astype(i32)
    before = carry_f[:, 0, :N_EXPERTS].astype(i32)
    run = jnp.concatenate([before[1:], counts[None]], axis=0) - before
    run_al = (run + ROW_ALIGN - 1) // ROW_ALIGN * ROW_ALIGN
    rows_end = jnp.cumsum(run_al, axis=0)
    rows_before = rows_end - run_al
    used = rows_end[-1]
    slack = WIN - ROW_ALIGN
    padded = (used + slack + TB - 1) // TB * TB
    pad_end = jnp.cumsum(padded)
    pad_start = pad_end - padded
    n_asg = n_tok * TOP_K
    n_rows = (n_asg + n_tiles * N_EXPERTS * (ROW_ALIGN - 1)
              + N_EXPERTS * (slack + TB - 1) + TB - 1) // TB * TB
    n_blocks = n_rows // TB
    block_starts = jnp.arange(n_blocks) * TB
    block_e = jnp.minimum(jnp.sum(pad_end[None, :] <= block_starts[:, None], axis=1),
                          N_EXPERTS - 1).astype(i32)
    n_used = (pad_end[-1:] // TB).astype(i32)

    run_end = jnp.cumsum(run_al, axis=1)
    run_start = run_end - run_al
    wins = (run + WIN - 1) // WIN
    wins_end = jnp.cumsum(wins, axis=1)
    wins_start = wins_end - wins
    n_win = wins_end[:, -1].astype(i32)
    slots = jnp.arange(N_SLOT)
    slot_e = jnp.sum(wins_end[:, None, :] <= slots[None, :, None], axis=-1)
    slot_is = slot_e[..., None] == experts

    def of_slot(table):
        return jnp.sum(jnp.where(slot_is, table[:, None, :], 0), axis=-1)

    win_off = (slots[None, :] - of_slot(wins_start)) * WIN
    win_src = (of_slot(run_start) + win_off).astype(i32)
    win_dst = (of_slot(pad_start[None, :] + rows_before) + win_off).astype(i32)
    asg_is = ri[:, :TOP_K, :, None] == experts
    pos = ri[:, TOP_K:, :] + jnp.sum(jnp.where(asg_is, (run_start - before)[:, None, None, :], 0), axis=-1)
    delta = jnp.pad((run_start - before).astype(F32), ((0, 0), (0, LANES - N_EXPERTS)))

    chunk_back = ZCHUNK * (1 + jnp.arange((slack + TB - 1 + ZCHUNK - 1) // ZCHUNK + 1))
    region_chunks = pad_end[:, None] - chunk_back[None, :]
    region_ok = (region_chunks >= pad_start[:, None]) & (region_chunks + ZCHUNK > (pad_start + used)[:, None])
    tail_chunks = pad_end[-1] + ZCHUNK * jnp.arange((n_rows - n_asg) // ZCHUNK)
    zero_start = jnp.concatenate([jnp.where(region_ok, region_chunks, -1).reshape(-1),
                                  jnp.where(tail_chunks < n_rows, tail_chunks, -1)]).astype(i32)

    xin = _dispatch(zero_start, n_win, win_src, win_dst, pos.astype(i32), meta, hn, n_rows)
    yb = _experts(block_e, n_used, xin, w_up, b_up, w_down, b_down)
    out = _combine(n_win, win_src, win_dst, ric, delta.reshape(-1, 1, LANES), yb, h1, lnf_w)
    return out.reshape(bsz, seq, D_MODEL)


def kernel(x, ln1_w, w_in, w_gk2, b_gk2, gla_norm_w, w_proj_a, w_proj_b, w_out, ln2_w, w_router,
           b_router, w_up, b_up, w_down, b_down, lnf_w):
    assert x.shape[-1] == D_MODEL and ln1_w.shape[0] == 1, "one layer of width D_MODEL"
    return _layer(x, ln1_w[0], w_in[0], w_gk2[0], b_gk2[0], gla_norm_w[0], w_proj_a[0], w_proj_b[0],
                  w_out[0], ln2_w[0], w_router[0], b_router[0], w_up[0], b_up[0], w_down[0],
                  b_down[0], lnf_w)
```

```python
import jax
import jax.numpy as jnp
from jax import lax
from jax.experimental import pallas as pl
from jax.experimental.pallas import tpu as pltpu

F32 = jnp.float32
BF16 = jnp.bfloat16
U32 = jnp.uint32

D_MODEL = 1024
DA_GROUPS = ((128, 1), (512, 4), (2048, 16))
DA_HEADS = 4
DA_HEAD_DIM = 128
DA_WIDTH = DA_HEADS * DA_HEAD_DIM
DA_BLOCK = 128
GLA_HEADS = 4
GLA_KEY_DIM = D_MODEL // 2
GLA_VALUE_DIM = D_MODEL
GLA_DK = GLA_KEY_DIM // GLA_HEADS
GLA_DV = GLA_VALUE_DIM // GLA_HEADS
GLA_GATE_RANK = 16
GLA_GATE_NORMALIZER = 16.0
GLA_CHUNK = 64
N_EXPERTS = 32
TOP_K = 4
D_FF = D_MODEL
SWIGLU_ALPHA = 1.702
SWIGLU_LIMIT = 7.0
RMS_EPS = 1e-5
NEG_INF = -1e30

LANES = 128
QKV_W = 3 * DA_WIDTH
GLA_W = 2 * GLA_KEY_DIM + 2 * GLA_VALUE_DIM + LANES
MERGE_W = 2 * D_MODEL

DA_QB = 4
LSE_LANES = LANES // DA_HEADS
TM_IN = 512
N_CHUNK = 512
T_GLA = 512
GLA_SUB = 256
GLA_HEADS_PER_STEP = 4
TM_MIX = 512
TB = 512
TB_STEP = 128
TM_ROW = 256
ROW_ALIGN = 8
WIN = 32
N_SLOT = N_EXPERTS + TM_ROW * TOP_K // WIN
SORT_ROWS = TM_ROW * TOP_K + N_EXPERTS * (ROW_ALIGN - 1) + WIN
ZCHUNK = 256
ROW_W = D_MODEL // 2 + LANES
VMEM_LIMIT = 56 * 1024 * 1024
VMEM_LIMIT_INPROJ = 62 * 1024 * 1024

_NT = (((1,), (1,)), ((), ()))
_TN = (((0,), (0,)), ((), ()))


def _dot(a, b):
    return jnp.dot(a, b, preferred_element_type=F32)


def _sigmoid(x):
    return 1.0 / (1.0 + jnp.exp(-x))


def _rms_scale(x):
    return lax.rsqrt(jnp.mean(x * x, axis=-1, keepdims=True) + RMS_EPS)


def _inproj_kernel(x_ref, ln_ref, wmain_hbm, wtail_hbm, pa0_ref, pa1_ref, pa2_ref, pg_ref, pm_ref,
                   w_ref, xs_ref, xn_ref, wsem):
    tm = x_ref.shape[1]
    n_slab = D_MODEL // LANES

    @pl.when((pl.program_id(0) == 0) & (pl.program_id(1) == 0))
    def _():
        n_main = wmain_hbm.shape[1]
        main = pltpu.make_async_copy(wmain_hbm, w_ref.at[:, :n_main], wsem.at[0])
        tail = pltpu.make_async_copy(wtail_hbm, w_ref.at[:, n_main:], wsem.at[1])
        main.start()
        tail.start()
        main.wait()
        tail.wait()

    def project(out_write, col0, width, post=None):
        for c0 in range(0, width, N_CHUNK):
            cw = min(N_CHUNK, width - c0)
            val = _dot(xs_ref[...], w_ref[:, col0 + c0:col0 + c0 + cw])
            out_write(c0, cw, (val if post is None else post(val)).astype(BF16))

    x = x_ref[0]
    xn = x * _rms_scale(x) * ln_ref[...]
    xs_ref[...] = xn.astype(BF16)
    for j in range(n_slab):
        xn_ref[j] = xn[:, j * LANES:(j + 1) * LANES]

    def write_to(ref):
        def write(c0, cw, val):
            ref[0, :, c0:c0 + cw] = val
        return write

    def write_qkv(out_ref, d, n):
        def write(c0, cw, val):
            for r in range(d):
                out_ref[0, r, c0 // DA_WIDTH] = val[r * n:(r + 1) * n]
        return write

    project(write_qkv(pa0_ref, 1, tm), 0, QKV_W)
    project(write_to(pg_ref), 3 * QKV_W, GLA_W)
    project(write_to(pm_ref), 3 * QKV_W + GLA_W, MERGE_W, post=_sigmoid)

    for gi, out_ref in ((1, pa1_ref), (2, pa2_ref)):
        d = DA_GROUPS[gi][1]
        n = tm // d
        for r in range(d):
            for j in range(n_slab):
                xs_ref[r * n:(r + 1) * n, j * LANES:(j + 1) * LANES] = (
                    xn_ref[j, pl.ds(r, n, stride=d), :].astype(BF16))
        project(write_qkv(out_ref, d, n), gi * QKV_W, QKV_W)


def _inproj(x, ln1_w, w_main, w_tail):
    assert N_CHUNK == DA_WIDTH and w_main.shape[1] % LANES == 0
    bsz, seq, _ = x.shape
    tm = TM_IN
    d1, d2 = DA_GROUPS[1][1], DA_GROUPS[2][1]

    def qkv_shape(d):
        return jax.ShapeDtypeStruct((bsz, d, 3, seq // d, DA_WIDTH), BF16)

    def qkv_spec(d):
        return pl.BlockSpec((1, d, 3, tm // d, DA_WIDTH), lambda b, i: (b, 0, 0, i, 0))

    return pl.pallas_call(
        _inproj_kernel,
        grid=(bsz, seq // tm),
        in_specs=[
            pl.BlockSpec((1, tm, D_MODEL), lambda b, i: (b, i, 0)),
            pl.BlockSpec((1, D_MODEL), lambda b, i: (0, 0)),
            pl.BlockSpec(memory_space=pl.ANY),
            pl.BlockSpec(memory_space=pl.ANY),
        ],
        out_specs=(
            qkv_spec(1), qkv_spec(d1), qkv_spec(d2),
            pl.BlockSpec((1, tm, GLA_W), lambda b, i: (b, i, 0)),
            pl.BlockSpec((1, tm, MERGE_W), lambda b, i: (b, i, 0)),
        ),
        out_shape=(qkv_shape(1), qkv_shape(d1), qkv_shape(d2),
                   jax.ShapeDtypeStruct((bsz, seq, GLA_W), BF16),
                   jax.ShapeDtypeStruct((bsz, seq, MERGE_W), BF16)),
        scratch_shapes=[pltpu.VMEM((D_MODEL, w_main.shape[1] + w_tail.shape[1]), BF16),
                        pltpu.VMEM((tm, D_MODEL), BF16),
                        pltpu.VMEM((D_MODEL // LANES, tm, LANES), F32),
                        pltpu.SemaphoreType.DMA((2,))],
        compiler_params=pltpu.CompilerParams(
            dimension_semantics=("arbitrary", "arbitrary"), vmem_limit_bytes=VMEM_LIMIT_INPROJ),
        name="inproj",
    )(x, ln1_w.reshape(1, D_MODEL), w_main, w_tail)


def _dil_attn_kernel(q_ref, kp_ref, kc_ref, vp_ref, vc_ref, o_ref, l_ref, s_ref, p_ref, r_ref):
    n = pl.program_id(2)
    blk = DA_BLOCK
    qi = lax.broadcasted_iota(jnp.int32, (blk, 2 * blk), 0)
    kj = lax.broadcasted_iota(jnp.int32, (blk, 2 * blk), 1)
    band = (kj >= qi) & (kj <= qi + blk)
    band_first = (kj >= jnp.where(n > 0, qi, blk)) & (kj <= qi + blk)
    scale = DA_HEAD_DIM ** -0.5
    items = [(b, h) for b in range(DA_QB) for h in range(DA_HEADS)]

    def rows(b):
        return slice(b * blk, (b + 1) * blk)

    def cols(h):
        return slice(h * DA_HEAD_DIM, (h + 1) * DA_HEAD_DIM)

    def window(prev_ref, cur_ref, b, h):
        before = prev_ref[0, 0, 0, :, cols(h)] if b == 0 else cur_ref[0, 0, 0, rows(b - 1), cols(h)]
        return jnp.concatenate([before, cur_ref[0, 0, 0, rows(b), cols(h)]], axis=0)

    def scores(i, b, h):
        s = lax.dot_general(q_ref[0, 0, 0, rows(b), cols(h)], window(kp_ref, kc_ref, b, h), _NT,
                            preferred_element_type=F32) * scale
        s_ref[i] = jnp.where(band_first if b == 0 else band, s, NEG_INF)

    def softmax(i, b, h):
        s = s_ref[i]
        m = jnp.max(s, axis=-1, keepdims=True)
        p = jnp.exp(s - m)
        l = jnp.sum(p, axis=-1, keepdims=True)
        p_ref[i] = p.astype(BF16)
        r_ref[i] = jnp.broadcast_to(1.0 / l, (blk, DA_HEAD_DIM))
        l_ref[0, 0, rows(b), h * LSE_LANES:(h + 1) * LSE_LANES] = jnp.broadcast_to(
            m + jnp.log(l), (blk, LSE_LANES))

    def values(i, b, h):
        acc = _dot(p_ref[i], window(vp_ref, vc_ref, b, h))
        o_ref[0, 0, rows(b), cols(h)] = (acc * r_ref[i]).astype(o_ref.dtype)

    stages = (scores, softmax, values)
    for t in range(len(items) + len(stages) - 1):
        for lag, stage in enumerate(stages):
            if 0 <= t - lag < len(items):
                stage(t - lag, *items[t - lag])


def _dil_attn(pa):
    bsz, d, _, sub_len, _ = pa.shape
    rows = DA_QB * DA_BLOCK
    n_items = DA_QB * DA_HEADS

    def cur(sec):
        return pl.BlockSpec((1, 1, 1, rows, DA_WIDTH), lambda b, r, n: (b, r, sec, n, 0))

    def prev(sec):
        return pl.BlockSpec((1, 1, 1, DA_BLOCK, DA_WIDTH),
                            lambda b, r, n: (b, r, sec, jnp.maximum(n * DA_QB - 1, 0), 0))

    def out(width):
        return pl.BlockSpec((1, 1, rows, width), lambda b, r, n: (b, r, n, 0))

    return pl.pallas_call(
        _dil_attn_kernel,
        grid=(bsz, d, sub_len // rows),
        in_specs=[cur(0), prev(1), cur(1), prev(2), cur(2)],
        out_specs=(out(DA_WIDTH), out(LANES)),
        out_shape=(jax.ShapeDtypeStruct((bsz, d, sub_len, DA_WIDTH), BF16),
                   jax.ShapeDtypeStruct((bsz, d, sub_len, LANES), F32)),
        scratch_shapes=[pltpu.VMEM((n_items, DA_BLOCK, 2 * DA_BLOCK), F32),
                        pltpu.VMEM((n_items, DA_BLOCK, 2 * DA_BLOCK), BF16),
                        pltpu.VMEM((n_items, DA_BLOCK, DA_HEAD_DIM), F32)],
        compiler_params=pltpu.CompilerParams(
            dimension_semantics=("parallel", "parallel", "arbitrary"), vmem_limit_bytes=VMEM_LIMIT),
        name=f"dil_attn_d{d}",
    )(pa, pa, pa, pa, pa)


def _gla_kernel(q_ref, k_ref, v_ref, og_ref, lr_ref, w2_ref, b2_ref, nw_ref, o_ref, st_ref, mask_ref, keep_ref):
    t = pl.program_id(2)
    c = GLA_CHUNK
    tt = q_ref.shape[1]
    n_c = tt // c

    sub = mask_ref.shape[0]

    @pl.when(t == 0)
    def _():
        st_ref[...] = jnp.zeros_like(st_ref)
        row = lax.broadcasted_iota(jnp.int32, (sub, sub), 0)
        col = lax.broadcasted_iota(jnp.int32, (sub, sub), 1)
        keep = (col <= row) & (col >= row - row % c)
        keep_ref[...] = jnp.where(keep, 1.0, 0.0)
        mask_ref[...] = jnp.where(keep, 1.0, 0.0).astype(BF16)

    heads = range(q_ref.shape[2] // GLA_DK)
    mask = mask_ref[...]

    def kcols(h):
        return slice(h * GLA_DK, (h + 1) * GLA_DK)

    def vcols(h):
        return slice(h * GLA_DV, (h + 1) * GLA_DV)

    gpre = _dot(lr_ref[0], w2_ref[...]) + b2_ref[...]
    forget = (jnp.minimum(gpre, 0.0) - jnp.log(1.0 + jnp.exp(-jnp.abs(gpre)))) / GLA_GATE_NORMALIZER
    g_hi = forget.astype(BF16)
    g_lo = (forget - g_hi.astype(F32)).astype(BF16)
    b, b_last, q_e, k_e, k_end = [], [], [], [], []
    for h in heads:
        g_cat = jnp.concatenate([g_hi[:, kcols(h)], g_lo[:, kcols(h)]], axis=-1)
        csum = jnp.concatenate([_dot(mask, g_cat[s0:s0 + sub]) for s0 in range(0, tt, sub)], axis=0)
        b.append(csum[:, :GLA_DK] + csum[:, GLA_DK:])
    for h in heads:
        b_last.append(b[h].reshape(n_c, c, GLA_DK)[:, c - 1:c, :])
        b_to_end = (b_last[h] - b[h].reshape(n_c, c, GLA_DK)).reshape(tt, GLA_DK)
        q = q_ref[0, :, kcols(h)].astype(F32)
        k = k_ref[0, :, kcols(h)].astype(F32)
        q_e.append((q * ((GLA_DK ** -0.5) * jnp.exp(b[h]))).astype(BF16))
        k_e.append((k * jnp.exp(-b[h])).astype(BF16))
        k_end.append((k * jnp.exp(b_to_end)).astype(BF16))
    o_intra = []
    for h in heads:
        parts = []
        for s0 in range(0, tt, sub):
            ss = slice(s0, s0 + sub)
            att = lax.dot_general(q_e[h][ss], k_e[h][ss], _NT, preferred_element_type=F32)
            att = jnp.where(keep_ref[...] > 0.0, att, 0.0).astype(BF16)
            parts.append(_dot(att, v_ref[0, ss, vcols(h)]))
        o_intra.append(jnp.concatenate(parts, axis=0))
    decay = [jnp.exp(b_last[h].reshape(n_c, GLA_DK).T) for h in heads]
    st = [st_ref[h] for h in heads]
    outs = [[] for _ in heads]
    for ci in range(n_c):
        rs = slice(ci * c, (ci + 1) * c)
        for h in heads:
            outs[h].append(o_intra[h][rs] + _dot(q_e[h][rs], st[h].astype(BF16)))
            st[h] = decay[h][:, ci:ci + 1] * st[h] + lax.dot_general(
                k_end[h][rs], v_ref[0, rs, vcols(h)], _TN, preferred_element_type=F32)
    for h in heads:
        st_ref[h] = st[h]
        o = jnp.concatenate(outs[h], axis=0)
        o = o * _rms_scale(o) * nw_ref[...]
        gate = og_ref[0, :, vcols(h)].astype(F32)
        o_ref[0, :, vcols(h)] = (o * (gate * _sigmoid(gate))).astype(BF16)


def _gla(pg, w2_pad, b_gk2, gla_norm_w):
    bsz, seq, _ = pg.shape
    t = T_GLA
    hps = GLA_HEADS_PER_STEP
    wk, wv = hps * GLA_DK, hps * GLA_DV
    kq = GLA_KEY_DIM // wk
    kv = 2 * GLA_KEY_DIM // wv
    kg = kv + GLA_VALUE_DIM // wv
    klr = (2 * GLA_KEY_DIM + 2 * GLA_VALUE_DIM) // LANES
    return pl.pallas_call(
        _gla_kernel,
        grid=(bsz, GLA_HEADS // hps, seq // t),
        in_specs=[
            pl.BlockSpec((1, t, wk), lambda b, h, i: (b, i, h)),
            pl.BlockSpec((1, t, wk), lambda b, h, i: (b, i, kq + h)),
            pl.BlockSpec((1, t, wv), lambda b, h, i: (b, i, kv + h)),
            pl.BlockSpec((1, t, wv), lambda b, h, i: (b, i, kg + h)),
            pl.BlockSpec((1, t, LANES), lambda b, h, i: (b, i, klr)),
            pl.BlockSpec((LANES, wk), lambda b, h, i: (0, h)),
            pl.BlockSpec((1, wk), lambda b, h, i: (0, h)),
            pl.BlockSpec((1, GLA_DV), lambda b, h, i: (0, 0)),
        ],
        out_specs=pl.BlockSpec((1, t, wv), lambda b, h, i: (b, i, h)),
        out_shape=jax.ShapeDtypeStruct((bsz, seq, GLA_VALUE_DIM), BF16),
        scratch_shapes=[pltpu.VMEM((hps, GLA_DK, GLA_DV), F32), pltpu.VMEM((GLA_SUB, GLA_SUB), BF16),
                        pltpu.VMEM((GLA_SUB, GLA_SUB), F32)],
        compiler_params=pltpu.CompilerParams(
            dimension_semantics=("parallel", "parallel", "arbitrary"), vmem_limit_bytes=VMEM_LIMIT),
        name="gla",
    )(pg, pg, pg, pg, pg, w2_pad, b_gk2.reshape(1, GLA_KEY_DIM), gla_norm_w.reshape(1, GLA_DV))


def _mix_kernel(x_ref, o0_ref, l0_ref, o1_ref, l1_ref, o2_ref, l2_ref, ob_ref, pm_ref,
                wa_ref, wb_ref, wo_ref, ln2_ref, wrh_ref, wrl_ref, br_ref,
                h1_ref, hn_ref, ri_ref, ric_ref, meta_ref, cbefore_ref, cnt_ref,
                po1_ref, pl1_ref, po2_ref, pl2_ref, carry_ref, logit_ref):
    step = pl.program_id(0)
    tm = x_ref.shape[0]

    @pl.when(step == 0)
    def _():
        carry_ref[...] = jnp.zeros_like(carry_ref)
        logit_ref[...] = jnp.zeros_like(logit_ref)

    routed = step > 0
    lane = lax.broadcasted_iota(jnp.int32, (tm, LANES), 1).astype(F32)
    route = {"work": None, "vals": [], "idxs": []}

    def topk_round():
        work = route["work"]
        m = jnp.max(work, axis=-1, keepdims=True)
        idx = jnp.min(jnp.where(work == m, lane, float(LANES)), axis=-1, keepdims=True)
        route["vals"].append(m)
        route["idxs"].append(idx)
        route["work"] = jnp.where(lane == idx, -jnp.inf, work)

    route["work"] = jnp.where(lane < N_EXPERTS, logit_ref[(step + 1) % 2], -jnp.inf)

    for o_ref, l_ref, po_ref, pl_ref, (_, d) in ((o1_ref, l1_ref, po1_ref, pl1_ref, DA_GROUPS[1]),
                                                 (o2_ref, l2_ref, po2_ref, pl2_ref, DA_GROUPS[2])):
        topk_round()
        n = tm // d
        for r in range(d):
            pl_ref[pl.ds(r, n, stride=d), :] = l_ref[0, r]
            for h in range(DA_HEADS):
                sl = slice(h * DA_HEAD_DIM, (h + 1) * DA_HEAD_DIM)
                po_ref[h, pl.ds(r, n, stride=d), :] = o_ref[0, r, :, sl].astype(F32)

    l0, l1, l2 = l0_ref[...], pl1_ref[...], pl2_ref[...]
    mx = jnp.maximum(jnp.maximum(l0, l1), l2)
    e0, e1, e2 = jnp.exp(l0 - mx), jnp.exp(l1 - mx), jnp.exp(l2 - mx)
    inv = 1.0 / (e0 + e1 + e2)
    w0, w1, w2 = e0 * inv, e1 * inv, e2 * inv
    topk_round()
    heads = []
    for h in range(DA_HEADS):
        sl = slice(h * DA_HEAD_DIM, (h + 1) * DA_HEAD_DIM)
        at = slice(h * LSE_LANES, h * LSE_LANES + 1)
        o_h = w0[:, at] * o0_ref[:, sl].astype(F32) + w1[:, at] * po1_ref[h] + w2[:, at] * po2_ref[h]
        heads.append(o_h.astype(BF16))
    o_a = jnp.concatenate(heads, axis=-1)
    topk_round()

    gates = pm_ref[...].astype(F32)
    mixed = (gates[:, :D_MODEL] * _dot(o_a, wa_ref[...])
             + gates[:, D_MODEL:] * _dot(ob_ref[...], wb_ref[...]))
    vals, idxs = route["vals"], route["idxs"]
    assert len(vals) == TOP_K
    exps = [jnp.exp(v - vals[0]) for v in vals]
    denom = exps[0] + exps[1] + exps[2] + exps[3]
    onehot = jnp.zeros((tm, LANES), F32)
    for idx in idxs:
        onehot = onehot + jnp.where(lane == idx, 1.0, 0.0)

    h1 = x_ref[...] + _dot(mixed.astype(BF16), wo_ref[...])
    h1_ref[...] = h1
    hn = h1 * _rms_scale(h1) * ln2_ref[...]

    row = lax.broadcasted_iota(jnp.int32, (tm, tm), 0)
    col = lax.broadcasted_iota(jnp.int32, (tm, tm), 1)
    below = jnp.where(col < row, 1.0, 0.0).astype(BF16)
    before = _dot(below, onehot.astype(BF16)) + carry_ref[0:1, :]
    ranks = [jnp.sum(jnp.where(lane == idx, before, 0.0), axis=-1, keepdims=True) for idx in idxs]
    running = carry_ref[0:1, :]
    for j in range(tm // TM_ROW):
        cbefore_ref[j] = jnp.broadcast_to(running, cbefore_ref.shape[1:])
        running = running + jnp.sum(onehot[j * TM_ROW:(j + 1) * TM_ROW], axis=0, keepdims=True)
    carry = jnp.where(routed, running, carry_ref[0:1, :])
    carry_ref[...] = jnp.broadcast_to(carry, carry_ref.shape)
    cnt_ref[...] = jnp.broadcast_to(carry, cnt_ref.shape)

    hn_hi = hn.astype(BF16)
    hn_ref[...] = hn_hi

    ri = jnp.zeros((tm, LANES), F32)
    for j, val in enumerate(idxs + ranks):
        ri = jnp.where(lane == float(j), val, ri)
    ric_ref[...] = ri.astype(jnp.int32)
    ri_t = ri.T[:2 * TOP_K].astype(jnp.int32)
    for j in range(tm // TM_ROW):
        ri_ref[j] = ri_t[:, j * TM_ROW:(j + 1) * TM_ROW]
    meta = jnp.zeros((tm, LANES), F32)
    for k in range(TOP_K):
        gate = exps[k] / denom
        gate_hi = gate.astype(BF16).astype(F32)
        meta = jnp.where(lane == float(k), idxs[k], meta)
        meta = jnp.where(lane == float(TOP_K + k), gate_hi, meta)
        meta = jnp.where(lane == float(2 * TOP_K + k), gate - gate_hi, meta)
    meta_ref[...] = meta.astype(BF16)

    hn_lo = (hn - hn_hi.astype(F32)).astype(BF16)
    logit_ref[step % 2] = (_dot(hn_hi, wrh_ref[...]) + _dot(hn_lo, wrh_ref[...])
                           + _dot(hn_hi, wrl_ref[...]) + br_ref[...])


def _mix(x2, o0, l0, o1, l1, o2, l2, o_b, pm, wa, wb, wo, ln2_w, wr_hi, wr_lo, br_pad):
    n_tok = x2.shape[0]
    tm = TM_MIX
    bsz = o1.shape[0]
    d1, d2 = DA_GROUPS[1][1], DA_GROUPS[2][1]
    tiles_per_seq = (n_tok // bsz) // tm
    n_tiles = n_tok // tm

    def tile(i):
        return jnp.minimum(i, n_tiles - 1)

    def routed(i):
        return jnp.maximum(i - 1, 0)

    def rows(width, which=tile):
        return pl.BlockSpec((tm, width), lambda i: (which(i), 0))

    def residue_major(d, width):
        return pl.BlockSpec((1, d, tm // d, width),
                            lambda i: (tile(i) // tiles_per_seq, 0, tile(i) % tiles_per_seq, 0))

    def whole(arr):
        return pl.BlockSpec(arr.shape, lambda i: (0,) * arr.ndim)

    ln2 = ln2_w.reshape(1, D_MODEL)
    return pl.pallas_call(
        _mix_kernel,
        grid=(n_tiles + 1,),
        in_specs=[rows(D_MODEL), rows(DA_WIDTH), rows(LANES),
                  residue_major(d1, DA_WIDTH), residue_major(d1, LANES),
                  residue_major(d2, DA_WIDTH), residue_major(d2, LANES),
                  rows(GLA_VALUE_DIM), rows(MERGE_W),
                  whole(wa), whole(wb), whole(wo), whole(ln2), whole(wr_hi), whole(wr_lo), whole(br_pad)],
        out_specs=(rows(D_MODEL), rows(D_MODEL),
                   pl.BlockSpec((tm // TM_ROW, 2 * TOP_K, TM_ROW), lambda i: (routed(i), 0, 0)),
                   rows(LANES, routed), rows(LANES, routed),
                   pl.BlockSpec((tm // TM_ROW, 8, LANES), lambda i: (routed(i), 0, 0)),
                   pl.BlockSpec((8, LANES), lambda i: (0, 0))),
        out_shape=(jax.ShapeDtypeStruct((n_tok, D_MODEL), F32),
                   jax.ShapeDtypeStruct((n_tok, D_MODEL), BF16),
                   jax.ShapeDtypeStruct((n_tok // TM_ROW, 2 * TOP_K, TM_ROW), jnp.int32),
                   jax.ShapeDtypeStruct((n_tok, LANES), jnp.int32),
                   jax.ShapeDtypeStruct((n_tok, LANES), BF16),
                   jax.ShapeDtypeStruct((n_tok // TM_ROW, 8, LANES), F32),
                   jax.ShapeDtypeStruct((8, LANES), F32)),
        scratch_shapes=[pltpu.VMEM((DA_HEADS, tm, DA_HEAD_DIM), F32), pltpu.VMEM((tm, LANES), F32),
                        pltpu.VMEM((DA_HEADS, tm, DA_HEAD_DIM), F32), pltpu.VMEM((tm, LANES), F32),
                        pltpu.VMEM((8, LANES), F32), pltpu.VMEM((2, tm, LANES), F32)],
        compiler_params=pltpu.CompilerParams(
            dimension_semantics=("arbitrary",), vmem_limit_bytes=VMEM_LIMIT),
        name="mix",
    )(x2, o0, l0, o1, l1, o2, l2, o_b, pm, wa, wb, wo, ln2, wr_hi, wr_lo, br_pad)


def _pack_pairs(x):
    n = x.shape[1] // 2
    rounded = x.astype(BF16).astype(F32)
    lo = lax.bitcast_convert_type(rounded[:, :n], U32) >> 16
    hi = lax.bitcast_convert_type(rounded[:, n:], U32) & jnp.uint32(0xFFFF0000)
    return hi | lo


def _unpack_pairs(u):
    lo = lax.bitcast_convert_type(u << 16, F32).astype(BF16)
    hi = lax.bitcast_convert_type(u & jnp.uint32(0xFFFF0000), F32).astype(BF16)
    return lo, hi


def _dispatch_kernel(zstart_ref, nwin_ref, wsrc_ref, wdst_ref, pos_ref, meta_ref, hn_ref, xin_ref,
                     buf_ref, zero_ref, sem, zsem):
    i = pl.program_id(0)
    tm = hn_ref.shape[0]
    n_buf_rows = buf_ref.shape[1]
    slot = i % 2

    def window_copy(s, buf_slot):
        src = pl.multiple_of(wsrc_ref[i * N_SLOT + s], ROW_ALIGN)
        dst = pl.multiple_of(wdst_ref[i * N_SLOT + s], ROW_ALIGN)
        return pltpu.make_async_copy(buf_ref.at[buf_slot, pl.ds(src, WIN), :],
                                     xin_ref.at[pl.ds(dst, WIN), :], sem)

    def wait_windows(step):
        def body(s, carry):
            pltpu.make_async_copy(buf_ref.at[0, pl.ds(0, WIN), :], xin_ref.at[pl.ds(0, WIN), :], sem).wait()
            return carry
        lax.fori_loop(0, nwin_ref[step], body, 0)

    @pl.when(i == 0)
    def _():
        zero_ref[...] = jnp.zeros_like(zero_ref)
        for j in range(zstart_ref.shape[0]):
            @pl.when(zstart_ref[j] >= 0)
            def _():
                start = pl.multiple_of(zstart_ref[j], ZCHUNK)
                cp = pltpu.make_async_copy(zero_ref, xin_ref.at[pl.ds(start, ZCHUNK), :], zsem)
                cp.start()
                cp.wait()

    row = lax.broadcasted_iota(jnp.int32, (n_buf_rows, tm), 0)

    perm = jnp.zeros((n_buf_rows, tm), F32)
    for k in range(TOP_K):
        perm = perm + jnp.where(row == pos_ref[0, k:k + 1, :], 1.0, 0.0)
    perm = perm.astype(BF16)
    buf_ref[slot, :, :D_MODEL // 2] = _pack_pairs(_dot(perm, hn_ref[...]))
    buf_ref[slot, :, D_MODEL // 2:] = lax.bitcast_convert_type(_dot(perm, meta_ref[...]), U32)

    @pl.when(i > 0)
    def _():
        wait_windows(i - 1)

    for buf_slot in range(2):
        @pl.when(slot == buf_slot)
        def _():
            def issue(s, carry):
                window_copy(s, buf_slot).start()
                return carry
            lax.fori_loop(0, nwin_ref[i], issue, 0)

    @pl.when(i == pl.num_programs(0) - 1)
    def _():
        wait_windows(i)


def _dispatch(zero_start, n_win, win_src, win_dst, pos, meta, hn, n_rows):
    n_tok = hn.shape[0]
    tm = TM_ROW
    n_tiles = n_tok // tm
    n_buf_rows = SORT_ROWS
    return pl.pallas_call(
        _dispatch_kernel,
        grid_spec=pltpu.PrefetchScalarGridSpec(
            num_scalar_prefetch=4,
            grid=(n_tiles,),
            in_specs=[
                pl.BlockSpec((1, TOP_K, tm), lambda i, *_: (i, 0, 0)),
                pl.BlockSpec((tm, LANES), lambda i, *_: (i, 0)),
                pl.BlockSpec((tm, D_MODEL), lambda i, *_: (i, 0)),
            ],
            out_specs=pl.BlockSpec(memory_space=pl.ANY),
            scratch_shapes=[pltpu.VMEM((2, n_buf_rows, ROW_W), U32),
                            pltpu.VMEM((ZCHUNK, ROW_W), U32),
                            pltpu.SemaphoreType.DMA(()), pltpu.SemaphoreType.DMA(())],
        ),
        out_shape=jax.ShapeDtypeStruct((n_rows, ROW_W), U32),
        compiler_params=pltpu.CompilerParams(
            dimension_semantics=("arbitrary",), vmem_limit_bytes=VMEM_LIMIT),
        name="dispatch",
    )(zero_start, n_win, win_src.reshape(-1), win_dst.reshape(-1), pos, meta, hn)


def _expert_kernel(be_ref, live_ref, x_ref, wu_ref, bu_ref, wd_ref, bd_ref, y_ref, wu16_ref, wd16_ref):
    i = pl.program_id(0)
    live = live_ref[i]

    @pl.when((i == 0) | (be_ref[i] != be_ref[jnp.maximum(i - 1, 0)]))
    def _():
        wu16_ref[...] = wu_ref[0].astype(BF16)
        wd16_ref[...] = wd_ref[0].astype(BF16)

    def compute(m):
        half = D_MODEL // 2
        x_lo, x_hi = _unpack_pairs(x_ref[:m, :half])
        meta = lax.bitcast_convert_type(x_ref[:m, half:half + 3 * TOP_K], F32)
        expert = be_ref[i].astype(F32)
        gate = jnp.zeros((m, 1), F32)
        for k in range(TOP_K):
            weight = meta[:, TOP_K + k:TOP_K + k + 1] + meta[:, 2 * TOP_K + k:2 * TOP_K + k + 1]
            gate = gate + jnp.where(meta[:, k:k + 1] == expert, weight, 0.0)
        hu = _dot(x_lo, wu16_ref[:half, :]) + _dot(x_hi, wu16_ref[half:, :]) + bu_ref[0]
        x_glu = jnp.minimum(hu[:, :D_FF], SWIGLU_LIMIT)
        x_lin = jnp.clip(hu[:, D_FF:], -SWIGLU_LIMIT, SWIGLU_LIMIT)
        act = x_glu * _sigmoid(SWIGLU_ALPHA * x_glu) * (x_lin + 1.0)
        y_ref[:m, :] = _pack_pairs((_dot(act.astype(BF16), wd16_ref[...]) + bd_ref[0]) * gate)
        if m < TB:
            y_ref[m:, :] = jnp.zeros((TB - m, y_ref.shape[1]), y_ref.dtype)

    @pl.when(live == 0)
    def _():
        y_ref[...] = jnp.zeros_like(y_ref)

    for m in range(TB_STEP, TB + 1, TB_STEP):
        @pl.when((live > m - TB_STEP) & (live <= m))
        def _():
            compute(m)


def _experts(block_e, block_live, xin, w_up, b_up, w_down, b_down):
    n_rows = xin.shape[0]
    return pl.pallas_call(
        _expert_kernel,
        grid_spec=pltpu.PrefetchScalarGridSpec(
            num_scalar_prefetch=2,
            grid=(n_rows // TB,),
            in_specs=[
                pl.BlockSpec((TB, ROW_W), lambda i, be, nu: (i, 0)),
                pl.BlockSpec((1, D_MODEL, 2 * D_FF), lambda i, be, nu: (be[i], 0, 0)),
                pl.BlockSpec((1, 1, 2 * D_FF), lambda i, be, nu: (be[i], 0, 0)),
                pl.BlockSpec((1, D_FF, D_MODEL), lambda i, be, nu: (be[i], 0, 0)),
                pl.BlockSpec((1, 1, D_MODEL), lambda i, be, nu: (be[i], 0, 0)),
            ],
            out_specs=pl.BlockSpec((TB, D_MODEL // 2), lambda i, be, nu: (i, 0)),
            scratch_shapes=[pltpu.VMEM((D_MODEL, 2 * D_FF), BF16), pltpu.VMEM((D_FF, D_MODEL), BF16)],
        ),
        out_shape=jax.ShapeDtypeStruct((n_rows, D_MODEL // 2), U32),
        compiler_params=pltpu.CompilerParams(
            dimension_semantics=("arbitrary",), vmem_limit_bytes=VMEM_LIMIT),
        name="experts",
    )(block_e, block_live, xin, w_up, b_up.reshape(N_EXPERTS, 1, 2 * D_FF),
      w_down, b_down.reshape(N_EXPERTS, 1, D_MODEL))


def _combine_kernel(nwin_ref, wsrc_ref, wdst_ref, ric_ref, delta_ref, ricn_ref, deltan_ref, yb_ref, h1_ref,
                    lnf_ref, o_ref, stage_ref, buf_ref, pick_ref, sem):
    i = pl.program_id(0)
    tm = h1_ref.shape[0]
    n_buf_rows = buf_ref.shape[0]

    parity = i % 2

    def fetch(step, stage_slot):
        def body(s, carry):
            dst = pl.multiple_of(wdst_ref[step * N_SLOT + s], ROW_ALIGN)
            pltpu.make_async_copy(
                yb_ref.at[pl.ds(dst, WIN), :],
                stage_ref.at[stage_slot, pl.ds(pl.multiple_of(s * WIN, WIN), WIN), :],
                sem.at[stage_slot]).start()
            return carry
        lax.fori_loop(0, nwin_ref[step], body, 0)

    lane = lax.broadcasted_iota(jnp.int32, (tm, LANES), 1).astype(F32)
    col = lax.broadcasted_iota(jnp.int32, (tm, n_buf_rows), 1)

    def pick_round(pick, k, ric, delta):
        offset = jnp.sum(jnp.where(lane == ric[:, k:k + 1], delta, 0.0), axis=-1, keepdims=True)
        pos = (ric[:, TOP_K + k:TOP_K + k + 1] + offset).astype(jnp.int32)
        return pick + jnp.where(col == pos, 1.0, 0.0)

    @pl.when(i == 0)
    def _():
        buf_ref[...] = jnp.zeros_like(buf_ref)
        fetch(i, 0)
        pick = jnp.zeros((tm, n_buf_rows), F32)
        ric = ric_ref[...].astype(F32)
        for k in range(TOP_K):
            pick = pick_round(pick, k, ric, delta_ref[0])
        pick_ref[0] = pick.astype(BF16)

    for stage_slot in range(2):
        @pl.when((i + 1 < pl.num_programs(0)) & (parity != stage_slot))
        def _():
            fetch(i + 1, stage_slot)

    for stage_slot in range(2):
        @pl.when(parity == stage_slot)
        def _():
            def drain(s, carry):
                pltpu.make_async_copy(yb_ref.at[pl.ds(0, WIN), :], stage_ref.at[stage_slot, pl.ds(0, WIN), :],
                                      sem.at[stage_slot]).wait()
                return carry

            def compact(s, carry):
                src = pl.multiple_of(wsrc_ref[i * N_SLOT + s], ROW_ALIGN)
                buf_ref[pl.ds(src, WIN), :] = stage_ref[stage_slot, pl.ds(pl.multiple_of(s * WIN, WIN), WIN), :]
                return carry

            lax.fori_loop(0, nwin_ref[i], drain, 0)
            lax.fori_loop(0, nwin_ref[i], compact, 0)

    ric_next = ricn_ref[...].astype(F32)
    delta_next = deltan_ref[0]
    pick = pick_ref[parity]
    nxt = jnp.zeros((tm, n_buf_rows), F32)
    y_lo, y_hi = _unpack_pairs(buf_ref[...])
    nxt = pick_round(nxt, 0, ric_next, delta_next)
    left = _dot(pick, y_lo)
    nxt = pick_round(nxt, 1, ric_next, delta_next)
    right = _dot(pick, y_hi)
    nxt = pick_round(nxt, 2, ric_next, delta_next)
    h2 = h1_ref[...] + jnp.concatenate([left, right], axis=-1)
    nxt = pick_round(nxt, 3, ric_next, delta_next)
    o_ref[...] = h2 * _rms_scale(h2) * lnf_ref[...]
    pick_ref[1 - parity] = nxt.astype(BF16)


def _combine(n_win, win_src, win_dst, ric, delta, yb, h1, lnf_w):
    n_tok = h1.shape[0]
    tm = TM_ROW
    n_tiles = n_tok // tm
    return pl.pallas_call(
        _combine_kernel,
        grid_spec=pltpu.PrefetchScalarGridSpec(
            num_scalar_prefetch=3,
            grid=(n_tiles,),
            in_specs=[
                pl.BlockSpec((tm, LANES), lambda i, *_: (i, 0)),
                pl.BlockSpec((1, 1, LANES), lambda i, *_: (i, 0, 0)),
                pl.BlockSpec((tm, LANES), lambda i, *_: (jnp.minimum(i + 1, n_tiles - 1), 0)),
                pl.BlockSpec((1, 1, LANES), lambda i, *_: (jnp.minimum(i + 1, n_tiles - 1), 0, 0)),
                pl.BlockSpec(memory_space=pl.ANY),
                pl.BlockSpec((tm, D_MODEL), lambda i, *_: (i, 0)),
                pl.BlockSpec((1, D_MODEL), lambda i, *_: (0, 0)),
            ],
            out_specs=pl.BlockSpec((tm, D_MODEL), lambda i, *_: (i, 0)),
            scratch_shapes=[pltpu.VMEM((2, N_SLOT * WIN, D_MODEL // 2), U32),
                            pltpu.VMEM((SORT_ROWS, D_MODEL // 2), U32),
                            pltpu.VMEM((2, tm, SORT_ROWS), BF16), pltpu.SemaphoreType.DMA((2,))],
        ),
        out_shape=jax.ShapeDtypeStruct((n_tok, D_MODEL), F32),
        compiler_params=pltpu.CompilerParams(
            dimension_semantics=("arbitrary",), vmem_limit_bytes=VMEM_LIMIT),
        name="combine",
    )(n_win, win_src.reshape(-1), win_dst.reshape(-1), ric, delta, ric, delta, yb, h1,
      lnf_w.reshape(1, D_MODEL))


def _layer(h, ln1_w, w_in, w_gk2, b_gk2, gla_norm_w, w_proj_a, w_proj_b, w_out,
           ln2_w, w_router, b_router, w_up, b_up, w_down, b_down, lnf_w):
    bsz, seq, _ = h.shape
    n_tok = bsz * seq

    n_main = 3 * QKV_W + 2 * GLA_KEY_DIM + 2 * GLA_VALUE_DIM
    pad = LANES - GLA_GATE_RANK
    w_main = w_in[:, :n_main].astype(BF16)
    w_tail = jnp.concatenate(
        [w_in[:, n_main:n_main + GLA_GATE_RANK], jnp.zeros((D_MODEL, pad), F32),
         w_in[:, n_main + GLA_GATE_RANK:]], axis=1).astype(BF16)
    w2_pad = jnp.concatenate([w_gk2, jnp.zeros((pad, GLA_KEY_DIM), F32)], axis=0).astype(BF16)
    wr_pad = jnp.concatenate([w_router, jnp.zeros((D_MODEL, LANES - N_EXPERTS), F32)], axis=1)
    wr_hi = wr_pad.astype(BF16)
    wr_lo = (wr_pad - wr_hi.astype(F32)).astype(BF16)
    br_pad = jnp.concatenate([b_router, jnp.zeros((LANES - N_EXPERTS,), F32)]).reshape(1, LANES)

    pa0, pa1, pa2, pg, pm = _inproj(h, ln1_w, w_main, w_tail)
    o0, l0 = _dil_attn(pa0)
    o1, l1 = _dil_attn(pa1)
    o2, l2 = _dil_attn(pa2)
    o_b = _gla(pg, w2_pad, b_gk2, gla_norm_w)

    h1, hn, ri, ric, meta, carry_f, cnt = _mix(
        h.reshape(n_tok, D_MODEL), o0.reshape(n_tok, DA_WIDTH), l0.reshape(n_tok, LANES),
        o1, l1, o2, l2, o_b.reshape(n_tok, GLA_VALUE_DIM), pm.reshape(n_tok, MERGE_W),
        w_proj_a.astype(BF16), w_proj_b.astype(BF16), w_out.astype(BF16), ln2_w, wr_hi, wr_lo, br_pad)

    i32 = jnp.int32
    n_tiles = n_tok // TM_ROW
    experts = jnp.arange(N_EXPERTS)
    counts = cnt[0, :N_EXPERTS].astype(i32)
    before = carry_f[:, 0, :N_EXPERTS].astype(i32)
    run = jnp.concatenate([before[1:], counts[None]], axis=0) - before
    run_al = (run + ROW_ALIGN - 1) // ROW_ALIGN * ROW_ALIGN
    rows_end = jnp.cumsum(run_al, axis=0)
    rows_before = rows_end - run_al
    used = rows_end[-1]
    slack = WIN - ROW_ALIGN
    padded = (used + slack + TB - 1) // TB * TB
    pad_end = jnp.cumsum(padded)
    pad_start = pad_end - padded
    n_asg = n_tok * TOP_K
    n_rows = (n_asg + n_tiles * N_EXPERTS * (ROW_ALIGN - 1)
              + N_EXPERTS * (slack + TB - 1) + TB - 1) // TB * TB
    n_blocks = n_rows // TB
    block_starts = jnp.arange(n_blocks) * TB
    block_e = jnp.minimum(jnp.sum(pad_end[None, :] <= block_starts[:, None], axis=1),
                          N_EXPERTS - 1).astype(i32)
    used_end = jnp.sum(jnp.where(block_e[:, None] == experts, (pad_start + used)[None, :], 0), axis=1)
    block_live = jnp.clip(used_end - block_starts, 0, TB).astype(i32)

    run_end = jnp.cumsum(run_al, axis=1)
    run_start = run_end - run_al
    wins = (run + WIN - 1) // WIN
    wins_end = jnp.cumsum(wins, axis=1)
    wins_start = wins_end - wins
    n_win = wins_end[:, -1].astype(i32)
    slots = jnp.arange(N_SLOT)
    slot_e = jnp.sum(wins_end[:, None, :] <= slots[None, :, None], axis=-1)
    slot_is = slot_e[..., None] == experts

    def of_slot(table):
        return jnp.sum(jnp.where(slot_is, table[:, None, :], 0), axis=-1)

    win_off = (slots[None, :] - of_slot(wins_start)) * WIN
    win_src = (of_slot(run_start) + win_off).astype(i32)
    win_dst = (of_slot(pad_start[None, :] + rows_before) + win_off).astype(i32)
    asg_is = ri[:, :TOP_K, :, None] == experts
    pos = ri[:, TOP_K:, :] + jnp.sum(jnp.where(asg_is, (run_start - before)[:, None, None, :], 0), axis=-1)
    delta = jnp.pad((run_start - before).astype(F32), ((0, 0), (0, LANES - N_EXPERTS)))

    chunk_back = ZCHUNK * (1 + jnp.arange((slack + TB - 1 + ZCHUNK - 1) // ZCHUNK + 1))
    region_chunks = pad_end[:, None] - chunk_back[None, :]
    region_ok = (region_chunks >= pad_start[:, None]) & (region_chunks + ZCHUNK > (pad_start + used)[:, None])
    tail_chunks = pad_end[-1] + ZCHUNK * jnp.arange((n_rows - n_asg) // ZCHUNK)
    zero_start = jnp.concatenate([jnp.where(region_ok, region_chunks, -1).reshape(-1),
                                  jnp.where(tail_chunks < n_rows, tail_chunks, -1)]).astype(i32)

    xin = _dispatch(zero_start, n_win, win_src, win_dst, pos.astype(i32), meta, hn, n_rows)
    yb = _experts(block_e, block_live, xin, w_up, b_up, w_down, b_down)
    out = _combine(n_win, win_src, win_dst, ric, delta.reshape(-1, 1, LANES), yb, h1, lnf_w)
    return out.reshape(bsz, seq, D_MODEL)


def kernel(x, ln1_w, w_in, w_gk2, b_gk2, gla_norm_w, w_proj_a, w_proj_b, w_out, ln2_w, w_router,
           b_router, w_up, b_up, w_down, b_down, lnf_w):
    assert x.shape[-1] == D_MODEL and ln1_w.shape[0] == 1, "one layer of width D_MODEL"
    return _layer(x, ln1_w[0], w_in[0], w_gk2[0], b_gk2[0], gla_norm_w[0], w_proj_a[0], w_proj_b[0],
                  w_out[0], ln2_w[0], w_router[0], b_router[0], w_up[0], b_up[0], w_down[0],
                  b_down[0], lnf_w)
```

```python
import jax
import jax.numpy as jnp
from jax import lax
from jax.experimental import pallas as pl
from jax.experimental.pallas import tpu as pltpu

F32 = jnp.float32
BF16 = jnp.bfloat16
U32 = jnp.uint32

D_MODEL = 1024
DA_GROUPS = ((128, 1), (512, 4), (2048, 16))
DA_HEADS = 4
DA_HEAD_DIM = 128
DA_WIDTH = DA_HEADS * DA_HEAD_DIM
DA_BLOCK = 128
GLA_HEADS = 4
GLA_KEY_DIM = D_MODEL // 2
GLA_VALUE_DIM = D_MODEL
GLA_DK = GLA_KEY_DIM // GLA_HEADS
GLA_DV = GLA_VALUE_DIM // GLA_HEADS
GLA_GATE_RANK = 16
GLA_GATE_NORMALIZER = 16.0
GLA_CHUNK = 64
N_EXPERTS = 32
TOP_K = 4
D_FF = D_MODEL
SWIGLU_ALPHA = 1.702
SWIGLU_LIMIT = 7.0
RMS_EPS = 1e-5
NEG_INF = -1e30

LANES = 128
QKV_W = 3 * DA_WIDTH
GLA_W = 2 * GLA_KEY_DIM + 2 * GLA_VALUE_DIM + LANES
MERGE_W = 2 * D_MODEL

DA_QB = 4
LSE_LANES = LANES // DA_HEADS
TM_IN = 512
N_CHUNK = 512
T_GLA = 512
GLA_SUB = 256
GLA_HEADS_PER_STEP = 4
TM_MIX = 512
TB = 512
TB_STEP = 128
TM_ROW = 256
ROW_ALIGN = 8
WIN = 32
N_SLOT = N_EXPERTS + TM_ROW * TOP_K // WIN
SORT_ROWS = TM_ROW * TOP_K + N_EXPERTS * (ROW_ALIGN - 1) + WIN
ZCHUNK = 256
ROW_W = D_MODEL // 2 + LANES
VMEM_LIMIT = 56 * 1024 * 1024
VMEM_LIMIT_INPROJ = 62 * 1024 * 1024

_NT = (((1,), (1,)), ((), ()))
_TN = (((0,), (0,)), ((), ()))


def _dot(a, b):
    return jnp.dot(a, b, preferred_element_type=F32)


def _sigmoid(x):
    return 1.0 / (1.0 + jnp.exp(-x))


def _rms_scale(x):
    return lax.rsqrt(jnp.mean(x * x, axis=-1, keepdims=True) + RMS_EPS)


def _inproj_kernel(x_ref, ln_ref, wmain_hbm, wtail_hbm, pa0_ref, pa1_ref, pa2_ref, pg_ref, pm_ref,
                   w_ref, xs_ref, xn_ref, wsem):
    tm = x_ref.shape[1]
    n_slab = D_MODEL // LANES

    @pl.when((pl.program_id(0) == 0) & (pl.program_id(1) == 0))
    def _():
        n_main = wmain_hbm.shape[1]
        main = pltpu.make_async_copy(wmain_hbm, w_ref.at[:, :n_main], wsem.at[0])
        tail = pltpu.make_async_copy(wtail_hbm, w_ref.at[:, n_main:], wsem.at[1])
        main.start()
        tail.start()
        main.wait()
        tail.wait()

    def project(out_write, col0, width, post=None):
        for c0 in range(0, width, N_CHUNK):
            cw = min(N_CHUNK, width - c0)
            val = _dot(xs_ref[...], w_ref[:, col0 + c0:col0 + c0 + cw])
            out_write(c0, cw, (val if post is None else post(val)).astype(BF16))

    x = x_ref[0]
    xn = x * _rms_scale(x) * ln_ref[...]
    xs_ref[...] = xn.astype(BF16)
    for j in range(n_slab):
        xn_ref[j] = xn[:, j * LANES:(j + 1) * LANES]

    def write_to(ref):
        def write(c0, cw, val):
            ref[0, :, c0:c0 + cw] = val
        return write

    def write_qkv(out_ref, d, n):
        def write(c0, cw, val):
            for r in range(d):
                out_ref[0, r, c0 // DA_WIDTH] = val[r * n:(r + 1) * n]
        return write

    project(write_qkv(pa0_ref, 1, tm), 0, QKV_W)
    project(write_to(pg_ref), 3 * QKV_W, GLA_W)
    project(write_to(pm_ref), 3 * QKV_W + GLA_W, MERGE_W, post=_sigmoid)

    for gi, out_ref in ((1, pa1_ref), (2, pa2_ref)):
        d = DA_GROUPS[gi][1]
        n = tm // d
        for r in range(d):
            for j in range(n_slab):
                xs_ref[r * n:(r + 1) * n, j * LANES:(j + 1) * LANES] = (
                    xn_ref[j, pl.ds(r, n, stride=d), :].astype(BF16))
        project(write_qkv(out_ref, d, n), gi * QKV_W, QKV_W)


def _inproj(x, ln1_w, w_main, w_tail):
    assert N_CHUNK == DA_WIDTH and w_main.shape[1] % LANES == 0
    bsz, seq, _ = x.shape
    tm = TM_IN
    d1, d2 = DA_GROUPS[1][1], DA_GROUPS[2][1]

    def qkv_shape(d):
        return jax.ShapeDtypeStruct((bsz, d, 3, seq // d, DA_WIDTH), BF16)

    def qkv_spec(d):
        return pl.BlockSpec((1, d, 3, tm // d, DA_WIDTH), lambda b, i: (b, 0, 0, i, 0))

    return pl.pallas_call(
        _inproj_kernel,
        grid=(bsz, seq // tm),
        in_specs=[
            pl.BlockSpec((1, tm, D_MODEL), lambda b, i: (b, i, 0)),
            pl.BlockSpec((1, D_MODEL), lambda b, i: (0, 0)),
            pl.BlockSpec(memory_space=pl.ANY),
            pl.BlockSpec(memory_space=pl.ANY),
        ],
        out_specs=(
            qkv_spec(1), qkv_spec(d1), qkv_spec(d2),
            pl.BlockSpec((1, tm, GLA_W), lambda b, i: (b, i, 0)),
            pl.BlockSpec((1, tm, MERGE_W), lambda b, i: (b, i, 0)),
        ),
        out_shape=(qkv_shape(1), qkv_shape(d1), qkv_shape(d2),
                   jax.ShapeDtypeStruct((bsz, seq, GLA_W), BF16),
                   jax.ShapeDtypeStruct((bsz, seq, MERGE_W), BF16)),
        scratch_shapes=[pltpu.VMEM((D_MODEL, w_main.shape[1] + w_tail.shape[1]), BF16),
                        pltpu.VMEM((tm, D_MODEL), BF16),
                        pltpu.VMEM((D_MODEL // LANES, tm, LANES), F32),
                        pltpu.SemaphoreType.DMA((2,))],
        compiler_params=pltpu.CompilerParams(
            dimension_semantics=("arbitrary", "arbitrary"), vmem_limit_bytes=VMEM_LIMIT_INPROJ),
        name="inproj",
    )(x, ln1_w.reshape(1, D_MODEL), w_main, w_tail)


def _dil_attn_kernel(q_ref, kp_ref, kc_ref, vp_ref, vc_ref, o_ref, l_ref, s_ref, p_ref, r_ref):
    n = pl.program_id(2)
    blk = DA_BLOCK
    qi = lax.broadcasted_iota(jnp.int32, (blk, 2 * blk), 0)
    kj = lax.broadcasted_iota(jnp.int32, (blk, 2 * blk), 1)
    band = (kj >= qi) & (kj <= qi + blk)
    band_first = (kj >= jnp.where(n > 0, qi, blk)) & (kj <= qi + blk)
    scale = DA_HEAD_DIM ** -0.5
    items = [(b, h) for b in range(DA_QB) for h in range(DA_HEADS)]

    def rows(b):
        return slice(b * blk, (b + 1) * blk)

    def cols(h):
        return slice(h * DA_HEAD_DIM, (h + 1) * DA_HEAD_DIM)

    def window(prev_ref, cur_ref, b, h):
        before = prev_ref[0, 0, 0, :, cols(h)] if b == 0 else cur_ref[0, 0, 0, rows(b - 1), cols(h)]
        return jnp.concatenate([before, cur_ref[0, 0, 0, rows(b), cols(h)]], axis=0)

    def scores(i, b, h):
        s = lax.dot_general(q_ref[0, 0, 0, rows(b), cols(h)], window(kp_ref, kc_ref, b, h), _NT,
                            preferred_element_type=F32) * scale
        s_ref[i] = jnp.where(band_first if b == 0 else band, s, NEG_INF)

    def softmax(i, b, h):
        s = s_ref[i]
        m = jnp.max(s, axis=-1, keepdims=True)
        p = jnp.exp(s - m)
        l = jnp.sum(p, axis=-1, keepdims=True)
        p_ref[i] = p.astype(BF16)
        r_ref[i] = jnp.broadcast_to(1.0 / l, (blk, DA_HEAD_DIM))
        l_ref[0, 0, rows(b), h * LSE_LANES:(h + 1) * LSE_LANES] = jnp.broadcast_to(
            m + jnp.log(l), (blk, LSE_LANES))

    def values(i, b, h):
        acc = _dot(p_ref[i], window(vp_ref, vc_ref, b, h))
        o_ref[0, 0, rows(b), cols(h)] = (acc * r_ref[i]).astype(o_ref.dtype)

    stages = (scores, softmax, values)
    for t in range(len(items) + len(stages) - 1):
        for lag, stage in enumerate(stages):
            if 0 <= t - lag < len(items):
                stage(t - lag, *items[t - lag])


def _dil_attn(pa):
    bsz, d, _, sub_len, _ = pa.shape
    rows = DA_QB * DA_BLOCK
    n_items = DA_QB * DA_HEADS

    def cur(sec):
        return pl.BlockSpec((1, 1, 1, rows, DA_WIDTH), lambda b, r, n: (b, r, sec, n, 0))

    def prev(sec):
        return pl.BlockSpec((1, 1, 1, DA_BLOCK, DA_WIDTH),
                            lambda b, r, n: (b, r, sec, jnp.maximum(n * DA_QB - 1, 0), 0))

    def out(width):
        return pl.BlockSpec((1, 1, rows, width), lambda b, r, n: (b, r, n, 0))

    return pl.pallas_call(
        _dil_attn_kernel,
        grid=(bsz, d, sub_len // rows),
        in_specs=[cur(0), prev(1), cur(1), prev(2), cur(2)],
        out_specs=(out(DA_WIDTH), out(LANES)),
        out_shape=(jax.ShapeDtypeStruct((bsz, d, sub_len, DA_WIDTH), BF16),
                   jax.ShapeDtypeStruct((bsz, d, sub_len, LANES), F32)),
        scratch_shapes=[pltpu.VMEM((n_items, DA_BLOCK, 2 * DA_BLOCK), F32),
                        pltpu.VMEM((n_items, DA_BLOCK, 2 * DA_BLOCK), BF16),
                        pltpu.VMEM((n_items, DA_BLOCK, DA_HEAD_DIM), F32)],
        compiler_params=pltpu.CompilerParams(
            dimension_semantics=("parallel", "parallel", "arbitrary"), vmem_limit_bytes=VMEM_LIMIT),
        name=f"dil_attn_d{d}",
    )(pa, pa, pa, pa, pa)


def _gla_kernel(q_ref, k_ref, v_ref, og_ref, lr_ref, w2_ref, b2_ref, nw_ref, o_ref, st_ref, mask_ref, keep_ref):
    t = pl.program_id(2)
    c = GLA_CHUNK
    tt = q_ref.shape[1]
    n_c = tt // c

    sub = mask_ref.shape[0]

    @pl.when(t == 0)
    def _():
        st_ref[...] = jnp.zeros_like(st_ref)
        row = lax.broadcasted_iota(jnp.int32, (sub, sub), 0)
        col = lax.broadcasted_iota(jnp.int32, (sub, sub), 1)
        keep = (col <= row) & (col >= row - row % c)
        keep_ref[...] = jnp.where(keep, 1.0, 0.0)
        mask_ref[...] = jnp.where(keep, 1.0, 0.0).astype(BF16)

    heads = range(q_ref.shape[2] // GLA_DK)
    mask = mask_ref[...]

    def kcols(h):
        return slice(h * GLA_DK, (h + 1) * GLA_DK)

    def vcols(h):
        return slice(h * GLA_DV, (h + 1) * GLA_DV)

    gpre = _dot(lr_ref[0], w2_ref[...]) + b2_ref[...]
    forget = (jnp.minimum(gpre, 0.0) - jnp.log(1.0 + jnp.exp(-jnp.abs(gpre)))) / GLA_GATE_NORMALIZER
    g_hi = forget.astype(BF16)
    g_lo = (forget - g_hi.astype(F32)).astype(BF16)
    b, b_last, q_e, k_e, k_end = [], [], [], [], []
    for h in heads:
        g_cat = jnp.concatenate([g_hi[:, kcols(h)], g_lo[:, kcols(h)]], axis=-1)
        csum = jnp.concatenate([_dot(mask, g_cat[s0:s0 + sub]) for s0 in range(0, tt, sub)], axis=0)
        b.append(csum[:, :GLA_DK] + csum[:, GLA_DK:])
    for h in heads:
        b_last.append(b[h].reshape(n_c, c, GLA_DK)[:, c - 1:c, :])
        b_to_end = (b_last[h] - b[h].reshape(n_c, c, GLA_DK)).reshape(tt, GLA_DK)
        q = q_ref[0, :, kcols(h)].astype(F32)
        k = k_ref[0, :, kcols(h)].astype(F32)
        q_e.append((q * ((GLA_DK ** -0.5) * jnp.exp(b[h]))).astype(BF16))
        k_e.append((k * jnp.exp(-b[h])).astype(BF16))
        k_end.append((k * jnp.exp(b_to_end)).astype(BF16))
    o_intra = []
    for h in heads:
        parts = []
        for s0 in range(0, tt, sub):
            ss = slice(s0, s0 + sub)
            att = lax.dot_general(q_e[h][ss], k_e[h][ss], _NT, preferred_element_type=F32)
            att = jnp.where(keep_ref[...] > 0.0, att, 0.0).astype(BF16)
            parts.append(_dot(att, v_ref[0, ss, vcols(h)]))
        o_intra.append(jnp.concatenate(parts, axis=0))
    decay = [jnp.exp(b_last[h].reshape(n_c, GLA_DK).T) for h in heads]
    st = [st_ref[h] for h in heads]
    outs = [[] for _ in heads]
    for ci in range(n_c):
        rs = slice(ci * c, (ci + 1) * c)
        for h in heads:
            outs[h].append(o_intra[h][rs] + _dot(q_e[h][rs], st[h].astype(BF16)))
            st[h] = decay[h][:, ci:ci + 1] * st[h] + lax.dot_general(
                k_end[h][rs], v_ref[0, rs, vcols(h)], _TN, preferred_element_type=F32)
    for h in heads:
        st_ref[h] = st[h]
        o = jnp.concatenate(outs[h], axis=0)
        o = o * _rms_scale(o) * nw_ref[...]
        gate = og_ref[0, :, vcols(h)].astype(F32)
        o_ref[0, :, vcols(h)] = (o * (gate * _sigmoid(gate))).astype(BF16)


def _gla(pg, w2_pad, b_gk2, gla_norm_w):
    bsz, seq, _ = pg.shape
    t = T_GLA
    hps = GLA_HEADS_PER_STEP
    wk, wv = hps * GLA_DK, hps * GLA_DV
    kq = GLA_KEY_DIM // wk
    kv = 2 * GLA_KEY_DIM // wv
    kg = kv + GLA_VALUE_DIM // wv
    klr = (2 * GLA_KEY_DIM + 2 * GLA_VALUE_DIM) // LANES
    return pl.pallas_call(
        _gla_kernel,
        grid=(bsz, GLA_HEADS // hps, seq // t),
        in_specs=[
            pl.BlockSpec((1, t, wk), lambda b, h, i: (b, i, h)),
            pl.BlockSpec((1, t, wk), lambda b, h, i: (b, i, kq + h)),
            pl.BlockSpec((1, t, wv), lambda b, h, i: (b, i, kv + h)),
            pl.BlockSpec((1, t, wv), lambda b, h, i: (b, i, kg + h)),
            pl.BlockSpec((1, t, LANES), lambda b, h, i: (b, i, klr)),
            pl.BlockSpec((LANES, wk), lambda b, h, i: (0, h)),
            pl.BlockSpec((1, wk), lambda b, h, i: (0, h)),
            pl.BlockSpec((1, GLA_DV), lambda b, h, i: (0, 0)),
        ],
        out_specs=pl.BlockSpec((1, t, wv), lambda b, h, i: (b, i, h)),
        out_shape=jax.ShapeDtypeStruct((bsz, seq, GLA_VALUE_DIM), BF16),
        scratch_shapes=[pltpu.VMEM((hps, GLA_DK, GLA_DV), F32), pltpu.VMEM((GLA_SUB, GLA_SUB), BF16),
                        pltpu.VMEM((GLA_SUB, GLA_SUB), F32)],
        compiler_params=pltpu.CompilerParams(
            dimension_semantics=("parallel", "parallel", "arbitrary"), vmem_limit_bytes=VMEM_LIMIT),
        name="gla",
    )(pg, pg, pg, pg, pg, w2_pad, b_gk2.reshape(1, GLA_KEY_DIM), gla_norm_w.reshape(1, GLA_DV))


def _mix_kernel(x_ref, o0_ref, l0_ref, o1_ref, l1_ref, o2_ref, l2_ref, ob_ref, pm_ref,
                wa_ref, wb_ref, wo_ref, ln2_ref, wrh_ref, wrl_ref, br_ref,
                h1_ref, hn_ref, ri_ref, ric_ref, meta_ref, cbefore_ref, cnt_ref,
                po1_ref, pl1_ref, po2_ref, pl2_ref, carry_ref, logit_ref):
    step = pl.program_id(0)
    tm = x_ref.shape[0]

    @pl.when(step == 0)
    def _():
        carry_ref[...] = jnp.zeros_like(carry_ref)
        logit_ref[...] = jnp.zeros_like(logit_ref)

    routed = step > 0
    lane = lax.broadcasted_iota(jnp.int32, (tm, LANES), 1).astype(F32)
    route = {"work": None, "vals": [], "idxs": []}

    def topk_round():
        work = route["work"]
        m = jnp.max(work, axis=-1, keepdims=True)
        idx = jnp.min(jnp.where(work == m, lane, float(LANES)), axis=-1, keepdims=True)
        route["vals"].append(m)
        route["idxs"].append(idx)
        route["work"] = jnp.where(lane == idx, -jnp.inf, work)

    route["work"] = jnp.where(lane < N_EXPERTS, logit_ref[(step + 1) % 2], -jnp.inf)

    for o_ref, l_ref, po_ref, pl_ref, (_, d) in ((o1_ref, l1_ref, po1_ref, pl1_ref, DA_GROUPS[1]),
                                                 (o2_ref, l2_ref, po2_ref, pl2_ref, DA_GROUPS[2])):
        topk_round()
        n = tm // d
        for r in range(d):
            pl_ref[pl.ds(r, n, stride=d), :] = l_ref[0, r]
            for h in range(DA_HEADS):
                sl = slice(h * DA_HEAD_DIM, (h + 1) * DA_HEAD_DIM)
                po_ref[h, pl.ds(r, n, stride=d), :] = o_ref[0, r, :, sl].astype(F32)

    l0, l1, l2 = l0_ref[...], pl1_ref[...], pl2_ref[...]
    mx = jnp.maximum(jnp.maximum(l0, l1), l2)
    e0, e1, e2 = jnp.exp(l0 - mx), jnp.exp(l1 - mx), jnp.exp(l2 - mx)
    inv = 1.0 / (e0 + e1 + e2)
    w0, w1, w2 = e0 * inv, e1 * inv, e2 * inv
    topk_round()
    heads = []
    for h in range(DA_HEADS):
        sl = slice(h * DA_HEAD_DIM, (h + 1) * DA_HEAD_DIM)
        at = slice(h * LSE_LANES, h * LSE_LANES + 1)
        o_h = w0[:, at] * o0_ref[:, sl].astype(F32) + w1[:, at] * po1_ref[h] + w2[:, at] * po2_ref[h]
        heads.append(o_h.astype(BF16))
    o_a = jnp.concatenate(heads, axis=-1)
    topk_round()

    gates = pm_ref[...].astype(F32)
    mixed = (gates[:, :D_MODEL] * _dot(o_a, wa_ref[...])
             + gates[:, D_MODEL:] * _dot(ob_ref[...], wb_ref[...]))
    vals, idxs = route["vals"], route["idxs"]
    assert len(vals) == TOP_K
    exps = [jnp.exp(v - vals[0]) for v in vals]
    denom = exps[0] + exps[1] + exps[2] + exps[3]
    onehot = jnp.zeros((tm, LANES), F32)
    for idx in idxs:
        onehot = onehot + jnp.where(lane == idx, 1.0, 0.0)

    h1 = x_ref[...] + _dot(mixed.astype(BF16), wo_ref[...])
    h1_ref[...] = h1
    hn = h1 * _rms_scale(h1) * ln2_ref[...]

    row = lax.broadcasted_iota(jnp.int32, (tm, tm), 0)
    col = lax.broadcasted_iota(jnp.int32, (tm, tm), 1)
    below = jnp.where(col < row, 1.0, 0.0).astype(BF16)
    before = _dot(below, onehot.astype(BF16)) + carry_ref[0:1, :]
    ranks = [jnp.sum(jnp.where(lane == idx, before, 0.0), axis=-1, keepdims=True) for idx in idxs]
    running = carry_ref[0:1, :]
    for j in range(tm // TM_ROW):
        cbefore_ref[j] = jnp.broadcast_to(running, cbefore_ref.shape[1:])
        running = running + jnp.sum(onehot[j * TM_ROW:(j + 1) * TM_ROW], axis=0, keepdims=True)
    carry = jnp.where(routed, running, carry_ref[0:1, :])
    carry_ref[...] = jnp.broadcast_to(carry, carry_ref.shape)
    cnt_ref[...] = jnp.broadcast_to(carry, cnt_ref.shape)

    hn_hi = hn.astype(BF16)
    hn_ref[...] = hn_hi

    ri = jnp.zeros((tm, LANES), F32)
    for j, val in enumerate(idxs + ranks):
        ri = jnp.where(lane == float(j), val, ri)
    ric_ref[...] = ri.astype(jnp.int32)
    ri_t = ri.T[:2 * TOP_K].astype(jnp.int32)
    for j in range(tm // TM_ROW):
        ri_ref[j] = ri_t[:, j * TM_ROW:(j + 1) * TM_ROW]
    meta = jnp.zeros((tm, LANES), F32)
    for k in range(TOP_K):
        gate = exps[k] / denom
        gate_hi = gate.astype(BF16).astype(F32)
        meta = jnp.where(lane == float(k), idxs[k], meta)
        meta = jnp.where(lane == float(TOP_K + k), gate_hi, meta)
        meta = jnp.where(lane == float(2 * TOP_K + k), gate - gate_hi, meta)
    meta_ref[...] = meta.astype(BF16)

    hn_lo = (hn - hn_hi.astype(F32)).astype(BF16)
    logit_ref[step % 2] = (_dot(hn_hi, wrh_ref[...]) + _dot(hn_lo, wrh_ref[...])
                           + _dot(hn_hi, wrl_ref[...]) + br_ref[...])


def _mix(x2, o0, l0, o1, l1, o2, l2, o_b, pm, wa, wb, wo, ln2_w, wr_hi, wr_lo, br_pad):
    n_tok = x2.shape[0]
    tm = TM_MIX
    bsz = o1.shape[0]
    d1, d2 = DA_GROUPS[1][1], DA_GROUPS[2][1]
    tiles_per_seq = (n_tok // bsz) // tm
    n_tiles = n_tok // tm

    def tile(i):
        return jnp.minimum(i, n_tiles - 1)

    def routed(i):
        return jnp.maximum(i - 1, 0)

    def rows(width, which=tile):
        return pl.BlockSpec((tm, width), lambda i: (which(i), 0))

    def residue_major(d, width):
        return pl.BlockSpec((1, d, tm // d, width),
                            lambda i: (tile(i) // tiles_per_seq, 0, tile(i) % tiles_per_seq, 0))

    def whole(arr):
        return pl.BlockSpec(arr.shape, lambda i: (0,) * arr.ndim)

    ln2 = ln2_w.reshape(1, D_MODEL)
    return pl.pallas_call(
        _mix_kernel,
        grid=(n_tiles + 1,),
        in_specs=[rows(D_MODEL), rows(DA_WIDTH), rows(LANES),
                  residue_major(d1, DA_WIDTH), residue_major(d1, LANES),
                  residue_major(d2, DA_WIDTH), residue_major(d2, LANES),
                  rows(GLA_VALUE_DIM), rows(MERGE_W),
                  whole(wa), whole(wb), whole(wo), whole(ln2), whole(wr_hi), whole(wr_lo), whole(br_pad)],
        out_specs=(rows(D_MODEL), rows(D_MODEL),
                   pl.BlockSpec((tm // TM_ROW, 2 * TOP_K, TM_ROW), lambda i: (routed(i), 0, 0)),
                   rows(LANES, routed), rows(LANES, routed),
                   pl.BlockSpec((tm // TM_ROW, 8, LANES), lambda i: (routed(i), 0, 0)),
                   pl.BlockSpec((8, LANES), lambda i: (0, 0))),
        out_shape=(jax.ShapeDtypeStruct((n_tok, D_MODEL), F32),
                   jax.ShapeDtypeStruct((n_tok, D_MODEL), BF16),
                   jax.ShapeDtypeStruct((n_tok // TM_ROW, 2 * TOP_K, TM_ROW), jnp.int32),
                   jax.ShapeDtypeStruct((n_tok, LANES), jnp.int32),
                   jax.ShapeDtypeStruct((n_tok, LANES), BF16),
                   jax.ShapeDtypeStruct((n_tok // TM_ROW, 8, LANES), F32),
                   jax.ShapeDtypeStruct((8, LANES), F32)),
        scratch_shapes=[pltpu.VMEM((DA_HEADS, tm, DA_HEAD_DIM), F32), pltpu.VMEM((tm, LANES), F32),
                        pltpu.VMEM((DA_HEADS, tm, DA_HEAD_DIM), F32), pltpu.VMEM((tm, LANES), F32),
                        pltpu.VMEM((8, LANES), F32), pltpu.VMEM((2, tm, LANES), F32)],
        compiler_params=pltpu.CompilerParams(
            dimension_semantics=("arbitrary",), vmem_limit_bytes=VMEM_LIMIT),
        name="mix",
    )(x2, o0, l0, o1, l1, o2, l2, o_b, pm, wa, wb, wo, ln2, wr_hi, wr_lo, br_pad)


def _pack_pairs(x):
    n = x.shape[1] // 2
    rounded = x.astype(BF16).astype(F32)
    lo = lax.bitcast_convert_type(rounded[:, :n], U32) >> 16
    hi = lax.bitcast_convert_type(rounded[:, n:], U32) & jnp.uint32(0xFFFF0000)
    return hi | lo


def _unpack_pairs(u):
    lo = lax.bitcast_convert_type(u << 16, F32).astype(BF16)
    hi = lax.bitcast_convert_type(u & jnp.uint32(0xFFFF0000), F32).astype(BF16)
    return lo, hi


def _dispatch_kernel(zstart_ref, nwin_ref, wsrc_ref, wdst_ref, pos_ref, meta_ref, hn_ref, xin_ref,
                     buf_ref, zero_ref, sem, zsem):
    i = pl.program_id(0)
    tm = hn_ref.shape[0]
    n_buf_rows = buf_ref.shape[1]
    slot = i % 2

    def window_copy(s, buf_slot):
        src = pl.multiple_of(wsrc_ref[i * N_SLOT + s], ROW_ALIGN)
        dst = pl.multiple_of(wdst_ref[i * N_SLOT + s], ROW_ALIGN)
        return pltpu.make_async_copy(buf_ref.at[buf_slot, pl.ds(src, WIN), :],
                                     xin_ref.at[pl.ds(dst, WIN), :], sem)

    def wait_windows(step):
        def body(s, carry):
            pltpu.make_async_copy(buf_ref.at[0, pl.ds(0, WIN), :], xin_ref.at[pl.ds(0, WIN), :], sem).wait()
            return carry
        lax.fori_loop(0, nwin_ref[step], body, 0)

    @pl.when(i == 0)
    def _():
        zero_ref[...] = jnp.zeros_like(zero_ref)
        for j in range(zstart_ref.shape[0]):
            @pl.when(zstart_ref[j] >= 0)
            def _():
                start = pl.multiple_of(zstart_ref[j], ZCHUNK)
                cp = pltpu.make_async_copy(zero_ref, xin_ref.at[pl.ds(start, ZCHUNK), :], zsem)
                cp.start()
                cp.wait()

    row = lax.broadcasted_iota(jnp.int32, (n_buf_rows, tm), 0)

    perm = jnp.zeros((n_buf_rows, tm), F32)
    for k in range(TOP_K):
        perm = perm + jnp.where(row == pos_ref[0, k:k + 1, :], 1.0, 0.0)
    perm = perm.astype(BF16)
    buf_ref[slot, :, :D_MODEL // 2] = _pack_pairs(_dot(perm, hn_ref[...]))
    buf_ref[slot, :, D_MODEL // 2:] = lax.bitcast_convert_type(_dot(perm, meta_ref[...]), U32)

    @pl.when(i > 0)
    def _():
        wait_windows(i - 1)

    for buf_slot in range(2):
        @pl.when(slot == buf_slot)
        def _():
            def issue(s, carry):
                window_copy(s, buf_slot).start()
                return carry
            lax.fori_loop(0, nwin_ref[i], issue, 0)

    @pl.when(i == pl.num_programs(0) - 1)
    def _():
        wait_windows(i)


def _dispatch(zero_start, n_win, win_src, win_dst, pos, meta, hn, n_rows):
    n_tok = hn.shape[0]
    tm = TM_ROW
    n_tiles = n_tok // tm
    n_buf_rows = SORT_ROWS
    return pl.pallas_call(
        _dispatch_kernel,
        grid_spec=pltpu.PrefetchScalarGridSpec(
            num_scalar_prefetch=4,
            grid=(n_tiles,),
            in_specs=[
                pl.BlockSpec((1, TOP_K, tm), lambda i, *_: (i, 0, 0)),
                pl.BlockSpec((tm, LANES), lambda i, *_: (i, 0)),
                pl.BlockSpec((tm, D_MODEL), lambda i, *_: (i, 0)),
            ],
            out_specs=pl.BlockSpec(memory_space=pl.ANY),
            scratch_shapes=[pltpu.VMEM((2, n_buf_rows, ROW_W), U32),
                            pltpu.VMEM((ZCHUNK, ROW_W), U32),
                            pltpu.SemaphoreType.DMA(()), pltpu.SemaphoreType.DMA(())],
        ),
        out_shape=jax.ShapeDtypeStruct((n_rows, ROW_W), U32),
        compiler_params=pltpu.CompilerParams(
            dimension_semantics=("arbitrary",), vmem_limit_bytes=VMEM_LIMIT),
        name="dispatch",
    )(zero_start, n_win, win_src.reshape(-1), win_dst.reshape(-1), pos, meta, hn)


def _expert_kernel(be_ref, live_ref, x_ref, wu_hbm, bu_ref, wd_hbm, bd_ref, y_ref,
                   wu32_ref, wd32_ref, wu16_ref, wd16_ref, wsem):
    i = pl.program_id(0)
    live = live_ref[i]
    expert_id = be_ref[i]

    def weight_copies(expert, slot):
        return (pltpu.make_async_copy(wu_hbm.at[expert], wu32_ref.at[slot], wsem.at[0, slot]),
                pltpu.make_async_copy(wd_hbm.at[expert], wd32_ref.at[slot], wsem.at[1, slot]))

    @pl.when(i == 0)
    def _():
        for cp in weight_copies(expert_id, 0):
            cp.start()

    for slot in range(2):
        @pl.when(((i == 0) | (expert_id != be_ref[jnp.maximum(i - 1, 0)])) & (expert_id % 2 == slot))
        def _():
            for cp in weight_copies(expert_id, slot):
                cp.wait()

            @pl.when(expert_id + 1 < N_EXPERTS)
            def _():
                for cp in weight_copies(expert_id + 1, 1 - slot):
                    cp.start()

            wu16_ref[...] = wu32_ref[slot].astype(BF16)
            wd16_ref[...] = wd32_ref[slot].astype(BF16)

    def compute(m):
        half = D_MODEL // 2
        x_lo, x_hi = _unpack_pairs(x_ref[:m, :half])
        meta = lax.bitcast_convert_type(x_ref[:m, half:half + 3 * TOP_K], F32)
        expert = be_ref[i].astype(F32)
        gate = jnp.zeros((m, 1), F32)
        for k in range(TOP_K):
            weight = meta[:, TOP_K + k:TOP_K + k + 1] + meta[:, 2 * TOP_K + k:2 * TOP_K + k + 1]
            gate = gate + jnp.where(meta[:, k:k + 1] == expert, weight, 0.0)
        hu = _dot(x_lo, wu16_ref[:half, :]) + _dot(x_hi, wu16_ref[half:, :]) + bu_ref[0]
        x_glu = jnp.minimum(hu[:, :D_FF], SWIGLU_LIMIT)
        x_lin = jnp.clip(hu[:, D_FF:], -SWIGLU_LIMIT, SWIGLU_LIMIT)
        act = x_glu * _sigmoid(SWIGLU_ALPHA * x_glu) * (x_lin + 1.0)
        y_ref[:m, :] = _pack_pairs((_dot(act.astype(BF16), wd16_ref[...]) + bd_ref[0]) * gate)
        if m < TB:
            y_ref[m:, :] = jnp.zeros((TB - m, y_ref.shape[1]), y_ref.dtype)

    @pl.when(live == 0)
    def _():
        y_ref[...] = jnp.zeros_like(y_ref)

    for m in range(TB_STEP, TB + 1, TB_STEP):
        @pl.when((live > m - TB_STEP) & (live <= m))
        def _():
            compute(m)


def _experts(block_e, block_live, xin, w_up, b_up, w_down, b_down):
    n_rows = xin.shape[0]
    return pl.pallas_call(
        _expert_kernel,
        grid_spec=pltpu.PrefetchScalarGridSpec(
            num_scalar_prefetch=2,
            grid=(n_rows // TB,),
            in_specs=[
                pl.BlockSpec((TB, ROW_W), lambda i, be, nu: (i, 0)),
                pl.BlockSpec(memory_space=pl.ANY),
                pl.BlockSpec((1, 1, 2 * D_FF), lambda i, be, nu: (be[i], 0, 0)),
                pl.BlockSpec(memory_space=pl.ANY),
                pl.BlockSpec((1, 1, D_MODEL), lambda i, be, nu: (be[i], 0, 0)),
            ],
            out_specs=pl.BlockSpec((TB, D_MODEL // 2), lambda i, be, nu: (i, 0)),
            scratch_shapes=[pltpu.VMEM((2, D_MODEL, 2 * D_FF), F32), pltpu.VMEM((2, D_FF, D_MODEL), F32),
                            pltpu.VMEM((D_MODEL, 2 * D_FF), BF16), pltpu.VMEM((D_FF, D_MODEL), BF16),
                            pltpu.SemaphoreType.DMA((2, 2))],
        ),
        out_shape=jax.ShapeDtypeStruct((n_rows, D_MODEL // 2), U32),
        compiler_params=pltpu.CompilerParams(
            dimension_semantics=("arbitrary",), vmem_limit_bytes=VMEM_LIMIT),
        name="experts",
    )(block_e, block_live, xin, w_up, b_up.reshape(N_EXPERTS, 1, 2 * D_FF),
      w_down, b_down.reshape(N_EXPERTS, 1, D_MODEL))


def _combine_kernel(nwin_ref, wsrc_ref, wdst_ref, ric_ref, delta_ref, ricn_ref, deltan_ref, yb_ref, h1_ref,
                    lnf_ref, o_ref, stage_ref, buf_ref, pick_ref, sem):
    i = pl.program_id(0)
    tm = h1_ref.shape[0]
    n_buf_rows = buf_ref.shape[0]

    parity = i % 2

    def fetch(step, stage_slot):
        def body(s, carry):
            dst = pl.multiple_of(wdst_ref[step * N_SLOT + s], ROW_ALIGN)
            pltpu.make_async_copy(
                yb_ref.at[pl.ds(dst, WIN), :],
                stage_ref.at[stage_slot, pl.ds(pl.multiple_of(s * WIN, WIN), WIN), :],
                sem.at[stage_slot]).start()
            return carry
        lax.fori_loop(0, nwin_ref[step], body, 0)

    lane = lax.broadcasted_iota(jnp.int32, (tm, LANES), 1).astype(F32)
    col = lax.broadcasted_iota(jnp.int32, (tm, n_buf_rows), 1)

    def pick_round(pick, k, ric, delta):
        offset = jnp.sum(jnp.where(lane == ric[:, k:k + 1], delta, 0.0), axis=-1, keepdims=True)
        pos = (ric[:, TOP_K + k:TOP_K + k + 1] + offset).astype(jnp.int32)
        return pick + jnp.where(col == pos, 1.0, 0.0)

    @pl.when(i == 0)
    def _():
        buf_ref[...] = jnp.zeros_like(buf_ref)
        fetch(i, 0)
        pick = jnp.zeros((tm, n_buf_rows), F32)
        ric = ric_ref[...].astype(F32)
        for k in range(TOP_K):
            pick = pick_round(pick, k, ric, delta_ref[0])
        pick_ref[0] = pick.astype(BF16)

    for stage_slot in range(2):
        @pl.when((i + 1 < pl.num_programs(0)) & (parity != stage_slot))
        def _():
            fetch(i + 1, stage_slot)

    for stage_slot in range(2):
        @pl.when(parity == stage_slot)
        def _():
            def drain(s, carry):
                pltpu.make_async_copy(yb_ref.at[pl.ds(0, WIN), :], stage_ref.at[stage_slot, pl.ds(0, WIN), :],
                                      sem.at[stage_slot]).wait()
                return carry

            def compact(s, carry):
                src = pl.multiple_of(wsrc_ref[i * N_SLOT + s], ROW_ALIGN)
                buf_ref[pl.ds(src, WIN), :] = stage_ref[stage_slot, pl.ds(pl.multiple_of(s * WIN, WIN), WIN), :]
                return carry

            lax.fori_loop(0, nwin_ref[i], drain, 0)
            lax.fori_loop(0, nwin_ref[i], compact, 0)

    ric_next = ricn_ref[...].astype(F32)
    delta_next = deltan_ref[0]
    pick = pick_ref[parity]
    nxt = jnp.zeros((tm, n_buf_rows), F32)
    y_lo, y_hi = _unpack_pairs(buf_ref[...])
    nxt = pick_round(nxt, 0, ric_next, delta_next)
    left = _dot(pick, y_lo)
    nxt = pick_round(nxt, 1, ric_next, delta_next)
    right = _dot(pick, y_hi)
    nxt = pick_round(nxt, 2, ric_next, delta_next)
    h2 = h1_ref[...] + jnp.concatenate([left, right], axis=-1)
    nxt = pick_round(nxt, 3, ric_next, delta_next)
    o_ref[...] = h2 * _rms_scale(h2) * lnf_ref[...]
    pick_ref[1 - parity] = nxt.astype(BF16)


def _combine(n_win, win_src, win_dst, ric, delta, yb, h1, lnf_w):
    n_tok = h1.shape[0]
    tm = TM_ROW
    n_tiles = n_tok // tm
    return pl.pallas_call(
        _combine_kernel,
        grid_spec=pltpu.PrefetchScalarGridSpec(
            num_scalar_prefetch=3,
            grid=(n_tiles,),
            in_specs=[
                pl.BlockSpec((tm, LANES), lambda i, *_: (i, 0)),
                pl.BlockSpec((1, 1, LANES), lambda i, *_: (i, 0, 0)),
                pl.BlockSpec((tm, LANES), lambda i, *_: (jnp.minimum(i + 1, n_tiles - 1), 0)),
                pl.BlockSpec((1, 1, LANES), lambda i, *_: (jnp.minimum(i + 1, n_tiles - 1), 0, 0)),
                pl.BlockSpec(memory_space=pl.ANY),
                pl.BlockSpec((tm, D_MODEL), lambda i, *_: (i, 0)),
                pl.BlockSpec((1, D_MODEL), lambda i, *_: (0, 0)),
            ],
            out_specs=pl.BlockSpec((tm, D_MODEL), lambda i, *_: (i, 0)),
            scratch_shapes=[pltpu.VMEM((2, N_SLOT * WIN, D_MODEL // 2), U32),
                            pltpu.VMEM((SORT_ROWS, D_MODEL // 2), U32),
                            pltpu.VMEM((2, tm, SORT_ROWS), BF16), pltpu.SemaphoreType.DMA((2,))],
        ),
        out_shape=jax.ShapeDtypeStruct((n_tok, D_MODEL), F32),
        compiler_params=pltpu.CompilerParams(
            dimension_semantics=("arbitrary",), vmem_limit_bytes=VMEM_LIMIT),
        name="combine",
    )(n_win, win_src.reshape(-1), win_dst.reshape(-1), ric, delta, ric, delta, yb, h1,
      lnf_w.reshape(1, D_MODEL))


def _layer(h, ln1_w, w_in, w_gk2, b_gk2, gla_norm_w, w_proj_a, w_proj_b, w_out,
           ln2_w, w_router, b_router, w_up, b_up, w_down, b_down, lnf_w):
    bsz, seq, _ = h.shape
    n_tok = bsz * seq

    n_main = 3 * QKV_W + 2 * GLA_KEY_DIM + 2 * GLA_VALUE_DIM
    pad = LANES - GLA_GATE_RANK
    w_main = w_in[:, :n_main].astype(BF16)
    w_tail = jnp.concatenate(
        [w_in[:, n_main:n_main + GLA_GATE_RANK], jnp.zeros((D_MODEL, pad), F32),
         w_in[:, n_main + GLA_GATE_RANK:]], axis=1).astype(BF16)
    w2_pad = jnp.concatenate([w_gk2, jnp.zeros((pad, GLA_KEY_DIM), F32)], axis=0).astype(BF16)
    wr_pad = jnp.concatenate([w_router, jnp.zeros((D_MODEL, LANES - N_EXPERTS), F32)], axis=1)
    wr_hi = wr_pad.astype(BF16)
    wr_lo = (wr_pad - wr_hi.astype(F32)).astype(BF16)
    br_pad = jnp.concatenate([b_router, jnp.zeros((LANES - N_EXPERTS,), F32)]).reshape(1, LANES)

    pa0, pa1, pa2, pg, pm = _inproj(h, ln1_w, w_main, w_tail)
    o0, l0 = _dil_attn(pa0)
    o1, l1 = _dil_attn(pa1)
    o2, l2 = _dil_attn(pa2)
    o_b = _gla(pg, w2_pad, b_gk2, gla_norm_w)

    h1, hn, ri, ric, meta, carry_f, cnt = _mix(
        h.reshape(n_tok, D_MODEL), o0.reshape(n_tok, DA_WIDTH), l0.reshape(n_tok, LANES),
        o1, l1, o2, l2, o_b.reshape(n_tok, GLA_VALUE_DIM), pm.reshape(n_tok, MERGE_W),
        w_proj_a.astype(BF16), w_proj_b.astype(BF16), w_out.astype(BF16), ln2_w, wr_hi, wr_lo, br_pad)

    i32 = jnp.int32
    n_tiles = n_tok // TM_ROW
    experts = jnp.arange(N_EXPERTS)
    counts = cnt[0, :N_EXPERTS].astype(i32)
    before = carry_f[:, 0, :N_EXPERTS].astype(i32)
    run = jnp.concatenate([before[1:], counts[None]], axis=0) - before
    run_al = (run + ROW_ALIGN - 1) // ROW_ALIGN * ROW_ALIGN
    rows_end = jnp.cumsum(run_al, axis=0)
    rows_before = rows_end - run_al
    used = rows_end[-1]
    slack = WIN - ROW_ALIGN
    padded = (used + slack + TB - 1) // TB * TB
    pad_end = jnp.cumsum(padded)
    pad_start = pad_end - padded
    n_asg = n_tok * TOP_K
    n_rows = (n_asg + n_tiles * N_EXPERTS * (ROW_ALIGN - 1)
              + N_EXPERTS * (slack + TB - 1) + TB - 1) // TB * TB
    n_blocks = n_rows // TB
    block_starts = jnp.arange(n_blocks) * TB
    block_e = jnp.minimum(jnp.sum(pad_end[None, :] <= block_starts[:, None], axis=1),
                          N_EXPERTS - 1).astype(i32)
    used_end = jnp.sum(jnp.where(block_e[:, None] == experts, (pad_start + used)[None, :], 0), axis=1)
    block_live = jnp.clip(used_end - block_starts, 0, TB).astype(i32)

    run_end = jnp.cumsum(run_al, axis=1)
    run_start = run_end - run_al
    wins = (run + WIN - 1) // WIN
    wins_end = jnp.cumsum(wins, axis=1)
    wins_start = wins_end - wins
    n_win = wins_end[:, -1].astype(i32)
    slots = jnp.arange(N_SLOT)
    slot_e = jnp.sum(wins_end[:, None, :] <= slots[None, :, None], axis=-1)
    slot_is = slot_e[..., None] == experts

    def of_slot(table):
        return jnp.sum(jnp.where(slot_is, table[:, None, :], 0), axis=-1)

    win_off = (slots[None, :] - of_slot(wins_start)) * WIN
    win_src = (of_slot(run_start) + win_off).astype(i32)
    win_dst = (of_slot(pad_start[None, :] + rows_before) + win_off).astype(i32)
    asg_is = ri[:, :TOP_K, :, None] == experts
    pos = ri[:, TOP_K:, :] + jnp.sum(jnp.where(asg_is, (run_start - before)[:, None, None, :], 0), axis=-1)
    delta = jnp.pad((run_start - before).astype(F32), ((0, 0), (0, LANES - N_EXPERTS)))

    chunk_back = ZCHUNK * (1 + jnp.arange((slack + TB - 1 + ZCHUNK - 1) // ZCHUNK + 1))
    region_chunks = pad_end[:, None] - chunk_back[None, :]
    region_ok = (region_chunks >= pad_start[:, None]) & (region_chunks + ZCHUNK > (pad_start + used)[:, None])
    tail_chunks = pad_end[-1] + ZCHUNK * jnp.arange((n_rows - n_asg) // ZCHUNK)
    zero_start = jnp.concatenate([jnp.where(region_ok, region_chunks, -1).reshape(-1),
                                  jnp.where(tail_chunks < n_rows, tail_chunks, -1)]).astype(i32)

    xin = _dispatch(zero_start, n_win, win_src, win_dst, pos.astype(i32), meta, hn, n_rows)
    yb = _experts(block_e, block_live, xin, w_up, b_up, w_down, b_down)
    out = _combine(n_win, win_src, win_dst, ric, delta.reshape(-1, 1, LANES), yb, h1, lnf_w)
    return out.reshape(bsz, seq, D_MODEL)


def kernel(x, ln1_w, w_in, w_gk2, b_gk2, gla_norm_w, w_proj_a, w_proj_b, w_out, ln2_w, w_router,
           b_router, w_up, b_up, w_down, b_down, lnf_w):
    assert x.shape[-1] == D_MODEL and ln1_w.shape[0] == 1, "one layer of width D_MODEL"
    return _layer(x, ln1_w[0], w_in[0], w_gk2[0], b_gk2[0], gla_norm_w[0], w_proj_a[0], w_proj_b[0],
                  w_out[0], ln2_w[0], w_router[0], b_router[0], w_up[0], b_up[0], w_down[0],
                  b_down[0], lnf_w)
```

```python
import jax
import jax.numpy as jnp
from jax import lax
from jax.experimental import pallas as pl
from jax.experimental.pallas import tpu as pltpu

F32 = jnp.float32
BF16 = jnp.bfloat16
U32 = jnp.uint32

D_MODEL = 1024
DA_GROUPS = ((128, 1), (512, 4), (2048, 16))
DA_HEADS = 4
DA_HEAD_DIM = 128
DA_WIDTH = DA_HEADS * DA_HEAD_DIM
DA_BLOCK = 128
GLA_HEADS = 4
GLA_KEY_DIM = D_MODEL // 2
GLA_VALUE_DIM = D_MODEL
GLA_DK = GLA_KEY_DIM // GLA_HEADS
GLA_DV = GLA_VALUE_DIM // GLA_HEADS
GLA_GATE_RANK = 16
GLA_GATE_NORMALIZER = 16.0
GLA_CHUNK = 64
N_EXPERTS = 32
TOP_K = 4
D_FF = D_MODEL
SWIGLU_ALPHA = 1.702
SWIGLU_LIMIT = 7.0
RMS_EPS = 1e-5
NEG_INF = -1e30

LANES = 128
QKV_W = 3 * DA_WIDTH
GLA_W = 2 * GLA_KEY_DIM + 2 * GLA_VALUE_DIM + LANES
MERGE_W = 2 * D_MODEL

DA_QB = 4
LSE_LANES = LANES // DA_HEADS
TM_IN = 512
N_CHUNK = 512
T_GLA = 512
GLA_SUB = 256
GLA_HEADS_PER_STEP = 4
TM_MIX = 512
TB = 512
TB_STEP = 128
TM_ROW = 256
ROW_ALIGN = 8
WIN = 32
N_SLOT = N_EXPERTS + TM_ROW * TOP_K // WIN
SORT_ROWS = TM_ROW * TOP_K + N_EXPERTS * (ROW_ALIGN - 1) + WIN
ZCHUNK = 256
ROW_W = D_MODEL // 2 + LANES
VMEM_LIMIT = 56 * 1024 * 1024
VMEM_LIMIT_INPROJ = 62 * 1024 * 1024

_NT = (((1,), (1,)), ((), ()))
_TN = (((0,), (0,)), ((), ()))


def _dot(a, b):
    return jnp.dot(a, b, preferred_element_type=F32)


def _sigmoid(x):
    return 1.0 / (1.0 + jnp.exp(-x))


def _rms_scale(x):
    return lax.rsqrt(jnp.mean(x * x, axis=-1, keepdims=True) + RMS_EPS)


def _inproj_kernel(x_ref, ln_ref, wmain_hbm, wtail_hbm, pa0_ref, pa1_ref, pa2_ref, pg_ref, pm_ref,
                   w_ref, xs_ref, xn_ref, wsem):
    tm = x_ref.shape[1]
    n_slab = D_MODEL // LANES

    @pl.when((pl.program_id(0) == 0) & (pl.program_id(1) == 0))
    def _():
        n_main = wmain_hbm.shape[1]
        main = pltpu.make_async_copy(wmain_hbm, w_ref.at[:, :n_main], wsem.at[0])
        tail = pltpu.make_async_copy(wtail_hbm, w_ref.at[:, n_main:], wsem.at[1])
        main.start()
        tail.start()
        main.wait()
        tail.wait()

    def project(out_write, col0, width, post=None):
        for c0 in range(0, width, N_CHUNK):
            cw = min(N_CHUNK, width - c0)
            val = _dot(xs_ref[...], w_ref[:, col0 + c0:col0 + c0 + cw])
            out_write(c0, cw, (val if post is None else post(val)).astype(BF16))

    x = x_ref[0]
    xn = x * _rms_scale(x) * ln_ref[...]
    xs_ref[...] = xn.astype(BF16)
    for j in range(n_slab):
        xn_ref[j] = xn[:, j * LANES:(j + 1) * LANES]

    def write_to(ref):
        def write(c0, cw, val):
            ref[0, :, c0:c0 + cw] = val
        return write

    def write_qkv(out_ref, d, n):
        def write(c0, cw, val):
            for r in range(d):
                out_ref[0, r, c0 // DA_WIDTH] = val[r * n:(r + 1) * n]
        return write

    project(write_qkv(pa0_ref, 1, tm), 0, QKV_W)
    project(write_to(pg_ref), 3 * QKV_W, GLA_W)
    project(write_to(pm_ref), 3 * QKV_W + GLA_W, MERGE_W, post=_sigmoid)

    for gi, out_ref in ((1, pa1_ref), (2, pa2_ref)):
        d = DA_GROUPS[gi][1]
        n = tm // d
        for r in range(d):
            for j in range(n_slab):
                xs_ref[r * n:(r + 1) * n, j * LANES:(j + 1) * LANES] = (
                    xn_ref[j, pl.ds(r, n, stride=d), :].astype(BF16))
        project(write_qkv(out_ref, d, n), gi * QKV_W, QKV_W)


def _inproj(x, ln1_w, w_main, w_tail):
    assert N_CHUNK == DA_WIDTH and w_main.shape[1] % LANES == 0
    bsz, seq, _ = x.shape
    tm = TM_IN
    d1, d2 = DA_GROUPS[1][1], DA_GROUPS[2][1]

    def qkv_shape(d):
        return jax.ShapeDtypeStruct((bsz, d, 3, seq // d, DA_WIDTH), BF16)

    def qkv_spec(d):
        return pl.BlockSpec((1, d, 3, tm // d, DA_WIDTH), lambda b, i: (b, 0, 0, i, 0))

    return pl.pallas_call(
        _inproj_kernel,
        grid=(bsz, seq // tm),
        in_specs=[
            pl.BlockSpec((1, tm, D_MODEL), lambda b, i: (b, i, 0)),
            pl.BlockSpec((1, D_MODEL), lambda b, i: (0, 0)),
            pl.BlockSpec(memory_space=pl.ANY),
            pl.BlockSpec(memory_space=pl.ANY),
        ],
        out_specs=(
            qkv_spec(1), qkv_spec(d1), qkv_spec(d2),
            pl.BlockSpec((1, tm, GLA_W), lambda b, i: (b, i, 0)),
            pl.BlockSpec((1, tm, MERGE_W), lambda b, i: (b, i, 0)),
        ),
        out_shape=(qkv_shape(1), qkv_shape(d1), qkv_shape(d2),
                   jax.ShapeDtypeStruct((bsz, seq, GLA_W), BF16),
                   jax.ShapeDtypeStruct((bsz, seq, MERGE_W), BF16)),
        scratch_shapes=[pltpu.VMEM((D_MODEL, w_main.shape[1] + w_tail.shape[1]), BF16),
                        pltpu.VMEM((tm, D_MODEL), BF16),
                        pltpu.VMEM((D_MODEL // LANES, tm, LANES), F32),
                        pltpu.SemaphoreType.DMA((2,))],
        compiler_params=pltpu.CompilerParams(
            dimension_semantics=("arbitrary", "arbitrary"), vmem_limit_bytes=VMEM_LIMIT_INPROJ),
        name="inproj",
    )(x, ln1_w.reshape(1, D_MODEL), w_main, w_tail)


def _dil_attn_kernel(q_ref, kp_ref, kc_ref, vp_ref, vc_ref, o_ref, l_ref, s_ref, p_ref, r_ref):
    n = pl.program_id(2)
    blk = DA_BLOCK
    qi = lax.broadcasted_iota(jnp.int32, (blk, 2 * blk), 0)
    kj = lax.broadcasted_iota(jnp.int32, (blk, 2 * blk), 1)
    band = (kj >= qi) & (kj <= qi + blk)
    band_first = (kj >= jnp.where(n > 0, qi, blk)) & (kj <= qi + blk)
    scale = DA_HEAD_DIM ** -0.5
    items = [(b, h) for b in range(DA_QB) for h in range(DA_HEADS)]

    def rows(b):
        return slice(b * blk, (b + 1) * blk)

    def cols(h):
        return slice(h * DA_HEAD_DIM, (h + 1) * DA_HEAD_DIM)

    def window(prev_ref, cur_ref, b, h):
        before = prev_ref[0, 0, 0, :, cols(h)] if b == 0 else cur_ref[0, 0, 0, rows(b - 1), cols(h)]
        return jnp.concatenate([before, cur_ref[0, 0, 0, rows(b), cols(h)]], axis=0)

    def scores(i, b, h):
        s = lax.dot_general(q_ref[0, 0, 0, rows(b), cols(h)], window(kp_ref, kc_ref, b, h), _NT,
                            preferred_element_type=F32) * scale
        s_ref[i] = jnp.where(band_first if b == 0 else band, s, NEG_INF)

    def softmax(i, b, h):
        s = s_ref[i]
        m = jnp.max(s, axis=-1, keepdims=True)
        p = jnp.exp(s - m)
        l = jnp.sum(p, axis=-1, keepdims=True)
        p_ref[i] = p.astype(BF16)
        r_ref[i] = jnp.broadcast_to(1.0 / l, (blk, DA_HEAD_DIM))
        l_ref[0, 0, rows(b), h * LSE_LANES:(h + 1) * LSE_LANES] = jnp.broadcast_to(
            m + jnp.log(l), (blk, LSE_LANES))

    def values(i, b, h):
        acc = _dot(p_ref[i], window(vp_ref, vc_ref, b, h))
        o_ref[0, 0, rows(b), cols(h)] = (acc * r_ref[i]).astype(o_ref.dtype)

    stages = (scores, softmax, values)
    for t in range(len(items) + len(stages) - 1):
        for lag, stage in enumerate(stages):
            if 0 <= t - lag < len(items):
                stage(t - lag, *items[t - lag])


def _dil_attn(pa):
    bsz, d, _, sub_len, _ = pa.shape
    rows = DA_QB * DA_BLOCK
    n_items = DA_QB * DA_HEADS

    def cur(sec):
        return pl.BlockSpec((1, 1, 1, rows, DA_WIDTH), lambda b, r, n: (b, r, sec, n, 0))

    def prev(sec):
        return pl.BlockSpec((1, 1, 1, DA_BLOCK, DA_WIDTH),
                            lambda b, r, n: (b, r, sec, jnp.maximum(n * DA_QB - 1, 0), 0))

    def out(width):
        return pl.BlockSpec((1, 1, rows, width), lambda b, r, n: (b, r, n, 0))

    return pl.pallas_call(
        _dil_attn_kernel,
        grid=(bsz, d, sub_len // rows),
        in_specs=[cur(0), prev(1), cur(1), prev(2), cur(2)],
        out_specs=(out(DA_WIDTH), out(LANES)),
        out_shape=(jax.ShapeDtypeStruct((bsz, d, sub_len, DA_WIDTH), BF16),
                   jax.ShapeDtypeStruct((bsz, d, sub_len, LANES), F32)),
        scratch_shapes=[pltpu.VMEM((n_items, DA_BLOCK, 2 * DA_BLOCK), F32),
                        pltpu.VMEM((n_items, DA_BLOCK, 2 * DA_BLOCK), BF16),
                        pltpu.VMEM((n_items, DA_BLOCK, DA_HEAD_DIM), F32)],
        compiler_params=pltpu.CompilerParams(
            dimension_semantics=("parallel", "parallel", "arbitrary"), vmem_limit_bytes=VMEM_LIMIT),
        name=f"dil_attn_d{d}",
    )(pa, pa, pa, pa, pa)


def _gla_kernel(q_ref, k_ref, v_ref, og_ref, lr_ref, w2_ref, b2_ref, nw_ref, o_ref, st_ref, mask_ref, keep_ref):
    t = pl.program_id(2)
    c = GLA_CHUNK
    tt = q_ref.shape[1]
    n_c = tt // c

    sub = mask_ref.shape[0]

    @pl.when(t == 0)
    def _():
        st_ref[...] = jnp.zeros_like(st_ref)
        row = lax.broadcasted_iota(jnp.int32, (sub, sub), 0)
        col = lax.broadcasted_iota(jnp.int32, (sub, sub), 1)
        keep = (col <= row) & (col >= row - row % c)
        keep_ref[...] = jnp.where(keep, 1.0, 0.0)
        mask_ref[...] = jnp.where(keep, 1.0, 0.0).astype(BF16)

    heads = range(q_ref.shape[2] // GLA_DK)
    mask = mask_ref[...]

    def kcols(h):
        return slice(h * GLA_DK, (h + 1) * GLA_DK)

    def vcols(h):
        return slice(h * GLA_DV, (h + 1) * GLA_DV)

    gpre = _dot(lr_ref[0], w2_ref[...]) + b2_ref[...]
    forget = (jnp.minimum(gpre, 0.0) - jnp.log(1.0 + jnp.exp(-jnp.abs(gpre)))) / GLA_GATE_NORMALIZER
    g_hi = forget.astype(BF16)
    g_lo = (forget - g_hi.astype(F32)).astype(BF16)
    b, b_last, q_e, k_e, k_end = [], [], [], [], []
    for h in heads:
        g_cat = jnp.concatenate([g_hi[:, kcols(h)], g_lo[:, kcols(h)]], axis=-1)
        csum = jnp.concatenate([_dot(mask, g_cat[s0:s0 + sub]) for s0 in range(0, tt, sub)], axis=0)
        b.append(csum[:, :GLA_DK] + csum[:, GLA_DK:])
    for h in heads:
        b_last.append(b[h].reshape(n_c, c, GLA_DK)[:, c - 1:c, :])
        b_to_end = (b_last[h] - b[h].reshape(n_c, c, GLA_DK)).reshape(tt, GLA_DK)
        q = q_ref[0, :, kcols(h)].astype(F32)
        k = k_ref[0, :, kcols(h)].astype(F32)
        q_e.append((q * ((GLA_DK ** -0.5) * jnp.exp(b[h]))).astype(BF16))
        k_e.append((k * jnp.exp(-b[h])).astype(BF16))
        k_end.append((k * jnp.exp(b_to_end)).astype(BF16))
    o_intra = []
    for h in heads:
        parts = []
        for s0 in range(0, tt, sub):
            ss = slice(s0, s0 + sub)
            att = lax.dot_general(q_e[h][ss], k_e[h][ss], _NT, preferred_element_type=F32)
            att = jnp.where(keep_ref[...] > 0.0, att, 0.0).astype(BF16)
            parts.append(_dot(att, v_ref[0, ss, vcols(h)]))
        o_intra.append(jnp.concatenate(parts, axis=0))
    decay = [jnp.exp(b_last[h].reshape(n_c, GLA_DK).T) for h in heads]
    st = [st_ref[h] for h in heads]
    outs = [[] for _ in heads]
    for ci in range(n_c):
        rs = slice(ci * c, (ci + 1) * c)
        for h in heads:
            outs[h].append(o_intra[h][rs] + _dot(q_e[h][rs], st[h].astype(BF16)))
            st[h] = decay[h][:, ci:ci + 1] * st[h] + lax.dot_general(
                k_end[h][rs], v_ref[0, rs, vcols(h)], _TN, preferred_element_type=F32)
    for h in heads:
        st_ref[h] = st[h]
        o = jnp.concatenate(outs[h], axis=0)
        o = o * _rms_scale(o) * nw_ref[...]
        gate = og_ref[0, :, vcols(h)].astype(F32)
        o_ref[0, :, vcols(h)] = (o * (gate * _sigmoid(gate))).astype(BF16)


def _gla(pg, w2_pad, b_gk2, gla_norm_w):
    bsz, seq, _ = pg.shape
    t = T_GLA
    hps = GLA_HEADS_PER_STEP
    wk, wv = hps * GLA_DK, hps * GLA_DV
    kq = GLA_KEY_DIM // wk
    kv = 2 * GLA_KEY_DIM // wv
    kg = kv + GLA_VALUE_DIM // wv
    klr = (2 * GLA_KEY_DIM + 2 * GLA_VALUE_DIM) // LANES
    return pl.pallas_call(
        _gla_kernel,
        grid=(bsz, GLA_HEADS // hps, seq // t),
        in_specs=[
            pl.BlockSpec((1, t, wk), lambda b, h, i: (b, i, h)),
            pl.BlockSpec((1, t, wk), lambda b, h, i: (b, i, kq + h)),
            pl.BlockSpec((1, t, wv), lambda b, h, i: (b, i, kv + h)),
            pl.BlockSpec((1, t, wv), lambda b, h, i: (b, i, kg + h)),
            pl.BlockSpec((1, t, LANES), lambda b, h, i: (b, i, klr)),
            pl.BlockSpec((LANES, wk), lambda b, h, i: (0, h)),
            pl.BlockSpec((1, wk), lambda b, h, i: (0, h)),
            pl.BlockSpec((1, GLA_DV), lambda b, h, i: (0, 0)),
        ],
        out_specs=pl.BlockSpec((1, t, wv), lambda b, h, i: (b, i, h)),
        out_shape=jax.ShapeDtypeStruct((bsz, seq, GLA_VALUE_DIM), BF16),
        scratch_shapes=[pltpu.VMEM((hps, GLA_DK, GLA_DV), F32), pltpu.VMEM((GLA_SUB, GLA_SUB), BF16),
                        pltpu.VMEM((GLA_SUB, GLA_SUB), F32)],
        compiler_params=pltpu.CompilerParams(
            dimension_semantics=("parallel", "parallel", "arbitrary"), vmem_limit_bytes=VMEM_LIMIT),
        name="gla",
    )(pg, pg, pg, pg, pg, w2_pad, b_gk2.reshape(1, GLA_KEY_DIM), gla_norm_w.reshape(1, GLA_DV))


def _mix_kernel(x_ref, o0_ref, l0_ref, o1_ref, l1_ref, o2_ref, l2_ref, ob_ref, pm_ref,
                wa_ref, wb_ref, wo_ref, ln2_ref, wrh_ref, wrl_ref, br_ref,
                h1_ref, hn_ref, ri_ref, ric_ref, meta_ref, cbefore_ref, cnt_ref,
                po1_ref, pl1_ref, po2_ref, pl2_ref, carry_ref, logit_ref):
    step = pl.program_id(0)
    tm = x_ref.shape[0]

    @pl.when(step == 0)
    def _():
        carry_ref[...] = jnp.zeros_like(carry_ref)
        logit_ref[...] = jnp.zeros_like(logit_ref)

    routed = step > 0
    lane = lax.broadcasted_iota(jnp.int32, (tm, LANES), 1).astype(F32)
    route = {"work": None, "vals": [], "idxs": []}

    def topk_round():
        work = route["work"]
        m = jnp.max(work, axis=-1, keepdims=True)
        idx = jnp.min(jnp.where(work == m, lane, float(LANES)), axis=-1, keepdims=True)
        route["vals"].append(m)
        route["idxs"].append(idx)
        route["work"] = jnp.where(lane == idx, -jnp.inf, work)

    route["work"] = jnp.where(lane < N_EXPERTS, logit_ref[(step + 1) % 2], -jnp.inf)

    for o_ref, l_ref, po_ref, pl_ref, (_, d) in ((o1_ref, l1_ref, po1_ref, pl1_ref, DA_GROUPS[1]),
                                                 (o2_ref, l2_ref, po2_ref, pl2_ref, DA_GROUPS[2])):
        topk_round()
        n = tm // d
        for r in range(d):
            pl_ref[pl.ds(r, n, stride=d), :] = l_ref[0, r]
            for h in range(DA_HEADS):
                sl = slice(h * DA_HEAD_DIM, (h + 1) * DA_HEAD_DIM)
                po_ref[h, pl.ds(r, n, stride=d), :] = o_ref[0, r, :, sl].astype(F32)

    l0, l1, l2 = l0_ref[...], pl1_ref[...], pl2_ref[...]
    mx = jnp.maximum(jnp.maximum(l0, l1), l2)
    e0, e1, e2 = jnp.exp(l0 - mx), jnp.exp(l1 - mx), jnp.exp(l2 - mx)
    inv = 1.0 / (e0 + e1 + e2)
    w0, w1, w2 = e0 * inv, e1 * inv, e2 * inv
    topk_round()
    heads = []
    for h in range(DA_HEADS):
        sl = slice(h * DA_HEAD_DIM, (h + 1) * DA_HEAD_DIM)
        at = slice(h * LSE_LANES, h * LSE_LANES + 1)
        o_h = w0[:, at] * o0_ref[:, sl].astype(F32) + w1[:, at] * po1_ref[h] + w2[:, at] * po2_ref[h]
        heads.append(o_h.astype(BF16))
    o_a = jnp.concatenate(heads, axis=-1)
    topk_round()

    gates = pm_ref[...].astype(F32)
    mixed = (gates[:, :D_MODEL] * _dot(o_a, wa_ref[...])
             + gates[:, D_MODEL:] * _dot(ob_ref[...], wb_ref[...]))
    vals, idxs = route["vals"], route["idxs"]
    assert len(vals) == TOP_K
    exps = [jnp.exp(v - vals[0]) for v in vals]
    denom = exps[0] + exps[1] + exps[2] + exps[3]
    onehot = jnp.zeros((tm, LANES), F32)
    for idx in idxs:
        onehot = onehot + jnp.where(lane == idx, 1.0, 0.0)

    h1 = x_ref[...] + _dot(mixed.astype(BF16), wo_ref[...])
    h1_ref[...] = h1
    hn = h1 * _rms_scale(h1) * ln2_ref[...]

    row = lax.broadcasted_iota(jnp.int32, (tm, tm), 0)
    col = lax.broadcasted_iota(jnp.int32, (tm, tm), 1)
    below = jnp.where(col < row, 1.0, 0.0).astype(BF16)
    before = _dot(below, onehot.astype(BF16)) + carry_ref[0:1, :]
    ranks = [jnp.sum(jnp.where(lane == idx, before, 0.0), axis=-1, keepdims=True) for idx in idxs]
    running = carry_ref[0:1, :]
    for j in range(tm // TM_ROW):
        cbefore_ref[j] = jnp.broadcast_to(running, cbefore_ref.shape[1:])
        running = running + jnp.sum(onehot[j * TM_ROW:(j + 1) * TM_ROW], axis=0, keepdims=True)
    carry = jnp.where(routed, running, carry_ref[0:1, :])
    carry_ref[...] = jnp.broadcast_to(carry, carry_ref.shape)
    cnt_ref[...] = jnp.broadcast_to(carry, cnt_ref.shape)

    hn_hi = hn.astype(BF16)
    hn_ref[...] = hn_hi

    ri = jnp.zeros((tm, LANES), F32)
    for j, val in enumerate(idxs + ranks):
        ri = jnp.where(lane == float(j), val, ri)
    ric_ref[...] = ri.astype(jnp.int32)
    ri_t = ri.T[:2 * TOP_K].astype(jnp.int32)
    for j in range(tm // TM_ROW):
        ri_ref[j] = ri_t[:, j * TM_ROW:(j + 1) * TM_ROW]
    meta = jnp.zeros((tm, LANES), F32)
    for k in range(TOP_K):
        gate = exps[k] / denom
        gate_hi = gate.astype(BF16).astype(F32)
        meta = jnp.where(lane == float(k), idxs[k], meta)
        meta = jnp.where(lane == float(TOP_K + k), gate_hi, meta)
        meta = jnp.where(lane == float(2 * TOP_K + k), gate - gate_hi, meta)
    meta_ref[...] = meta.astype(BF16)

    hn_lo = (hn - hn_hi.astype(F32)).astype(BF16)
    logit_ref[step % 2] = (_dot(hn_hi, wrh_ref[...]) + _dot(hn_lo, wrh_ref[...])
                           + _dot(hn_hi, wrl_ref[...]) + br_ref[...])


def _mix(x2, o0, l0, o1, l1, o2, l2, o_b, pm, wa, wb, wo, ln2_w, wr_hi, wr_lo, br_pad):
    n_tok = x2.shape[0]
    tm = TM_MIX
    bsz = o1.shape[0]
    d1, d2 = DA_GROUPS[1][1], DA_GROUPS[2][1]
    tiles_per_seq = (n_tok // bsz) // tm
    n_tiles = n_tok // tm

    def tile(i):
        return jnp.minimum(i, n_tiles - 1)

    def routed(i):
        return jnp.maximum(i - 1, 0)

    def rows(width, which=tile):
        return pl.BlockSpec((tm, width), lambda i: (which(i), 0))

    def residue_major(d, width):
        return pl.BlockSpec((1, d, tm // d, width),
                            lambda i: (tile(i) // tiles_per_seq, 0, tile(i) % tiles_per_seq, 0))

    def whole(arr):
        return pl.BlockSpec(arr.shape, lambda i: (0,) * arr.ndim)

    ln2 = ln2_w.reshape(1, D_MODEL)
    return pl.pallas_call(
        _mix_kernel,
        grid=(n_tiles + 1,),
        in_specs=[rows(D_MODEL), rows(DA_WIDTH), rows(LANES),
                  residue_major(d1, DA_WIDTH), residue_major(d1, LANES),
                  residue_major(d2, DA_WIDTH), residue_major(d2, LANES),
                  rows(GLA_VALUE_DIM), rows(MERGE_W),
                  whole(wa), whole(wb), whole(wo), whole(ln2), whole(wr_hi), whole(wr_lo), whole(br_pad)],
        out_specs=(rows(D_MODEL), rows(D_MODEL),
                   pl.BlockSpec((tm // TM_ROW, 2 * TOP_K, TM_ROW), lambda i: (routed(i), 0, 0)),
                   rows(LANES, routed), rows(LANES, routed),
                   pl.BlockSpec((tm // TM_ROW, 8, LANES), lambda i: (routed(i), 0, 0)),
                   pl.BlockSpec((8, LANES), lambda i: (0, 0))),
        out_shape=(jax.ShapeDtypeStruct((n_tok, D_MODEL), F32),
                   jax.ShapeDtypeStruct((n_tok, D_MODEL), BF16),
                   jax.ShapeDtypeStruct((n_tok // TM_ROW, 2 * TOP_K, TM_ROW), jnp.int32),
                   jax.ShapeDtypeStruct((n_tok, LANES), jnp.int32),
                   jax.ShapeDtypeStruct((n_tok, LANES), BF16),
                   jax.ShapeDtypeStruct((n_tok // TM_ROW, 8, LANES), F32),
                   jax.ShapeDtypeStruct((8, LANES), F32)),
        scratch_shapes=[pltpu.VMEM((DA_HEADS, tm, DA_HEAD_DIM), F32), pltpu.VMEM((tm, LANES), F32),
                        pltpu.VMEM((DA_HEADS, tm, DA_HEAD_DIM), F32), pltpu.VMEM((tm, LANES), F32),
                        pltpu.VMEM((8, LANES), F32), pltpu.VMEM((2, tm, LANES), F32)],
        compiler_params=pltpu.CompilerParams(
            dimension_semantics=("arbitrary",), vmem_limit_bytes=VMEM_LIMIT),
        name="mix",
    )(x2, o0, l0, o1, l1, o2, l2, o_b, pm, wa, wb, wo, ln2, wr_hi, wr_lo, br_pad)


def _pack_pairs(x):
    n = x.shape[1] // 2
    rounded = x.astype(BF16).astype(F32)
    lo = lax.bitcast_convert_type(rounded[:, :n], U32) >> 16
    hi = lax.bitcast_convert_type(rounded[:, n:], U32) & jnp.uint32(0xFFFF0000)
    return hi | lo


def _unpack_pairs(u):
    lo = lax.bitcast_convert_type(u << 16, F32).astype(BF16)
    hi = lax.bitcast_convert_type(u & jnp.uint32(0xFFFF0000), F32).astype(BF16)
    return lo, hi


def _dispatch_kernel(zstart_ref, nwin_ref, wsrc_ref, wdst_ref, pos_ref, meta_ref, hn_ref, xin_ref,
                     buf_ref, zero_ref, sem, zsem):
    i = pl.program_id(0)
    tm = hn_ref.shape[0]
    n_buf_rows = buf_ref.shape[1]
    slot = i % 2

    def window_copy(s, buf_slot):
        src = pl.multiple_of(wsrc_ref[i * N_SLOT + s], ROW_ALIGN)
        dst = pl.multiple_of(wdst_ref[i * N_SLOT + s], ROW_ALIGN)
        return pltpu.make_async_copy(buf_ref.at[buf_slot, pl.ds(src, WIN), :],
                                     xin_ref.at[pl.ds(dst, WIN), :], sem)

    def wait_windows(step):
        def body(s, carry):
            pltpu.make_async_copy(buf_ref.at[0, pl.ds(0, WIN), :], xin_ref.at[pl.ds(0, WIN), :], sem).wait()
            return carry
        lax.fori_loop(0, nwin_ref[step], body, 0)

    @pl.when(i == 0)
    def _():
        zero_ref[...] = jnp.zeros_like(zero_ref)

        def zero_copy(j):
            start = pl.multiple_of(jnp.maximum(zstart_ref[j], 0), ZCHUNK)
            return pltpu.make_async_copy(zero_ref, xin_ref.at[pl.ds(start, ZCHUNK), :], zsem)

        def start_one(j, carry):
            @pl.when(zstart_ref[j] >= 0)
            def _():
                zero_copy(j).start()
            return carry

        def wait_one(j, carry):
            @pl.when(zstart_ref[j] >= 0)
            def _():
                zero_copy(j).wait()
            return carry

        lax.fori_loop(0, zstart_ref.shape[0], start_one, 0)
        lax.fori_loop(0, zstart_ref.shape[0], wait_one, 0)

    row = lax.broadcasted_iota(jnp.int32, (n_buf_rows, tm), 0)

    perm = jnp.zeros((n_buf_rows, tm), F32)
    for k in range(TOP_K):
        perm = perm + jnp.where(row == pos_ref[0, k:k + 1, :], 1.0, 0.0)
    perm = perm.astype(BF16)
    buf_ref[slot, :, :D_MODEL // 2] = _pack_pairs(_dot(perm, hn_ref[...]))
    buf_ref[slot, :, D_MODEL // 2:] = lax.bitcast_convert_type(_dot(perm, meta_ref[...]), U32)

    @pl.when(i > 0)
    def _():
        wait_windows(i - 1)

    for buf_slot in range(2):
        @pl.when(slot == buf_slot)
        def _():
            def issue(s, carry):
                window_copy(s, buf_slot).start()
                return carry
            lax.fori_loop(0, nwin_ref[i], issue, 0)

    @pl.when(i == pl.num_programs(0) - 1)
    def _():
        wait_windows(i)


def _dispatch(zero_start, n_win, win_src, win_dst, pos, meta, hn, n_rows):
    n_tok = hn.shape[0]
    tm = TM_ROW
    n_tiles = n_tok // tm
    n_buf_rows = SORT_ROWS
    return pl.pallas_call(
        _dispatch_kernel,
        grid_spec=pltpu.PrefetchScalarGridSpec(
            num_scalar_prefetch=4,
            grid=(n_tiles,),
            in_specs=[
                pl.BlockSpec((1, TOP_K, tm), lambda i, *_: (i, 0, 0)),
                pl.BlockSpec((tm, LANES), lambda i, *_: (i, 0)),
                pl.BlockSpec((tm, D_MODEL), lambda i, *_: (i, 0)),
            ],
            out_specs=pl.BlockSpec(memory_space=pl.ANY),
            scratch_shapes=[pltpu.VMEM((2, n_buf_rows, ROW_W), U32),
                            pltpu.VMEM((ZCHUNK, ROW_W), U32),
                            pltpu.SemaphoreType.DMA(()), pltpu.SemaphoreType.DMA(())],
        ),
        out_shape=jax.ShapeDtypeStruct((n_rows, ROW_W), U32),
        compiler_params=pltpu.CompilerParams(
            dimension_semantics=("arbitrary",), vmem_limit_bytes=VMEM_LIMIT),
        name="dispatch",
    )(zero_start, n_win, win_src.reshape(-1), win_dst.reshape(-1), pos, meta, hn)


def _expert_kernel(be_ref, live_ref, x_ref, wu_hbm, bu_ref, wd_hbm, bd_ref, y_ref,
                   wu32_ref, wd32_ref, wu16_ref, wd16_ref, wsem):
    i = pl.program_id(0)
    live = live_ref[i]
    expert_id = be_ref[i]

    def weight_copies(expert, slot):
        return (pltpu.make_async_copy(wu_hbm.at[expert], wu32_ref.at[slot], wsem.at[0, slot]),
                pltpu.make_async_copy(wd_hbm.at[expert], wd32_ref.at[slot], wsem.at[1, slot]))

    @pl.when(i == 0)
    def _():
        for cp in weight_copies(expert_id, 0):
            cp.start()

    for slot in range(2):
        @pl.when(((i == 0) | (expert_id != be_ref[jnp.maximum(i - 1, 0)])) & (expert_id % 2 == slot))
        def _():
            for cp in weight_copies(expert_id, slot):
                cp.wait()

            @pl.when(expert_id + 1 < N_EXPERTS)
            def _():
                for cp in weight_copies(expert_id + 1, 1 - slot):
                    cp.start()

            wu16_ref[...] = wu32_ref[slot].astype(BF16)
            wd16_ref[...] = wd32_ref[slot].astype(BF16)

    def compute(m):
        half = D_MODEL // 2
        x_lo, x_hi = _unpack_pairs(x_ref[:m, :half])
        meta = lax.bitcast_convert_type(x_ref[:m, half:half + 3 * TOP_K], F32)
        expert = be_ref[i].astype(F32)
        gate = jnp.zeros((m, 1), F32)
        for k in range(TOP_K):
            weight = meta[:, TOP_K + k:TOP_K + k + 1] + meta[:, 2 * TOP_K + k:2 * TOP_K + k + 1]
            gate = gate + jnp.where(meta[:, k:k + 1] == expert, weight, 0.0)
        hu = _dot(x_lo, wu16_ref[:half, :]) + _dot(x_hi, wu16_ref[half:, :]) + bu_ref[0]
        x_glu = jnp.minimum(hu[:, :D_FF], SWIGLU_LIMIT)
        x_lin = jnp.clip(hu[:, D_FF:], -SWIGLU_LIMIT, SWIGLU_LIMIT)
        act = x_glu * _sigmoid(SWIGLU_ALPHA * x_glu) * (x_lin + 1.0)
        y_ref[:m, :] = _pack_pairs((_dot(act.astype(BF16), wd16_ref[...]) + bd_ref[0]) * gate)
        if m < TB:
            y_ref[m:, :] = jnp.zeros((TB - m, y_ref.shape[1]), y_ref.dtype)

    @pl.when(live == 0)
    def _():
        y_ref[...] = jnp.zeros_like(y_ref)

    for m in range(TB_STEP, TB + 1, TB_STEP):
        @pl.when((live > m - TB_STEP) & (live <= m))
        def _():
            compute(m)


def _experts(block_e, block_live, xin, w_up, b_up, w_down, b_down):
    n_rows = xin.shape[0]
    return pl.pallas_call(
        _expert_kernel,
        grid_spec=pltpu.PrefetchScalarGridSpec(
            num_scalar_prefetch=2,
            grid=(n_rows // TB,),
            in_specs=[
                pl.BlockSpec((TB, ROW_W), lambda i, be, nu: (i, 0)),
                pl.BlockSpec(memory_space=pl.ANY),
                pl.BlockSpec((1, 1, 2 * D_FF), lambda i, be, nu: (be[i], 0, 0)),
                pl.BlockSpec(memory_space=pl.ANY),
                pl.BlockSpec((1, 1, D_MODEL), lambda i, be, nu: (be[i], 0, 0)),
            ],
            out_specs=pl.BlockSpec((TB, D_MODEL // 2), lambda i, be, nu: (i, 0)),
            scratch_shapes=[pltpu.VMEM((2, D_MODEL, 2 * D_FF), F32), pltpu.VMEM((2, D_FF, D_MODEL), F32),
                            pltpu.VMEM((D_MODEL, 2 * D_FF), BF16), pltpu.VMEM((D_FF, D_MODEL), BF16),
                            pltpu.SemaphoreType.DMA((2, 2))],
        ),
        out_shape=jax.ShapeDtypeStruct((n_rows, D_MODEL // 2), U32),
        compiler_params=pltpu.CompilerParams(
            dimension_semantics=("arbitrary",), vmem_limit_bytes=VMEM_LIMIT),
        name="experts",
    )(block_e, block_live, xin, w_up, b_up.reshape(N_EXPERTS, 1, 2 * D_FF),
      w_down, b_down.reshape(N_EXPERTS, 1, D_MODEL))


def _combine_kernel(nwin_ref, wsrc_ref, wdst_ref, ric_ref, delta_ref, ricn_ref, deltan_ref, yb_ref, h1_ref,
                    lnf_ref, o_ref, stage_ref, buf_ref, pick_ref, sem):
    i = pl.program_id(0)
    tm = h1_ref.shape[0]
    n_buf_rows = buf_ref.shape[0]

    parity = i % 2

    def fetch(step, stage_slot):
        def body(s, carry):
            dst = pl.multiple_of(wdst_ref[step * N_SLOT + s], ROW_ALIGN)
            pltpu.make_async_copy(
                yb_ref.at[pl.ds(dst, WIN), :],
                stage_ref.at[stage_slot, pl.ds(pl.multiple_of(s * WIN, WIN), WIN), :],
                sem.at[stage_slot]).start()
            return carry
        lax.fori_loop(0, nwin_ref[step], body, 0)

    lane = lax.broadcasted_iota(jnp.int32, (tm, LANES), 1).astype(F32)
    col = lax.broadcasted_iota(jnp.int32, (tm, n_buf_rows), 1)

    def pick_round(pick, k, ric, delta):
        offset = jnp.sum(jnp.where(lane == ric[:, k:k + 1], delta, 0.0), axis=-1, keepdims=True)
        pos = (ric[:, TOP_K + k:TOP_K + k + 1] + offset).astype(jnp.int32)
        return pick + jnp.where(col == pos, 1.0, 0.0)

    @pl.when(i == 0)
    def _():
        buf_ref[...] = jnp.zeros_like(buf_ref)
        fetch(i, 0)
        pick = jnp.zeros((tm, n_buf_rows), F32)
        ric = ric_ref[...].astype(F32)
        for k in range(TOP_K):
            pick = pick_round(pick, k, ric, delta_ref[0])
        pick_ref[0] = pick.astype(BF16)

    for stage_slot in range(2):
        @pl.when((i + 1 < pl.num_programs(0)) & (parity != stage_slot))
        def _():
            fetch(i + 1, stage_slot)

    for stage_slot in range(2):
        @pl.when(parity == stage_slot)
        def _():
            def drain(s, carry):
                pltpu.make_async_copy(yb_ref.at[pl.ds(0, WIN), :], stage_ref.at[stage_slot, pl.ds(0, WIN), :],
                                      sem.at[stage_slot]).wait()
                return carry

            def compact(s, carry):
                src = pl.multiple_of(wsrc_ref[i * N_SLOT + s], ROW_ALIGN)
                buf_ref[pl.ds(src, WIN), :] = stage_ref[stage_slot, pl.ds(pl.multiple_of(s * WIN, WIN), WIN), :]
                return carry

            lax.fori_loop(0, nwin_ref[i], drain, 0)
            lax.fori_loop(0, nwin_ref[i], compact, 0)

    ric_next = ricn_ref[...].astype(F32)
    delta_next = deltan_ref[0]
    pick = pick_ref[parity]
    nxt = jnp.zeros((tm, n_buf_rows), F32)
    y_lo, y_hi = _unpack_pairs(buf_ref[...])
    nxt = pick_round(nxt, 0, ric_next, delta_next)
    left = _dot(pick, y_lo)
    nxt = pick_round(nxt, 1, ric_next, delta_next)
    right = _dot(pick, y_hi)
    nxt = pick_round(nxt, 2, ric_next, delta_next)
    h2 = h1_ref[...] + jnp.concatenate([left, right], axis=-1)
    nxt = pick_round(nxt, 3, ric_next, delta_next)
    o_ref[...] = h2 * _rms_scale(h2) * lnf_ref[...]
    pick_ref[1 - parity] = nxt.astype(BF16)


def _combine(n_win, win_src, win_dst, ric, delta, yb, h1, lnf_w):
    n_tok = h1.shape[0]
    tm = TM_ROW
    n_tiles = n_tok // tm
    return pl.pallas_call(
        _combine_kernel,
        grid_spec=pltpu.PrefetchScalarGridSpec(
            num_scalar_prefetch=3,
            grid=(n_tiles,),
            in_specs=[
                pl.BlockSpec((tm, LANES), lambda i, *_: (i, 0)),
                pl.BlockSpec((1, 1, LANES), lambda i, *_: (i, 0, 0)),
                pl.BlockSpec((tm, LANES), lambda i, *_: (jnp.minimum(i + 1, n_tiles - 1), 0)),
                pl.BlockSpec((1, 1, LANES), lambda i, *_: (jnp.minimum(i + 1, n_tiles - 1), 0, 0)),
                pl.BlockSpec(memory_space=pl.ANY),
                pl.BlockSpec((tm, D_MODEL), lambda i, *_: (i, 0)),
                pl.BlockSpec((1, D_MODEL), lambda i, *_: (0, 0)),
            ],
            out_specs=pl.BlockSpec((tm, D_MODEL), lambda i, *_: (i, 0)),
            scratch_shapes=[pltpu.VMEM((2, N_SLOT * WIN, D_MODEL // 2), U32),
                            pltpu.VMEM((SORT_ROWS, D_MODEL // 2), U32),
                            pltpu.VMEM((2, tm, SORT_ROWS), BF16), pltpu.SemaphoreType.DMA((2,))],
        ),
        out_shape=jax.ShapeDtypeStruct((n_tok, D_MODEL), F32),
        compiler_params=pltpu.CompilerParams(
            dimension_semantics=("arbitrary",), vmem_limit_bytes=VMEM_LIMIT),
        name="combine",
    )(n_win, win_src.reshape(-1), win_dst.reshape(-1), ric, delta, ric, delta, yb, h1,
      lnf_w.reshape(1, D_MODEL))


def _layer(h, ln1_w, w_in, w_gk2, b_gk2, gla_norm_w, w_proj_a, w_proj_b, w_out,
           ln2_w, w_router, b_router, w_up, b_up, w_down, b_down, lnf_w):
    bsz, seq, _ = h.shape
    n_tok = bsz * seq

    n_main = 3 * QKV_W + 2 * GLA_KEY_DIM + 2 * GLA_VALUE_DIM
    pad = LANES - GLA_GATE_RANK
    w_main = w_in[:, :n_main].astype(BF16)
    w_tail = jnp.concatenate(
        [w_in[:, n_main:n_main + GLA_GATE_RANK], jnp.zeros((D_MODEL, pad), F32),
         w_in[:, n_main + GLA_GATE_RANK:]], axis=1).astype(BF16)
    w2_pad = jnp.concatenate([w_gk2, jnp.zeros((pad, GLA_KEY_DIM), F32)], axis=0).astype(BF16)
    wr_pad = jnp.concatenate([w_router, jnp.zeros((D_MODEL, LANES - N_EXPERTS), F32)], axis=1)
    wr_hi = wr_pad.astype(BF16)
    wr_lo = (wr_pad - wr_hi.astype(F32)).astype(BF16)
    br_pad = jnp.concatenate([b_router, jnp.zeros((LANES - N_EXPERTS,), F32)]).reshape(1, LANES)

    pa0, pa1, pa2, pg, pm = _inproj(h, ln1_w, w_main, w_tail)
    o0, l0 = _dil_attn(pa0)
    o1, l1 = _dil_attn(pa1)
    o2, l2 = _dil_attn(pa2)
    o_b = _gla(pg, w2_pad, b_gk2, gla_norm_w)

    h1, hn, ri, ric, meta, carry_f, cnt = _mix(
        h.reshape(n_tok, D_MODEL), o0.reshape(n_tok, DA_WIDTH), l0.reshape(n_tok, LANES),
        o1, l1, o2, l2, o_b.reshape(n_tok, GLA_VALUE_DIM), pm.reshape(n_tok, MERGE_W),
        w_proj_a.astype(BF16), w_proj_b.astype(BF16), w_out.astype(BF16), ln2_w, wr_hi, wr_lo, br_pad)

    i32 = jnp.int32
    n_tiles = n_tok // TM_ROW
    experts = jnp.arange(N_EXPERTS)
    counts = cnt[0, :N_EXPERTS].astype(i32)
    before = carry_f[:, 0, :N_EXPERTS].astype(i32)
    run = jnp.concatenate([before[1:], counts[None]], axis=0) - before
    run_al = (run + ROW_ALIGN - 1) // ROW_ALIGN * ROW_ALIGN
    rows_end = jnp.cumsum(run_al, axis=0)
    rows_before = rows_end - run_al
    used = rows_end[-1]
    slack = WIN - ROW_ALIGN
    padded = (used + slack + TB - 1) // TB * TB
    pad_end = jnp.cumsum(padded)
    pad_start = pad_end - padded
    n_asg = n_tok * TOP_K
    n_rows = (n_asg + n_tiles * N_EXPERTS * (ROW_ALIGN - 1)
              + N_EXPERTS * (slack + TB - 1) + TB - 1) // TB * TB
    n_blocks = n_rows // TB
    block_starts = jnp.arange(n_blocks) * TB
    block_e = jnp.minimum(jnp.sum(pad_end[None, :] <= block_starts[:, None], axis=1),
                          N_EXPERTS - 1).astype(i32)
    used_end = jnp.sum(jnp.where(block_e[:, None] == experts, (pad_start + used)[None, :], 0), axis=1)
    block_live = jnp.clip(used_end - block_starts, 0, TB).astype(i32)

    run_end = jnp.cumsum(run_al, axis=1)
    run_start = run_end - run_al
    wins = (run + WIN - 1) // WIN
    wins_end = jnp.cumsum(wins, axis=1)
    wins_start = wins_end - wins
    n_win = wins_end[:, -1].astype(i32)
    slots = jnp.arange(N_SLOT)
    slot_e = jnp.sum(wins_end[:, None, :] <= slots[None, :, None], axis=-1)
    slot_is = slot_e[..., None] == experts

    def of_slot(table):
        return jnp.sum(jnp.where(slot_is, table[:, None, :], 0), axis=-1)

    win_off = (slots[None, :] - of_slot(wins_start)) * WIN
    win_src = (of_slot(run_start) + win_off).astype(i32)
    win_dst = (of_slot(pad_start[None, :] + rows_before) + win_off).astype(i32)
    asg_is = ri[:, :TOP_K, :, None] == experts
    pos = ri[:, TOP_K:, :] + jnp.sum(jnp.where(asg_is, (run_start - before)[:, None, None, :], 0), axis=-1)
    delta = jnp.pad((run_start - before).astype(F32), ((0, 0), (0, LANES - N_EXPERTS)))

    chunk_back = ZCHUNK * (1 + jnp.arange((slack + TB - 1 + ZCHUNK - 1) // ZCHUNK + 1))
    region_chunks = pad_end[:, None] - chunk_back[None, :]
    region_ok = (region_chunks >= pad_start[:, None]) & (region_chunks + ZCHUNK > (pad_start + used)[:, None])
    tail_chunks = pad_end[-1] + ZCHUNK * jnp.arange((n_rows - n_asg) // ZCHUNK)
    zero_start = jnp.concatenate([jnp.where(region_ok, region_chunks, -1).reshape(-1),
                                  jnp.where(tail_chunks < n_rows, tail_chunks, -1)]).astype(i32)

    xin = _dispatch(zero_start, n_win, win_src, win_dst, pos.astype(i32), meta, hn, n_rows)
    yb = _experts(block_e, block_live, xin, w_up, b_up, w_down, b_down)
    out = _combine(n_win, win_src, win_dst, ric, delta.reshape(-1, 1, LANES), yb, h1, lnf_w)
    return out.reshape(bsz, seq, D_MODEL)


def kernel(x, ln1_w, w_in, w_gk2, b_gk2, gla_norm_w, w_proj_a, w_proj_b, w_out, ln2_w, w_router,
           b_router, w_up, b_up, w_down, b_down, lnf_w):
    assert x.shape[-1] == D_MODEL and ln1_w.shape[0] == 1, "one layer of width D_MODEL"
    return _layer(x, ln1_w[0], w_in[0], w_gk2[0], b_gk2[0], gla_norm_w[0], w_proj_a[0], w_proj_b[0],
                  w_out[0], ln2_w[0], w_router[0], b_router[0], w_up[0], b_up[0], w_down[0],
                  b_down[0], lnf_w)
```

```python
import jax
import jax.numpy as jnp
from jax import lax
from jax.experimental import pallas as pl
from jax.experimental.pallas import tpu as pltpu

F32 = jnp.float32
BF16 = jnp.bfloat16
U32 = jnp.uint32

D_MODEL = 1024
DA_GROUPS = ((128, 1), (512, 4), (2048, 16))
DA_HEADS = 4
DA_HEAD_DIM = 128
DA_WIDTH = DA_HEADS * DA_HEAD_DIM
DA_BLOCK = 128
GLA_HEADS = 4
GLA_KEY_DIM = D_MODEL // 2
GLA_VALUE_DIM = D_MODEL
GLA_DK = GLA_KEY_DIM // GLA_HEADS
GLA_DV = GLA_VALUE_DIM // GLA_HEADS
GLA_GATE_RANK = 16
GLA_GATE_NORMALIZER = 16.0
GLA_CHUNK = 64
N_EXPERTS = 32
TOP_K = 4
D_FF = D_MODEL
SWIGLU_ALPHA = 1.702
SWIGLU_LIMIT = 7.0
RMS_EPS = 1e-5
NEG_INF = -1e30

LANES = 128
QKV_W = 3 * DA_WIDTH
GLA_W = 2 * GLA_KEY_DIM + 2 * GLA_VALUE_DIM + LANES
MERGE_W = 2 * D_MODEL

DA_QB = 4
LSE_LANES = LANES // DA_HEADS
TM_IN = 512
N_CHUNK = 512
T_GLA = 512
GLA_SUB = 256
GLA_HEADS_PER_STEP = 4
TM_MIX = 512
TB = 512
TB_STEP = 128
TM_ROW = 256
ROW_ALIGN = 8
WIN = 32
N_SLOT = N_EXPERTS + TM_ROW * TOP_K // WIN
SORT_ROWS = TM_ROW * TOP_K + N_EXPERTS * (ROW_ALIGN - 1) + WIN
ZCHUNK = 256
ROW_W = D_MODEL // 2 + LANES
VMEM_LIMIT = 56 * 1024 * 1024
VMEM_LIMIT_INPROJ = 62 * 1024 * 1024

_NT = (((1,), (1,)), ((), ()))
_TN = (((0,), (0,)), ((), ()))


def _dot(a, b):
    return jnp.dot(a, b, preferred_element_type=F32)


def _sigmoid(x):
    return 1.0 / (1.0 + jnp.exp(-x))


def _rms_scale(x):
    return lax.rsqrt(jnp.mean(x * x, axis=-1, keepdims=True) + RMS_EPS)


def _inproj_kernel(x_ref, ln_ref, wmain_hbm, wtail_hbm, pa0_ref, pa1_ref, pa2_ref, pg_ref, pm_ref,
                   w_ref, xs_ref, xn_ref, wsem):
    tm = x_ref.shape[1]
    n_slab = D_MODEL // LANES

    @pl.when((pl.program_id(0) == 0) & (pl.program_id(1) == 0))
    def _():
        n_main = wmain_hbm.shape[1]
        main = pltpu.make_async_copy(wmain_hbm, w_ref.at[:, :n_main], wsem.at[0])
        tail = pltpu.make_async_copy(wtail_hbm, w_ref.at[:, n_main:], wsem.at[1])
        main.start()
        tail.start()
        main.wait()
        tail.wait()

    def project(out_write, col0, width, post=None):
        for c0 in range(0, width, N_CHUNK):
            cw = min(N_CHUNK, width - c0)
            val = _dot(xs_ref[...], w_ref[:, col0 + c0:col0 + c0 + cw])
            out_write(c0, cw, (val if post is None else post(val)).astype(BF16))

    x = x_ref[0]
    xn = x * _rms_scale(x) * ln_ref[...]
    xs_ref[...] = xn.astype(BF16)
    for j in range(n_slab):
        xn_ref[j] = xn[:, j * LANES:(j + 1) * LANES]

    def write_to(ref):
        def write(c0, cw, val):
            ref[0, :, c0:c0 + cw] = val
        return write

    def write_qkv(out_ref, d, n):
        def write(c0, cw, val):
            for r in range(d):
                out_ref[0, r, c0 // DA_WIDTH] = val[r * n:(r + 1) * n]
        return write

    project(write_qkv(pa0_ref, 1, tm), 0, QKV_W)
    project(write_to(pg_ref), 3 * QKV_W, GLA_W)
    project(write_to(pm_ref), 3 * QKV_W + GLA_W, MERGE_W, post=_sigmoid)

    for gi, out_ref in ((1, pa1_ref), (2, pa2_ref)):
        d = DA_GROUPS[gi][1]
        n = tm // d
        for r in range(d):
            for j in range(n_slab):
                xs_ref[r * n:(r + 1) * n, j * LANES:(j + 1) * LANES] = (
                    xn_ref[j, pl.ds(r, n, stride=d), :].astype(BF16))
        project(write_qkv(out_ref, d, n), gi * QKV_W, QKV_W)


def _inproj(x, ln1_w, w_main, w_tail):
    assert N_CHUNK == DA_WIDTH and w_main.shape[1] % LANES == 0
    bsz, seq, _ = x.shape
    tm = TM_IN
    d1, d2 = DA_GROUPS[1][1], DA_GROUPS[2][1]

    def qkv_shape(d):
        return jax.ShapeDtypeStruct((bsz, d, 3, seq // d, DA_WIDTH), BF16)

    def qkv_spec(d):
        return pl.BlockSpec((1, d, 3, tm // d, DA_WIDTH), lambda b, i: (b, 0, 0, i, 0))

    return pl.pallas_call(
        _inproj_kernel,
        grid=(bsz, seq // tm),
        in_specs=[
            pl.BlockSpec((1, tm, D_MODEL), lambda b, i: (b, i, 0)),
            pl.BlockSpec((1, D_MODEL), lambda b, i: (0, 0)),
            pl.BlockSpec(memory_space=pl.ANY),
            pl.BlockSpec(memory_space=pl.ANY),
        ],
        out_specs=(
            qkv_spec(1), qkv_spec(d1), qkv_spec(d2),
            pl.BlockSpec((1, tm, GLA_W), lambda b, i: (b, i, 0)),
            pl.BlockSpec((1, tm, MERGE_W), lambda b, i: (b, i, 0)),
        ),
        out_shape=(qkv_shape(1), qkv_shape(d1), qkv_shape(d2),
                   jax.ShapeDtypeStruct((bsz, seq, GLA_W), BF16),
                   jax.ShapeDtypeStruct((bsz, seq, MERGE_W), BF16)),
        scratch_shapes=[pltpu.VMEM((D_MODEL, w_main.shape[1] + w_tail.shape[1]), BF16),
                        pltpu.VMEM((tm, D_MODEL), BF16),
                        pltpu.VMEM((D_MODEL // LANES, tm, LANES), F32),
                        pltpu.SemaphoreType.DMA((2,))],
        compiler_params=pltpu.CompilerParams(
            dimension_semantics=("arbitrary", "arbitrary"), vmem_limit_bytes=VMEM_LIMIT_INPROJ),
        name="inproj",
    )(x, ln1_w.reshape(1, D_MODEL), w_main, w_tail)


def _dil_attn_kernel(q_ref, kp_ref, kc_ref, vp_ref, vc_ref, o_ref, l_ref, s_ref, p_ref, r_ref):
    n = pl.program_id(2)
    blk = DA_BLOCK
    qi = lax.broadcasted_iota(jnp.int32, (blk, 2 * blk), 0)
    kj = lax.broadcasted_iota(jnp.int32, (blk, 2 * blk), 1)
    band = (kj >= qi) & (kj <= qi + blk)
    band_first = (kj >= jnp.where(n > 0, qi, blk)) & (kj <= qi + blk)
    scale = DA_HEAD_DIM ** -0.5
    items = [(b, h) for b in range(DA_QB) for h in range(DA_HEADS)]

    def rows(b):
        return slice(b * blk, (b + 1) * blk)

    def cols(h):
        return slice(h * DA_HEAD_DIM, (h + 1) * DA_HEAD_DIM)

    def window(prev_ref, cur_ref, b, h):
        before = prev_ref[0, 0, 0, :, cols(h)] if b == 0 else cur_ref[0, 0, 0, rows(b - 1), cols(h)]
        return jnp.concatenate([before, cur_ref[0, 0, 0, rows(b), cols(h)]], axis=0)

    def scores(i, b, h):
        s = lax.dot_general(q_ref[0, 0, 0, rows(b), cols(h)], window(kp_ref, kc_ref, b, h), _NT,
                            preferred_element_type=F32) * scale
        s_ref[i] = jnp.where(band_first if b == 0 else band, s, NEG_INF)

    def softmax(i, b, h):
        s = s_ref[i]
        m = jnp.max(s, axis=-1, keepdims=True)
        p = jnp.exp(s - m)
        l = jnp.sum(p, axis=-1, keepdims=True)
        p_ref[i] = p.astype(BF16)
        r_ref[i] = jnp.broadcast_to(1.0 / l, (blk, DA_HEAD_DIM))
        l_ref[0, 0, rows(b), h * LSE_LANES:(h + 1) * LSE_LANES] = jnp.broadcast_to(
            m + jnp.log(l), (blk, LSE_LANES))

    def values(i, b, h):
        acc = _dot(p_ref[i], window(vp_ref, vc_ref, b, h))
        o_ref[0, 0, rows(b), cols(h)] = (acc * r_ref[i]).astype(o_ref.dtype)

    stages = (scores, softmax, values)
    for t in range(len(items) + len(stages) - 1):
        for lag, stage in enumerate(stages):
            if 0 <= t - lag < len(items):
                stage(t - lag, *items[t - lag])


def _dil_attn(pa):
    bsz, d, _, sub_len, _ = pa.shape
    rows = DA_QB * DA_BLOCK
    n_items = DA_QB * DA_HEADS

    def cur(sec):
        return pl.BlockSpec((1, 1, 1, rows, DA_WIDTH), lambda b, r, n: (b, r, sec, n, 0))

    def prev(sec):
        return pl.BlockSpec((1, 1, 1, DA_BLOCK, DA_WIDTH),
                            lambda b, r, n: (b, r, sec, jnp.maximum(n * DA_QB - 1, 0), 0))

    def out(width):
        return pl.BlockSpec((1, 1, rows, width), lambda b, r, n: (b, r, n, 0))

    return pl.pallas_call(
        _dil_attn_kernel,
        grid=(bsz, d, sub_len // rows),
        in_specs=[cur(0), prev(1), cur(1), prev(2), cur(2)],
        out_specs=(out(DA_WIDTH), out(LANES)),
        out_shape=(jax.ShapeDtypeStruct((bsz, d, sub_len, DA_WIDTH), BF16),
                   jax.ShapeDtypeStruct((bsz, d, sub_len, LANES), F32)),
        scratch_shapes=[pltpu.VMEM((n_items, DA_BLOCK, 2 * DA_BLOCK), F32),
                        pltpu.VMEM((n_items, DA_BLOCK, 2 * DA_BLOCK), BF16),
                        pltpu.VMEM((n_items, DA_BLOCK, DA_HEAD_DIM), F32)],
        compiler_params=pltpu.CompilerParams(
            dimension_semantics=("parallel", "parallel", "arbitrary"), vmem_limit_bytes=VMEM_LIMIT),
        name=f"dil_attn_d{d}",
    )(pa, pa, pa, pa, pa)


def _gla_kernel(q_ref, k_ref, v_ref, og_ref, lr_ref, w2_ref, b2_ref, nw_ref, o_ref, st_ref, mask_ref, keep_ref):
    t = pl.program_id(2)
    c = GLA_CHUNK
    tt = q_ref.shape[1]
    n_c = tt // c

    sub = mask_ref.shape[0]

    @pl.when(t == 0)
    def _():
        st_ref[...] = jnp.zeros_like(st_ref)
        row = lax.broadcasted_iota(jnp.int32, (sub, sub), 0)
        col = lax.broadcasted_iota(jnp.int32, (sub, sub), 1)
        keep = (col <= row) & (col >= row - row % c)
        keep_ref[...] = jnp.where(keep, 1.0, 0.0)
        mask_ref[...] = jnp.where(keep, 1.0, 0.0).astype(BF16)

    heads = range(q_ref.shape[2] // GLA_DK)
    mask = mask_ref[...]

    def kcols(h):
        return slice(h * GLA_DK, (h + 1) * GLA_DK)

    def vcols(h):
        return slice(h * GLA_DV, (h + 1) * GLA_DV)

    gpre = _dot(lr_ref[0], w2_ref[...]) + b2_ref[...]
    forget = (jnp.minimum(gpre, 0.0) - jnp.log(1.0 + jnp.exp(-jnp.abs(gpre)))) / GLA_GATE_NORMALIZER
    g_hi = forget.astype(BF16)
    g_lo = (forget - g_hi.astype(F32)).astype(BF16)
    b, b_last, q_e, k_e, k_end = [], [], [], [], []
    for h in heads:
        g_cat = jnp.concatenate([g_hi[:, kcols(h)], g_lo[:, kcols(h)]], axis=-1)
        csum = jnp.concatenate([_dot(mask, g_cat[s0:s0 + sub]) for s0 in range(0, tt, sub)], axis=0)
        b.append(csum[:, :GLA_DK] + csum[:, GLA_DK:])
    for h in heads:
        b_last.append(b[h].reshape(n_c, c, GLA_DK)[:, c - 1:c, :])
        b_to_end = (b_last[h] - b[h].reshape(n_c, c, GLA_DK)).reshape(tt, GLA_DK)
        q = q_ref[0, :, kcols(h)].astype(F32)
        k = k_ref[0, :, kcols(h)].astype(F32)
        q_e.append((q * ((GLA_DK ** -0.5) * jnp.exp(b[h]))).astype(BF16))
        k_e.append((k * jnp.exp(-b[h])).astype(BF16))
        k_end.append((k * jnp.exp(b_to_end)).astype(BF16))
    o_intra = []
    for h in heads:
        parts = []
        for s0 in range(0, tt, sub):
            ss = slice(s0, s0 + sub)
            att = lax.dot_general(q_e[h][ss], k_e[h][ss], _NT, preferred_element_type=F32)
            att = jnp.where(keep_ref[...] > 0.0, att, 0.0).astype(BF16)
            parts.append(_dot(att, v_ref[0, ss, vcols(h)]))
        o_intra.append(jnp.concatenate(parts, axis=0))
    decay = [jnp.exp(b_last[h].reshape(n_c, GLA_DK).T) for h in heads]
    st = [st_ref[h] for h in heads]
    outs = [[] for _ in heads]
    for ci in range(n_c):
        rs = slice(ci * c, (ci + 1) * c)
        for h in heads:
            outs[h].append(o_intra[h][rs] + _dot(q_e[h][rs], st[h].astype(BF16)))
            st[h] = decay[h][:, ci:ci + 1] * st[h] + lax.dot_general(
                k_end[h][rs], v_ref[0, rs, vcols(h)], _TN, preferred_element_type=F32)
    for h in heads:
        st_ref[h] = st[h]
        o = jnp.concatenate(outs[h], axis=0)
        o = o * _rms_scale(o) * nw_ref[...]
        gate = og_ref[0, :, vcols(h)].astype(F32)
        o_ref[0, :, vcols(h)] = (o * (gate * _sigmoid(gate))).astype(BF16)


def _gla(pg, w2_pad, b_gk2, gla_norm_w):
    bsz, seq, _ = pg.shape
    t = T_GLA
    hps = GLA_HEADS_PER_STEP
    wk, wv = hps * GLA_DK, hps * GLA_DV
    kq = GLA_KEY_DIM // wk
    kv = 2 * GLA_KEY_DIM // wv
    kg = kv + GLA_VALUE_DIM // wv
    klr = (2 * GLA_KEY_DIM + 2 * GLA_VALUE_DIM) // LANES
    return pl.pallas_call(
        _gla_kernel,
        grid=(bsz, GLA_HEADS // hps, seq // t),
        in_specs=[
            pl.BlockSpec((1, t, wk), lambda b, h, i: (b, i, h)),
            pl.BlockSpec((1, t, wk), lambda b, h, i: (b, i, kq + h)),
            pl.BlockSpec((1, t, wv), lambda b, h, i: (b, i, kv + h)),
            pl.BlockSpec((1, t, wv), lambda b, h, i: (b, i, kg + h)),
            pl.BlockSpec((1, t, LANES), lambda b, h, i: (b, i, klr)),
            pl.BlockSpec((LANES, wk), lambda b, h, i: (0, h)),
            pl.BlockSpec((1, wk), lambda b, h, i: (0, h)),
            pl.BlockSpec((1, GLA_DV), lambda b, h, i: (0, 0)),
        ],
        out_specs=pl.BlockSpec((1, t, wv), lambda b, h, i: (b, i, h)),
        out_shape=jax.ShapeDtypeStruct((bsz, seq, GLA_VALUE_DIM), BF16),
        scratch_shapes=[pltpu.VMEM((hps, GLA_DK, GLA_DV), F32), pltpu.VMEM((GLA_SUB, GLA_SUB), BF16),
                        pltpu.VMEM((GLA_SUB, GLA_SUB), F32)],
        compiler_params=pltpu.CompilerParams(
            dimension_semantics=("parallel", "parallel", "arbitrary"), vmem_limit_bytes=VMEM_LIMIT),
        name="gla",
    )(pg, pg, pg, pg, pg, w2_pad, b_gk2.reshape(1, GLA_KEY_DIM), gla_norm_w.reshape(1, GLA_DV))


def _mix_kernel(x_ref, o0_ref, l0_ref, o1_ref, l1_ref, o2_ref, l2_ref, ob_ref, pm_ref,
                wa_ref, wb_ref, wo_ref, ln2_ref, wrh_ref, wrl_ref, br_ref,
                h1_ref, hn_ref, ri_ref, ric_ref, meta_ref, cbefore_ref, cnt_ref,
                po1_ref, pl1_ref, po2_ref, pl2_ref, carry_ref, logit_ref, oa_ref):
    step = pl.program_id(0)
    tm = x_ref.shape[0]

    @pl.when(step == 0)
    def _():
        carry_ref[...] = jnp.zeros_like(carry_ref)
        logit_ref[...] = jnp.zeros_like(logit_ref)
        oa_ref[...] = jnp.zeros_like(oa_ref)

    routed = step > 1
    lane = lax.broadcasted_iota(jnp.int32, (tm, LANES), 1).astype(F32)
    route = {"work": None, "vals": [], "idxs": []}

    gates = pm_ref[...].astype(F32)
    mixed = (gates[:, :D_MODEL] * _dot(oa_ref[(step + 1) % 2], wa_ref[...])
             + gates[:, D_MODEL:] * _dot(ob_ref[...], wb_ref[...]))

    def topk_round():
        work = route["work"]
        m = jnp.max(work, axis=-1, keepdims=True)
        idx = jnp.min(jnp.where(work == m, lane, float(LANES)), axis=-1, keepdims=True)
        route["vals"].append(m)
        route["idxs"].append(idx)
        route["work"] = jnp.where(lane == idx, -jnp.inf, work)

    route["work"] = jnp.where(lane < N_EXPERTS, logit_ref[(step + 1) % 2], -jnp.inf)

    for o_ref, l_ref, po_ref, pl_ref, (_, d) in ((o1_ref, l1_ref, po1_ref, pl1_ref, DA_GROUPS[1]),
                                                 (o2_ref, l2_ref, po2_ref, pl2_ref, DA_GROUPS[2])):
        topk_round()
        n = tm // d
        for r in range(d):
            pl_ref[pl.ds(r, n, stride=d), :] = l_ref[0, r]
            for h in range(DA_HEADS):
                sl = slice(h * DA_HEAD_DIM, (h + 1) * DA_HEAD_DIM)
                po_ref[h, pl.ds(r, n, stride=d), :] = o_ref[0, r, :, sl].astype(F32)

    l0, l1, l2 = l0_ref[...], pl1_ref[...], pl2_ref[...]
    mx = jnp.maximum(jnp.maximum(l0, l1), l2)
    e0, e1, e2 = jnp.exp(l0 - mx), jnp.exp(l1 - mx), jnp.exp(l2 - mx)
    inv = 1.0 / (e0 + e1 + e2)
    w0, w1, w2 = e0 * inv, e1 * inv, e2 * inv
    topk_round()
    heads = []
    for h in range(DA_HEADS):
        sl = slice(h * DA_HEAD_DIM, (h + 1) * DA_HEAD_DIM)
        at = slice(h * LSE_LANES, h * LSE_LANES + 1)
        o_h = w0[:, at] * o0_ref[:, sl].astype(F32) + w1[:, at] * po1_ref[h] + w2[:, at] * po2_ref[h]
        heads.append(o_h.astype(BF16))
    oa_ref[step % 2] = jnp.concatenate(heads, axis=-1)
    topk_round()

    vals, idxs = route["vals"], route["idxs"]
    assert len(vals) == TOP_K
    exps = [jnp.exp(v - vals[0]) for v in vals]
    denom = exps[0] + exps[1] + exps[2] + exps[3]
    onehot = jnp.zeros((tm, LANES), F32)
    for idx in idxs:
        onehot = onehot + jnp.where(lane == idx, 1.0, 0.0)

    h1 = x_ref[...] + _dot(mixed.astype(BF16), wo_ref[...])
    h1_ref[...] = h1
    hn = h1 * _rms_scale(h1) * ln2_ref[...]

    row = lax.broadcasted_iota(jnp.int32, (tm, tm), 0)
    col = lax.broadcasted_iota(jnp.int32, (tm, tm), 1)
    below = jnp.where(col < row, 1.0, 0.0).astype(BF16)
    before = _dot(below, onehot.astype(BF16)) + carry_ref[0:1, :]
    ranks = [jnp.sum(jnp.where(lane == idx, before, 0.0), axis=-1, keepdims=True) for idx in idxs]
    running = carry_ref[0:1, :]
    for j in range(tm // TM_ROW):
        cbefore_ref[j] = jnp.broadcast_to(running, cbefore_ref.shape[1:])
        running = running + jnp.sum(onehot[j * TM_ROW:(j + 1) * TM_ROW], axis=0, keepdims=True)
    carry = jnp.where(routed, running, carry_ref[0:1, :])
    carry_ref[...] = jnp.broadcast_to(carry, carry_ref.shape)
    cnt_ref[...] = jnp.broadcast_to(carry, cnt_ref.shape)

    hn_hi = hn.astype(BF16)
    hn_ref[...] = hn_hi

    ri = jnp.zeros((tm, LANES), F32)
    for j, val in enumerate(idxs + ranks):
        ri = jnp.where(lane == float(j), val, ri)
    ric_ref[...] = ri.astype(jnp.int32)
    ri_t = ri.T[:2 * TOP_K].astype(jnp.int32)
    for j in range(tm // TM_ROW):
        ri_ref[j] = ri_t[:, j * TM_ROW:(j + 1) * TM_ROW]
    meta = jnp.zeros((tm, LANES), F32)
    for k in range(TOP_K):
        gate = exps[k] / denom
        gate_hi = gate.astype(BF16).astype(F32)
        meta = jnp.where(lane == float(k), idxs[k], meta)
        meta = jnp.where(lane == float(TOP_K + k), gate_hi, meta)
        meta = jnp.where(lane == float(2 * TOP_K + k), gate - gate_hi, meta)
    meta_ref[...] = meta.astype(BF16)

    hn_lo = (hn - hn_hi.astype(F32)).astype(BF16)
    logit_ref[step % 2] = (_dot(hn_hi, wrh_ref[...]) + _dot(hn_lo, wrh_ref[...])
                           + _dot(hn_hi, wrl_ref[...]) + br_ref[...])


def _mix(x2, o0, l0, o1, l1, o2, l2, o_b, pm, wa, wb, wo, ln2_w, wr_hi, wr_lo, br_pad):
    n_tok = x2.shape[0]
    tm = TM_MIX
    bsz = o1.shape[0]
    d1, d2 = DA_GROUPS[1][1], DA_GROUPS[2][1]
    tiles_per_seq = (n_tok // bsz) // tm
    n_tiles = n_tok // tm

    def merged(i):
        return jnp.minimum(i, n_tiles - 1)

    def tile(i):
        return jnp.clip(i - 1, 0, n_tiles - 1)

    def routed(i):
        return jnp.clip(i - 2, 0, n_tiles - 1)

    def rows(width, which=tile):
        return pl.BlockSpec((tm, width), lambda i: (which(i), 0))

    def residue_major(d, width):
        return pl.BlockSpec((1, d, tm // d, width),
                            lambda i: (merged(i) // tiles_per_seq, 0, merged(i) % tiles_per_seq, 0))

    def whole(arr):
        return pl.BlockSpec(arr.shape, lambda i: (0,) * arr.ndim)

    ln2 = ln2_w.reshape(1, D_MODEL)
    return pl.pallas_call(
        _mix_kernel,
        grid=(n_tiles + 2,),
        in_specs=[rows(D_MODEL), rows(DA_WIDTH, merged), rows(LANES, merged),
                  residue_major(d1, DA_WIDTH), residue_major(d1, LANES),
                  residue_major(d2, DA_WIDTH), residue_major(d2, LANES),
                  rows(GLA_VALUE_DIM), rows(MERGE_W),
                  whole(wa), whole(wb), whole(wo), whole(ln2), whole(wr_hi), whole(wr_lo), whole(br_pad)],
        out_specs=(rows(D_MODEL), rows(D_MODEL),
                   pl.BlockSpec((tm // TM_ROW, 2 * TOP_K, TM_ROW), lambda i: (routed(i), 0, 0)),
                   rows(LANES, routed), rows(LANES, routed),
                   pl.BlockSpec((tm // TM_ROW, 8, LANES), lambda i: (routed(i), 0, 0)),
                   pl.BlockSpec((8, LANES), lambda i: (0, 0))),
        out_shape=(jax.ShapeDtypeStruct((n_tok, D_MODEL), F32),
                   jax.ShapeDtypeStruct((n_tok, D_MODEL), BF16),
                   jax.ShapeDtypeStruct((n_tok // TM_ROW, 2 * TOP_K, TM_ROW), jnp.int32),
                   jax.ShapeDtypeStruct((n_tok, LANES), jnp.int32),
                   jax.ShapeDtypeStruct((n_tok, LANES), BF16),
                   jax.ShapeDtypeStruct((n_tok // TM_ROW, 8, LANES), F32),
                   jax.ShapeDtypeStruct((8, LANES), F32)),
        scratch_shapes=[pltpu.VMEM((DA_HEADS, tm, DA_HEAD_DIM), F32), pltpu.VMEM((tm, LANES), F32),
                        pltpu.VMEM((DA_HEADS, tm, DA_HEAD_DIM), F32), pltpu.VMEM((tm, LANES), F32),
                        pltpu.VMEM((8, LANES), F32), pltpu.VMEM((2, tm, LANES), F32),
                        pltpu.VMEM((2, tm, DA_WIDTH), BF16)],
        compiler_params=pltpu.CompilerParams(
            dimension_semantics=("arbitrary",), vmem_limit_bytes=VMEM_LIMIT),
        name="mix",
    )(x2, o0, l0, o1, l1, o2, l2, o_b, pm, wa, wb, wo, ln2, wr_hi, wr_lo, br_pad)


def _pack_pairs(x):
    n = x.shape[1] // 2
    rounded = x.astype(BF16).astype(F32)
    lo = lax.bitcast_convert_type(rounded[:, :n], U32) >> 16
    hi = lax.bitcast_convert_type(rounded[:, n:], U32) & jnp.uint32(0xFFFF0000)
    return hi | lo


def _unpack_pairs(u):
    lo = lax.bitcast_convert_type(u << 16, F32).astype(BF16)
    hi = lax.bitcast_convert_type(u & jnp.uint32(0xFFFF0000), F32).astype(BF16)
    return lo, hi


def _dispatch_kernel(zstart_ref, nwin_ref, wsrc_ref, wdst_ref, pos_ref, meta_ref, hn_ref, xin_ref,
                     buf_ref, zero_ref, sem, zsem):
    i = pl.program_id(0)
    tm = hn_ref.shape[0]
    n_buf_rows = buf_ref.shape[1]
    slot = i % 2

    def window_copy(s, buf_slot):
        src = pl.multiple_of(wsrc_ref[i * N_SLOT + s], ROW_ALIGN)
        dst = pl.multiple_of(wdst_ref[i * N_SLOT + s], ROW_ALIGN)
        return pltpu.make_async_copy(buf_ref.at[buf_slot, pl.ds(src, WIN), :],
                                     xin_ref.at[pl.ds(dst, WIN), :], sem)

    def wait_windows(step):
        def body(s, carry):
            pltpu.make_async_copy(buf_ref.at[0, pl.ds(0, WIN), :], xin_ref.at[pl.ds(0, WIN), :], sem).wait()
            return carry
        lax.fori_loop(0, nwin_ref[step], body, 0)

    @pl.when(i == 0)
    def _():
        zero_ref[...] = jnp.zeros_like(zero_ref)

        def zero_copy(j):
            start = pl.multiple_of(jnp.maximum(zstart_ref[j], 0), ZCHUNK)
            return pltpu.make_async_copy(zero_ref, xin_ref.at[pl.ds(start, ZCHUNK), :], zsem)

        def start_one(j, carry):
            @pl.when(zstart_ref[j] >= 0)
            def _():
                zero_copy(j).start()
            return carry

        def wait_one(j, carry):
            @pl.when(zstart_ref[j] >= 0)
            def _():
                zero_copy(j).wait()
            return carry

        lax.fori_loop(0, zstart_ref.shape[0], start_one, 0)
        lax.fori_loop(0, zstart_ref.shape[0], wait_one, 0)

    row = lax.broadcasted_iota(jnp.int32, (n_buf_rows, tm), 0)

    perm = jnp.zeros((n_buf_rows, tm), F32)
    for k in range(TOP_K):
        perm = perm + jnp.where(row == pos_ref[0, k:k + 1, :], 1.0, 0.0)
    perm = perm.astype(BF16)
    buf_ref[slot, :, :D_MODEL // 2] = _pack_pairs(_dot(perm, hn_ref[...]))
    buf_ref[slot, :, D_MODEL // 2:] = lax.bitcast_convert_type(_dot(perm, meta_ref[...]), U32)

    @pl.when(i > 0)
    def _():
        wait_windows(i - 1)

    for buf_slot in range(2):
        @pl.when(slot == buf_slot)
        def _():
            def issue(s, carry):
                window_copy(s, buf_slot).start()
                return carry
            lax.fori_loop(0, nwin_ref[i], issue, 0)

    @pl.when(i == pl.num_programs(0) - 1)
    def _():
        wait_windows(i)


def _dispatch(zero_start, n_win, win_src, win_dst, pos, meta, hn, n_rows):
    n_tok = hn.shape[0]
    tm = TM_ROW
    n_tiles = n_tok // tm
    n_buf_rows = SORT_ROWS
    return pl.pallas_call(
        _dispatch_kernel,
        grid_spec=pltpu.PrefetchScalarGridSpec(
            num_scalar_prefetch=4,
            grid=(n_tiles,),
            in_specs=[
                pl.BlockSpec((1, TOP_K, tm), lambda i, *_: (i, 0, 0)),
                pl.BlockSpec((tm, LANES), lambda i, *_: (i, 0)),
                pl.BlockSpec((tm, D_MODEL), lambda i, *_: (i, 0)),
            ],
            out_specs=pl.BlockSpec(memory_space=pl.ANY),
            scratch_shapes=[pltpu.VMEM((2, n_buf_rows, ROW_W), U32),
                            pltpu.VMEM((ZCHUNK, ROW_W), U32),
                            pltpu.SemaphoreType.DMA(()), pltpu.SemaphoreType.DMA(())],
        ),
        out_shape=jax.ShapeDtypeStruct((n_rows, ROW_W), U32),
        compiler_params=pltpu.CompilerParams(
            dimension_semantics=("arbitrary",), vmem_limit_bytes=VMEM_LIMIT),
        name="dispatch",
    )(zero_start, n_win, win_src.reshape(-1), win_dst.reshape(-1), pos, meta, hn)


def _expert_kernel(be_ref, live_ref, x_ref, wu_hbm, bu_ref, wd_hbm, bd_ref, y_ref,
                   wu32_ref, wd32_ref, wu16_ref, wd16_ref, wsem):
    i = pl.program_id(0)
    live = live_ref[i]
    expert_id = be_ref[i]

    def weight_copies(expert, slot):
        return (pltpu.make_async_copy(wu_hbm.at[expert], wu32_ref.at[slot], wsem.at[0, slot]),
                pltpu.make_async_copy(wd_hbm.at[expert], wd32_ref.at[slot], wsem.at[1, slot]))

    @pl.when(i == 0)
    def _():
        for cp in weight_copies(expert_id, 0):
            cp.start()

    for slot in range(2):
        @pl.when(((i == 0) | (expert_id != be_ref[jnp.maximum(i - 1, 0)])) & (expert_id % 2 == slot))
        def _():
            for cp in weight_copies(expert_id, slot):
                cp.wait()

            @pl.when(expert_id + 1 < N_EXPERTS)
            def _():
                for cp in weight_copies(expert_id + 1, 1 - slot):
                    cp.start()

            wu16_ref[...] = wu32_ref[slot].astype(BF16)
            wd16_ref[...] = wd32_ref[slot].astype(BF16)

    def compute(m):
        half = D_MODEL // 2
        x_lo, x_hi = _unpack_pairs(x_ref[:m, :half])
        meta = lax.bitcast_convert_type(x_ref[:m, half:half + 3 * TOP_K], F32)
        expert = be_ref[i].astype(F32)
        gate = jnp.zeros((m, 1), F32)
        for k in range(TOP_K):
            weight = meta[:, TOP_K + k:TOP_K + k + 1] + meta[:, 2 * TOP_K + k:2 * TOP_K + k + 1]
            gate = gate + jnp.where(meta[:, k:k + 1] == expert, weight, 0.0)
        hu = _dot(x_lo, wu16_ref[:half, :]) + _dot(x_hi, wu16_ref[half:, :]) + bu_ref[0]
        x_glu = jnp.minimum(hu[:, :D_FF], SWIGLU_LIMIT)
        x_lin = jnp.clip(hu[:, D_FF:], -SWIGLU_LIMIT, SWIGLU_LIMIT)
        act = x_glu * _sigmoid(SWIGLU_ALPHA * x_glu) * (x_lin + 1.0)
        y_ref[:m, :] = _pack_pairs((_dot(act.astype(BF16), wd16_ref[...]) + bd_ref[0]) * gate)
        if m < TB:
            y_ref[m:, :] = jnp.zeros((TB - m, y_ref.shape[1]), y_ref.dtype)

    @pl.when(live == 0)
    def _():
        y_ref[...] = jnp.zeros_like(y_ref)

    for m in range(TB_STEP, TB + 1, TB_STEP):
        @pl.when((live > m - TB_STEP) & (live <= m))
        def _():
            compute(m)


def _experts(block_e, block_live, xin, w_up, b_up, w_down, b_down):
    n_rows = xin.shape[0]
    return pl.pallas_call(
        _expert_kernel,
        grid_spec=pltpu.PrefetchScalarGridSpec(
            num_scalar_prefetch=2,
            grid=(n_rows // TB,),
            in_specs=[
                pl.BlockSpec((TB, ROW_W), lambda i, be, nu: (i, 0)),
                pl.BlockSpec(memory_space=pl.ANY),
                pl.BlockSpec((1, 1, 2 * D_FF), lambda i, be, nu: (be[i], 0, 0)),
                pl.BlockSpec(memory_space=pl.ANY),
                pl.BlockSpec((1, 1, D_MODEL), lambda i, be, nu: (be[i], 0, 0)),
            ],
            out_specs=pl.BlockSpec((TB, D_MODEL // 2), lambda i, be, nu: (i, 0)),
            scratch_shapes=[pltpu.VMEM((2, D_MODEL, 2 * D_FF), F32), pltpu.VMEM((2, D_FF, D_MODEL), F32),
                            pltpu.VMEM((D_MODEL, 2 * D_FF), BF16), pltpu.VMEM((D_FF, D_MODEL), BF16),
                            pltpu.SemaphoreType.DMA((2, 2))],
        ),
        out_shape=jax.ShapeDtypeStruct((n_rows, D_MODEL // 2), U32),
        compiler_params=pltpu.CompilerParams(
            dimension_semantics=("arbitrary",), vmem_limit_bytes=VMEM_LIMIT),
        name="experts",
    )(block_e, block_live, xin, w_up, b_up.reshape(N_EXPERTS, 1, 2 * D_FF),
      w_down, b_down.reshape(N_EXPERTS, 1, D_MODEL))


def _combine_kernel(nwin_ref, wsrc_ref, wdst_ref, ric_ref, delta_ref, ricn_ref, deltan_ref, yb_ref, h1_ref,
                    lnf_ref, o_ref, stage_ref, buf_ref, pick_ref, sem):
    i = pl.program_id(0)
    tm = h1_ref.shape[0]
    n_buf_rows = buf_ref.shape[0]

    parity = i % 2

    def fetch(step, stage_slot):
        def body(s, carry):
            dst = pl.multiple_of(wdst_ref[step * N_SLOT + s], ROW_ALIGN)
            pltpu.make_async_copy(
                yb_ref.at[pl.ds(dst, WIN), :],
                stage_ref.at[stage_slot, pl.ds(pl.multiple_of(s * WIN, WIN), WIN), :],
                sem.at[stage_slot]).start()
            return carry
        lax.fori_loop(0, nwin_ref[step], body, 0)

    lane = lax.broadcasted_iota(jnp.int32, (tm, LANES), 1).astype(F32)
    col = lax.broadcasted_iota(jnp.int32, (tm, n_buf_rows), 1)

    def pick_round(pick, k, ric, delta):
        offset = jnp.sum(jnp.where(lane == ric[:, k:k + 1], delta, 0.0), axis=-1, keepdims=True)
        pos = (ric[:, TOP_K + k:TOP_K + k + 1] + offset).astype(jnp.int32)
        return pick + jnp.where(col == pos, 1.0, 0.0)

    @pl.when(i == 0)
    def _():
        buf_ref[...] = jnp.zeros_like(buf_ref)
        fetch(i, 0)
        pick = jnp.zeros((tm, n_buf_rows), F32)
        ric = ric_ref[...].astype(F32)
        for k in range(TOP_K):
            pick = pick_round(pick, k, ric, delta_ref[0])
        pick_ref[0] = pick.astype(BF16)

    for stage_slot in range(2):
        @pl.when((i + 1 < pl.num_programs(0)) & (parity != stage_slot))
        def _():
            fetch(i + 1, stage_slot)

    for stage_slot in range(2):
        @pl.when(parity == stage_slot)
        def _():
            def drain(s, carry):
                pltpu.make_async_copy(yb_ref.at[pl.ds(0, WIN), :], stage_ref.at[stage_slot, pl.ds(0, WIN), :],
                                      sem.at[stage_slot]).wait()
                return carry

            def compact(s, carry):
                src = pl.multiple_of(wsrc_ref[i * N_SLOT + s], ROW_ALIGN)
                buf_ref[pl.ds(src, WIN), :] = stage_ref[stage_slot, pl.ds(pl.multiple_of(s * WIN, WIN), WIN), :]
                return carry

            lax.fori_loop(0, nwin_ref[i], drain, 0)
            lax.fori_loop(0, nwin_ref[i], compact, 0)

    ric_next = ricn_ref[...].astype(F32)
    delta_next = deltan_ref[0]
    pick = pick_ref[parity]
    nxt = jnp.zeros((tm, n_buf_rows), F32)
    y_lo, y_hi = _unpack_pairs(buf_ref[...])
    nxt = pick_round(nxt, 0, ric_next, delta_next)
    left = _dot(pick, y_lo)
    nxt = pick_round(nxt, 1, ric_next, delta_next)
    right = _dot(pick, y_hi)
    nxt = pick_round(nxt, 2, ric_next, delta_next)
    h2 = h1_ref[...] + jnp.concatenate([left, right], axis=-1)
    nxt = pick_round(nxt, 3, ric_next, delta_next)
    o_ref[...] = h2 * _rms_scale(h2) * lnf_ref[...]
    pick_ref[1 - parity] = nxt.astype(BF16)


def _combine(n_win, win_src, win_dst, ric, delta, yb, h1, lnf_w):
    n_tok = h1.shape[0]
    tm = TM_ROW
    n_tiles = n_tok // tm
    return pl.pallas_call(
        _combine_kernel,
        grid_spec=pltpu.PrefetchScalarGridSpec(
            num_scalar_prefetch=3,
            grid=(n_tiles,),
            in_specs=[
                pl.BlockSpec((tm, LANES), lambda i, *_: (i, 0)),
                pl.BlockSpec((1, 1, LANES), lambda i, *_: (i, 0, 0)),
                pl.BlockSpec((tm, LANES), lambda i, *_: (jnp.minimum(i + 1, n_tiles - 1), 0)),
                pl.BlockSpec((1, 1, LANES), lambda i, *_: (jnp.minimum(i + 1, n_tiles - 1), 0, 0)),
                pl.BlockSpec(memory_space=pl.ANY),
                pl.BlockSpec((tm, D_MODEL), lambda i, *_: (i, 0)),
                pl.BlockSpec((1, D_MODEL), lambda i, *_: (0, 0)),
            ],
            out_specs=pl.BlockSpec((tm, D_MODEL), lambda i, *_: (i, 0)),
            scratch_shapes=[pltpu.VMEM((2, N_SLOT * WIN, D_MODEL // 2), U32),
                            pltpu.VMEM((SORT_ROWS, D_MODEL // 2), U32),
                            pltpu.VMEM((2, tm, SORT_ROWS), BF16), pltpu.SemaphoreType.DMA((2,))],
        ),
        out_shape=jax.ShapeDtypeStruct((n_tok, D_MODEL), F32),
        compiler_params=pltpu.CompilerParams(
            dimension_semantics=("arbitrary",), vmem_limit_bytes=VMEM_LIMIT),
        name="combine",
    )(n_win, win_src.reshape(-1), win_dst.reshape(-1), ric, delta, ric, delta, yb, h1,
      lnf_w.reshape(1, D_MODEL))


def _layer(h, ln1_w, w_in, w_gk2, b_gk2, gla_norm_w, w_proj_a, w_proj_b, w_out,
           ln2_w, w_router, b_router, w_up, b_up, w_down, b_down, lnf_w):
    bsz, seq, _ = h.shape
    n_tok = bsz * seq

    n_main = 3 * QKV_W + 2 * GLA_KEY_DIM + 2 * GLA_VALUE_DIM
    pad = LANES - GLA_GATE_RANK
    w_main = w_in[:, :n_main].astype(BF16)
    w_tail = jnp.concatenate(
        [w_in[:, n_main:n_main + GLA_GATE_RANK], jnp.zeros((D_MODEL, pad), F32),
         w_in[:, n_main + GLA_GATE_RANK:]], axis=1).astype(BF16)
    w2_pad = jnp.concatenate([w_gk2, jnp.zeros((pad, GLA_KEY_DIM), F32)], axis=0).astype(BF16)
    wr_pad = jnp.concatenate([w_router, jnp.zeros((D_MODEL, LANES - N_EXPERTS), F32)], axis=1)
    wr_hi = wr_pad.astype(BF16)
    wr_lo = (wr_pad - wr_hi.astype(F32)).astype(BF16)
    br_pad = jnp.concatenate([b_router, jnp.zeros((LANES - N_EXPERTS,), F32)]).reshape(1, LANES)

    pa0, pa1, pa2, pg, pm = _inproj(h, ln1_w, w_main, w_tail)
    o0, l0 = _dil_attn(pa0)
    o1, l1 = _dil_attn(pa1)
    o2, l2 = _dil_attn(pa2)
    o_b = _gla(pg, w2_pad, b_gk2, gla_norm_w)

    h1, hn, ri, ric, meta, carry_f, cnt = _mix(
        h.reshape(n_tok, D_MODEL), o0.reshape(n_tok, DA_WIDTH), l0.reshape(n_tok, LANES),
        o1, l1, o2, l2, o_b.reshape(n_tok, GLA_VALUE_DIM), pm.reshape(n_tok, MERGE_W),
        w_proj_a.astype(BF16), w_proj_b.astype(BF16), w_out.astype(BF16), ln2_w, wr_hi, wr_lo, br_pad)

    i32 = jnp.int32
    n_tiles = n_tok // TM_ROW
    experts = jnp.arange(N_EXPERTS)
    counts = cnt[0, :N_EXPERTS].astype(i32)
    before = carry_f[:, 0, :N_EXPERTS].astype(i32)
    run = jnp.concatenate([before[1:], counts[None]], axis=0) - before
    run_al = (run + ROW_ALIGN - 1) // ROW_ALIGN * ROW_ALIGN
    rows_end = jnp.cumsum(run_al, axis=0)
    rows_before = rows_end - run_al
    used = rows_end[-1]
    slack = WIN - ROW_ALIGN
    padded = (used + slack + TB - 1) // TB * TB
    pad_end = jnp.cumsum(padded)
    pad_start = pad_end - padded
    n_asg = n_tok * TOP_K
    n_rows = (n_asg + n_tiles * N_EXPERTS * (ROW_ALIGN - 1)
              + N_EXPERTS * (slack + TB - 1) + TB - 1) // TB * TB
    n_blocks = n_rows // TB
    block_starts = jnp.arange(n_blocks) * TB
    block_e = jnp.minimum(jnp.sum(pad_end[None, :] <= block_starts[:, None], axis=1),
                          N_EXPERTS - 1).astype(i32)
    used_end = jnp.sum(jnp.where(block_e[:, None] == experts, (pad_start + used)[None, :], 0), axis=1)
    block_live = jnp.clip(used_end - block_starts, 0, TB).astype(i32)

    run_end = jnp.cumsum(run_al, axis=1)
    run_start = run_end - run_al
    wins = (run + WIN - 1) // WIN
    wins_end = jnp.cumsum(wins, axis=1)
    wins_start = wins_end - wins
    n_win = wins_end[:, -1].astype(i32)
    slots = jnp.arange(N_SLOT)
    slot_e = jnp.sum(wins_end[:, None, :] <= slots[None, :, None], axis=-1)
    slot_is = slot_e[..., None] == experts

    def of_slot(table):
        return jnp.sum(jnp.where(slot_is, table[:, None, :], 0), axis=-1)

    win_off = (slots[None, :] - of_slot(wins_start)) * WIN
    win_src = (of_slot(run_start) + win_off).astype(i32)
    win_dst = (of_slot(pad_start[None, :] + rows_before) + win_off).astype(i32)
    asg_is = ri[:, :TOP_K, :, None] == experts
    pos = ri[:, TOP_K:, :] + jnp.sum(jnp.where(asg_is, (run_start - before)[:, None, None, :], 0), axis=-1)
    delta = jnp.pad((run_start - before).astype(F32), ((0, 0), (0, LANES - N_EXPERTS)))

    chunk_back = ZCHUNK * (1 + jnp.arange((slack + TB - 1 + ZCHUNK - 1) // ZCHUNK + 1))
    region_chunks = pad_end[:, None] - chunk_back[None, :]
    region_ok = (region_chunks >= pad_start[:, None]) & (region_chunks + ZCHUNK > (pad_start + used)[:, None])
    tail_chunks = pad_end[-1] + ZCHUNK * jnp.arange((n_rows - n_asg) // ZCHUNK)
    zero_start = jnp.concatenate([jnp.where(region_ok, region_chunks, -1).reshape(-1),
                                  jnp.where(tail_chunks < n_rows, tail_chunks, -1)]).astype(i32)

    xin = _dispatch(zero_start, n_win, win_src, win_dst, pos.astype(i32), meta, hn, n_rows)
    yb = _experts(block_e, block_live, xin, w_up, b_up, w_down, b_down)
    out = _combine(n_win, win_src, win_dst, ric, delta.reshape(-1, 1, LANES), yb, h1, lnf_w)
    return out.reshape(bsz, seq, D_MODEL)


def kernel(x, ln1_w, w_in, w_gk2, b_gk2, gla_norm_w, w_proj_a, w_proj_b, w_out, ln2_w, w_router,
           b_router, w_up, b_up, w_down, b_down, lnf_w):
    assert x.shape[-1] == D_MODEL and ln1_w.shape[0] == 1, "one layer of width D_MODEL"
    return _layer(x, ln1_w[0], w_in[0], w_gk2[0], b_gk2[0], gla_norm_w[0], w_proj_a[0], w_proj_b[0],
                  w_out[0], ln2_w[0], w_router[0], b_router[0], w_up[0], b_up[0], w_down[0],
                  b_down[0], lnf_w)
```

```python
import jax
import jax.numpy as jnp
from jax import lax
from jax.experimental import pallas as pl
from jax.experimental.pallas import tpu as pltpu

F32 = jnp.float32
BF16 = jnp.bfloat16
U32 = jnp.uint32

D_MODEL = 1024
DA_GROUPS = ((128, 1), (512, 4), (2048, 16))
DA_HEADS = 4
DA_HEAD_DIM = 128
DA_WIDTH = DA_HEADS * DA_HEAD_DIM
DA_BLOCK = 128
GLA_HEADS = 4
GLA_KEY_DIM = D_MODEL // 2
GLA_VALUE_DIM = D_MODEL
GLA_DK = GLA_KEY_DIM // GLA_HEADS
GLA_DV = GLA_VALUE_DIM // GLA_HEADS
GLA_GATE_RANK = 16
GLA_GATE_NORMALIZER = 16.0
GLA_CHUNK = 64
N_EXPERTS = 32
TOP_K = 4
D_FF = D_MODEL
SWIGLU_ALPHA = 1.702
SWIGLU_LIMIT = 7.0
RMS_EPS = 1e-5
NEG_INF = -1e30

LANES = 128
QKV_W = 3 * DA_WIDTH
GLA_W = 2 * GLA_KEY_DIM + 2 * GLA_VALUE_DIM + LANES
MERGE_W = 2 * D_MODEL

DA_QB = 4
LSE_LANES = LANES // DA_HEADS
TM_IN = 512
N_CHUNK = 512
T_GLA = 512
GLA_SUB = 256
GLA_HEADS_PER_STEP = 4
TM_MIX = 512
TB = 512
TB_STEP = 128
TM_ROW = 256
ROW_ALIGN = 8
WIN = 64
N_SLOT = N_EXPERTS + TM_ROW * TOP_K // WIN
SORT_ROWS = TM_ROW * TOP_K + N_EXPERTS * (ROW_ALIGN - 1) + WIN
ZCHUNK = 256
ROW_W = D_MODEL // 2 + LANES
VMEM_LIMIT = 56 * 1024 * 1024
VMEM_LIMIT_INPROJ = 62 * 1024 * 1024

_NT = (((1,), (1,)), ((), ()))
_TN = (((0,), (0,)), ((), ()))


def _dot(a, b):
    return jnp.dot(a, b, preferred_element_type=F32)


def _sigmoid(x):
    return 1.0 / (1.0 + jnp.exp(-x))


def _rms_scale(x):
    return lax.rsqrt(jnp.mean(x * x, axis=-1, keepdims=True) + RMS_EPS)


def _inproj_kernel(x_ref, ln_ref, wmain_hbm, wtail_hbm, pa0_ref, pa1_ref, pa2_ref, pg_ref, pm_ref,
                   w_ref, xs_ref, xn_ref, wsem):
    tm = x_ref.shape[1]
    n_slab = D_MODEL // LANES

    @pl.when((pl.program_id(0) == 0) & (pl.program_id(1) == 0))
    def _():
        n_main = wmain_hbm.shape[1]
        main = pltpu.make_async_copy(wmain_hbm, w_ref.at[:, :n_main], wsem.at[0])
        tail = pltpu.make_async_copy(wtail_hbm, w_ref.at[:, n_main:], wsem.at[1])
        main.start()
        tail.start()
        main.wait()
        tail.wait()

    def project(out_write, col0, width, post=None):
        for c0 in range(0, width, N_CHUNK):
            cw = min(N_CHUNK, width - c0)
            val = _dot(xs_ref[...], w_ref[:, col0 + c0:col0 + c0 + cw])
            out_write(c0, cw, (val if post is None else post(val)).astype(BF16))

    x = x_ref[0]
    xn = x * _rms_scale(x) * ln_ref[...]
    xs_ref[...] = xn.astype(BF16)
    for j in range(n_slab):
        xn_ref[j] = xn[:, j * LANES:(j + 1) * LANES]

    def write_to(ref):
        def write(c0, cw, val):
            ref[0, :, c0:c0 + cw] = val
        return write

    def write_qkv(out_ref, d, n):
        def write(c0, cw, val):
            for r in range(d):
                out_ref[0, r, c0 // DA_WIDTH] = val[r * n:(r + 1) * n]
        return write

    project(write_qkv(pa0_ref, 1, tm), 0, QKV_W)
    project(write_to(pg_ref), 3 * QKV_W, GLA_W)
    project(write_to(pm_ref), 3 * QKV_W + GLA_W, MERGE_W, post=_sigmoid)

    for gi, out_ref in ((1, pa1_ref), (2, pa2_ref)):
        d = DA_GROUPS[gi][1]
        n = tm // d
        for r in range(d):
            for j in range(n_slab):
                xs_ref[r * n:(r + 1) * n, j * LANES:(j + 1) * LANES] = (
                    xn_ref[j, pl.ds(r, n, stride=d), :].astype(BF16))
        project(write_qkv(out_ref, d, n), gi * QKV_W, QKV_W)


def _inproj(x, ln1_w, w_main, w_tail):
    assert N_CHUNK == DA_WIDTH and w_main.shape[1] % LANES == 0
    bsz, seq, _ = x.shape
    tm = TM_IN
    d1, d2 = DA_GROUPS[1][1], DA_GROUPS[2][1]

    def qkv_shape(d):
        return jax.ShapeDtypeStruct((bsz, d, 3, seq // d, DA_WIDTH), BF16)

    def qkv_spec(d):
        return pl.BlockSpec((1, d, 3, tm // d, DA_WIDTH), lambda b, i: (b, 0, 0, i, 0))

    return pl.pallas_call(
        _inproj_kernel,
        grid=(bsz, seq // tm),
        in_specs=[
            pl.BlockSpec((1, tm, D_MODEL), lambda b, i: (b, i, 0)),
            pl.BlockSpec((1, D_MODEL), lambda b, i: (0, 0)),
            pl.BlockSpec(memory_space=pl.ANY),
            pl.BlockSpec(memory_space=pl.ANY),
        ],
        out_specs=(
            qkv_spec(1), qkv_spec(d1), qkv_spec(d2),
            pl.BlockSpec((1, tm, GLA_W), lambda b, i: (b, i, 0)),
            pl.BlockSpec((1, tm, MERGE_W), lambda b, i: (b, i, 0)),
        ),
        out_shape=(qkv_shape(1), qkv_shape(d1), qkv_shape(d2),
                   jax.ShapeDtypeStruct((bsz, seq, GLA_W), BF16),
                   jax.ShapeDtypeStruct((bsz, seq, MERGE_W), BF16)),
        scratch_shapes=[pltpu.VMEM((D_MODEL, w_main.shape[1] + w_tail.shape[1]), BF16),
                        pltpu.VMEM((tm, D_MODEL), BF16),
                        pltpu.VMEM((D_MODEL // LANES, tm, LANES), F32),
                        pltpu.SemaphoreType.DMA((2,))],
        compiler_params=pltpu.CompilerParams(
            dimension_semantics=("arbitrary", "arbitrary"), vmem_limit_bytes=VMEM_LIMIT_INPROJ),
        name="inproj",
    )(x, ln1_w.reshape(1, D_MODEL), w_main, w_tail)


def _dil_attn_kernel(q_ref, kp_ref, kc_ref, vp_ref, vc_ref, o_ref, l_ref, s_ref, p_ref, r_ref):
    n = pl.program_id(2)
    blk = DA_BLOCK
    qi = lax.broadcasted_iota(jnp.int32, (blk, 2 * blk), 0)
    kj = lax.broadcasted_iota(jnp.int32, (blk, 2 * blk), 1)
    band = (kj >= qi) & (kj <= qi + blk)
    band_first = (kj >= jnp.where(n > 0, qi, blk)) & (kj <= qi + blk)
    scale = DA_HEAD_DIM ** -0.5
    items = [(b, h) for b in range(DA_QB) for h in range(DA_HEADS)]

    def rows(b):
        return slice(b * blk, (b + 1) * blk)

    def cols(h):
        return slice(h * DA_HEAD_DIM, (h + 1) * DA_HEAD_DIM)

    def window(prev_ref, cur_ref, b, h):
        before = prev_ref[0, 0, 0, :, cols(h)] if b == 0 else cur_ref[0, 0, 0, rows(b - 1), cols(h)]
        return jnp.concatenate([before, cur_ref[0, 0, 0, rows(b), cols(h)]], axis=0)

    def scores(i, b, h):
        s = lax.dot_general(q_ref[0, 0, 0, rows(b), cols(h)], window(kp_ref, kc_ref, b, h), _NT,
                            preferred_element_type=F32) * scale
        s_ref[i] = jnp.where(band_first if b == 0 else band, s, NEG_INF)

    def softmax(i, b, h):
        s = s_ref[i]
        m = jnp.max(s, axis=-1, keepdims=True)
        p = jnp.exp(s - m)
        l = jnp.sum(p, axis=-1, keepdims=True)
        p_ref[i] = p.astype(BF16)
        r_ref[i] = jnp.broadcast_to(1.0 / l, (blk, DA_HEAD_DIM))
        l_ref[0, 0, rows(b), h * LSE_LANES:(h + 1) * LSE_LANES] = jnp.broadcast_to(
            m + jnp.log(l), (blk, LSE_LANES))

    def values(i, b, h):
        acc = _dot(p_ref[i], window(vp_ref, vc_ref, b, h))
        o_ref[0, 0, rows(b), cols(h)] = (acc * r_ref[i]).astype(o_ref.dtype)

    stages = (scores, softmax, values)
    for t in range(len(items) + len(stages) - 1):
        for lag, stage in enumerate(stages):
            if 0 <= t - lag < len(items):
                stage(t - lag, *items[t - lag])


def _dil_attn(pa):
    bsz, d, _, sub_len, _ = pa.shape
    rows = DA_QB * DA_BLOCK
    n_items = DA_QB * DA_HEADS

    def cur(sec):
        return pl.BlockSpec((1, 1, 1, rows, DA_WIDTH), lambda b, r, n: (b, r, sec, n, 0))

    def prev(sec):
        return pl.BlockSpec((1, 1, 1, DA_BLOCK, DA_WIDTH),
                            lambda b, r, n: (b, r, sec, jnp.maximum(n * DA_QB - 1, 0), 0))

    def out(width):
        return pl.BlockSpec((1, 1, rows, width), lambda b, r, n: (b, r, n, 0))

    return pl.pallas_call(
        _dil_attn_kernel,
        grid=(bsz, d, sub_len // rows),
        in_specs=[cur(0), prev(1), cur(1), prev(2), cur(2)],
        out_specs=(out(DA_WIDTH), out(LANES)),
        out_shape=(jax.ShapeDtypeStruct((bsz, d, sub_len, DA_WIDTH), BF16),
                   jax.ShapeDtypeStruct((bsz, d, sub_len, LANES), F32)),
        scratch_shapes=[pltpu.VMEM((n_items, DA_BLOCK, 2 * DA_BLOCK), F32),
                        pltpu.VMEM((n_items, DA_BLOCK, 2 * DA_BLOCK), BF16),
                        pltpu.VMEM((n_items, DA_BLOCK, DA_HEAD_DIM), F32)],
        compiler_params=pltpu.CompilerParams(
            dimension_semantics=("parallel", "parallel", "arbitrary"), vmem_limit_bytes=VMEM_LIMIT),
        name=f"dil_attn_d{d}",
    )(pa, pa, pa, pa, pa)


def _gla_kernel(q_ref, k_ref, v_ref, og_ref, lr_ref, w2_ref, b2_ref, nw_ref, o_ref, st_ref, mask_ref, keep_ref):
    t = pl.program_id(2)
    c = GLA_CHUNK
    tt = q_ref.shape[1]
    n_c = tt // c

    sub = mask_ref.shape[0]

    @pl.when(t == 0)
    def _():
        st_ref[...] = jnp.zeros_like(st_ref)
        row = lax.broadcasted_iota(jnp.int32, (sub, sub), 0)
        col = lax.broadcasted_iota(jnp.int32, (sub, sub), 1)
        keep = (col <= row) & (col >= row - row % c)
        keep_ref[...] = jnp.where(keep, 1.0, 0.0)
        mask_ref[...] = jnp.where(keep, 1.0, 0.0).astype(BF16)

    heads = range(q_ref.shape[2] // GLA_DK)
    mask = mask_ref[...]

    def kcols(h):
        return slice(h * GLA_DK, (h + 1) * GLA_DK)

    def vcols(h):
        return slice(h * GLA_DV, (h + 1) * GLA_DV)

    gpre = _dot(lr_ref[0], w2_ref[...]) + b2_ref[...]
    forget = (jnp.minimum(gpre, 0.0) - jnp.log(1.0 + jnp.exp(-jnp.abs(gpre)))) / GLA_GATE_NORMALIZER
    g_hi = forget.astype(BF16)
    g_lo = (forget - g_hi.astype(F32)).astype(BF16)
    b, b_last, q_e, k_e, k_end = [], [], [], [], []
    for h in heads:
        g_cat = jnp.concatenate([g_hi[:, kcols(h)], g_lo[:, kcols(h)]], axis=-1)
        csum = jnp.concatenate([_dot(mask, g_cat[s0:s0 + sub]) for s0 in range(0, tt, sub)], axis=0)
        b.append(csum[:, :GLA_DK] + csum[:, GLA_DK:])
    for h in heads:
        b_last.append(b[h].reshape(n_c, c, GLA_DK)[:, c - 1:c, :])
        b_to_end = (b_last[h] - b[h].reshape(n_c, c, GLA_DK)).reshape(tt, GLA_DK)
        q = q_ref[0, :, kcols(h)].astype(F32)
        k = k_ref[0, :, kcols(h)].astype(F32)
        q_e.append((q * ((GLA_DK ** -0.5) * jnp.exp(b[h]))).astype(BF16))
        k_e.append((k * jnp.exp(-b[h])).astype(BF16))
        k_end.append((k * jnp.exp(b_to_end)).astype(BF16))
    o_intra = []
    for h in heads:
        parts = []
        for s0 in range(0, tt, sub):
            ss = slice(s0, s0 + sub)
            att = lax.dot_general(q_e[h][ss], k_e[h][ss], _NT, preferred_element_type=F32)
            att = jnp.where(keep_ref[...] > 0.0, att, 0.0).astype(BF16)
            parts.append(_dot(att, v_ref[0, ss, vcols(h)]))
        o_intra.append(jnp.concatenate(parts, axis=0))
    decay = [jnp.exp(b_last[h].reshape(n_c, GLA_DK).T) for h in heads]
    st = [st_ref[h] for h in heads]
    outs = [[] for _ in heads]
    for ci in range(n_c):
        rs = slice(ci * c, (ci + 1) * c)
        for h in heads:
            outs[h].append(o_intra[h][rs] + _dot(q_e[h][rs], st[h].astype(BF16)))
            st[h] = decay[h][:, ci:ci + 1] * st[h] + lax.dot_general(
                k_end[h][rs], v_ref[0, rs, vcols(h)], _TN, preferred_element_type=F32)
    for h in heads:
        st_ref[h] = st[h]
        o = jnp.concatenate(outs[h], axis=0)
        o = o * _rms_scale(o) * nw_ref[...]
        gate = og_ref[0, :, vcols(h)].astype(F32)
        o_ref[0, :, vcols(h)] = (o * (gate * _sigmoid(gate))).astype(BF16)


def _gla(pg, w2_pad, b_gk2, gla_norm_w):
    bsz, seq, _ = pg.shape
    t = T_GLA
    hps = GLA_HEADS_PER_STEP
    wk, wv = hps * GLA_DK, hps * GLA_DV
    kq = GLA_KEY_DIM // wk
    kv = 2 * GLA_KEY_DIM // wv
    kg = kv + GLA_VALUE_DIM // wv
    klr = (2 * GLA_KEY_DIM + 2 * GLA_VALUE_DIM) // LANES
    return pl.pallas_call(
        _gla_kernel,
        grid=(bsz, GLA_HEADS // hps, seq // t),
        in_specs=[
            pl.BlockSpec((1, t, wk), lambda b, h, i: (b, i, h)),
            pl.BlockSpec((1, t, wk), lambda b, h, i: (b, i, kq + h)),
            pl.BlockSpec((1, t, wv), lambda b, h, i: (b, i, kv + h)),
            pl.BlockSpec((1, t, wv), lambda b, h, i: (b, i, kg + h)),
            pl.BlockSpec((1, t, LANES), lambda b, h, i: (b, i, klr)),
            pl.BlockSpec((LANES, wk), lambda b, h, i: (0, h)),
            pl.BlockSpec((1, wk), lambda b, h, i: (0, h)),
            pl.BlockSpec((1, GLA_DV), lambda b, h, i: (0, 0)),
        ],
        out_specs=pl.BlockSpec((1, t, wv), lambda b, h, i: (b, i, h)),
        out_shape=jax.ShapeDtypeStruct((bsz, seq, GLA_VALUE_DIM), BF16),
        scratch_shapes=[pltpu.VMEM((hps, GLA_DK, GLA_DV), F32), pltpu.VMEM((GLA_SUB, GLA_SUB), BF16),
                        pltpu.VMEM((GLA_SUB, GLA_SUB), F32)],
        compiler_params=pltpu.CompilerParams(
            dimension_semantics=("parallel", "parallel", "arbitrary"), vmem_limit_bytes=VMEM_LIMIT),
        name="gla",
    )(pg, pg, pg, pg, pg, w2_pad, b_gk2.reshape(1, GLA_KEY_DIM), gla_norm_w.reshape(1, GLA_DV))


def _mix_kernel(x_ref, o0_ref, l0_ref, o1_ref, l1_ref, o2_ref, l2_ref, ob_ref, pm_ref,
                wa_ref, wb_ref, wo_ref, ln2_ref, wrh_ref, wrl_ref, br_ref,
                h1_ref, hn_ref, ri_ref, ric_ref, meta_ref, cbefore_ref, cnt_ref,
                po1_ref, pl1_ref, po2_ref, pl2_ref, carry_ref, logit_ref):
    step = pl.program_id(0)
    tm = x_ref.shape[0]

    @pl.when(step == 0)
    def _():
        carry_ref[...] = jnp.zeros_like(carry_ref)
        logit_ref[...] = jnp.zeros_like(logit_ref)

    routed = step > 0
    lane = lax.broadcasted_iota(jnp.int32, (tm, LANES), 1).astype(F32)
    route = {"work": None, "vals": [], "idxs": []}

    def topk_round():
        work = route["work"]
        m = jnp.max(work, axis=-1, keepdims=True)
        idx = jnp.min(jnp.where(work == m, lane, float(LANES)), axis=-1, keepdims=True)
        route["vals"].append(m)
        route["idxs"].append(idx)
        route["work"] = jnp.where(lane == idx, -jnp.inf, work)

    route["work"] = jnp.where(lane < N_EXPERTS, logit_ref[(step + 1) % 2], -jnp.inf)

    for o_ref, l_ref, po_ref, pl_ref, (_, d) in ((o1_ref, l1_ref, po1_ref, pl1_ref, DA_GROUPS[1]),
                                                 (o2_ref, l2_ref, po2_ref, pl2_ref, DA_GROUPS[2])):
        topk_round()
        n = tm // d
        for r in range(d):
            pl_ref[pl.ds(r, n, stride=d), :] = l_ref[0, r]
            for h in range(DA_HEADS):
                sl = slice(h * DA_HEAD_DIM, (h + 1) * DA_HEAD_DIM)
                po_ref[h, pl.ds(r, n, stride=d), :] = o_ref[0, r, :, sl].astype(F32)

    l0, l1, l2 = l0_ref[...], pl1_ref[...], pl2_ref[...]
    mx = jnp.maximum(jnp.maximum(l0, l1), l2)
    e0, e1, e2 = jnp.exp(l0 - mx), jnp.exp(l1 - mx), jnp.exp(l2 - mx)
    inv = 1.0 / (e0 + e1 + e2)
    w0, w1, w2 = e0 * inv, e1 * inv, e2 * inv
    topk_round()
    heads = []
    for h in range(DA_HEADS):
        sl = slice(h * DA_HEAD_DIM, (h + 1) * DA_HEAD_DIM)
        at = slice(h * LSE_LANES, h * LSE_LANES + 1)
        o_h = w0[:, at] * o0_ref[:, sl].astype(F32) + w1[:, at] * po1_ref[h] + w2[:, at] * po2_ref[h]
        heads.append(o_h.astype(BF16))
    o_a = jnp.concatenate(heads, axis=-1)
    topk_round()

    gates = pm_ref[...].astype(F32)
    mixed = (gates[:, :D_MODEL] * _dot(o_a, wa_ref[...])
             + gates[:, D_MODEL:] * _dot(ob_ref[...], wb_ref[...]))
    vals, idxs = route["vals"], route["idxs"]
    assert len(vals) == TOP_K
    exps = [jnp.exp(v - vals[0]) for v in vals]
    denom = exps[0] + exps[1] + exps[2] + exps[3]
    onehot = jnp.zeros((tm, LANES), F32)
    for idx in idxs:
        onehot = onehot + jnp.where(lane == idx, 1.0, 0.0)

    h1 = x_ref[...] + _dot(mixed.astype(BF16), wo_ref[...])
    h1_ref[...] = h1
    hn = h1 * _rms_scale(h1) * ln2_ref[...]

    row = lax.broadcasted_iota(jnp.int32, (tm, tm), 0)
    col = lax.broadcasted_iota(jnp.int32, (tm, tm), 1)
    below = jnp.where(col < row, 1.0, 0.0).astype(BF16)
    before = _dot(below, onehot.astype(BF16)) + carry_ref[0:1, :]
    ranks = [jnp.sum(jnp.where(lane == idx, before, 0.0), axis=-1, keepdims=True) for idx in idxs]
    running = carry_ref[0:1, :]
    for j in range(tm // TM_ROW):
        cbefore_ref[j] = jnp.broadcast_to(running, cbefore_ref.shape[1:])
        running = running + jnp.sum(onehot[j * TM_ROW:(j + 1) * TM_ROW], axis=0, keepdims=True)
    carry = jnp.where(routed, running, carry_ref[0:1, :])
    carry_ref[...] = jnp.broadcast_to(carry, carry_ref.shape)
    cnt_ref[...] = jnp.broadcast_to(carry, cnt_ref.shape)

    hn_hi = hn.astype(BF16)
    hn_ref[...] = hn_hi

    ri = jnp.zeros((tm, LANES), F32)
    for j, val in enumerate(idxs + ranks):
        ri = jnp.where(lane == float(j), val, ri)
    ric_ref[...] = ri.astype(jnp.int32)
    ri_t = ri.T[:2 * TOP_K].astype(jnp.int32)
    for j in range(tm // TM_ROW):
        ri_ref[j] = ri_t[:, j * TM_ROW:(j + 1) * TM_ROW]
    meta = jnp.zeros((tm, LANES), F32)
    for k in range(TOP_K):
        gate = exps[k] / denom
        gate_hi = gate.astype(BF16).astype(F32)
        meta = jnp.where(lane == float(k), idxs[k], meta)
        meta = jnp.where(lane == float(TOP_K + k), gate_hi, meta)
        meta = jnp.where(lane == float(2 * TOP_K + k), gate - gate_hi, meta)
    meta_ref[...] = meta.astype(BF16)

    hn_lo = (hn - hn_hi.astype(F32)).astype(BF16)
    logit_ref[step % 2] = (_dot(hn_hi, wrh_ref[...]) + _dot(hn_lo, wrh_ref[...])
                           + _dot(hn_hi, wrl_ref[...]) + br_ref[...])


def _mix(x2, o0, l0, o1, l1, o2, l2, o_b, pm, wa, wb, wo, ln2_w, wr_hi, wr_lo, br_pad):
    n_tok = x2.shape[0]
    tm = TM_MIX
    bsz = o1.shape[0]
    d1, d2 = DA_GROUPS[1][1], DA_GROUPS[2][1]
    tiles_per_seq = (n_tok // bsz) // tm
    n_tiles = n_tok // tm

    def tile(i):
        return jnp.minimum(i, n_tiles - 1)

    def routed(i):
        return jnp.maximum(i - 1, 0)

    def rows(width, which=tile):
        return pl.BlockSpec((tm, width), lambda i: (which(i), 0))

    def residue_major(d, width):
        return pl.BlockSpec((1, d, tm // d, width),
                            lambda i: (tile(i) // tiles_per_seq, 0, tile(i) % tiles_per_seq, 0))

    def whole(arr):
        return pl.BlockSpec(arr.shape, lambda i: (0,) * arr.ndim)

    ln2 = ln2_w.reshape(1, D_MODEL)
    return pl.pallas_call(
        _mix_kernel,
        grid=(n_tiles + 1,),
        in_specs=[rows(D_MODEL), rows(DA_WIDTH), rows(LANES),
                  residue_major(d1, DA_WIDTH), residue_major(d1, LANES),
                  residue_major(d2, DA_WIDTH), residue_major(d2, LANES),
                  rows(GLA_VALUE_DIM), rows(MERGE_W),
                  whole(wa), whole(wb), whole(wo), whole(ln2), whole(wr_hi), whole(wr_lo), whole(br_pad)],
        out_specs=(rows(D_MODEL), rows(D_MODEL),
                   pl.BlockSpec((tm // TM_ROW, 2 * TOP_K, TM_ROW), lambda i: (routed(i), 0, 0)),
                   rows(LANES, routed), rows(LANES, routed),
                   pl.BlockSpec((tm // TM_ROW, 8, LANES), lambda i: (routed(i), 0, 0)),
                   pl.BlockSpec((8, LANES), lambda i: (0, 0))),
        out_shape=(jax.ShapeDtypeStruct((n_tok, D_MODEL), F32),
                   jax.ShapeDtypeStruct((n_tok, D_MODEL), BF16),
                   jax.ShapeDtypeStruct((n_tok // TM_ROW, 2 * TOP_K, TM_ROW), jnp.int32),
                   jax.ShapeDtypeStruct((n_tok, LANES), jnp.int32),
                   jax.ShapeDtypeStruct((n_tok, LANES), BF16),
                   jax.ShapeDtypeStruct((n_tok // TM_ROW, 8, LANES), F32),
                   jax.ShapeDtypeStruct((8, LANES), F32)),
        scratch_shapes=[pltpu.VMEM((DA_HEADS, tm, DA_HEAD_DIM), F32), pltpu.VMEM((tm, LANES), F32),
                        pltpu.VMEM((DA_HEADS, tm, DA_HEAD_DIM), F32), pltpu.VMEM((tm, LANES), F32),
                        pltpu.VMEM((8, LANES), F32), pltpu.VMEM((2, tm, LANES), F32)],
        compiler_params=pltpu.CompilerParams(
            dimension_semantics=("arbitrary",), vmem_limit_bytes=VMEM_LIMIT),
        name="mix",
    )(x2, o0, l0, o1, l1, o2, l2, o_b, pm, wa, wb, wo, ln2, wr_hi, wr_lo, br_pad)


def _pack_pairs(x):
    n = x.shape[1] // 2
    rounded = x.astype(BF16).astype(F32)
    lo = lax.bitcast_convert_type(rounded[:, :n], U32) >> 16
    hi = lax.bitcast_convert_type(rounded[:, n:], U32) & jnp.uint32(0xFFFF0000)
    return hi | lo


def _unpack_pairs(u):
    lo = lax.bitcast_convert_type(u << 16, F32).astype(BF16)
    hi = lax.bitcast_convert_type(u & jnp.uint32(0xFFFF0000), F32).astype(BF16)
    return lo, hi


def _dispatch_kernel(zstart_ref, nwin_ref, wsrc_ref, wdst_ref, pos_ref, meta_ref, hn_ref, xin_ref,
                     buf_ref, zero_ref, sem, zsem):
    i = pl.program_id(0)
    tm = hn_ref.shape[0]
    n_buf_rows = buf_ref.shape[1]
    slot = i % 2

    def window_copy(s, buf_slot):
        src = pl.multiple_of(wsrc_ref[i * N_SLOT + s], ROW_ALIGN)
        dst = pl.multiple_of(wdst_ref[i * N_SLOT + s], ROW_ALIGN)
        return pltpu.make_async_copy(buf_ref.at[buf_slot, pl.ds(src, WIN), :],
                                     xin_ref.at[pl.ds(dst, WIN), :], sem)

    def wait_windows(step):
        def body(s, carry):
            pltpu.make_async_copy(buf_ref.at[0, pl.ds(0, WIN), :], xin_ref.at[pl.ds(0, WIN), :], sem).wait()
            return carry
        lax.fori_loop(0, nwin_ref[step], body, 0)

    @pl.when(i == 0)
    def _():
        zero_ref[...] = jnp.zeros_like(zero_ref)

        def zero_copy(j):
            start = pl.multiple_of(jnp.maximum(zstart_ref[j], 0), ZCHUNK)
            return pltpu.make_async_copy(zero_ref, xin_ref.at[pl.ds(start, ZCHUNK), :], zsem)

        def start_one(j, carry):
            @pl.when(zstart_ref[j] >= 0)
            def _():
                zero_copy(j).start()
            return carry

        def wait_one(j, carry):
            @pl.when(zstart_ref[j] >= 0)
            def _():
                zero_copy(j).wait()
            return carry

        lax.fori_loop(0, zstart_ref.shape[0], start_one, 0)
        lax.fori_loop(0, zstart_ref.shape[0], wait_one, 0)

    row = lax.broadcasted_iota(jnp.int32, (n_buf_rows, tm), 0)

    perm = jnp.zeros((n_buf_rows, tm), F32)
    for k in range(TOP_K):
        perm = perm + jnp.where(row == pos_ref[0, k:k + 1, :], 1.0, 0.0)
    perm = perm.astype(BF16)
    buf_ref[slot, :, :D_MODEL // 2] = _pack_pairs(_dot(perm, hn_ref[...]))
    buf_ref[slot, :, D_MODEL // 2:] = lax.bitcast_convert_type(_dot(perm, meta_ref[...]), U32)

    @pl.when(i > 0)
    def _():
        wait_windows(i - 1)

    for buf_slot in range(2):
        @pl.when(slot == buf_slot)
        def _():
            def issue(s, carry):
                window_copy(s, buf_slot).start()
                return carry
            lax.fori_loop(0, nwin_ref[i], issue, 0)

    @pl.when(i == pl.num_programs(0) - 1)
    def _():
        wait_windows(i)


def _dispatch(zero_start, n_win, win_src, win_dst, pos, meta, hn, n_rows):
    n_tok = hn.shape[0]
    tm = TM_ROW
    n_tiles = n_tok // tm
    n_buf_rows = SORT_ROWS
    return pl.pallas_call(
        _dispatch_kernel,
        grid_spec=pltpu.PrefetchScalarGridSpec(
            num_scalar_prefetch=4,
            grid=(n_tiles,),
            in_specs=[
                pl.BlockSpec((1, TOP_K, tm), lambda i, *_: (i, 0, 0)),
                pl.BlockSpec((tm, LANES), lambda i, *_: (i, 0)),
                pl.BlockSpec((tm, D_MODEL), lambda i, *_: (i, 0)),
            ],
            out_specs=pl.BlockSpec(memory_space=pl.ANY),
            scratch_shapes=[pltpu.VMEM((2, n_buf_rows, ROW_W), U32),
                            pltpu.VMEM((ZCHUNK, ROW_W), U32),
                            pltpu.SemaphoreType.DMA(()), pltpu.SemaphoreType.DMA(())],
        ),
        out_shape=jax.ShapeDtypeStruct((n_rows, ROW_W), U32),
        compiler_params=pltpu.CompilerParams(
            dimension_semantics=("arbitrary",), vmem_limit_bytes=VMEM_LIMIT),
        name="dispatch",
    )(zero_start, n_win, win_src.reshape(-1), win_dst.reshape(-1), pos, meta, hn)


def _expert_kernel(be_ref, live_ref, x_ref, wu_hbm, bu_ref, wd_hbm, bd_ref, y_ref,
                   wu32_ref, wd32_ref, wu16_ref, wd16_ref, wsem):
    i = pl.program_id(0)
    live = live_ref[i]
    expert_id = be_ref[i]

    def weight_copies(expert, slot):
        return (pltpu.make_async_copy(wu_hbm.at[expert], wu32_ref.at[slot], wsem.at[0, slot]),
                pltpu.make_async_copy(wd_hbm.at[expert], wd32_ref.at[slot], wsem.at[1, slot]))

    @pl.when(i == 0)
    def _():
        for cp in weight_copies(expert_id, 0):
            cp.start()

    for slot in range(2):
        @pl.when(((i == 0) | (expert_id != be_ref[jnp.maximum(i - 1, 0)])) & (expert_id % 2 == slot))
        def _():
            for cp in weight_copies(expert_id, slot):
                cp.wait()

            @pl.when(expert_id + 1 < N_EXPERTS)
            def _():
                for cp in weight_copies(expert_id + 1, 1 - slot):
                    cp.start()

            wu16_ref[...] = wu32_ref[slot].astype(BF16)
            wd16_ref[...] = wd32_ref[slot].astype(BF16)

    def compute(m):
        half = D_MODEL // 2
        x_lo, x_hi = _unpack_pairs(x_ref[:m, :half])
        meta = lax.bitcast_convert_type(x_ref[:m, half:half + 3 * TOP_K], F32)
        expert = be_ref[i].astype(F32)
        gate = jnp.zeros((m, 1), F32)
        for k in range(TOP_K):
            weight = meta[:, TOP_K + k:TOP_K + k + 1] + meta[:, 2 * TOP_K + k:2 * TOP_K + k + 1]
            gate = gate + jnp.where(meta[:, k:k + 1] == expert, weight, 0.0)
        hu = _dot(x_lo, wu16_ref[:half, :]) + _dot(x_hi, wu16_ref[half:, :]) + bu_ref[0]
        x_glu = jnp.minimum(hu[:, :D_FF], SWIGLU_LIMIT)
        x_lin = jnp.clip(hu[:, D_FF:], -SWIGLU_LIMIT, SWIGLU_LIMIT)
        act = x_glu * _sigmoid(SWIGLU_ALPHA * x_glu) * (x_lin + 1.0)
        y_ref[:m, :] = _pack_pairs((_dot(act.astype(BF16), wd16_ref[...]) + bd_ref[0]) * gate)
        if m < TB:
            y_ref[m:, :] = jnp.zeros((TB - m, y_ref.shape[1]), y_ref.dtype)

    @pl.when(live == 0)
    def _():
        y_ref[...] = jnp.zeros_like(y_ref)

    for m in range(TB_STEP, TB + 1, TB_STEP):
        @pl.when((live > m - TB_STEP) & (live <= m))
        def _():
            compute(m)


def _experts(block_e, block_live, xin, w_up, b_up, w_down, b_down):
    n_rows = xin.shape[0]
    return pl.pallas_call(
        _expert_kernel,
        grid_spec=pltpu.PrefetchScalarGridSpec(
            num_scalar_prefetch=2,
            grid=(n_rows // TB,),
            in_specs=[
                pl.BlockSpec((TB, ROW_W), lambda i, be, nu: (i, 0)),
                pl.BlockSpec(memory_space=pl.ANY),
                pl.BlockSpec((1, 1, 2 * D_FF), lambda i, be, nu: (be[i], 0, 0)),
                pl.BlockSpec(memory_space=pl.ANY),
                pl.BlockSpec((1, 1, D_MODEL), lambda i, be, nu: (be[i], 0, 0)),
            ],
            out_specs=pl.BlockSpec((TB, D_MODEL // 2), lambda i, be, nu: (i, 0)),
            scratch_shapes=[pltpu.VMEM((2, D_MODEL, 2 * D_FF), F32), pltpu.VMEM((2, D_FF, D_MODEL), F32),
                            pltpu.VMEM((D_MODEL, 2 * D_FF), BF16), pltpu.VMEM((D_FF, D_MODEL), BF16),
                            pltpu.SemaphoreType.DMA((2, 2))],
        ),
        out_shape=jax.ShapeDtypeStruct((n_rows, D_MODEL // 2), U32),
        compiler_params=pltpu.CompilerParams(
            dimension_semantics=("arbitrary",), vmem_limit_bytes=VMEM_LIMIT),
        name="experts",
    )(block_e, block_live, xin, w_up, b_up.reshape(N_EXPERTS, 1, 2 * D_FF),
      w_down, b_down.reshape(N_EXPERTS, 1, D_MODEL))


def _combine_kernel(nwin_ref, wsrc_ref, wdst_ref, ric_ref, delta_ref, ricn_ref, deltan_ref, yb_ref, h1_ref,
                    lnf_ref, o_ref, stage_ref, buf_ref, pick_ref, sem):
    i = pl.program_id(0)
    tm = h1_ref.shape[0]
    n_buf_rows = buf_ref.shape[0]

    parity = i % 2

    def fetch(step, stage_slot):
        def body(s, carry):
            dst = pl.multiple_of(wdst_ref[step * N_SLOT + s], ROW_ALIGN)
            pltpu.make_async_copy(
                yb_ref.at[pl.ds(dst, WIN), :],
                stage_ref.at[stage_slot, pl.ds(pl.multiple_of(s * WIN, WIN), WIN), :],
                sem.at[stage_slot]).start()
            return carry
        lax.fori_loop(0, nwin_ref[step], body, 0)

    lane = lax.broadcasted_iota(jnp.int32, (tm, LANES), 1).astype(F32)
    col = lax.broadcasted_iota(jnp.int32, (tm, n_buf_rows), 1)

    def pick_round(pick, k, ric, delta):
        offset = jnp.sum(jnp.where(lane == ric[:, k:k + 1], delta, 0.0), axis=-1, keepdims=True)
        pos = (ric[:, TOP_K + k:TOP_K + k + 1] + offset).astype(jnp.int32)
        return pick + jnp.where(col == pos, 1.0, 0.0)

    @pl.when(i == 0)
    def _():
        buf_ref[...] = jnp.zeros_like(buf_ref)
        fetch(i, 0)
        pick = jnp.zeros((tm, n_buf_rows), F32)
        ric = ric_ref[...].astype(F32)
        for k in range(TOP_K):
            pick = pick_round(pick, k, ric, delta_ref[0])
        pick_ref[0] = pick.astype(BF16)

    for stage_slot in range(2):
        @pl.when((i + 1 < pl.num_programs(0)) & (parity != stage_slot))
        def _():
            fetch(i + 1, stage_slot)

    for stage_slot in range(2):
        @pl.when(parity == stage_slot)
        def _():
            def drain(s, carry):
                pltpu.make_async_copy(yb_ref.at[pl.ds(0, WIN), :], stage_ref.at[stage_slot, pl.ds(0, WIN), :],
                                      sem.at[stage_slot]).wait()
                return carry

            def compact(s, carry):
                src = pl.multiple_of(wsrc_ref[i * N_SLOT + s], ROW_ALIGN)
                buf_ref[pl.ds(src, WIN), :] = stage_ref[stage_slot, pl.ds(pl.multiple_of(s * WIN, WIN), WIN), :]
                return carry

            lax.fori_loop(0, nwin_ref[i], drain, 0)
            lax.fori_loop(0, nwin_ref[i], compact, 0)

    ric_next = ricn_ref[...].astype(F32)
    delta_next = deltan_ref[0]
    pick = pick_ref[parity]
    nxt = jnp.zeros((tm, n_buf_rows), F32)
    y_lo, y_hi = _unpack_pairs(buf_ref[...])
    nxt = pick_round(nxt, 0, ric_next, delta_next)
    left = _dot(pick, y_lo)
    nxt = pick_round(nxt, 1, ric_next, delta_next)
    right = _dot(pick, y_hi)
    nxt = pick_round(nxt, 2, ric_next, delta_next)
    h2 = h1_ref[...] + jnp.concatenate([left, right], axis=-1)
    nxt = pick_round(nxt, 3, ric_next, delta_next)
    o_ref[...] = h2 * _rms_scale(h2) * lnf_ref[...]
    pick_ref[1 - parity] = nxt.astype(BF16)


def _combine(n_win, win_src, win_dst, ric, delta, yb, h1, lnf_w):
    n_tok = h1.shape[0]
    tm = TM_ROW
    n_tiles = n_tok // tm
    return pl.pallas_call(
        _combine_kernel,
        grid_spec=pltpu.PrefetchScalarGridSpec(
            num_scalar_prefetch=3,
            grid=(n_tiles,),
            in_specs=[
                pl.BlockSpec((tm, LANES), lambda i, *_: (i, 0)),
                pl.BlockSpec((1, 1, LANES), lambda i, *_: (i, 0, 0)),
                pl.BlockSpec((tm, LANES), lambda i, *_: (jnp.minimum(i + 1, n_tiles - 1), 0)),
                pl.BlockSpec((1, 1, LANES), lambda i, *_: (jnp.minimum(i + 1, n_tiles - 1), 0, 0)),
                pl.BlockSpec(memory_space=pl.ANY),
                pl.BlockSpec((tm, D_MODEL), lambda i, *_: (i, 0)),
                pl.BlockSpec((1, D_MODEL), lambda i, *_: (0, 0)),
            ],
            out_specs=pl.BlockSpec((tm, D_MODEL), lambda i, *_: (i, 0)),
            scratch_shapes=[pltpu.VMEM((2, N_SLOT * WIN, D_MODEL // 2), U32),
                            pltpu.VMEM((SORT_ROWS, D_MODEL // 2), U32),
                            pltpu.VMEM((2, tm, SORT_ROWS), BF16), pltpu.SemaphoreType.DMA((2,))],
        ),
        out_shape=jax.ShapeDtypeStruct((n_tok, D_MODEL), F32),
        compiler_params=pltpu.CompilerParams(
            dimension_semantics=("arbitrary",), vmem_limit_bytes=VMEM_LIMIT),
        name="combine",
    )(n_win, win_src.reshape(-1), win_dst.reshape(-1), ric, delta, ric, delta, yb, h1,
      lnf_w.reshape(1, D_MODEL))


def _layer(h, ln1_w, w_in, w_gk2, b_gk2, gla_norm_w, w_proj_a, w_proj_b, w_out,
           ln2_w, w_router, b_router, w_up, b_up, w_down, b_down, lnf_w):
    bsz, seq, _ = h.shape
    n_tok = bsz * seq

    n_main = 3 * QKV_W + 2 * GLA_KEY_DIM + 2 * GLA_VALUE_DIM
    pad = LANES - GLA_GATE_RANK
    w_main = w_in[:, :n_main].astype(BF16)
    w_tail = jnp.concatenate(
        [w_in[:, n_main:n_main + GLA_GATE_RANK], jnp.zeros((D_MODEL, pad), F32),
         w_in[:, n_main + GLA_GATE_RANK:]], axis=1).astype(BF16)
    w2_pad = jnp.concatenate([w_gk2, jnp.zeros((pad, GLA_KEY_DIM), F32)], axis=0).astype(BF16)
    wr_pad = jnp.concatenate([w_router, jnp.zeros((D_MODEL, LANES - N_EXPERTS), F32)], axis=1)
    wr_hi = wr_pad.astype(BF16)
    wr_lo = (wr_pad - wr_hi.astype(F32)).astype(BF16)
    br_pad = jnp.concatenate([b_router, jnp.zeros((LANES - N_EXPERTS,), F32)]).reshape(1, LANES)

    pa0, pa1, pa2, pg, pm = _inproj(h, ln1_w, w_main, w_tail)
    o0, l0 = _dil_attn(pa0)
    o1, l1 = _dil_attn(pa1)
    o2, l2 = _dil_attn(pa2)
    o_b = _gla(pg, w2_pad, b_gk2, gla_norm_w)

    h1, hn, ri, ric, meta, carry_f, cnt = _mix(
        h.reshape(n_tok, D_MODEL), o0.reshape(n_tok, DA_WIDTH), l0.reshape(n_tok, LANES),
        o1, l1, o2, l2, o_b.reshape(n_tok, GLA_VALUE_DIM), pm.reshape(n_tok, MERGE_W),
        w_proj_a.astype(BF16), w_proj_b.astype(BF16), w_out.astype(BF16), ln2_w, wr_hi, wr_lo, br_pad)

    i32 = jnp.int32
    n_tiles = n_tok // TM_ROW
    experts = jnp.arange(N_EXPERTS)
    counts = cnt[0, :N_EXPERTS].astype(i32)
    before = carry_f[:, 0, :N_EXPERTS].astype(i32)
    run = jnp.concatenate([before[1:], counts[None]], axis=0) - before
    run_al = (run + ROW_ALIGN - 1) // ROW_ALIGN * ROW_ALIGN
    rows_end = jnp.cumsum(run_al, axis=0)
    rows_before = rows_end - run_al
    used = rows_end[-1]
    slack = WIN - ROW_ALIGN
    padded = (used + slack + TB - 1) // TB * TB
    pad_end = jnp.cumsum(padded)
    pad_start = pad_end - padded
    n_asg = n_tok * TOP_K
    n_rows = (n_asg + n_tiles * N_EXPERTS * (ROW_ALIGN - 1)
              + N_EXPERTS * (slack + TB - 1) + TB - 1) // TB * TB
    n_blocks = n_rows // TB
    block_starts = jnp.arange(n_blocks) * TB
    block_e = jnp.minimum(jnp.sum(pad_end[None, :] <= block_starts[:, None], axis=1),
                          N_EXPERTS - 1).astype(i32)
    used_end = jnp.sum(jnp.where(block_e[:, None] == experts, (pad_start + used)[None, :], 0), axis=1)
    block_live = jnp.clip(used_end - block_starts, 0, TB).astype(i32)

    run_end = jnp.cumsum(run_al, axis=1)
    run_start = run_end - run_al
    wins = (run + WIN - 1) // WIN
    wins_end = jnp.cumsum(wins, axis=1)
    wins_start = wins_end - wins
    n_win = wins_end[:, -1].astype(i32)
    slots = jnp.arange(N_SLOT)
    slot_e = jnp.sum(wins_end[:, None, :] <= slots[None, :, None], axis=-1)
    slot_is = slot_e[..., None] == experts

    def of_slot(table):
        return jnp.sum(jnp.where(slot_is, table[:, None, :], 0), axis=-1)

    win_off = (slots[None, :] - of_slot(wins_start)) * WIN
    win_src = (of_slot(run_start) + win_off).astype(i32)
    win_dst = (of_slot(pad_start[None, :] + rows_before) + win_off).astype(i32)
    asg_is = ri[:, :TOP_K, :, None] == experts
    pos = ri[:, TOP_K:, :] + jnp.sum(jnp.where(asg_is, (run_start - before)[:, None, None, :], 0), axis=-1)
    delta = jnp.pad((run_start - before).astype(F32), ((0, 0), (0, LANES - N_EXPERTS)))

    chunk_back = ZCHUNK * (1 + jnp.arange((slack + TB - 1 + ZCHUNK - 1) // ZCHUNK + 1))
    region_chunks = pad_end[:, None] - chunk_back[None, :]
    region_ok = (region_chunks >= pad_start[:, None]) & (region_chunks + ZCHUNK > (pad_start + used)[:, None])
    tail_chunks = pad_end[-1] + ZCHUNK * jnp.arange((n_rows - n_asg) // ZCHUNK)
    zero_start = jnp.concatenate([jnp.where(region_ok, region_chunks, -1).reshape(-1),
                                  jnp.where(tail_chunks < n_rows, tail_chunks, -1)]).astype(i32)

    xin = _dispatch(zero_start, n_win, win_src, win_dst, pos.astype(i32), meta, hn, n_rows)
    yb = _experts(block_e, block_live, xin, w_up, b_up, w_down, b_down)
    out = _combine(n_win, win_src, win_dst, ric, delta.reshape(-1, 1, LANES), yb, h1, lnf_w)
    return out.reshape(bsz, seq, D_MODEL)


def kernel(x, ln1_w, w_in, w_gk2, b_gk2, gla_norm_w, w_proj_a, w_proj_b, w_out, ln2_w, w_router,
           b_router, w_up, b_up, w_down, b_down, lnf_w):
    assert x.shape[-1] == D_MODEL and ln1_w.shape[0] == 1, "one layer of width D_MODEL"
    return _layer(x, ln1_w[0], w_in[0], w_gk2[0], b_gk2[0], gla_norm_w[0], w_proj_a[0], w_proj_b[0],
                  w_out[0], ln2_w[0], w_router[0], b_router[0], w_up[0], b_up[0], w_down[0],
                  b_down[0], lnf_w)
```

```python
import jax
import jax.numpy as jnp
from jax import lax
from jax.experimental import pallas as pl
from jax.experimental.pallas import tpu as pltpu

F32 = jnp.float32
BF16 = jnp.bfloat16
U32 = jnp.uint32

D_MODEL = 1024
DA_GROUPS = ((128, 1), (512, 4), (2048, 16))
DA_HEADS = 4
DA_HEAD_DIM = 128
DA_WIDTH = DA_HEADS * DA_HEAD_DIM
DA_BLOCK = 128
GLA_HEADS = 4
GLA_KEY_DIM = D_MODEL // 2
GLA_VALUE_DIM = D_MODEL
GLA_DK = GLA_KEY_DIM // GLA_HEADS
GLA_DV = GLA_VALUE_DIM // GLA_HEADS
GLA_GATE_RANK = 16
GLA_GATE_NORMALIZER = 16.0
GLA_CHUNK = 64
N_EXPERTS = 32
TOP_K = 4
D_FF = D_MODEL
SWIGLU_ALPHA = 1.702
SWIGLU_LIMIT = 7.0
RMS_EPS = 1e-5
NEG_INF = -1e30

LANES = 128
QKV_W = 3 * DA_WIDTH
GLA_W = 2 * GLA_KEY_DIM + 2 * GLA_VALUE_DIM + LANES
MERGE_W = 2 * D_MODEL

DA_QB = 4
LSE_LANES = LANES // DA_HEADS
TM_IN = 512
N_CHUNK = 512
T_GLA = 512
GLA_SUB = 256
GLA_HEADS_PER_STEP = 4
TM_MIX = 512
TB = 512
TB_STEP = 128
TM_ROW = 256
ROW_ALIGN = 8
WIN_D = 32
WIN_C = 64


def _n_slot(win):
    return N_EXPERTS + TM_ROW * TOP_K // win


def _sort_rows(win):
    return TM_ROW * TOP_K + N_EXPERTS * (ROW_ALIGN - 1) + win
ZCHUNK = 256
ROW_W = D_MODEL // 2 + LANES
VMEM_LIMIT = 56 * 1024 * 1024
VMEM_LIMIT_INPROJ = 62 * 1024 * 1024

_NT = (((1,), (1,)), ((), ()))
_TN = (((0,), (0,)), ((), ()))


def _dot(a, b):
    return jnp.dot(a, b, preferred_element_type=F32)


def _sigmoid(x):
    return 1.0 / (1.0 + jnp.exp(-x))


def _rms_scale(x):
    return lax.rsqrt(jnp.mean(x * x, axis=-1, keepdims=True) + RMS_EPS)


def _inproj_kernel(x_ref, ln_ref, wmain_hbm, wtail_hbm, pa0_ref, pa1_ref, pa2_ref, pg_ref, pm_ref,
                   w_ref, xs_ref, xn_ref, wsem):
    tm = x_ref.shape[1]
    n_slab = D_MODEL // LANES

    @pl.when((pl.program_id(0) == 0) & (pl.program_id(1) == 0))
    def _():
        n_main = wmain_hbm.shape[1]
        main = pltpu.make_async_copy(wmain_hbm, w_ref.at[:, :n_main], wsem.at[0])
        tail = pltpu.make_async_copy(wtail_hbm, w_ref.at[:, n_main:], wsem.at[1])
        main.start()
        tail.start()
        main.wait()
        tail.wait()

    def project(out_write, col0, width, post=None):
        for c0 in range(0, width, N_CHUNK):
            cw = min(N_CHUNK, width - c0)
            val = _dot(xs_ref[...], w_ref[:, col0 + c0:col0 + c0 + cw])
            out_write(c0, cw, (val if post is None else post(val)).astype(BF16))

    x = x_ref[0]
    xn = x * _rms_scale(x) * ln_ref[...]
    xs_ref[...] = xn.astype(BF16)
    for j in range(n_slab):
        xn_ref[j] = xn[:, j * LANES:(j + 1) * LANES]

    def write_to(ref):
        def write(c0, cw, val):
            ref[0, :, c0:c0 + cw] = val
        return write

    def write_qkv(out_ref, d, n):
        def write(c0, cw, val):
            for r in range(d):
                out_ref[0, r, c0 // DA_WIDTH] = val[r * n:(r + 1) * n]
        return write

    project(write_qkv(pa0_ref, 1, tm), 0, QKV_W)
    project(write_to(pg_ref), 3 * QKV_W, GLA_W)
    project(write_to(pm_ref), 3 * QKV_W + GLA_W, MERGE_W, post=_sigmoid)

    for gi, out_ref in ((1, pa1_ref), (2, pa2_ref)):
        d = DA_GROUPS[gi][1]
        n = tm // d
        for r in range(d):
            for j in range(n_slab):
                xs_ref[r * n:(r + 1) * n, j * LANES:(j + 1) * LANES] = (
                    xn_ref[j, pl.ds(r, n, stride=d), :].astype(BF16))
        project(write_qkv(out_ref, d, n), gi * QKV_W, QKV_W)


def _inproj(x, ln1_w, w_main, w_tail):
    assert N_CHUNK == DA_WIDTH and w_main.shape[1] % LANES == 0
    bsz, seq, _ = x.shape
    tm = TM_IN
    d1, d2 = DA_GROUPS[1][1], DA_GROUPS[2][1]

    def qkv_shape(d):
        return jax.ShapeDtypeStruct((bsz, d, 3, seq // d, DA_WIDTH), BF16)

    def qkv_spec(d):
        return pl.BlockSpec((1, d, 3, tm // d, DA_WIDTH), lambda b, i: (b, 0, 0, i, 0))

    return pl.pallas_call(
        _inproj_kernel,
        grid=(bsz, seq // tm),
        in_specs=[
            pl.BlockSpec((1, tm, D_MODEL), lambda b, i: (b, i, 0)),
            pl.BlockSpec((1, D_MODEL), lambda b, i: (0, 0)),
            pl.BlockSpec(memory_space=pl.ANY),
            pl.BlockSpec(memory_space=pl.ANY),
        ],
        out_specs=(
            qkv_spec(1), qkv_spec(d1), qkv_spec(d2),
            pl.BlockSpec((1, tm, GLA_W), lambda b, i: (b, i, 0)),
            pl.BlockSpec((1, tm, MERGE_W), lambda b, i: (b, i, 0)),
        ),
        out_shape=(qkv_shape(1), qkv_shape(d1), qkv_shape(d2),
                   jax.ShapeDtypeStruct((bsz, seq, GLA_W), BF16),
                   jax.ShapeDtypeStruct((bsz, seq, MERGE_W), BF16)),
        scratch_shapes=[pltpu.VMEM((D_MODEL, w_main.shape[1] + w_tail.shape[1]), BF16),
                        pltpu.VMEM((tm, D_MODEL), BF16),
                        pltpu.VMEM((D_MODEL // LANES, tm, LANES), F32),
                        pltpu.SemaphoreType.DMA((2,))],
        compiler_params=pltpu.CompilerParams(
            dimension_semantics=("arbitrary", "arbitrary"), vmem_limit_bytes=VMEM_LIMIT_INPROJ),
        name="inproj",
    )(x, ln1_w.reshape(1, D_MODEL), w_main, w_tail)


def _dil_attn_kernel(q_ref, kp_ref, kc_ref, vp_ref, vc_ref, o_ref, l_ref, s_ref, p_ref, r_ref):
    n = pl.program_id(2)
    blk = DA_BLOCK
    qi = lax.broadcasted_iota(jnp.int32, (blk, 2 * blk), 0)
    kj = lax.broadcasted_iota(jnp.int32, (blk, 2 * blk), 1)
    band = (kj >= qi) & (kj <= qi + blk)
    band_first = (kj >= jnp.where(n > 0, qi, blk)) & (kj <= qi + blk)
    scale = DA_HEAD_DIM ** -0.5
    items = [(b, h) for b in range(DA_QB) for h in range(DA_HEADS)]

    def rows(b):
        return slice(b * blk, (b + 1) * blk)

    def cols(h):
        return slice(h * DA_HEAD_DIM, (h + 1) * DA_HEAD_DIM)

    def window(prev_ref, cur_ref, b, h):
        before = prev_ref[0, 0, 0, :, cols(h)] if b == 0 else cur_ref[0, 0, 0, rows(b - 1), cols(h)]
        return jnp.concatenate([before, cur_ref[0, 0, 0, rows(b), cols(h)]], axis=0)

    def scores(i, b, h):
        s = lax.dot_general(q_ref[0, 0, 0, rows(b), cols(h)], window(kp_ref, kc_ref, b, h), _NT,
                            preferred_element_type=F32) * scale
        s_ref[i] = jnp.where(band_first if b == 0 else band, s, NEG_INF)

    def softmax(i, b, h):
        s = s_ref[i]
        m = jnp.max(s, axis=-1, keepdims=True)
        p = jnp.exp(s - m)
        l = jnp.sum(p, axis=-1, keepdims=True)
        p_ref[i] = p.astype(BF16)
        r_ref[i] = jnp.broadcast_to(1.0 / l, (blk, DA_HEAD_DIM))
        l_ref[0, 0, rows(b), h * LSE_LANES:(h + 1) * LSE_LANES] = jnp.broadcast_to(
            m + jnp.log(l), (blk, LSE_LANES))

    def values(i, b, h):
        acc = _dot(p_ref[i], window(vp_ref, vc_ref, b, h))
        o_ref[0, 0, rows(b), cols(h)] = (acc * r_ref[i]).astype(o_ref.dtype)

    stages = (scores, softmax, values)
    for t in range(len(items) + len(stages) - 1):
        for lag, stage in enumerate(stages):
            if 0 <= t - lag < len(items):
                stage(t - lag, *items[t - lag])


def _dil_attn(pa):
    bsz, d, _, sub_len, _ = pa.shape
    rows = DA_QB * DA_BLOCK
    n_items = DA_QB * DA_HEADS

    def cur(sec):
        return pl.BlockSpec((1, 1, 1, rows, DA_WIDTH), lambda b, r, n: (b, r, sec, n, 0))

    def prev(sec):
        return pl.BlockSpec((1, 1, 1, DA_BLOCK, DA_WIDTH),
                            lambda b, r, n: (b, r, sec, jnp.maximum(n * DA_QB - 1, 0), 0))

    def out(width):
        return pl.BlockSpec((1, 1, rows, width), lambda b, r, n: (b, r, n, 0))

    return pl.pallas_call(
        _dil_attn_kernel,
        grid=(bsz, d, sub_len // rows),
        in_specs=[cur(0), prev(1), cur(1), prev(2), cur(2)],
        out_specs=(out(DA_WIDTH), out(LANES)),
        out_shape=(jax.ShapeDtypeStruct((bsz, d, sub_len, DA_WIDTH), BF16),
                   jax.ShapeDtypeStruct((bsz, d, sub_len, LANES), F32)),
        scratch_shapes=[pltpu.VMEM((n_items, DA_BLOCK, 2 * DA_BLOCK), F32),
                        pltpu.VMEM((n_items, DA_BLOCK, 2 * DA_BLOCK), BF16),
                        pltpu.VMEM((n_items, DA_BLOCK, DA_HEAD_DIM), F32)],
        compiler_params=pltpu.CompilerParams(
            dimension_semantics=("parallel", "parallel", "arbitrary"), vmem_limit_bytes=VMEM_LIMIT),
        name=f"dil_attn_d{d}",
    )(pa, pa, pa, pa, pa)


def _gla_kernel(q_ref, k_ref, v_ref, og_ref, lr_ref, w2_ref, b2_ref, nw_ref, o_ref, st_ref, mask_ref, keep_ref):
    t = pl.program_id(2)
    c = GLA_CHUNK
    tt = q_ref.shape[1]
    n_c = tt // c

    sub = mask_ref.shape[0]

    @pl.when(t == 0)
    def _():
        st_ref[...] = jnp.zeros_like(st_ref)
        row = lax.broadcasted_iota(jnp.int32, (sub, sub), 0)
        col = lax.broadcasted_iota(jnp.int32, (sub, sub), 1)
        keep = (col <= row) & (col >= row - row % c)
        keep_ref[...] = jnp.where(keep, 1.0, 0.0)
        mask_ref[...] = jnp.where(keep, 1.0, 0.0).astype(BF16)

    heads = range(q_ref.shape[2] // GLA_DK)
    mask = mask_ref[...]

    def kcols(h):
        return slice(h * GLA_DK, (h + 1) * GLA_DK)

    def vcols(h):
        return slice(h * GLA_DV, (h + 1) * GLA_DV)

    gpre = _dot(lr_ref[0], w2_ref[...]) + b2_ref[...]
    forget = (jnp.minimum(gpre, 0.0) - jnp.log(1.0 + jnp.exp(-jnp.abs(gpre)))) / GLA_GATE_NORMALIZER
    g_hi = forget.astype(BF16)
    g_lo = (forget - g_hi.astype(F32)).astype(BF16)
    b, b_last, q_e, k_e, k_end = [], [], [], [], []
    for h in heads:
        g_cat = jnp.concatenate([g_hi[:, kcols(h)], g_lo[:, kcols(h)]], axis=-1)
        csum = jnp.concatenate([_dot(mask, g_cat[s0:s0 + sub]) for s0 in range(0, tt, sub)], axis=0)
        b.append(csum[:, :GLA_DK] + csum[:, GLA_DK:])
    for h in heads:
        b_last.append(b[h].reshape(n_c, c, GLA_DK)[:, c - 1:c, :])
        b_to_end = (b_last[h] - b[h].reshape(n_c, c, GLA_DK)).reshape(tt, GLA_DK)
        q = q_ref[0, :, kcols(h)].astype(F32)
        k = k_ref[0, :, kcols(h)].astype(F32)
        q_e.append((q * ((GLA_DK ** -0.5) * jnp.exp(b[h]))).astype(BF16))
        k_e.append((k * jnp.exp(-b[h])).astype(BF16))
        k_end.append((k * jnp.exp(b_to_end)).astype(BF16))
    o_intra = []
    for h in heads:
        parts = []
        for s0 in range(0, tt, sub):
            ss = slice(s0, s0 + sub)
            att = lax.dot_general(q_e[h][ss], k_e[h][ss], _NT, preferred_element_type=F32)
            att = jnp.where(keep_ref[...] > 0.0, att, 0.0).astype(BF16)
            parts.append(_dot(att, v_ref[0, ss, vcols(h)]))
        o_intra.append(jnp.concatenate(parts, axis=0))
    decay = [jnp.exp(b_last[h].reshape(n_c, GLA_DK).T) for h in heads]
    st = [st_ref[h] for h in heads]
    outs = [[] for _ in heads]
    for ci in range(n_c):
        rs = slice(ci * c, (ci + 1) * c)
        for h in heads:
            outs[h].append(o_intra[h][rs] + _dot(q_e[h][rs], st[h].astype(BF16)))
            st[h] = decay[h][:, ci:ci + 1] * st[h] + lax.dot_general(
                k_end[h][rs], v_ref[0, rs, vcols(h)], _TN, preferred_element_type=F32)
    for h in heads:
        st_ref[h] = st[h]
        o = jnp.concatenate(outs[h], axis=0)
        o = o * _rms_scale(o) * nw_ref[...]
        gate = og_ref[0, :, vcols(h)].astype(F32)
        o_ref[0, :, vcols(h)] = (o * (gate * _sigmoid(gate))).astype(BF16)


def _gla(pg, w2_pad, b_gk2, gla_norm_w):
    bsz, seq, _ = pg.shape
    t = T_GLA
    hps = GLA_HEADS_PER_STEP
    wk, wv = hps * GLA_DK, hps * GLA_DV
    kq = GLA_KEY_DIM // wk
    kv = 2 * GLA_KEY_DIM // wv
    kg = kv + GLA_VALUE_DIM // wv
    klr = (2 * GLA_KEY_DIM + 2 * GLA_VALUE_DIM) // LANES
    return pl.pallas_call(
        _gla_kernel,
        grid=(bsz, GLA_HEADS // hps, seq // t),
        in_specs=[
            pl.BlockSpec((1, t, wk), lambda b, h, i: (b, i, h)),
            pl.BlockSpec((1, t, wk), lambda b, h, i: (b, i, kq + h)),
            pl.BlockSpec((1, t, wv), lambda b, h, i: (b, i, kv + h)),
            pl.BlockSpec((1, t, wv), lambda b, h, i: (b, i, kg + h)),
            pl.BlockSpec((1, t, LANES), lambda b, h, i: (b, i, klr)),
            pl.BlockSpec((LANES, wk), lambda b, h, i: (0, h)),
            pl.BlockSpec((1, wk), lambda b, h, i: (0, h)),
            pl.BlockSpec((1, GLA_DV), lambda b, h, i: (0, 0)),
        ],
        out_specs=pl.BlockSpec((1, t, wv), lambda b, h, i: (b, i, h)),
        out_shape=jax.ShapeDtypeStruct((bsz, seq, GLA_VALUE_DIM), BF16),
        scratch_shapes=[pltpu.VMEM((hps, GLA_DK, GLA_DV), F32), pltpu.VMEM((GLA_SUB, GLA_SUB), BF16),
                        pltpu.VMEM((GLA_SUB, GLA_SUB), F32)],
        compiler_params=pltpu.CompilerParams(
            dimension_semantics=("parallel", "parallel", "arbitrary"), vmem_limit_bytes=VMEM_LIMIT),
        name="gla",
    )(pg, pg, pg, pg, pg, w2_pad, b_gk2.reshape(1, GLA_KEY_DIM), gla_norm_w.reshape(1, GLA_DV))


def _mix_kernel(x_ref, o0_ref, l0_ref, o1_ref, l1_ref, o2_ref, l2_ref, ob_ref, pm_ref,
                wa_ref, wb_ref, wo_ref, ln2_ref, wrh_ref, wrl_ref, br_ref,
                h1_ref, hn_ref, ri_ref, ric_ref, meta_ref, cbefore_ref, cnt_ref,
                po1_ref, pl1_ref, po2_ref, pl2_ref, carry_ref, logit_ref):
    step = pl.program_id(0)
    tm = x_ref.shape[0]

    @pl.when(step == 0)
    def _():
        carry_ref[...] = jnp.zeros_like(carry_ref)
        logit_ref[...] = jnp.zeros_like(logit_ref)

    routed = step > 0
    lane = lax.broadcasted_iota(jnp.int32, (tm, LANES), 1).astype(F32)
    route = {"work": None, "vals": [], "idxs": []}

    def topk_round():
        work = route["work"]
        m = jnp.max(work, axis=-1, keepdims=True)
        idx = jnp.min(jnp.where(work == m, lane, float(LANES)), axis=-1, keepdims=True)
        route["vals"].append(m)
        route["idxs"].append(idx)
        route["work"] = jnp.where(lane == idx, -jnp.inf, work)

    route["work"] = jnp.where(lane < N_EXPERTS, logit_ref[(step + 1) % 2], -jnp.inf)

    for o_ref, l_ref, po_ref, pl_ref, (_, d) in ((o1_ref, l1_ref, po1_ref, pl1_ref, DA_GROUPS[1]),
                                                 (o2_ref, l2_ref, po2_ref, pl2_ref, DA_GROUPS[2])):
        topk_round()
        n = tm // d
        for r in range(d):
            pl_ref[pl.ds(r, n, stride=d), :] = l_ref[0, r]
            for h in range(DA_HEADS):
                sl = slice(h * DA_HEAD_DIM, (h + 1) * DA_HEAD_DIM)
                po_ref[h, pl.ds(r, n, stride=d), :] = o_ref[0, r, :, sl].astype(F32)

    l0, l1, l2 = l0_ref[...], pl1_ref[...], pl2_ref[...]
    mx = jnp.maximum(jnp.maximum(l0, l1), l2)
    e0, e1, e2 = jnp.exp(l0 - mx), jnp.exp(l1 - mx), jnp.exp(l2 - mx)
    inv = 1.0 / (e0 + e1 + e2)
    w0, w1, w2 = e0 * inv, e1 * inv, e2 * inv
    topk_round()
    heads = []
    for h in range(DA_HEADS):
        sl = slice(h * DA_HEAD_DIM, (h + 1) * DA_HEAD_DIM)
        at = slice(h * LSE_LANES, h * LSE_LANES + 1)
        o_h = w0[:, at] * o0_ref[:, sl].astype(F32) + w1[:, at] * po1_ref[h] + w2[:, at] * po2_ref[h]
        heads.append(o_h.astype(BF16))
    o_a = jnp.concatenate(heads, axis=-1)
    topk_round()

    gates = pm_ref[...].astype(F32)
    mixed = (gates[:, :D_MODEL] * _dot(o_a, wa_ref[...])
             + gates[:, D_MODEL:] * _dot(ob_ref[...], wb_ref[...]))
    vals, idxs = route["vals"], route["idxs"]
    assert len(vals) == TOP_K
    exps = [jnp.exp(v - vals[0]) for v in vals]
    denom = exps[0] + exps[1] + exps[2] + exps[3]
    onehot = jnp.zeros((tm, LANES), F32)
    for idx in idxs:
        onehot = onehot + jnp.where(lane == idx, 1.0, 0.0)

    h1 = x_ref[...] + _dot(mixed.astype(BF16), wo_ref[...])
    h1_ref[...] = h1
    hn = h1 * _rms_scale(h1) * ln2_ref[...]

    row = lax.broadcasted_iota(jnp.int32, (tm, tm), 0)
    col = lax.broadcasted_iota(jnp.int32, (tm, tm), 1)
    below = jnp.where(col < row, 1.0, 0.0).astype(BF16)
    before = _dot(below, onehot.astype(BF16)) + carry_ref[0:1, :]
    ranks = [jnp.sum(jnp.where(lane == idx, before, 0.0), axis=-1, keepdims=True) for idx in idxs]
    running = carry_ref[0:1, :]
    for j in range(tm // TM_ROW):
        cbefore_ref[j] = jnp.broadcast_to(running, cbefore_ref.shape[1:])
        running = running + jnp.sum(onehot[j * TM_ROW:(j + 1) * TM_ROW], axis=0, keepdims=True)
    carry = jnp.where(routed, running, carry_ref[0:1, :])
    carry_ref[...] = jnp.broadcast_to(carry, carry_ref.shape)
    cnt_ref[...] = jnp.broadcast_to(carry, cnt_ref.shape)

    hn_hi = hn.astype(BF16)
    hn_ref[...] = hn_hi

    ri = jnp.zeros((tm, LANES), F32)
    for j, val in enumerate(idxs + ranks):
        ri = jnp.where(lane == float(j), val, ri)
    ric_ref[...] = ri.astype(jnp.int32)
    ri_t = ri.T[:2 * TOP_K].astype(jnp.int32)
    for j in range(tm // TM_ROW):
        ri_ref[j] = ri_t[:, j * TM_ROW:(j + 1) * TM_ROW]
    meta = jnp.zeros((tm, LANES), F32)
    for k in range(TOP_K):
        gate = exps[k] / denom
        gate_hi = gate.astype(BF16).astype(F32)
        meta = jnp.where(lane == float(k), idxs[k], meta)
        meta = jnp.where(lane == float(TOP_K + k), gate_hi, meta)
        meta = jnp.where(lane == float(2 * TOP_K + k), gate - gate_hi, meta)
    meta_ref[...] = meta.astype(BF16)

    hn_lo = (hn - hn_hi.astype(F32)).astype(BF16)
    logit_ref[step % 2] = (_dot(hn_hi, wrh_ref[...]) + _dot(hn_lo, wrh_ref[...])
                           + _dot(hn_hi, wrl_ref[...]) + br_ref[...])


def _mix(x2, o0, l0, o1, l1, o2, l2, o_b, pm, wa, wb, wo, ln2_w, wr_hi, wr_lo, br_pad):
    n_tok = x2.shape[0]
    tm = TM_MIX
    bsz = o1.shape[0]
    d1, d2 = DA_GROUPS[1][1], DA_GROUPS[2][1]
    tiles_per_seq = (n_tok // bsz) // tm
    n_tiles = n_tok // tm

    def tile(i):
        return jnp.minimum(i, n_tiles - 1)

    def routed(i):
        return jnp.maximum(i - 1, 0)

    def rows(width, which=tile):
        return pl.BlockSpec((tm, width), lambda i: (which(i), 0))

    def residue_major(d, width):
        return pl.BlockSpec((1, d, tm // d, width),
                            lambda i: (tile(i) // tiles_per_seq, 0, tile(i) % tiles_per_seq, 0))

    def whole(arr):
        return pl.BlockSpec(arr.shape, lambda i: (0,) * arr.ndim)

    ln2 = ln2_w.reshape(1, D_MODEL)
    return pl.pallas_call(
        _mix_kernel,
        grid=(n_tiles + 1,),
        in_specs=[rows(D_MODEL), rows(DA_WIDTH), rows(LANES),
                  residue_major(d1, DA_WIDTH), residue_major(d1, LANES),
                  residue_major(d2, DA_WIDTH), residue_major(d2, LANES),
                  rows(GLA_VALUE_DIM), rows(MERGE_W),
                  whole(wa), whole(wb), whole(wo), whole(ln2), whole(wr_hi), whole(wr_lo), whole(br_pad)],
        out_specs=(rows(D_MODEL), rows(D_MODEL),
                   pl.BlockSpec((tm // TM_ROW, 2 * TOP_K, TM_ROW), lambda i: (routed(i), 0, 0)),
                   rows(LANES, routed), rows(LANES, routed),
                   pl.BlockSpec((tm // TM_ROW, 8, LANES), lambda i: (routed(i), 0, 0)),
                   pl.BlockSpec((8, LANES), lambda i: (0, 0))),
        out_shape=(jax.ShapeDtypeStruct((n_tok, D_MODEL), F32),
                   jax.ShapeDtypeStruct((n_tok, D_MODEL), BF16),
                   jax.ShapeDtypeStruct((n_tok // TM_ROW, 2 * TOP_K, TM_ROW), jnp.int32),
                   jax.ShapeDtypeStruct((n_tok, LANES), jnp.int32),
                   jax.ShapeDtypeStruct((n_tok, LANES), BF16),
                   jax.ShapeDtypeStruct((n_tok // TM_ROW, 8, LANES), F32),
                   jax.ShapeDtypeStruct((8, LANES), F32)),
        scratch_shapes=[pltpu.VMEM((DA_HEADS, tm, DA_HEAD_DIM), F32), pltpu.VMEM((tm, LANES), F32),
                        pltpu.VMEM((DA_HEADS, tm, DA_HEAD_DIM), F32), pltpu.VMEM((tm, LANES), F32),
                        pltpu.VMEM((8, LANES), F32), pltpu.VMEM((2, tm, LANES), F32)],
        compiler_params=pltpu.CompilerParams(
            dimension_semantics=("arbitrary",), vmem_limit_bytes=VMEM_LIMIT),
        name="mix",
    )(x2, o0, l0, o1, l1, o2, l2, o_b, pm, wa, wb, wo, ln2, wr_hi, wr_lo, br_pad)


def _pack_pairs(x):
    n = x.shape[1] // 2
    rounded = x.astype(BF16).astype(F32)
    lo = lax.bitcast_convert_type(rounded[:, :n], U32) >> 16
    hi = lax.bitcast_convert_type(rounded[:, n:], U32) & jnp.uint32(0xFFFF0000)
    return hi | lo


def _unpack_pairs(u):
    lo = lax.bitcast_convert_type(u << 16, F32).astype(BF16)
    hi = lax.bitcast_convert_type(u & jnp.uint32(0xFFFF0000), F32).astype(BF16)
    return lo, hi


def _dispatch_kernel(zstart_ref, nwin_ref, wsrc_ref, wdst_ref, pos_ref, meta_ref, hn_ref, xin_ref,
                     buf_ref, zero_ref, sem, zsem):
    i = pl.program_id(0)
    tm = hn_ref.shape[0]
    n_buf_rows = buf_ref.shape[1]
    slot = i % 2

    def window_copy(s, buf_slot):
        src = pl.multiple_of(wsrc_ref[i * _n_slot(WIN_D) + s], ROW_ALIGN)
        dst = pl.multiple_of(wdst_ref[i * _n_slot(WIN_D) + s], ROW_ALIGN)
        return pltpu.make_async_copy(buf_ref.at[buf_slot, pl.ds(src, WIN_D), :],
                                     xin_ref.at[pl.ds(dst, WIN_D), :], sem)

    def wait_windows(step):
        def body(s, carry):
            pltpu.make_async_copy(buf_ref.at[0, pl.ds(0, WIN_D), :], xin_ref.at[pl.ds(0, WIN_D), :],
                                  sem).wait()
            return carry
        lax.fori_loop(0, nwin_ref[step], body, 0)

    @pl.when(i == 0)
    def _():
        zero_ref[...] = jnp.zeros_like(zero_ref)

        def zero_copy(j):
            start = pl.multiple_of(jnp.maximum(zstart_ref[j], 0), ZCHUNK)
            return pltpu.make_async_copy(zero_ref, xin_ref.at[pl.ds(start, ZCHUNK), :], zsem)

        def start_one(j, carry):
            @pl.when(zstart_ref[j] >= 0)
            def _():
                zero_copy(j).start()
            return carry

        def wait_one(j, carry):
            @pl.when(zstart_ref[j] >= 0)
            def _():
                zero_copy(j).wait()
            return carry

        lax.fori_loop(0, zstart_ref.shape[0], start_one, 0)
        lax.fori_loop(0, zstart_ref.shape[0], wait_one, 0)

    row = lax.broadcasted_iota(jnp.int32, (n_buf_rows, tm), 0)

    perm = jnp.zeros((n_buf_rows, tm), F32)
    for k in range(TOP_K):
        perm = perm + jnp.where(row == pos_ref[0, k:k + 1, :], 1.0, 0.0)
    perm = perm.astype(BF16)
    buf_ref[slot, :, :D_MODEL // 2] = _pack_pairs(_dot(perm, hn_ref[...]))
    buf_ref[slot, :, D_MODEL // 2:] = lax.bitcast_convert_type(_dot(perm, meta_ref[...]), U32)

    @pl.when(i > 0)
    def _():
        wait_windows(i - 1)

    for buf_slot in range(2):
        @pl.when(slot == buf_slot)
        def _():
            def issue(s, carry):
                window_copy(s, buf_slot).start()
                return carry
            lax.fori_loop(0, nwin_ref[i], issue, 0)

    @pl.when(i == pl.num_programs(0) - 1)
    def _():
        wait_windows(i)


def _dispatch(zero_start, n_win, win_src, win_dst, pos, meta, hn, n_rows):
    n_tok = hn.shape[0]
    tm = TM_ROW
    n_tiles = n_tok // tm
    n_buf_rows = _sort_rows(WIN_D)
    return pl.pallas_call(
        _dispatch_kernel,
        grid_spec=pltpu.PrefetchScalarGridSpec(
            num_scalar_prefetch=4,
            grid=(n_tiles,),
            in_specs=[
                pl.BlockSpec((1, TOP_K, tm), lambda i, *_: (i, 0, 0)),
                pl.BlockSpec((tm, LANES), lambda i, *_: (i, 0)),
                pl.BlockSpec((tm, D_MODEL), lambda i, *_: (i, 0)),
            ],
            out_specs=pl.BlockSpec(memory_space=pl.ANY),
            scratch_shapes=[pltpu.VMEM((2, n_buf_rows, ROW_W), U32),
                            pltpu.VMEM((ZCHUNK, ROW_W), U32),
                            pltpu.SemaphoreType.DMA(()), pltpu.SemaphoreType.DMA(())],
        ),
        out_shape=jax.ShapeDtypeStruct((n_rows, ROW_W), U32),
        compiler_params=pltpu.CompilerParams(
            dimension_semantics=("arbitrary",), vmem_limit_bytes=VMEM_LIMIT),
        name="dispatch",
    )(zero_start, n_win, win_src.reshape(-1), win_dst.reshape(-1), pos, meta, hn)


def _expert_kernel(be_ref, live_ref, x_ref, wu_hbm, bu_ref, wd_hbm, bd_ref, y_ref,
                   wu32_ref, wd32_ref, wu16_ref, wd16_ref, wsem):
    i = pl.program_id(0)
    live = live_ref[i]
    expert_id = be_ref[i]

    def weight_copies(expert, slot):
        return (pltpu.make_async_copy(wu_hbm.at[expert], wu32_ref.at[slot], wsem.at[0, slot]),
                pltpu.make_async_copy(wd_hbm.at[expert], wd32_ref.at[slot], wsem.at[1, slot]))

    @pl.when(i == 0)
    def _():
        for cp in weight_copies(expert_id, 0):
            cp.start()

    for slot in range(2):
        @pl.when(((i == 0) | (expert_id != be_ref[jnp.maximum(i - 1, 0)])) & (expert_id % 2 == slot))
        def _():
            for cp in weight_copies(expert_id, slot):
                cp.wait()

            @pl.when(expert_id + 1 < N_EXPERTS)
            def _():
                for cp in weight_copies(expert_id + 1, 1 - slot):
                    cp.start()

            wu16_ref[...] = wu32_ref[slot].astype(BF16)
            wd16_ref[...] = wd32_ref[slot].astype(BF16)

    def compute(m):
        half = D_MODEL // 2
        x_lo, x_hi = _unpack_pairs(x_ref[:m, :half])
        meta = lax.bitcast_convert_type(x_ref[:m, half:half + 3 * TOP_K], F32)
        expert = be_ref[i].astype(F32)
        gate = jnp.zeros((m, 1), F32)
        for k in range(TOP_K):
            weight = meta[:, TOP_K + k:TOP_K + k + 1] + meta[:, 2 * TOP_K + k:2 * TOP_K + k + 1]
            gate = gate + jnp.where(meta[:, k:k + 1] == expert, weight, 0.0)
        hu = _dot(x_lo, wu16_ref[:half, :]) + _dot(x_hi, wu16_ref[half:, :]) + bu_ref[0]
        x_glu = jnp.minimum(hu[:, :D_FF], SWIGLU_LIMIT)
        x_lin = jnp.clip(hu[:, D_FF:], -SWIGLU_LIMIT, SWIGLU_LIMIT)
        act = x_glu * _sigmoid(SWIGLU_ALPHA * x_glu) * (x_lin + 1.0)
        y_ref[:m, :] = _pack_pairs((_dot(act.astype(BF16), wd16_ref[...]) + bd_ref[0]) * gate)
        if m < TB:
            y_ref[m:, :] = jnp.zeros((TB - m, y_ref.shape[1]), y_ref.dtype)

    @pl.when(live == 0)
    def _():
        y_ref[...] = jnp.zeros_like(y_ref)

    for m in range(TB_STEP, TB + 1, TB_STEP):
        @pl.when((live > m - TB_STEP) & (live <= m))
        def _():
            compute(m)


def _experts(block_e, block_live, xin, w_up, b_up, w_down, b_down):
    n_rows = xin.shape[0]
    return pl.pallas_call(
        _expert_kernel,
        grid_spec=pltpu.PrefetchScalarGridSpec(
            num_scalar_prefetch=2,
            grid=(n_rows // TB,),
            in_specs=[
                pl.BlockSpec((TB, ROW_W), lambda i, be, nu: (i, 0)),
                pl.BlockSpec(memory_space=pl.ANY),
                pl.BlockSpec((1, 1, 2 * D_FF), lambda i, be, nu: (be[i], 0, 0)),
                pl.BlockSpec(memory_space=pl.ANY),
                pl.BlockSpec((1, 1, D_MODEL), lambda i, be, nu: (be[i], 0, 0)),
            ],
            out_specs=pl.BlockSpec((TB, D_MODEL // 2), lambda i, be, nu: (i, 0)),
            scratch_shapes=[pltpu.VMEM((2, D_MODEL, 2 * D_FF), F32), pltpu.VMEM((2, D_FF, D_MODEL), F32),
                            pltpu.VMEM((D_MODEL, 2 * D_FF), BF16), pltpu.VMEM((D_FF, D_MODEL), BF16),
                            pltpu.SemaphoreType.DMA((2, 2))],
        ),
        out_shape=jax.ShapeDtypeStruct((n_rows, D_MODEL // 2), U32),
        compiler_params=pltpu.CompilerParams(
            dimension_semantics=("arbitrary",), vmem_limit_bytes=VMEM_LIMIT),
        name="experts",
    )(block_e, block_live, xin, w_up, b_up.reshape(N_EXPERTS, 1, 2 * D_FF),
      w_down, b_down.reshape(N_EXPERTS, 1, D_MODEL))


def _combine_kernel(nwin_ref, wsrc_ref, wdst_ref, ric_ref, delta_ref, ricn_ref, deltan_ref, yb_ref, h1_ref,
                    lnf_ref, o_ref, stage_ref, buf_ref, pick_ref, sem):
    i = pl.program_id(0)
    tm = h1_ref.shape[0]
    n_buf_rows = buf_ref.shape[0]

    parity = i % 2

    def fetch(step, stage_slot):
        def body(s, carry):
            dst = pl.multiple_of(wdst_ref[step * _n_slot(WIN_C) + s], ROW_ALIGN)
            pltpu.make_async_copy(
                yb_ref.at[pl.ds(dst, WIN_C), :],
                stage_ref.at[stage_slot, pl.ds(pl.multiple_of(s * WIN_C, WIN_C), WIN_C), :],
                sem.at[stage_slot]).start()
            return carry
        lax.fori_loop(0, nwin_ref[step], body, 0)

    lane = lax.broadcasted_iota(jnp.int32, (tm, LANES), 1).astype(F32)
    col = lax.broadcasted_iota(jnp.int32, (tm, n_buf_rows), 1)

    def pick_round(pick, k, ric, delta):
        offset = jnp.sum(jnp.where(lane == ric[:, k:k + 1], delta, 0.0), axis=-1, keepdims=True)
        pos = (ric[:, TOP_K + k:TOP_K + k + 1] + offset).astype(jnp.int32)
        return pick + jnp.where(col == pos, 1.0, 0.0)

    @pl.when(i == 0)
    def _():
        buf_ref[...] = jnp.zeros_like(buf_ref)
        fetch(i, 0)
        pick = jnp.zeros((tm, n_buf_rows), F32)
        ric = ric_ref[...].astype(F32)
        for k in range(TOP_K):
            pick = pick_round(pick, k, ric, delta_ref[0])
        pick_ref[0] = pick.astype(BF16)

    for stage_slot in range(2):
        @pl.when((i + 1 < pl.num_programs(0)) & (parity != stage_slot))
        def _():
            fetch(i + 1, stage_slot)

    for stage_slot in range(2):
        @pl.when(parity == stage_slot)
        def _():
            def drain(s, carry):
                pltpu.make_async_copy(yb_ref.at[pl.ds(0, WIN_C), :],
                                      stage_ref.at[stage_slot, pl.ds(0, WIN_C), :], sem.at[stage_slot]).wait()
                return carry

            def compact(s, carry):
                src = pl.multiple_of(wsrc_ref[i * _n_slot(WIN_C) + s], ROW_ALIGN)
                buf_ref[pl.ds(src, WIN_C), :] = stage_ref[
                    stage_slot, pl.ds(pl.multiple_of(s * WIN_C, WIN_C), WIN_C), :]
                return carry

            lax.fori_loop(0, nwin_ref[i], drain, 0)
            lax.fori_loop(0, nwin_ref[i], compact, 0)

    ric_next = ricn_ref[...].astype(F32)
    delta_next = deltan_ref[0]
    pick = pick_ref[parity]
    nxt = jnp.zeros((tm, n_buf_rows), F32)
    y_lo, y_hi = _unpack_pairs(buf_ref[...])
    nxt = pick_round(nxt, 0, ric_next, delta_next)
    left = _dot(pick, y_lo)
    nxt = pick_round(nxt, 1, ric_next, delta_next)
    right = _dot(pick, y_hi)
    nxt = pick_round(nxt, 2, ric_next, delta_next)
    h2 = h1_ref[...] + jnp.concatenate([left, right], axis=-1)
    nxt = pick_round(nxt, 3, ric_next, delta_next)
    o_ref[...] = h2 * _rms_scale(h2) * lnf_ref[...]
    pick_ref[1 - parity] = nxt.astype(BF16)


def _combine(n_win, win_src, win_dst, ric, delta, yb, h1, lnf_w):
    n_tok = h1.shape[0]
    tm = TM_ROW
    n_tiles = n_tok // tm
    return pl.pallas_call(
        _combine_kernel,
        grid_spec=pltpu.PrefetchScalarGridSpec(
            num_scalar_prefetch=3,
            grid=(n_tiles,),
            in_specs=[
                pl.BlockSpec((tm, LANES), lambda i, *_: (i, 0)),
                pl.BlockSpec((1, 1, LANES), lambda i, *_: (i, 0, 0)),
                pl.BlockSpec((tm, LANES), lambda i, *_: (jnp.minimum(i + 1, n_tiles - 1), 0)),
                pl.BlockSpec((1, 1, LANES), lambda i, *_: (jnp.minimum(i + 1, n_tiles - 1), 0, 0)),
                pl.BlockSpec(memory_space=pl.ANY),
                pl.BlockSpec((tm, D_MODEL), lambda i, *_: (i, 0)),
                pl.BlockSpec((1, D_MODEL), lambda i, *_: (0, 0)),
            ],
            out_specs=pl.BlockSpec((tm, D_MODEL), lambda i, *_: (i, 0)),
            scratch_shapes=[pltpu.VMEM((2, _n_slot(WIN_C) * WIN_C, D_MODEL // 2), U32),
                            pltpu.VMEM((_sort_rows(WIN_C), D_MODEL // 2), U32),
                            pltpu.VMEM((2, tm, _sort_rows(WIN_C)), BF16), pltpu.SemaphoreType.DMA((2,))],
        ),
        out_shape=jax.ShapeDtypeStruct((n_tok, D_MODEL), F32),
        compiler_params=pltpu.CompilerParams(
            dimension_semantics=("arbitrary",), vmem_limit_bytes=VMEM_LIMIT),
        name="combine",
    )(n_win, win_src.reshape(-1), win_dst.reshape(-1), ric, delta, ric, delta, yb, h1,
      lnf_w.reshape(1, D_MODEL))


def _layer(h, ln1_w, w_in, w_gk2, b_gk2, gla_norm_w, w_proj_a, w_proj_b, w_out,
           ln2_w, w_router, b_router, w_up, b_up, w_down, b_down, lnf_w):
    bsz, seq, _ = h.shape
    n_tok = bsz * seq

    n_main = 3 * QKV_W + 2 * GLA_KEY_DIM + 2 * GLA_VALUE_DIM
    pad = LANES - GLA_GATE_RANK
    w_main = w_in[:, :n_main].astype(BF16)
    w_tail = jnp.concatenate(
        [w_in[:, n_main:n_main + GLA_GATE_RANK], jnp.zeros((D_MODEL, pad), F32),
         w_in[:, n_main + GLA_GATE_RANK:]], axis=1).astype(BF16)
    w2_pad = jnp.concatenate([w_gk2, jnp.zeros((pad, GLA_KEY_DIM), F32)], axis=0).astype(BF16)
    wr_pad = jnp.concatenate([w_router, jnp.zeros((D_MODEL, LANES - N_EXPERTS), F32)], axis=1)
    wr_hi = wr_pad.astype(BF16)
    wr_lo = (wr_pad - wr_hi.astype(F32)).astype(BF16)
    br_pad = jnp.concatenate([b_router, jnp.zeros((LANES - N_EXPERTS,), F32)]).reshape(1, LANES)

    pa0, pa1, pa2, pg, pm = _inproj(h, ln1_w, w_main, w_tail)
    o0, l0 = _dil_attn(pa0)
    o1, l1 = _dil_attn(pa1)
    o2, l2 = _dil_attn(pa2)
    o_b = _gla(pg, w2_pad, b_gk2, gla_norm_w)

    h1, hn, ri, ric, meta, carry_f, cnt = _mix(
        h.reshape(n_tok, D_MODEL), o0.reshape(n_tok, DA_WIDTH), l0.reshape(n_tok, LANES),
        o1, l1, o2, l2, o_b.reshape(n_tok, GLA_VALUE_DIM), pm.reshape(n_tok, MERGE_W),
        w_proj_a.astype(BF16), w_proj_b.astype(BF16), w_out.astype(BF16), ln2_w, wr_hi, wr_lo, br_pad)

    i32 = jnp.int32
    n_tiles = n_tok // TM_ROW
    experts = jnp.arange(N_EXPERTS)
    counts = cnt[0, :N_EXPERTS].astype(i32)
    before = carry_f[:, 0, :N_EXPERTS].astype(i32)
    run = jnp.concatenate([before[1:], counts[None]], axis=0) - before
    run_al = (run + ROW_ALIGN - 1) // ROW_ALIGN * ROW_ALIGN
    rows_end = jnp.cumsum(run_al, axis=0)
    rows_before = rows_end - run_al
    used = rows_end[-1]
    slack = max(WIN_D, WIN_C) - ROW_ALIGN
    padded = (used + slack + TB - 1) // TB * TB
    pad_end = jnp.cumsum(padded)
    pad_start = pad_end - padded
    n_asg = n_tok * TOP_K
    n_rows = (n_asg + n_tiles * N_EXPERTS * (ROW_ALIGN - 1)
              + N_EXPERTS * (slack + TB - 1) + TB - 1) // TB * TB
    n_blocks = n_rows // TB
    block_starts = jnp.arange(n_blocks) * TB
    block_e = jnp.minimum(jnp.sum(pad_end[None, :] <= block_starts[:, None], axis=1),
                          N_EXPERTS - 1).astype(i32)
    used_end = jnp.sum(jnp.where(block_e[:, None] == experts, (pad_start + used)[None, :], 0), axis=1)
    block_live = jnp.clip(used_end - block_starts, 0, TB).astype(i32)

    run_end = jnp.cumsum(run_al, axis=1)
    run_start = run_end - run_al

    def windows(win):
        wins = (run + win - 1) // win
        wins_end = jnp.cumsum(wins, axis=1)
        wins_start = wins_end - wins
        slots = jnp.arange(_n_slot(win))
        slot_e = jnp.sum(wins_end[:, None, :] <= slots[None, :, None], axis=-1)
        slot_is = slot_e[..., None] == experts

        def of_slot(table):
            return jnp.sum(jnp.where(slot_is, table[:, None, :], 0), axis=-1)

        win_off = (slots[None, :] - of_slot(wins_start)) * win
        return (wins_end[:, -1].astype(i32), (of_slot(run_start) + win_off).astype(i32),
                (of_slot(pad_start[None, :] + rows_before) + win_off).astype(i32))

    asg_is = ri[:, :TOP_K, :, None] == experts
    pos = ri[:, TOP_K:, :] + jnp.sum(jnp.where(asg_is, (run_start - before)[:, None, None, :], 0), axis=-1)
    delta = jnp.pad((run_start - before).astype(F32), ((0, 0), (0, LANES - N_EXPERTS)))

    chunk_back = ZCHUNK * (1 + jnp.arange((slack + TB - 1 + ZCHUNK - 1) // ZCHUNK + 1))
    region_chunks = pad_end[:, None] - chunk_back[None, :]
    region_ok = (region_chunks >= pad_start[:, None]) & (region_chunks + ZCHUNK > (pad_start + used)[:, None])
    tail_chunks = pad_end[-1] + ZCHUNK * jnp.arange((n_rows - n_asg) // ZCHUNK)
    zero_start = jnp.concatenate([jnp.where(region_ok, region_chunks, -1).reshape(-1),
                                  jnp.where(tail_chunks < n_rows, tail_chunks, -1)]).astype(i32)

    xin = _dispatch(zero_start, *windows(WIN_D), pos.astype(i32), meta, hn, n_rows)
    yb = _experts(block_e, block_live, xin, w_up, b_up, w_down, b_down)
    out = _combine(*windows(WIN_C), ric, delta.reshape(-1, 1, LANES), yb, h1, lnf_w)
    return out.reshape(bsz, seq, D_MODEL)


def kernel(x, ln1_w, w_in, w_gk2, b_gk2, gla_norm_w, w_proj_a, w_proj_b, w_out, ln2_w, w_router,
           b_router, w_up, b_up, w_down, b_down, lnf_w):
    assert x.shape[-1] == D_MODEL and ln1_w.shape[0] == 1, "one layer of width D_MODEL"
    return _layer(x, ln1_w[0], w_in[0], w_gk2[0], b_gk2[0], gla_norm_w[0], w_proj_a[0], w_proj_b[0],
                  w_out[0], ln2_w[0], w_router[0], b_router[0], w_up[0], b_up[0], w_down[0],
                  b_down[0], lnf_w)
```

```python
import jax
import jax.numpy as jnp
from jax import lax
from jax.experimental import pallas as pl
from jax.experimental.pallas import tpu as pltpu

F32 = jnp.float32
BF16 = jnp.bfloat16
U32 = jnp.uint32

D_MODEL = 1024
DA_GROUPS = ((128, 1), (512, 4), (2048, 16))
DA_HEADS = 4
DA_HEAD_DIM = 128
DA_WIDTH = DA_HEADS * DA_HEAD_DIM
DA_BLOCK = 128
GLA_HEADS = 4
GLA_KEY_DIM = D_MODEL // 2
GLA_VALUE_DIM = D_MODEL
GLA_DK = GLA_KEY_DIM // GLA_HEADS
GLA_DV = GLA_VALUE_DIM // GLA_HEADS
GLA_GATE_RANK = 16
GLA_GATE_NORMALIZER = 16.0
GLA_CHUNK = 64
N_EXPERTS = 32
TOP_K = 4
D_FF = D_MODEL
SWIGLU_ALPHA = 1.702
SWIGLU_LIMIT = 7.0
RMS_EPS = 1e-5
NEG_INF = -1e30

LANES = 128
QKV_W = 3 * DA_WIDTH
GLA_W = 2 * GLA_KEY_DIM + 2 * GLA_VALUE_DIM + LANES
MERGE_W = 2 * D_MODEL

DA_QB = 4
LSE_LANES = LANES // DA_HEADS
TM_IN = 512
N_CHUNK = 512
T_GLA = 1024
GLA_SUB = 256
GLA_HEADS_PER_STEP = 4
TM_MIX = 512
TB = 512
TB_STEP = 128
TM_ROW = 256
ROW_ALIGN = 8
WIN_D = 32
WIN_C = 64


def _n_slot(win):
    return N_EXPERTS + TM_ROW * TOP_K // win


def _sort_rows(win):
    return TM_ROW * TOP_K + N_EXPERTS * (ROW_ALIGN - 1) + win
ZCHUNK = 256
ROW_W = D_MODEL // 2 + LANES
VMEM_LIMIT = 56 * 1024 * 1024
VMEM_LIMIT_INPROJ = 62 * 1024 * 1024

_NT = (((1,), (1,)), ((), ()))
_TN = (((0,), (0,)), ((), ()))


def _dot(a, b):
    return jnp.dot(a, b, preferred_element_type=F32)


def _sigmoid(x):
    return 1.0 / (1.0 + jnp.exp(-x))


def _rms_scale(x):
    return lax.rsqrt(jnp.mean(x * x, axis=-1, keepdims=True) + RMS_EPS)


def _inproj_kernel(x_ref, ln_ref, wmain_hbm, wtail_hbm, pa0_ref, pa1_ref, pa2_ref, pg_ref, pm_ref,
                   w_ref, xs_ref, xn_ref, wsem):
    tm = x_ref.shape[1]
    n_slab = D_MODEL // LANES

    @pl.when((pl.program_id(0) == 0) & (pl.program_id(1) == 0))
    def _():
        n_main = wmain_hbm.shape[1]
        main = pltpu.make_async_copy(wmain_hbm, w_ref.at[:, :n_main], wsem.at[0])
        tail = pltpu.make_async_copy(wtail_hbm, w_ref.at[:, n_main:], wsem.at[1])
        main.start()
        tail.start()
        main.wait()
        tail.wait()

    def project(out_write, col0, width, post=None):
        for c0 in range(0, width, N_CHUNK):
            cw = min(N_CHUNK, width - c0)
            val = _dot(xs_ref[...], w_ref[:, col0 + c0:col0 + c0 + cw])
            out_write(c0, cw, (val if post is None else post(val)).astype(BF16))

    x = x_ref[0]
    xn = x * _rms_scale(x) * ln_ref[...]
    xs_ref[...] = xn.astype(BF16)
    for j in range(n_slab):
        xn_ref[j] = xn[:, j * LANES:(j + 1) * LANES]

    def write_to(ref):
        def write(c0, cw, val):
            ref[0, :, c0:c0 + cw] = val
        return write

    def write_qkv(out_ref, d, n):
        def write(c0, cw, val):
            for r in range(d):
                out_ref[0, r, c0 // DA_WIDTH] = val[r * n:(r + 1) * n]
        return write

    project(write_qkv(pa0_ref, 1, tm), 0, QKV_W)
    project(write_to(pg_ref), 3 * QKV_W, GLA_W)
    project(write_to(pm_ref), 3 * QKV_W + GLA_W, MERGE_W, post=_sigmoid)

    for gi, out_ref in ((1, pa1_ref), (2, pa2_ref)):
        d = DA_GROUPS[gi][1]
        n = tm // d
        for r in range(d):
            for j in range(n_slab):
                xs_ref[r * n:(r + 1) * n, j * LANES:(j + 1) * LANES] = (
                    xn_ref[j, pl.ds(r, n, stride=d), :].astype(BF16))
        project(write_qkv(out_ref, d, n), gi * QKV_W, QKV_W)


def _inproj(x, ln1_w, w_main, w_tail):
    assert N_CHUNK == DA_WIDTH and w_main.shape[1] % LANES == 0
    bsz, seq, _ = x.shape
    tm = TM_IN
    d1, d2 = DA_GROUPS[1][1], DA_GROUPS[2][1]

    def qkv_shape(d):
        return jax.ShapeDtypeStruct((bsz, d, 3, seq // d, DA_WIDTH), BF16)

    def qkv_spec(d):
        return pl.BlockSpec((1, d, 3, tm // d, DA_WIDTH), lambda b, i: (b, 0, 0, i, 0))

    return pl.pallas_call(
        _inproj_kernel,
        grid=(bsz, seq // tm),
        in_specs=[
            pl.BlockSpec((1, tm, D_MODEL), lambda b, i: (b, i, 0)),
            pl.BlockSpec((1, D_MODEL), lambda b, i: (0, 0)),
            pl.BlockSpec(memory_space=pl.ANY),
            pl.BlockSpec(memory_space=pl.ANY),
        ],
        out_specs=(
            qkv_spec(1), qkv_spec(d1), qkv_spec(d2),
            pl.BlockSpec((1, tm, GLA_W), lambda b, i: (b, i, 0)),
            pl.BlockSpec((1, tm, MERGE_W), lambda b, i: (b, i, 0)),
        ),
        out_shape=(qkv_shape(1), qkv_shape(d1), qkv_shape(d2),
                   jax.ShapeDtypeStruct((bsz, seq, GLA_W), BF16),
                   jax.ShapeDtypeStruct((bsz, seq, MERGE_W), BF16)),
        scratch_shapes=[pltpu.VMEM((D_MODEL, w_main.shape[1] + w_tail.shape[1]), BF16),
                        pltpu.VMEM((tm, D_MODEL), BF16),
                        pltpu.VMEM((D_MODEL // LANES, tm, LANES), F32),
                        pltpu.SemaphoreType.DMA((2,))],
        compiler_params=pltpu.CompilerParams(
            dimension_semantics=("arbitrary", "arbitrary"), vmem_limit_bytes=VMEM_LIMIT_INPROJ),
        name="inproj",
    )(x, ln1_w.reshape(1, D_MODEL), w_main, w_tail)


def _dil_attn_kernel(q_ref, kp_ref, kc_ref, vp_ref, vc_ref, o_ref, l_ref, s_ref, p_ref, r_ref):
    n = pl.program_id(2)
    blk = DA_BLOCK
    qi = lax.broadcasted_iota(jnp.int32, (blk, 2 * blk), 0)
    kj = lax.broadcasted_iota(jnp.int32, (blk, 2 * blk), 1)
    band = (kj >= qi) & (kj <= qi + blk)
    band_first = (kj >= jnp.where(n > 0, qi, blk)) & (kj <= qi + blk)
    scale = DA_HEAD_DIM ** -0.5
    items = [(b, h) for b in range(DA_QB) for h in range(DA_HEADS)]

    def rows(b):
        return slice(b * blk, (b + 1) * blk)

    def cols(h):
        return slice(h * DA_HEAD_DIM, (h + 1) * DA_HEAD_DIM)

    def window(prev_ref, cur_ref, b, h):
        before = prev_ref[0, 0, 0, :, cols(h)] if b == 0 else cur_ref[0, 0, 0, rows(b - 1), cols(h)]
        return jnp.concatenate([before, cur_ref[0, 0, 0, rows(b), cols(h)]], axis=0)

    def scores(i, b, h):
        s = lax.dot_general(q_ref[0, 0, 0, rows(b), cols(h)], window(kp_ref, kc_ref, b, h), _NT,
                            preferred_element_type=F32) * scale
        s_ref[i] = jnp.where(band_first if b == 0 else band, s, NEG_INF)

    def softmax(i, b, h):
        s = s_ref[i]
        m = jnp.max(s, axis=-1, keepdims=True)
        p = jnp.exp(s - m)
        l = jnp.sum(p, axis=-1, keepdims=True)
        p_ref[i] = p.astype(BF16)
        r_ref[i] = jnp.broadcast_to(1.0 / l, (blk, DA_HEAD_DIM))
        l_ref[0, 0, rows(b), h * LSE_LANES:(h + 1) * LSE_LANES] = jnp.broadcast_to(
            m + jnp.log(l), (blk, LSE_LANES))

    def values(i, b, h):
        acc = _dot(p_ref[i], window(vp_ref, vc_ref, b, h))
        o_ref[0, 0, rows(b), cols(h)] = (acc * r_ref[i]).astype(o_ref.dtype)

    stages = (scores, softmax, values)
    for t in range(len(items) + len(stages) - 1):
        for lag, stage in enumerate(stages):
            if 0 <= t - lag < len(items):
                stage(t - lag, *items[t - lag])


def _dil_attn(pa):
    bsz, d, _, sub_len, _ = pa.shape
    rows = DA_QB * DA_BLOCK
    n_items = DA_QB * DA_HEADS

    def cur(sec):
        return pl.BlockSpec((1, 1, 1, rows, DA_WIDTH), lambda b, r, n: (b, r, sec, n, 0))

    def prev(sec):
        return pl.BlockSpec((1, 1, 1, DA_BLOCK, DA_WIDTH),
                            lambda b, r, n: (b, r, sec, jnp.maximum(n * DA_QB - 1, 0), 0))

    def out(width):
        return pl.BlockSpec((1, 1, rows, width), lambda b, r, n: (b, r, n, 0))

    return pl.pallas_call(
        _dil_attn_kernel,
        grid=(bsz, d, sub_len // rows),
        in_specs=[cur(0), prev(1), cur(1), prev(2), cur(2)],
        out_specs=(out(DA_WIDTH), out(LANES)),
        out_shape=(jax.ShapeDtypeStruct((bsz, d, sub_len, DA_WIDTH), BF16),
                   jax.ShapeDtypeStruct((bsz, d, sub_len, LANES), F32)),
        scratch_shapes=[pltpu.VMEM((n_items, DA_BLOCK, 2 * DA_BLOCK), F32),
                        pltpu.VMEM((n_items, DA_BLOCK, 2 * DA_BLOCK), BF16),
                        pltpu.VMEM((n_items, DA_BLOCK, DA_HEAD_DIM), F32)],
        compiler_params=pltpu.CompilerParams(
            dimension_semantics=("parallel", "parallel", "arbitrary"), vmem_limit_bytes=VMEM_LIMIT),
        name=f"dil_attn_d{d}",
    )(pa, pa, pa, pa, pa)


def _gla_kernel(q_ref, k_ref, v_ref, og_ref, lr_ref, w2_ref, b2_ref, nw_ref, o_ref, st_ref, mask_ref, keep_ref):
    t = pl.program_id(2)
    c = GLA_CHUNK
    tt = q_ref.shape[1]
    n_c = tt // c

    sub = mask_ref.shape[0]

    @pl.when(t == 0)
    def _():
        st_ref[...] = jnp.zeros_like(st_ref)
        row = lax.broadcasted_iota(jnp.int32, (sub, sub), 0)
        col = lax.broadcasted_iota(jnp.int32, (sub, sub), 1)
        keep = (col <= row) & (col >= row - row % c)
        keep_ref[...] = jnp.where(keep, 1.0, 0.0)
        mask_ref[...] = jnp.where(keep, 1.0, 0.0).astype(BF16)

    heads = range(q_ref.shape[2] // GLA_DK)
    mask = mask_ref[...]

    def kcols(h):
        return slice(h * GLA_DK, (h + 1) * GLA_DK)

    def vcols(h):
        return slice(h * GLA_DV, (h + 1) * GLA_DV)

    gpre = _dot(lr_ref[0], w2_ref[...]) + b2_ref[...]
    forget = (jnp.minimum(gpre, 0.0) - jnp.log(1.0 + jnp.exp(-jnp.abs(gpre)))) / GLA_GATE_NORMALIZER
    g_hi = forget.astype(BF16)
    g_lo = (forget - g_hi.astype(F32)).astype(BF16)
    b, b_last, q_e, k_e, k_end = [], [], [], [], []
    for h in heads:
        g_cat = jnp.concatenate([g_hi[:, kcols(h)], g_lo[:, kcols(h)]], axis=-1)
        csum = jnp.concatenate([_dot(mask, g_cat[s0:s0 + sub]) for s0 in range(0, tt, sub)], axis=0)
        b.append(csum[:, :GLA_DK] + csum[:, GLA_DK:])
    for h in heads:
        b_last.append(b[h].reshape(n_c, c, GLA_DK)[:, c - 1:c, :])
        b_to_end = (b_last[h] - b[h].reshape(n_c, c, GLA_DK)).reshape(tt, GLA_DK)
        q = q_ref[0, :, kcols(h)].astype(F32)
        k = k_ref[0, :, kcols(h)].astype(F32)
        q_e.append((q * ((GLA_DK ** -0.5) * jnp.exp(b[h]))).astype(BF16))
        k_e.append((k * jnp.exp(-b[h])).astype(BF16))
        k_end.append((k * jnp.exp(b_to_end)).astype(BF16))
    o_intra = []
    for h in heads:
        parts = []
        for s0 in range(0, tt, sub):
            ss = slice(s0, s0 + sub)
            att = lax.dot_general(q_e[h][ss], k_e[h][ss], _NT, preferred_element_type=F32)
            att = jnp.where(keep_ref[...] > 0.0, att, 0.0).astype(BF16)
            parts.append(_dot(att, v_ref[0, ss, vcols(h)]))
        o_intra.append(jnp.concatenate(parts, axis=0))
    decay = [jnp.exp(b_last[h].reshape(n_c, GLA_DK).T) for h in heads]
    st = [st_ref[h] for h in heads]
    outs = [[] for _ in heads]
    for ci in range(n_c):
        rs = slice(ci * c, (ci + 1) * c)
        for h in heads:
            outs[h].append(o_intra[h][rs] + _dot(q_e[h][rs], st[h].astype(BF16)))
            st[h] = decay[h][:, ci:ci + 1] * st[h] + lax.dot_general(
                k_end[h][rs], v_ref[0, rs, vcols(h)], _TN, preferred_element_type=F32)
    for h in heads:
        st_ref[h] = st[h]
        o = jnp.concatenate(outs[h], axis=0)
        o = o * _rms_scale(o) * nw_ref[...]
        gate = og_ref[0, :, vcols(h)].astype(F32)
        o_ref[0, :, vcols(h)] = (o * (gate * _sigmoid(gate))).astype(BF16)


def _gla(pg, w2_pad, b_gk2, gla_norm_w):
    bsz, seq, _ = pg.shape
    t = T_GLA
    hps = GLA_HEADS_PER_STEP
    wk, wv = hps * GLA_DK, hps * GLA_DV
    kq = GLA_KEY_DIM // wk
    kv = 2 * GLA_KEY_DIM // wv
    kg = kv + GLA_VALUE_DIM // wv
    klr = (2 * GLA_KEY_DIM + 2 * GLA_VALUE_DIM) // LANES
    return pl.pallas_call(
        _gla_kernel,
        grid=(bsz, GLA_HEADS // hps, seq // t),
        in_specs=[
            pl.BlockSpec((1, t, wk), lambda b, h, i: (b, i, h)),
            pl.BlockSpec((1, t, wk), lambda b, h, i: (b, i, kq + h)),
            pl.BlockSpec((1, t, wv), lambda b, h, i: (b, i, kv + h)),
            pl.BlockSpec((1, t, wv), lambda b, h, i: (b, i, kg + h)),
            pl.BlockSpec((1, t, LANES), lambda b, h, i: (b, i, klr)),
            pl.BlockSpec((LANES, wk), lambda b, h, i: (0, h)),
            pl.BlockSpec((1, wk), lambda b, h, i: (0, h)),
            pl.BlockSpec((1, GLA_DV), lambda b, h, i: (0, 0)),
        ],
        out_specs=pl.BlockSpec((1, t, wv), lambda b, h, i: (b, i, h)),
        out_shape=jax.ShapeDtypeStruct((bsz, seq, GLA_VALUE_DIM), BF16),
        scratch_shapes=[pltpu.VMEM((hps, GLA_DK, GLA_DV), F32), pltpu.VMEM((GLA_SUB, GLA_SUB), BF16),
                        pltpu.VMEM((GLA_SUB, GLA_SUB), F32)],
        compiler_params=pltpu.CompilerParams(
            dimension_semantics=("parallel", "parallel", "arbitrary"), vmem_limit_bytes=VMEM_LIMIT),
        name="gla",
    )(pg, pg, pg, pg, pg, w2_pad, b_gk2.reshape(1, GLA_KEY_DIM), gla_norm_w.reshape(1, GLA_DV))


def _mix_kernel(x_ref, o0_ref, l0_ref, o1_ref, l1_ref, o2_ref, l2_ref, ob_ref, pm_ref,
                wa_ref, wb_ref, wo_ref, ln2_ref, wrh_ref, wrl_ref, br_ref,
                h1_ref, hn_ref, ri_ref, ric_ref, meta_ref, cbefore_ref, cnt_ref,
                po1_ref, pl1_ref, po2_ref, pl2_ref, carry_ref, logit_ref):
    step = pl.program_id(0)
    tm = x_ref.shape[0]

    @pl.when(step == 0)
    def _():
        carry_ref[...] = jnp.zeros_like(carry_ref)
        logit_ref[...] = jnp.zeros_like(logit_ref)

    routed = step > 0
    lane = lax.broadcasted_iota(jnp.int32, (tm, LANES), 1).astype(F32)
    route = {"work": None, "vals": [], "idxs": []}

    def topk_round():
        work = route["work"]
        m = jnp.max(work, axis=-1, keepdims=True)
        idx = jnp.min(jnp.where(work == m, lane, float(LANES)), axis=-1, keepdims=True)
        route["vals"].append(m)
        route["idxs"].append(idx)
        route["work"] = jnp.where(lane == idx, -jnp.inf, work)

    route["work"] = jnp.where(lane < N_EXPERTS, logit_ref[(step + 1) % 2], -jnp.inf)

    for o_ref, l_ref, po_ref, pl_ref, (_, d) in ((o1_ref, l1_ref, po1_ref, pl1_ref, DA_GROUPS[1]),
                                                 (o2_ref, l2_ref, po2_ref, pl2_ref, DA_GROUPS[2])):
        topk_round()
        n = tm // d
        for r in range(d):
            pl_ref[pl.ds(r, n, stride=d), :] = l_ref[0, r]
            for h in range(DA_HEADS):
                sl = slice(h * DA_HEAD_DIM, (h + 1) * DA_HEAD_DIM)
                po_ref[h, pl.ds(r, n, stride=d), :] = o_ref[0, r, :, sl].astype(F32)

    l0, l1, l2 = l0_ref[...], pl1_ref[...], pl2_ref[...]
    mx = jnp.maximum(jnp.maximum(l0, l1), l2)
    e0, e1, e2 = jnp.exp(l0 - mx), jnp.exp(l1 - mx), jnp.exp(l2 - mx)
    inv = 1.0 / (e0 + e1 + e2)
    w0, w1, w2 = e0 * inv, e1 * inv, e2 * inv
    topk_round()
    heads = []
    for h in range(DA_HEADS):
        sl = slice(h * DA_HEAD_DIM, (h + 1) * DA_HEAD_DIM)
        at = slice(h * LSE_LANES, h * LSE_LANES + 1)
        o_h = w0[:, at] * o0_ref[:, sl].astype(F32) + w1[:, at] * po1_ref[h] + w2[:, at] * po2_ref[h]
        heads.append(o_h.astype(BF16))
    o_a = jnp.concatenate(heads, axis=-1)
    topk_round()

    gates = pm_ref[...].astype(F32)
    mixed = (gates[:, :D_MODEL] * _dot(o_a, wa_ref[...])
             + gates[:, D_MODEL:] * _dot(ob_ref[...], wb_ref[...]))
    vals, idxs = route["vals"], route["idxs"]
    assert len(vals) == TOP_K
    exps = [jnp.exp(v - vals[0]) for v in vals]
    denom = exps[0] + exps[1] + exps[2] + exps[3]
    onehot = jnp.zeros((tm, LANES), F32)
    for idx in idxs:
        onehot = onehot + jnp.where(lane == idx, 1.0, 0.0)

    h1 = x_ref[...] + _dot(mixed.astype(BF16), wo_ref[...])
    h1_ref[...] = h1
    hn = h1 * _rms_scale(h1) * ln2_ref[...]

    row = lax.broadcasted_iota(jnp.int32, (tm, tm), 0)
    col = lax.broadcasted_iota(jnp.int32, (tm, tm), 1)
    below = jnp.where(col < row, 1.0, 0.0).astype(BF16)
    before = _dot(below, onehot.astype(BF16)) + carry_ref[0:1, :]
    ranks = [jnp.sum(jnp.where(lane == idx, before, 0.0), axis=-1, keepdims=True) for idx in idxs]
    running = carry_ref[0:1, :]
    for j in range(tm // TM_ROW):
        cbefore_ref[j] = jnp.broadcast_to(running, cbefore_ref.shape[1:])
        running = running + jnp.sum(onehot[j * TM_ROW:(j + 1) * TM_ROW], axis=0, keepdims=True)
    carry = jnp.where(routed, running, carry_ref[0:1, :])
    carry_ref[...] = jnp.broadcast_to(carry, carry_ref.shape)
    cnt_ref[...] = jnp.broadcast_to(carry, cnt_ref.shape)

    hn_hi = hn.astype(BF16)
    hn_ref[...] = hn_hi

    ri = jnp.zeros((tm, LANES), F32)
    for j, val in enumerate(idxs + ranks):
        ri = jnp.where(lane == float(j), val, ri)
    ric_ref[...] = ri.astype(jnp.int32)
    ri_t = ri.T[:2 * TOP_K].astype(jnp.int32)
    for j in range(tm // TM_ROW):
        ri_ref[j] = ri_t[:, j * TM_ROW:(j + 1) * TM_ROW]
    meta = jnp.zeros((tm, LANES), F32)
    for k in range(TOP_K):
        gate = exps[k] / denom
        gate_hi = gate.astype(BF16).astype(F32)
        meta = jnp.where(lane == float(k), idxs[k], meta)
        meta = jnp.where(lane == float(TOP_K + k), gate_hi, meta)
        meta = jnp.where(lane == float(2 * TOP_K + k), gate - gate_hi, meta)
    meta_ref[...] = meta.astype(BF16)

    hn_lo = (hn - hn_hi.astype(F32)).astype(BF16)
    logit_ref[step % 2] = (_dot(hn_hi, wrh_ref[...]) + _dot(hn_lo, wrh_ref[...])
                           + _dot(hn_hi, wrl_ref[...]) + br_ref[...])


def _mix(x2, o0, l0, o1, l1, o2, l2, o_b, pm, wa, wb, wo, ln2_w, wr_hi, wr_lo, br_pad):
    n_tok = x2.shape[0]
    tm = TM_MIX
    bsz = o1.shape[0]
    d1, d2 = DA_GROUPS[1][1], DA_GROUPS[2][1]
    tiles_per_seq = (n_tok // bsz) // tm
    n_tiles = n_tok // tm

    def tile(i):
        return jnp.minimum(i, n_tiles - 1)

    def routed(i):
        return jnp.maximum(i - 1, 0)

    def rows(width, which=tile):
        return pl.BlockSpec((tm, width), lambda i: (which(i), 0))

    def residue_major(d, width):
        return pl.BlockSpec((1, d, tm // d, width),
                            lambda i: (tile(i) // tiles_per_seq, 0, tile(i) % tiles_per_seq, 0))

    def whole(arr):
        return pl.BlockSpec(arr.shape, lambda i: (0,) * arr.ndim)

    ln2 = ln2_w.reshape(1, D_MODEL)
    return pl.pallas_call(
        _mix_kernel,
        grid=(n_tiles + 1,),
        in_specs=[rows(D_MODEL), rows(DA_WIDTH), rows(LANES),
                  residue_major(d1, DA_WIDTH), residue_major(d1, LANES),
                  residue_major(d2, DA_WIDTH), residue_major(d2, LANES),
                  rows(GLA_VALUE_DIM), rows(MERGE_W),
                  whole(wa), whole(wb), whole(wo), whole(ln2), whole(wr_hi), whole(wr_lo), whole(br_pad)],
        out_specs=(rows(D_MODEL), rows(D_MODEL),
                   pl.BlockSpec((tm // TM_ROW, 2 * TOP_K, TM_ROW), lambda i: (routed(i), 0, 0)),
                   rows(LANES, routed), rows(LANES, routed),
                   pl.BlockSpec((tm // TM_ROW, 8, LANES), lambda i: (routed(i), 0, 0)),
                   pl.BlockSpec((8, LANES), lambda i: (0, 0))),
        out_shape=(jax.ShapeDtypeStruct((n_tok, D_MODEL), F32),
                   jax.ShapeDtypeStruct((n_tok, D_MODEL), BF16),
                   jax.ShapeDtypeStruct((n_tok // TM_ROW, 2 * TOP_K, TM_ROW), jnp.int32),
                   jax.ShapeDtypeStruct((n_tok, LANES), jnp.int32),
                   jax.ShapeDtypeStruct((n_tok, LANES), BF16),
                   jax.ShapeDtypeStruct((n_tok // TM_ROW, 8, LANES), F32),
                   jax.ShapeDtypeStruct((8, LANES), F32)),
        scratch_shapes=[pltpu.VMEM((DA_HEADS, tm, DA_HEAD_DIM), F32), pltpu.VMEM((tm, LANES), F32),
                        pltpu.VMEM((DA_HEADS, tm, DA_HEAD_DIM), F32), pltpu.VMEM((tm, LANES), F32),
                        pltpu.VMEM((8, LANES), F32), pltpu.VMEM((2, tm, LANES), F32)],
        compiler_params=pltpu.CompilerParams(
            dimension_semantics=("arbitrary",), vmem_limit_bytes=VMEM_LIMIT),
        name="mix",
    )(x2, o0, l0, o1, l1, o2, l2, o_b, pm, wa, wb, wo, ln2, wr_hi, wr_lo, br_pad)


def _pack_pairs(x):
    n = x.shape[1] // 2
    rounded = x.astype(BF16).astype(F32)
    lo = lax.bitcast_convert_type(rounded[:, :n], U32) >> 16
    hi = lax.bitcast_convert_type(rounded[:, n:], U32) & jnp.uint32(0xFFFF0000)
    return hi | lo


def _unpack_pairs(u):
    lo = lax.bitcast_convert_type(u << 16, F32).astype(BF16)
    hi = lax.bitcast_convert_type(u & jnp.uint32(0xFFFF0000), F32).astype(BF16)
    return lo, hi


def _dispatch_kernel(zstart_ref, nwin_ref, wsrc_ref, wdst_ref, pos_ref, meta_ref, hn_ref, xin_ref,
                     buf_ref, zero_ref, sem, zsem):
    i = pl.program_id(0)
    tm = hn_ref.shape[0]
    n_buf_rows = buf_ref.shape[1]
    slot = i % 2

    def window_copy(s, buf_slot):
        src = pl.multiple_of(wsrc_ref[i * _n_slot(WIN_D) + s], ROW_ALIGN)
        dst = pl.multiple_of(wdst_ref[i * _n_slot(WIN_D) + s], ROW_ALIGN)
        return pltpu.make_async_copy(buf_ref.at[buf_slot, pl.ds(src, WIN_D), :],
                                     xin_ref.at[pl.ds(dst, WIN_D), :], sem)

    def wait_windows(step):
        def body(s, carry):
            pltpu.make_async_copy(buf_ref.at[0, pl.ds(0, WIN_D), :], xin_ref.at[pl.ds(0, WIN_D), :],
                                  sem).wait()
            return carry
        lax.fori_loop(0, nwin_ref[step], body, 0)

    @pl.when(i == 0)
    def _():
        zero_ref[...] = jnp.zeros_like(zero_ref)

        def zero_copy(j):
            start = pl.multiple_of(jnp.maximum(zstart_ref[j], 0), ZCHUNK)
            return pltpu.make_async_copy(zero_ref, xin_ref.at[pl.ds(start, ZCHUNK), :], zsem)

        def start_one(j, carry):
            @pl.when(zstart_ref[j] >= 0)
            def _():
                zero_copy(j).start()
            return carry

        def wait_one(j, carry):
            @pl.when(zstart_ref[j] >= 0)
            def _():
                zero_copy(j).wait()
            return carry

        lax.fori_loop(0, zstart_ref.shape[0], start_one, 0)
        lax.fori_loop(0, zstart_ref.shape[0], wait_one, 0)

    row = lax.broadcasted_iota(jnp.int32, (n_buf_rows, tm), 0)

    perm = jnp.zeros((n_buf_rows, tm), F32)
    for k in range(TOP_K):
        perm = perm + jnp.where(row == pos_ref[0, k:k + 1, :], 1.0, 0.0)
    perm = perm.astype(BF16)
    buf_ref[slot, :, :D_MODEL // 2] = _pack_pairs(_dot(perm, hn_ref[...]))
    buf_ref[slot, :, D_MODEL // 2:] = lax.bitcast_convert_type(_dot(perm, meta_ref[...]), U32)

    @pl.when(i > 0)
    def _():
        wait_windows(i - 1)

    for buf_slot in range(2):
        @pl.when(slot == buf_slot)
        def _():
            def issue(s, carry):
                window_copy(s, buf_slot).start()
                return carry
            lax.fori_loop(0, nwin_ref[i], issue, 0)

    @pl.when(i == pl.num_programs(0) - 1)
    def _():
        wait_windows(i)


def _dispatch(zero_start, n_win, win_src, win_dst, pos, meta, hn, n_rows):
    n_tok = hn.shape[0]
    tm = TM_ROW
    n_tiles = n_tok // tm
    n_buf_rows = _sort_rows(WIN_D)
    return pl.pallas_call(
        _dispatch_kernel,
        grid_spec=pltpu.PrefetchScalarGridSpec(
            num_scalar_prefetch=4,
            grid=(n_tiles,),
            in_specs=[
                pl.BlockSpec((1, TOP_K, tm), lambda i, *_: (i, 0, 0)),
                pl.BlockSpec((tm, LANES), lambda i, *_: (i, 0)),
                pl.BlockSpec((tm, D_MODEL), lambda i, *_: (i, 0)),
            ],
            out_specs=pl.BlockSpec(memory_space=pl.ANY),
            scratch_shapes=[pltpu.VMEM((2, n_buf_rows, ROW_W), U32),
                            pltpu.VMEM((ZCHUNK, ROW_W), U32),
                            pltpu.SemaphoreType.DMA(()), pltpu.SemaphoreType.DMA(())],
        ),
        out_shape=jax.ShapeDtypeStruct((n_rows, ROW_W), U32),
        compiler_params=pltpu.CompilerParams(
            dimension_semantics=("arbitrary",), vmem_limit_bytes=VMEM_LIMIT),
        name="dispatch",
    )(zero_start, n_win, win_src.reshape(-1), win_dst.reshape(-1), pos, meta, hn)


def _expert_kernel(be_ref, live_ref, x_ref, wu_hbm, bu_ref, wd_hbm, bd_ref, y_ref,
                   wu32_ref, wd32_ref, wu16_ref, wd16_ref, wsem):
    i = pl.program_id(0)
    live = live_ref[i]
    expert_id = be_ref[i]

    def weight_copies(expert, slot):
        return (pltpu.make_async_copy(wu_hbm.at[expert], wu32_ref.at[slot], wsem.at[0, slot]),
                pltpu.make_async_copy(wd_hbm.at[expert], wd32_ref.at[slot], wsem.at[1, slot]))

    @pl.when(i == 0)
    def _():
        for cp in weight_copies(expert_id, 0):
            cp.start()

    for slot in range(2):
        @pl.when(((i == 0) | (expert_id != be_ref[jnp.maximum(i - 1, 0)])) & (expert_id % 2 == slot))
        def _():
            for cp in weight_copies(expert_id, slot):
                cp.wait()

            @pl.when(expert_id + 1 < N_EXPERTS)
            def _():
                for cp in weight_copies(expert_id + 1, 1 - slot):
                    cp.start()

            wu16_ref[...] = wu32_ref[slot].astype(BF16)
            wd16_ref[...] = wd32_ref[slot].astype(BF16)

    def compute(m):
        half = D_MODEL // 2
        x_lo, x_hi = _unpack_pairs(x_ref[:m, :half])
        meta = lax.bitcast_convert_type(x_ref[:m, half:half + 3 * TOP_K], F32)
        expert = be_ref[i].astype(F32)
        gate = jnp.zeros((m, 1), F32)
        for k in range(TOP_K):
            weight = meta[:, TOP_K + k:TOP_K + k + 1] + meta[:, 2 * TOP_K + k:2 * TOP_K + k + 1]
            gate = gate + jnp.where(meta[:, k:k + 1] == expert, weight, 0.0)
        hu = _dot(x_lo, wu16_ref[:half, :]) + _dot(x_hi, wu16_ref[half:, :]) + bu_ref[0]
        x_glu = jnp.minimum(hu[:, :D_FF], SWIGLU_LIMIT)
        x_lin = jnp.clip(hu[:, D_FF:], -SWIGLU_LIMIT, SWIGLU_LIMIT)
        act = x_glu * _sigmoid(SWIGLU_ALPHA * x_glu) * (x_lin + 1.0)
        y_ref[:m, :] = _pack_pairs((_dot(act.astype(BF16), wd16_ref[...]) + bd_ref[0]) * gate)
        if m < TB:
            y_ref[m:, :] = jnp.zeros((TB - m, y_ref.shape[1]), y_ref.dtype)

    @pl.when(live == 0)
    def _():
        y_ref[...] = jnp.zeros_like(y_ref)

    for m in range(TB_STEP, TB + 1, TB_STEP):
        @pl.when((live > m - TB_STEP) & (live <= m))
        def _():
            compute(m)


def _experts(block_e, block_live, xin, w_up, b_up, w_down, b_down):
    n_rows = xin.shape[0]
    return pl.pallas_call(
        _expert_kernel,
        grid_spec=pltpu.PrefetchScalarGridSpec(
            num_scalar_prefetch=2,
            grid=(n_rows // TB,),
            in_specs=[
                pl.BlockSpec((TB, ROW_W), lambda i, be, live: (jnp.where(live[i] > 0, i, 0), 0)),
                pl.BlockSpec(memory_space=pl.ANY),
                pl.BlockSpec((1, 1, 2 * D_FF), lambda i, be, nu: (be[i], 0, 0)),
                pl.BlockSpec(memory_space=pl.ANY),
                pl.BlockSpec((1, 1, D_MODEL), lambda i, be, nu: (be[i], 0, 0)),
            ],
            out_specs=pl.BlockSpec((TB, D_MODEL // 2), lambda i, be, nu: (i, 0)),
            scratch_shapes=[pltpu.VMEM((2, D_MODEL, 2 * D_FF), F32), pltpu.VMEM((2, D_FF, D_MODEL), F32),
                            pltpu.VMEM((D_MODEL, 2 * D_FF), BF16), pltpu.VMEM((D_FF, D_MODEL), BF16),
                            pltpu.SemaphoreType.DMA((2, 2))],
        ),
        out_shape=jax.ShapeDtypeStruct((n_rows, D_MODEL // 2), U32),
        compiler_params=pltpu.CompilerParams(
            dimension_semantics=("arbitrary",), vmem_limit_bytes=VMEM_LIMIT),
        name="experts",
    )(block_e, block_live, xin, w_up, b_up.reshape(N_EXPERTS, 1, 2 * D_FF),
      w_down, b_down.reshape(N_EXPERTS, 1, D_MODEL))


def _combine_kernel(nwin_ref, wsrc_ref, wdst_ref, ric_ref, delta_ref, ricn_ref, deltan_ref, yb_ref, h1_ref,
                    lnf_ref, o_ref, stage_ref, buf_ref, pick_ref, sem):
    i = pl.program_id(0)
    tm = h1_ref.shape[0]
    n_buf_rows = buf_ref.shape[0]

    parity = i % 2

    def fetch(step, stage_slot):
        def body(s, carry):
            dst = pl.multiple_of(wdst_ref[step * _n_slot(WIN_C) + s], ROW_ALIGN)
            pltpu.make_async_copy(
                yb_ref.at[pl.ds(dst, WIN_C), :],
                stage_ref.at[stage_slot, pl.ds(pl.multiple_of(s * WIN_C, WIN_C), WIN_C), :],
                sem.at[stage_slot]).start()
            return carry
        lax.fori_loop(0, nwin_ref[step], body, 0)

    lane = lax.broadcasted_iota(jnp.int32, (tm, LANES), 1).astype(F32)
    col = lax.broadcasted_iota(jnp.int32, (tm, n_buf_rows), 1)

    def pick_round(pick, k, ric, delta):
        offset = jnp.sum(jnp.where(lane == ric[:, k:k + 1], delta, 0.0), axis=-1, keepdims=True)
        pos = (ric[:, TOP_K + k:TOP_K + k + 1] + offset).astype(jnp.int32)
        return pick + jnp.where(col == pos, 1.0, 0.0)

    @pl.when(i == 0)
    def _():
        buf_ref[...] = jnp.zeros_like(buf_ref)
        fetch(i, 0)
        pick = jnp.zeros((tm, n_buf_rows), F32)
        ric = ric_ref[...].astype(F32)
        for k in range(TOP_K):
            pick = pick_round(pick, k, ric, delta_ref[0])
        pick_ref[0] = pick.astype(BF16)

    for stage_slot in range(2):
        @pl.when((i + 1 < pl.num_programs(0)) & (parity != stage_slot))
        def _():
            fetch(i + 1, stage_slot)

    for stage_slot in range(2):
        @pl.when(parity == stage_slot)
        def _():
            def drain(s, carry):
                pltpu.make_async_copy(yb_ref.at[pl.ds(0, WIN_C), :],
                                      stage_ref.at[stage_slot, pl.ds(0, WIN_C), :], sem.at[stage_slot]).wait()
                return carry

            def compact(s, carry):
                src = pl.multiple_of(wsrc_ref[i * _n_slot(WIN_C) + s], ROW_ALIGN)
                buf_ref[pl.ds(src, WIN_C), :] = stage_ref[
                    stage_slot, pl.ds(pl.multiple_of(s * WIN_C, WIN_C), WIN_C), :]
                return carry

            lax.fori_loop(0, nwin_ref[i], drain, 0)
            lax.fori_loop(0, nwin_ref[i], compact, 0)

    ric_next = ricn_ref[...].astype(F32)
    delta_next = deltan_ref[0]
    pick = pick_ref[parity]
    nxt = jnp.zeros((tm, n_buf_rows), F32)
    y_lo, y_hi = _unpack_pairs(buf_ref[...])
    nxt = pick_round(nxt, 0, ric_next, delta_next)
    left = _dot(pick, y_lo)
    nxt = pick_round(nxt, 1, ric_next, delta_next)
    right = _dot(pick, y_hi)
    nxt = pick_round(nxt, 2, ric_next, delta_next)
    h2 = h1_ref[...] + jnp.concatenate([left, right], axis=-1)
    nxt = pick_round(nxt, 3, ric_next, delta_next)
    o_ref[...] = h2 * _rms_scale(h2) * lnf_ref[...]
    pick_ref[1 - parity] = nxt.astype(BF16)


def _combine(n_win, win_src, win_dst, ric, delta, yb, h1, lnf_w):
    n_tok = h1.shape[0]
    tm = TM_ROW
    n_tiles = n_tok // tm
    return pl.pallas_call(
        _combine_kernel,
        grid_spec=pltpu.PrefetchScalarGridSpec(
            num_scalar_prefetch=3,
            grid=(n_tiles,),
            in_specs=[
                pl.BlockSpec((tm, LANES), lambda i, *_: (i, 0)),
                pl.BlockSpec((1, 1, LANES), lambda i, *_: (i, 0, 0)),
                pl.BlockSpec((tm, LANES), lambda i, *_: (jnp.minimum(i + 1, n_tiles - 1), 0)),
                pl.BlockSpec((1, 1, LANES), lambda i, *_: (jnp.minimum(i + 1, n_tiles - 1), 0, 0)),
                pl.BlockSpec(memory_space=pl.ANY),
                pl.BlockSpec((tm, D_MODEL), lambda i, *_: (i, 0)),
                pl.BlockSpec((1, D_MODEL), lambda i, *_: (0, 0)),
            ],
            out_specs=pl.BlockSpec((tm, D_MODEL), lambda i, *_: (i, 0)),
            scratch_shapes=[pltpu.VMEM((2, _n_slot(WIN_C) * WIN_C, D_MODEL // 2), U32),
                            pltpu.VMEM((_sort_rows(WIN_C), D_MODEL // 2), U32),
                            pltpu.VMEM((2, tm, _sort_rows(WIN_C)), BF16), pltpu.SemaphoreType.DMA((2,))],
        ),
        out_shape=jax.ShapeDtypeStruct((n_tok, D_MODEL), F32),
        compiler_params=pltpu.CompilerParams(
            dimension_semantics=("arbitrary",), vmem_limit_bytes=VMEM_LIMIT),
        name="combine",
    )(n_win, win_src.reshape(-1), win_dst.reshape(-1), ric, delta, ric, delta, yb, h1,
      lnf_w.reshape(1, D_MODEL))


def _layer(h, ln1_w, w_in, w_gk2, b_gk2, gla_norm_w, w_proj_a, w_proj_b, w_out,
           ln2_w, w_router, b_router, w_up, b_up, w_down, b_down, lnf_w):
    bsz, seq, _ = h.shape
    n_tok = bsz * seq

    n_main = 3 * QKV_W + 2 * GLA_KEY_DIM + 2 * GLA_VALUE_DIM
    pad = LANES - GLA_GATE_RANK
    w_main = w_in[:, :n_main].astype(BF16)
    w_tail = jnp.concatenate(
        [w_in[:, n_main:n_main + GLA_GATE_RANK], jnp.zeros((D_MODEL, pad), F32),
         w_in[:, n_main + GLA_GATE_RANK:]], axis=1).astype(BF16)
    w2_pad = jnp.concatenate([w_gk2, jnp.zeros((pad, GLA_KEY_DIM), F32)], axis=0).astype(BF16)
    wr_pad = jnp.concatenate([w_router, jnp.zeros((D_MODEL, LANES - N_EXPERTS), F32)], axis=1)
    wr_hi = wr_pad.astype(BF16)
    wr_lo = (wr_pad - wr_hi.astype(F32)).astype(BF16)
    br_pad = jnp.concatenate([b_router, jnp.zeros((LANES - N_EXPERTS,), F32)]).reshape(1, LANES)

    pa0, pa1, pa2, pg, pm = _inproj(h, ln1_w, w_main, w_tail)
    o0, l0 = _dil_attn(pa0)
    o1, l1 = _dil_attn(pa1)
    o2, l2 = _dil_attn(pa2)
    o_b = _gla(pg, w2_pad, b_gk2, gla_norm_w)

    h1, hn, ri, ric, meta, carry_f, cnt = _mix(
        h.reshape(n_tok, D_MODEL), o0.reshape(n_tok, DA_WIDTH), l0.reshape(n_tok, LANES),
        o1, l1, o2, l2, o_b.reshape(n_tok, GLA_VALUE_DIM), pm.reshape(n_tok, MERGE_W),
        w_proj_a.astype(BF16), w_proj_b.astype(BF16), w_out.astype(BF16), ln2_w, wr_hi, wr_lo, br_pad)

    i32 = jnp.int32
    n_tiles = n_tok // TM_ROW
    experts = jnp.arange(N_EXPERTS)
    counts = cnt[0, :N_EXPERTS].astype(i32)
    before = carry_f[:, 0, :N_EXPERTS].astype(i32)
    run = jnp.concatenate([before[1:], counts[None]], axis=0) - before
    run_al = (run + ROW_ALIGN - 1) // ROW_ALIGN * ROW_ALIGN
    rows_end = jnp.cumsum(run_al, axis=0)
    rows_before = rows_end - run_al
    used = rows_end[-1]
    slack = max(WIN_D, WIN_C) - ROW_ALIGN
    padded = (used + slack + TB - 1) // TB * TB
    pad_end = jnp.cumsum(padded)
    pad_start = pad_end - padded
    n_asg = n_tok * TOP_K
    n_rows = (n_asg + n_tiles * N_EXPERTS * (ROW_ALIGN - 1)
              + N_EXPERTS * (slack + TB - 1) + TB - 1) // TB * TB
    n_blocks = n_rows // TB
    block_starts = jnp.arange(n_blocks) * TB
    block_e = jnp.minimum(jnp.sum(pad_end[None, :] <= block_starts[:, None], axis=1),
                          N_EXPERTS - 1).astype(i32)
    used_end = jnp.sum(jnp.where(block_e[:, None] == experts, (pad_start + used)[None, :], 0), axis=1)
    block_live = jnp.clip(used_end - block_starts, 0, TB).astype(i32)

    run_end = jnp.cumsum(run_al, axis=1)
    run_start = run_end - run_al

    def windows(win):
        wins = (run + win - 1) // win
        wins_end = jnp.cumsum(wins, axis=1)
        wins_start = wins_end - wins
        slots = jnp.arange(_n_slot(win))
        slot_e = jnp.sum(wins_end[:, None, :] <= slots[None, :, None], axis=-1)
        slot_is = slot_e[..., None] == experts

        def of_slot(table):
            return jnp.sum(jnp.where(slot_is, table[:, None, :], 0), axis=-1)

        win_off = (slots[None, :] - of_slot(wins_start)) * win
        return (wins_end[:, -1].astype(i32), (of_slot(run_start) + win_off).astype(i32),
                (of_slot(pad_start[None, :] + rows_before) + win_off).astype(i32))

    asg_is = ri[:, :TOP_K, :, None] == experts
    pos = ri[:, TOP_K:, :] + jnp.sum(jnp.where(asg_is, (run_start - before)[:, None, None, :], 0), axis=-1)
    delta = jnp.pad((run_start - before).astype(F32), ((0, 0), (0, LANES - N_EXPERTS)))

    chunk_back = ZCHUNK * (1 + jnp.arange((slack + TB - 1 + ZCHUNK - 1) // ZCHUNK + 1))
    region_chunks = pad_end[:, None] - chunk_back[None, :]
    region_ok = (region_chunks >= pad_start[:, None]) & (region_chunks + ZCHUNK > (pad_start + used)[:, None])
    tail_chunks = pad_end[-1] + ZCHUNK * jnp.arange((n_rows - n_asg) // ZCHUNK)
    zero_start = jnp.concatenate([jnp.where(region_ok, region_chunks, -1).reshape(-1),
                                  jnp.where(tail_chunks < n_rows, tail_chunks, -1)]).astype(i32)

    xin = _dispatch(zero_start, *windows(WIN_D), pos.astype(i32), meta, hn, n_rows)
    yb = _experts(block_e, block_live, xin, w_up, b_up, w_down, b_down)
    out = _combine(*windows(WIN_C), ric, delta.reshape(-1, 1, LANES), yb, h1, lnf_w)
    return out.reshape(bsz, seq, D_MODEL)


def kernel(x, ln1_w, w_in, w_gk2, b_gk2, gla_norm_w, w_proj_a, w_proj_b, w_out, ln2_w, w_router,
           b_router, w_up, b_up, w_down, b_down, lnf_w):
    assert x.shape[-1] == D_MODEL and ln1_w.shape[0] == 1, "one layer of width D_MODEL"
    return _layer(x, ln1_w[0], w_in[0], w_gk2[0], b_gk2[0], gla_norm_w[0], w_proj_a[0], w_proj_b[0],
                  w_out[0], ln2_w[0], w_router[0], b_router[0], w_up[0], b_up[0], w_down[0],
                  b_down[0], lnf_w)
```
